```python
import jax, jax.numpy as jnp
from jax import lax
import numpy as np

D_MODEL = 4096
BATCH = 8
SEQ = 4096
DEPTH = 2

N_A_LAYERS = DEPTH // 2
N_B_LAYERS = DEPTH - N_A_LAYERS
HEAD_DIM = 128
A_HEADS = D_MODEL // HEAD_DIM
A_WIDTH = A_HEADS * HEAD_DIM
B_Q_HEADS = D_MODEL // HEAD_DIM
B_KV_HEADS = 4
B_GROUP = B_Q_HEADS // B_KV_HEADS
B_WIDTH = B_Q_HEADS * HEAD_DIM
KV_WIDTH = B_KV_HEADS * HEAD_DIM
WINDOW = 128
BLOCK = 128
ROPE_THETA = 10000.0
EPS = 1e-6

kernel_name = "yoco_fox_swa_sink_adaln_hybrid"


def rms_norm(x, g):
    xf = x.astype(jnp.float32)
    y = xf * lax.rsqrt(jnp.mean(xf * xf, axis=-1, keepdims=True) + EPS)
    return (y * g.astype(jnp.float32)).astype(x.dtype)


def rope(x, pos):
    half = HEAD_DIM // 2
    inv = ROPE_THETA ** (-jnp.arange(half, dtype=jnp.float32) / half)
    ang = pos.astype(jnp.float32)[:, None] * inv[None, :]
    cos = jnp.cos(ang)[None, :, None, :]
    sin = jnp.sin(ang)[None, :, None, :]
    xf = x.astype(jnp.float32)
    x1, x2 = xf[..., :half], xf[..., half:]
    return jnp.concatenate([x1 * cos - x2 * sin, x2 * cos + x1 * sin], axis=-1).astype(x.dtype)


def adaln(c, w_ada, b_ada):
    mod = jax.nn.silu(c) @ w_ada + b_ada
    shift, scale, gate = jnp.split(mod, 3, axis=-1)
    return shift[:, None, :], scale[:, None, :], gate[:, None, :]


def forgetting_attention(q, k, v, log_f):
    B, H, S, d = q.shape
    n_blocks = S // BLOCK
    F = jnp.cumsum(log_f, axis=-1)
    key_pos = jnp.arange(S)
    scale = HEAD_DIM ** -0.5

    def one_block(i):
        start = i * BLOCK
        qb = lax.dynamic_slice_in_dim(q, start, BLOCK, axis=2)
        Fq = lax.dynamic_slice_in_dim(F, start, BLOCK, axis=2)
        s = jnp.einsum('bhqd,bhkd->bhqk', qb, k).astype(jnp.float32) * scale
        s = s + (Fq[..., :, None] - F[..., None, :])
        q_pos = start + jnp.arange(BLOCK)
        causal = key_pos[None, :] <= q_pos[:, None]
        s = jnp.where(causal[None, None], s, -jnp.inf)
        p = jax.nn.softmax(s, axis=-1).astype(v.dtype)
        return jnp.einsum('bhqk,bhkd->bhqd', p, v)

    out = lax.map(one_block, jnp.arange(n_blocks))
    return out.transpose(1, 2, 0, 3, 4).reshape(B, H, S, d)


def sliding_window_sink_attention(q, k, v, sinks):
    B, S = q.shape[0], q.shape[1]
    nb = S // BLOCK
    qb = q.reshape(B, nb, BLOCK, B_KV_HEADS, B_GROUP, HEAD_DIM)

    def band(t):
        tb = t.reshape(B, nb, BLOCK, B_KV_HEADS, HEAD_DIM)
        prev = jnp.pad(tb[:, :-1], ((0, 0), (1, 0), (0, 0), (0, 0), (0, 0)))
        return jnp.concatenate([prev, tb], axis=2)

    kb, vb = band(k), band(v)
    s = jnp.einsum('bnqkgd,bnskd->bnkgqs', qb, kb).astype(jnp.float32) * (HEAD_DIM ** -0.5)
    qi = jnp.arange(BLOCK)[:, None]
    sj = jnp.arange(2 * BLOCK)[None, :]
    diff = qi + BLOCK - sj
    in_window = (diff >= 0) & (diff < WINDOW)
    key_abs = jnp.arange(nb)[:, None] * BLOCK + jnp.arange(2 * BLOCK)[None, :] - BLOCK
    valid = in_window[None] & (key_abs >= 0)[:, None, :]
    s = jnp.where(valid[None, :, None, None], s, -jnp.inf)
    sink = jnp.broadcast_to(
        sinks.astype(jnp.float32).reshape(B_KV_HEADS, B_GROUP)[None, None, :, :, None, None],
        s.shape[:-1] + (1,))
    p = jax.nn.softmax(jnp.concatenate([s, sink], axis=-1), axis=-1)[..., :-1].astype(v.dtype)
    o = jnp.einsum('bnkgqs,bnskd->bnqkgd', p, vb)
    return o.reshape(B, S, B_Q_HEADS, HEAD_DIM)


def fox_layer(x, c, g, w_ada, b_ada, w_in, b_f, w_out):
    B, S, _ = x.shape
    shift, scale, gate = adaln(c, w_ada, b_ada)
    h = rms_norm(x, g) * (1 + scale) + shift
    proj = h @ w_in
    q, k, v, z, f = jnp.split(proj, [A_WIDTH, 2 * A_WIDTH, 3 * A_WIDTH, 4 * A_WIDTH], axis=-1)
    heads = lambda t: t.reshape(B, S, A_HEADS, HEAD_DIM).transpose(0, 2, 1, 3)
    log_f = jax.nn.log_sigmoid(f.astype(jnp.float32) + b_f.astype(jnp.float32)).transpose(0, 2, 1)
    o = forgetting_attention(heads(q), heads(k), heads(v), log_f)
    o = o.transpose(0, 2, 1, 3).reshape(B, S, A_WIDTH)
    y = (o * jax.nn.silu(z)) @ w_out
    return x + gate * y


def shared_kv(h, g_kv, w_kv, pos):
    B, S, _ = h.shape
    kv = rms_norm(h, g_kv) @ w_kv
    k, v = jnp.split(kv, 2, axis=-1)
    k = rope(k.reshape(B, S, B_KV_HEADS, HEAD_DIM), pos)
    v = v.reshape(B, S, B_KV_HEADS, HEAD_DIM)
    return k, v


def swa_layer(x, c, g, w_ada, b_ada, w_in, sinks, w_out, k, v, pos):
    B, S, _ = x.shape
    shift, scale, gate = adaln(c, w_ada, b_ada)
    h = rms_norm(x, g) * (1 + scale) + shift
    proj = h @ w_in
    q, z = jnp.split(proj, 2, axis=-1)
    q = rope(q.reshape(B, S, B_Q_HEADS, HEAD_DIM), pos)
    o = sliding_window_sink_attention(q, k, v, sinks).reshape(B, S, B_WIDTH)
    y = (o * jax.nn.silu(z)) @ w_out
    return x + gate * y


def _fwd_setup_inputs(seed: int = 0) -> dict:
    key = jax.random.key(seed)
    ks = jax.random.split(key, 14)
    f32 = jnp.float32
    nrm = lambda k, shape, s: jax.random.normal(k, shape, f32) * s
    d_in_a = 4 * A_WIDTH + A_HEADS
    return {
        "x": nrm(ks[0], (BATCH, SEQ, D_MODEL), 1.0),
        "c": nrm(ks[1], (BATCH, D_MODEL), 1.0),
        "norm_g": 1.0 + nrm(ks[2], (DEPTH, D_MODEL), 0.02),
        "ada_w": nrm(ks[3], (DEPTH, D_MODEL, 3 * D_MODEL), 0.5 * D_MODEL ** -0.5),
        "ada_b": nrm(ks[4], (DEPTH, 3 * D_MODEL), 0.02),
        "a_w_in": nrm(ks[5], (N_A_LAYERS, D_MODEL, d_in_a), D_MODEL ** -0.5),
        "a_b_f": jax.random.uniform(ks[6], (N_A_LAYERS, A_HEADS), f32, 1.0, 4.0),
        "a_w_out": nrm(ks[7], (N_A_LAYERS, A_WIDTH, D_MODEL), A_WIDTH ** -0.5),
        "kv_norm_g": 1.0 + nrm(ks[8], (D_MODEL,), 0.02),
        "kv_w": nrm(ks[9], (D_MODEL, 2 * KV_WIDTH), D_MODEL ** -0.5),
        "b_w_in": nrm(ks[10], (N_B_LAYERS, D_MODEL, 2 * B_WIDTH), D_MODEL ** -0.5),
        "b_sinks": nrm(ks[11], (N_B_LAYERS, B_Q_HEADS), 0.5),
        "b_w_out": nrm(ks[12], (N_B_LAYERS, B_WIDTH, D_MODEL), B_WIDTH ** -0.5),
        "final_norm_g": 1.0 + nrm(ks[13], (D_MODEL,), 0.02),
    }


def _fwd_reference(x, c, norm_g, ada_w, ada_b, a_w_in, a_b_f, a_w_out, kv_norm_g, kv_w,
              b_w_in, b_sinks, b_w_out, final_norm_g):
    pos = jnp.arange(x.shape[1])
    h = x
    k = v = None
    for layer in range(DEPTH):
        if layer < N_A_LAYERS:
            h = fox_layer(h, c, norm_g[layer], ada_w[layer], ada_b[layer],
                          a_w_in[layer], a_b_f[layer], a_w_out[layer])
        else:
            if layer == N_A_LAYERS:
                k, v = shared_kv(h, kv_norm_g, kv_w, pos)
            j = layer - N_A_LAYERS
            h = swa_layer(h, c, norm_g[layer], ada_w[layer], ada_b[layer],
                          b_w_in[j], b_sinks[j], b_w_out[j], k, v, pos)
    return rms_norm(h, final_norm_g)


import jax as _jax
import jax.numpy as _jnp

TWIN_FORMAT = 'train_step'
FWD_PARAMS = ['x', 'c', 'norm_g', 'ada_w', 'ada_b', 'a_w_in', 'a_b_f', 'a_w_out', 'kv_norm_g', 'kv_w', 'b_w_in', 'b_sinks', 'b_w_out', 'final_norm_g']
TWIN_WEIGHTS = ['norm_g', 'ada_w', 'ada_b', 'a_w_in', 'a_b_f', 'a_w_out', 'kv_norm_g', 'kv_w', 'b_w_in', 'b_sinks', 'b_w_out', 'final_norm_g']
TWIN_DIFF_INPUT = 'x'
TWIN_INPUTS = ['x', 'c', 'norm_g', 'ada_w', 'ada_b', 'a_w_in', 'a_b_f', 'a_w_out', 'kv_norm_g', 'kv_w', 'b_w_in', 'b_sinks', 'b_w_out', 'final_norm_g', 'loss_target', 'm_norm_g', 'm_ada_w', 'm_ada_b', 'm_a_w_in', 'm_a_b_f', 'm_a_w_out', 'm_kv_norm_g', 'm_kv_w', 'm_b_w_in', 'm_b_sinks', 'm_b_w_out', 'm_final_norm_g', 'v_norm_g', 'v_ada_w', 'v_ada_b', 'v_a_w_in', 'v_a_b_f', 'v_a_w_out', 'v_kv_norm_g', 'v_kv_w', 'v_b_w_in', 'v_b_sinks', 'v_b_w_out', 'v_final_norm_g']
TWIN_OUTPUTS = ['loss', 'grad_x', 'grad_norm_g', 'grad_ada_w', 'grad_ada_b', 'grad_a_w_in', 'grad_a_b_f', 'grad_a_w_out', 'grad_kv_norm_g', 'grad_kv_w', 'grad_b_w_in', 'grad_b_sinks', 'grad_b_w_out', 'grad_final_norm_g', 'delta_norm_g', 'delta_ada_w', 'delta_ada_b', 'delta_a_w_in', 'delta_a_b_f', 'delta_a_w_out', 'delta_kv_norm_g', 'delta_kv_w', 'delta_b_w_in', 'delta_b_sinks', 'delta_b_w_out', 'delta_final_norm_g', 'new_m_norm_g', 'new_m_ada_w', 'new_m_ada_b', 'new_m_a_w_in', 'new_m_a_b_f', 'new_m_a_w_out', 'new_m_kv_norm_g', 'new_m_kv_w', 'new_m_b_w_in', 'new_m_b_sinks', 'new_m_b_w_out', 'new_m_final_norm_g', 'new_v_norm_g', 'new_v_ada_w', 'new_v_ada_b', 'new_v_a_w_in', 'new_v_a_b_f', 'new_v_a_w_out', 'new_v_kv_norm_g', 'new_v_kv_w', 'new_v_b_w_in', 'new_v_b_sinks', 'new_v_b_w_out', 'new_v_final_norm_g']
TWIN_LEAF_KINDS = {'loss': 'loss', 'grad_x': 'grad_x', 'grad_norm_g': 'grad_w', 'grad_ada_w': 'grad_w', 'grad_ada_b': 'grad_w', 'grad_a_w_in': 'grad_w', 'grad_a_b_f': 'grad_w', 'grad_a_w_out': 'grad_w', 'grad_kv_norm_g': 'grad_w', 'grad_kv_w': 'grad_w', 'grad_b_w_in': 'grad_w', 'grad_b_sinks': 'grad_w', 'grad_b_w_out': 'grad_w', 'grad_final_norm_g': 'grad_w', 'delta_norm_g': 'delta_w', 'delta_ada_w': 'delta_w', 'delta_ada_b': 'delta_w', 'delta_a_w_in': 'delta_w', 'delta_a_b_f': 'delta_w', 'delta_a_w_out': 'delta_w', 'delta_kv_norm_g': 'delta_w', 'delta_kv_w': 'delta_w', 'delta_b_w_in': 'delta_w', 'delta_b_sinks': 'delta_w', 'delta_b_w_out': 'delta_w', 'delta_final_norm_g': 'delta_w', 'new_m_norm_g': 'new_m', 'new_m_ada_w': 'new_m', 'new_m_ada_b': 'new_m', 'new_m_a_w_in': 'new_m', 'new_m_a_b_f': 'new_m', 'new_m_a_w_out': 'new_m', 'new_m_kv_norm_g': 'new_m', 'new_m_kv_w': 'new_m', 'new_m_b_w_in': 'new_m', 'new_m_b_sinks': 'new_m', 'new_m_b_w_out': 'new_m', 'new_m_final_norm_g': 'new_m', 'new_v_norm_g': 'new_v', 'new_v_ada_w': 'new_v', 'new_v_ada_b': 'new_v', 'new_v_a_w_in': 'new_v', 'new_v_a_b_f': 'new_v', 'new_v_a_w_out': 'new_v', 'new_v_kv_norm_g': 'new_v', 'new_v_kv_w': 'new_v', 'new_v_b_w_in': 'new_v', 'new_v_b_sinks': 'new_v', 'new_v_b_w_out': 'new_v', 'new_v_final_norm_g': 'new_v'}


def _forward(args):
    return _fwd_reference(*[args[k] for k in FWD_PARAMS])


def _output_shape():
    out = _jax.eval_shape(lambda: _forward(_fwd_setup_inputs(0)))
    return out.shape, out.dtype

N_MICROBATCH = 1
ADAM_LR = 0.001
ADAM_B1 = 0.9
ADAM_B2 = 0.999
ADAM_EPS = 1e-08
ADAM_WD = 0.01
ADAM_STEP = 10
PER_EXAMPLE_BATCH_AXIS = {'x': 0, 'c': 0, 'loss_target': 0}
SHARED_INPUTS = []
_WEIGHT_DTYPES = {'norm_g': _jnp.float32, 'ada_w': _jnp.float32, 'ada_b': _jnp.float32, 'a_w_in': _jnp.float32, 'a_b_f': _jnp.float32, 'a_w_out': _jnp.float32, 'kv_norm_g': _jnp.float32, 'kv_w': _jnp.float32, 'b_w_in': _jnp.float32, 'b_sinks': _jnp.float32, 'b_w_out': _jnp.float32, 'final_norm_g': _jnp.float32}
MOMENT_SCALE = {'norm_g': 5.370729e-03, 'ada_w': 5.095983e-03, 'ada_b': 8.502465e-03, 'a_w_in': 3.930863e-03, 'a_b_f': 5.667523e-02, 'a_w_out': 4.526682e-03, 'kv_norm_g': 2.543466e-03, 'kv_w': 5.108808e-03, 'b_w_in': 1.840258e-03, 'b_sinks': 2.212854e-03, 'b_w_out': 1.877985e-03, 'final_norm_g': 7.985066e+00}


def _to_microbatches(a, axis):
    t = _jnp.moveaxis(a, axis, 0)
    t = t.reshape((N_MICROBATCH, t.shape[0] // N_MICROBATCH) + t.shape[1:])
    return _jnp.moveaxis(t, 1, axis + 1)


def setup_inputs(seed: int = 0) -> dict:
    inp = _fwd_setup_inputs(seed)
    key = _jax.random.fold_in(_jax.random.key(seed), 7919)
    shape, _ = _output_shape()
    out = dict(inp)
    out["loss_target"] = _jax.random.normal(_jax.random.fold_in(key, 0), shape, _jnp.float32)
    for i, name in enumerate(TWIN_WEIGHTS):
        w = inp[name].astype(_jnp.float32)
        if MOMENT_SCALE is None:
            s = _jnp.sqrt(_jnp.mean(_jnp.square(w)) + 1e-30)
        else:
            s = MOMENT_SCALE[name]
        km, kv = _jax.random.split(_jax.random.fold_in(key, i + 1))
        out[name] = w
        out["m_" + name] = s * _jax.random.normal(km, w.shape, _jnp.float32)
        out["v_" + name] = (s * s) * _jax.random.uniform(kv, w.shape, _jnp.float32, 0.5, 1.5)
    if N_MICROBATCH > 1:
        for name, axis in PER_EXAMPLE_BATCH_AXIS.items():
            out[name] = _to_microbatches(out[name], axis)
    return {'x': out['x'], 'c': out['c'], 'norm_g': out['norm_g'], 'ada_w': out['ada_w'], 'ada_b': out['ada_b'], 'a_w_in': out['a_w_in'], 'a_b_f': out['a_b_f'], 'a_w_out': out['a_w_out'], 'kv_norm_g': out['kv_norm_g'], 'kv_w': out['kv_w'], 'b_w_in': out['b_w_in'], 'b_sinks': out['b_sinks'], 'b_w_out': out['b_w_out'], 'final_norm_g': out['final_norm_g'], 'loss_target': out['loss_target'], 'm_norm_g': out['m_norm_g'], 'm_ada_w': out['m_ada_w'], 'm_ada_b': out['m_ada_b'], 'm_a_w_in': out['m_a_w_in'], 'm_a_b_f': out['m_a_b_f'], 'm_a_w_out': out['m_a_w_out'], 'm_kv_norm_g': out['m_kv_norm_g'], 'm_kv_w': out['m_kv_w'], 'm_b_w_in': out['m_b_w_in'], 'm_b_sinks': out['m_b_sinks'], 'm_b_w_out': out['m_b_w_out'], 'm_final_norm_g': out['m_final_norm_g'], 'v_norm_g': out['v_norm_g'], 'v_ada_w': out['v_ada_w'], 'v_ada_b': out['v_ada_b'], 'v_a_w_in': out['v_a_w_in'], 'v_a_b_f': out['v_a_b_f'], 'v_a_w_out': out['v_a_w_out'], 'v_kv_norm_g': out['v_kv_norm_g'], 'v_kv_w': out['v_kv_w'], 'v_b_w_in': out['v_b_w_in'], 'v_b_sinks': out['v_b_sinks'], 'v_b_w_out': out['v_b_w_out'], 'v_final_norm_g': out['v_final_norm_g']}


def _loss(weights, diff, rest, loss_target):
    with _jax.named_scope("forward"):
        args = {**rest, TWIN_DIFF_INPUT: diff, **{k: w.astype(_WEIGHT_DTYPES[k]) for k, w in weights.items()}}
        y = _forward(args)
    with _jax.named_scope("loss_head"):
        err = _jnp.square(y.astype(_jnp.float32) - loss_target)
        return 0.5 * _jnp.sum(_jnp.mean(err, axis=-1)) if err.ndim else 0.5 * err


def _adamw(w, g, m, v):
    m = ADAM_B1 * m + (1.0 - ADAM_B1) * g
    v = ADAM_B2 * v + (1.0 - ADAM_B2) * _jnp.square(g)
    m_hat = m / (1.0 - ADAM_B1 ** ADAM_STEP)
    v_hat = v / (1.0 - ADAM_B2 ** ADAM_STEP)
    delta = -ADAM_LR * (m_hat / (_jnp.sqrt(v_hat) + ADAM_EPS) + ADAM_WD * w)
    return delta, m, v


def reference(x, c, norm_g, ada_w, ada_b, a_w_in, a_b_f, a_w_out, kv_norm_g, kv_w, b_w_in, b_sinks, b_w_out, final_norm_g, loss_target, m_norm_g, m_ada_w, m_ada_b, m_a_w_in, m_a_b_f, m_a_w_out, m_kv_norm_g, m_kv_w, m_b_w_in, m_b_sinks, m_b_w_out, m_final_norm_g, v_norm_g, v_ada_w, v_ada_b, v_a_w_in, v_a_b_f, v_a_w_out, v_kv_norm_g, v_kv_w, v_b_w_in, v_b_sinks, v_b_w_out, v_final_norm_g):
    given = dict(x=x, c=c, norm_g=norm_g, ada_w=ada_w, ada_b=ada_b, a_w_in=a_w_in, a_b_f=a_b_f, a_w_out=a_w_out, kv_norm_g=kv_norm_g, kv_w=kv_w, b_w_in=b_w_in, b_sinks=b_sinks, b_w_out=b_w_out, final_norm_g=final_norm_g, loss_target=loss_target, m_norm_g=m_norm_g, m_ada_w=m_ada_w, m_ada_b=m_ada_b, m_a_w_in=m_a_w_in, m_a_b_f=m_a_b_f, m_a_w_out=m_a_w_out, m_kv_norm_g=m_kv_norm_g, m_kv_w=m_kv_w, m_b_w_in=m_b_w_in, m_b_sinks=m_b_sinks, m_b_w_out=m_b_w_out, m_final_norm_g=m_final_norm_g, v_norm_g=v_norm_g, v_ada_w=v_ada_w, v_ada_b=v_ada_b, v_a_w_in=v_a_w_in, v_a_b_f=v_a_b_f, v_a_w_out=v_a_w_out, v_kv_norm_g=v_kv_norm_g, v_kv_w=v_kv_w, v_b_w_in=v_b_w_in, v_b_sinks=v_b_sinks, v_b_w_out=v_b_w_out, v_final_norm_g=v_final_norm_g)
    weights = {n: given[n] for n in TWIN_WEIGHTS}
    shared = {n: given[n] for n in SHARED_INPUTS}
    per_example = {n: given[n] for n in ['x', 'c']}
    grad_fn = _jax.value_and_grad(_loss, argnums=(0, 1))

    def one_microbatch(ex, loss_target):
        ex = dict(ex)
        diff = ex.pop(TWIN_DIFF_INPUT)
        return grad_fn(weights, diff, {**shared, **ex}, loss_target)

    if N_MICROBATCH == 1:
        loss, (grad_w, grad_x) = one_microbatch(per_example, given["loss_target"])
    else:
        def body(carry, xs):
            loss_sum, grad_sum = carry
            l_k, (gw_k, gx_k) = one_microbatch(xs[0], xs[1])
            with _jax.named_scope("update"):
                return (loss_sum + l_k, _jax.tree.map(_jnp.add, grad_sum, gw_k)), gx_k

        init = (_jnp.zeros((), _jnp.float32), _jax.tree.map(_jnp.zeros_like, weights))
        (loss, grad_w), grad_x = _jax.lax.scan(body, init, (per_example, given["loss_target"]))
    with _jax.named_scope("update"):
        delta_w, new_m, new_v = {}, {}, {}
        for n in TWIN_WEIGHTS:
            delta_w[n], new_m[n], new_v[n] = _adamw(weights[n], grad_w[n], given["m_" + n], given["v_" + n])
    return (loss, grad_x, *[grad_w[n] for n in TWIN_WEIGHTS], *[delta_w[n] for n in TWIN_WEIGHTS],
            *[new_m[n] for n in TWIN_WEIGHTS], *[new_v[n] for n in TWIN_WEIGHTS])
```

```python
import jax
import jax.numpy as jnp
from jax import lax
from jax.experimental import pallas as pl
from jax.experimental.pallas import tpu as pltpu

F32 = jnp.float32
BF16 = jnp.bfloat16
LANE = 128
SUBLANE = 8
HEAD_DIM = 128
SWA_BLOCK = 128
N_DEV = 8
N_PEER = N_DEV - 1
RMS_EPS = 1e-6
ROPE_THETA = 10000.0
NEG = -1e30
VMEM_LIMIT = 56 * 2 ** 20
MESH = pl.DeviceIdType.MESH
HIGHEST = lax.Precision.HIGHEST

ADAM_LR = 0.001
ADAM_B1 = 0.9
ADAM_B2 = 0.999
ADAM_EPS = 1e-08
ADAM_WD = 0.01
ADAM_STEP = 10


def _tile(dim, pref):
    return pref if dim % pref == 0 else dim


def _params(n_axes):
    return pltpu.CompilerParams(dimension_semantics=("arbitrary",) * n_axes,
                                vmem_limit_bytes=VMEM_LIMIT)


def _dot(a, b):
    return jnp.dot(a, b, preferred_element_type=F32)


def _dot_nt(a, b):
    return lax.dot_general(a, b, (((1,), (1,)), ((), ())), preferred_element_type=F32)


def _sigmoid(z):
    return 1.0 / (1.0 + jnp.exp(-z))


def _iota(shape, dim):
    return lax.broadcasted_iota(jnp.int32, shape, dim)


def _pick_lane(block, lane_index):
    lane = _iota(block.shape, 1)
    return jnp.sum(jnp.where(lane == lane_index, block, 0.0), axis=1, keepdims=True)


def _adamw(w, g, m, v):
    m = ADAM_B1 * m + (1.0 - ADAM_B1) * g
    v = ADAM_B2 * v + (1.0 - ADAM_B2) * (g * g)
    m_hat = m / (1.0 - ADAM_B1 ** ADAM_STEP)
    v_hat = v / (1.0 - ADAM_B2 ** ADAM_STEP)
    delta = -ADAM_LR * (m_hat / (jnp.sqrt(v_hat) + ADAM_EPS) + ADAM_WD * w)
    return delta, m, v


def _mesh_pos():
    return lax.axis_index("x"), lax.axis_index("y"), lax.axis_index("c")


def _slot(pos):
    return 4 * pos[0] + 2 * pos[1] + pos[2]


def _all_gather(name, arrays):
    n = len(arrays)

    def body(*refs):
        ins, outs = refs[:n], refs[n:2 * n]
        send_sems, recv_sems, local_sems = refs[2 * n:]
        x, y, c = _mesh_pos()
        me, sibling = (x, y, c), (x, y, 1 - c)
        chips = [(1 - x, y), (x, 1 - y), (1 - x, 1 - y)]

        def copy(a, k, block, to, src=None):
            dst = outs[a].at[_slot(block)]
            return pltpu.make_async_remote_copy(
                src_ref=dst if src is None else src, dst_ref=dst,
                send_sem=send_sems.at[N_PEER * a + k], recv_sem=recv_sems.at[N_PEER * a + k],
                device_id=to, device_id_type=MESH)

        local, first, passed = [], [], []
        for a in range(n):
            cp = pltpu.make_async_copy(ins[a], outs[a].at[_slot(me)], local_sems.at[a])
            cp.start()
            local.append(cp)
            sends = [copy(a, 0, me, sibling, src=ins[a])]
            sends += [copy(a, 1 + j, me, (*chip, c), src=ins[a]) for j, chip in enumerate(chips)]
            for cp in sends:
                cp.start()
            first += sends
        for a in range(n):
            for j, chip in enumerate(chips):
                copy(a, 1 + j, (*chip, c), me).wait_recv()
                cp = copy(a, 4 + j, (*chip, c), sibling)
                cp.start()
                passed.append(cp)
        for a in range(n):
            copy(a, 0, sibling, me).wait_recv()
            for j, chip in enumerate(chips):
                copy(a, 4 + j, (*chip, 1 - c), me).wait_recv()
        for cp in first + passed:
            cp.wait_send()
        for cp in local:
            cp.wait()

    any_spec = pl.BlockSpec(memory_space=pl.ANY)
    return pl.pallas_call(
        body, name=name,
        out_shape=[jax.ShapeDtypeStruct((N_DEV,) + a.shape, a.dtype) for a in arrays],
        in_specs=[any_spec] * n, out_specs=[any_spec] * n,
        scratch_shapes=[pltpu.SemaphoreType.DMA((N_PEER * n,)),
                        pltpu.SemaphoreType.DMA((N_PEER * n,)),
                        pltpu.SemaphoreType.DMA((n,))],
    )(*arrays)


def _all_to_all(name, arrays):
    n = len(arrays)

    def body(*refs):
        ins, outs = refs[:n], refs[n:2 * n]
        send_sems, recv_sems, local_sems = refs[2 * n:]
        x, y, c = _mesh_pos()
        me = _slot((x, y, c))
        local, sends, recvs = [], [], []
        for a in range(n):
            cp = pltpu.make_async_copy(ins[a].at[me], outs[a].at[me], local_sems.at[a])
            cp.start()
            local.append(cp)
        for k in range(1, N_DEV):
            peer = (1 - x if k & 4 else x, 1 - y if k & 2 else y, 1 - c if k & 1 else c)
            ps = _slot(peer)
            for a in range(n):
                sem = N_PEER * a + k - 1
                cp = pltpu.make_async_remote_copy(
                    src_ref=ins[a].at[ps], dst_ref=outs[a].at[me],
                    send_sem=send_sems.at[sem], recv_sem=recv_sems.at[sem],
                    device_id=peer, device_id_type=MESH)
                cp.start()
                sends.append(cp)
                recvs.append(pltpu.make_async_remote_copy(
                    src_ref=ins[a].at[ps], dst_ref=outs[a].at[ps],
                    send_sem=send_sems.at[sem], recv_sem=recv_sems.at[sem],
                    device_id=peer, device_id_type=MESH))
        for cp in recvs:
            cp.wait_recv()
        for cp in sends:
            cp.wait_send()
        for cp in local:
            cp.wait()

    any_spec = pl.BlockSpec(memory_space=pl.ANY)
    return pl.pallas_call(
        body, name=name,
        out_shape=[jax.ShapeDtypeStruct(a.shape, a.dtype) for a in arrays],
        in_specs=[any_spec] * n, out_specs=[any_spec] * n,
        scratch_shapes=[pltpu.SemaphoreType.DMA((N_PEER * n,)),
                        pltpu.SemaphoreType.DMA((N_PEER * n,)),
                        pltpu.SemaphoreType.DMA((n,))],
    )(*arrays)


def _matmul(name, a, b, *, nt, tm, tn, tk, n_cols, out_shape, out_specs, epilogue,
            b_spec=None, extra=(), extra_specs=()):
    m_rows, k_dim = a.shape
    tm, tn, tk = _tile(m_rows, tm), _tile(n_cols, tn), _tile(k_dim, tk)
    grid = (m_rows // tm, n_cols // tn, k_dim // tk)
    n_k = grid[2]
    a_spec = pl.BlockSpec((tm, tk), lambda i, j, k: (i, k))
    if b_spec is None:
        b_spec = (pl.BlockSpec((tn, tk), lambda i, j, k: (j, k)) if nt
                  else pl.BlockSpec((tk, tn), lambda i, j, k: (k, j)))
    n_extra, n_out = len(extra), len(out_shape)

    def body(a_ref, b_ref, *rest):
        extra_refs = rest[:n_extra]
        out_refs = rest[n_extra:n_extra + n_out]
        acc_ref = rest[n_extra + n_out]
        k = pl.program_id(2)

        @pl.when(k == 0)
        def _():
            acc_ref[...] = jnp.zeros_like(acc_ref)

        if nt:
            acc_ref[...] += _dot_nt(a_ref[...], b_ref[...])
        else:
            acc_ref[...] += _dot(a_ref[...], b_ref[...])

        @pl.when(k == n_k - 1)
        def _():
            epilogue(acc_ref, extra_refs, out_refs)

    return pl.pallas_call(
        body, name=name, grid=grid,
        in_specs=[a_spec, b_spec, *extra_specs], out_specs=out_specs, out_shape=out_shape,
        scratch_shapes=[pltpu.VMEM((tm, tn), F32)],
        compiler_params=_params(3),
    )(a, b, *extra), (tm, tn, tk)


def _mm_plain(name, a, b, *, nt=False, n_cols=None, out_dtype=BF16, init=None,
              tm=1024, tn=1024, tk=512, b_spec=None, out_3d=None):
    m_rows = a.shape[0]
    if n_cols is None:
        n_cols = b.shape[0] if nt else b.shape[1]
    tm, tn = _tile(m_rows, tm), _tile(n_cols, tn)
    if out_3d is None:
        shape = jax.ShapeDtypeStruct((m_rows, n_cols), out_dtype)
        spec = pl.BlockSpec((tm, tn), lambda i, j, k: (i, j))
    else:
        slabs, width = out_3d
        assert tn == width and slabs * width == n_cols
        shape = jax.ShapeDtypeStruct((slabs, m_rows, width), out_dtype)
        spec = pl.BlockSpec((None, tm, width), lambda i, j, k: (j, i, 0))
    extra, extra_specs = (), ()
    if init is not None:
        extra = (init,)
        extra_specs = (pl.BlockSpec((tm, tn), lambda i, j, k: (i, j)),)

    def epilogue(acc_ref, extra_refs, out_refs):
        acc = acc_ref[...]
        if init is not None:
            acc = acc + extra_refs[0][...]
        out_refs[0][...] = acc.astype(out_dtype)

    (out,), _ = _matmul(name, a, b, nt=nt, tm=tm, tn=tn, tk=tk, n_cols=n_cols,
                        out_shape=[shape], out_specs=[spec], epilogue=epilogue,
                        b_spec=b_spec, extra=extra, extra_specs=extra_specs)
    return out


def _mm_rope(name, a, b, cos, sin, *, n_cols, rope_cols, tn, b_spec=None):
    m_rows = a.shape[0]
    tm = _tile(m_rows, 1024)
    tn = _tile(n_cols, tn)
    assert rope_cols % tn == 0 and tn % HEAD_DIM == 0
    rope_blocks = rope_cols // tn
    table_spec = pl.BlockSpec((tm, LANE), lambda i, j, k: (i, 0))

    def epilogue(acc_ref, extra_refs, out_refs):
        cos_ref, sin_ref = extra_refs
        j = pl.program_id(1)

        @pl.when(j < rope_blocks)
        def _():
            for head in range(tn // HEAD_DIM):
                cols = slice(head * HEAD_DIM, (head + 1) * HEAD_DIM)
                blk = acc_ref[:, cols]
                rot = pltpu.roll(blk, HEAD_DIM // 2, 1)
                out_refs[0][:, cols] = (blk * cos_ref[...] + rot * sin_ref[...]).astype(BF16)

        @pl.when(j >= rope_blocks)
        def _():
            out_refs[0][...] = acc_ref[...].astype(BF16)

    (out,), _ = _matmul(name, a, b, nt=False, tm=tm, tn=tn, tk=512, n_cols=n_cols,
                        out_shape=[jax.ShapeDtypeStruct((m_rows, n_cols), BF16)],
                        out_specs=[pl.BlockSpec((tm, tn), lambda i, j, k: (i, j))],
                        epilogue=epilogue, b_spec=b_spec,
                        extra=(cos, sin), extra_specs=(table_spec, table_spec))
    return out


def _mm_residual(name, u, w, x_in, gate):
    m_rows, n_cols = x_in.shape
    tm, tn = _tile(m_rows, 1024), _tile(n_cols, 1024)
    blk = pl.BlockSpec((tm, tn), lambda i, j, k: (i, j))

    def epilogue(acc_ref, extra_refs, out_refs):
        x_ref, gate_ref = extra_refs
        y = acc_ref[...]
        out_refs[0][...] = y
        out_refs[1][...] = x_ref[...] + gate_ref[...] * y

    (y, x_out), _ = _matmul(
        name, u, w, nt=False, tm=tm, tn=tn, tk=512, n_cols=n_cols,
        out_shape=[jax.ShapeDtypeStruct((m_rows, n_cols), F32)] * 2, out_specs=[blk, blk],
        epilogue=epilogue, extra=(x_in, gate),
        extra_specs=(blk, pl.BlockSpec((1, tn), lambda i, j, k: (0, j))))
    return y, x_out


def _mm_gate_bwd(name, dy, w_out, z_src, z_col0, o):
    m_rows = dy.shape[0]
    n_cols = w_out.shape[0]
    tm, tn = _tile(m_rows, 1024), _tile(n_cols, 1024)
    assert z_col0 % tn == 0 and tn % HEAD_DIM == 0 and n_cols // HEAD_DIM <= LANE
    z_blk0 = z_col0 // tn
    blk = pl.BlockSpec((tm, tn), lambda i, j, k: (i, j))

    def epilogue(acc_ref, extra_refs, out_refs):
        z_ref, o_ref = extra_refs
        do_ref, dz_ref, delta_ref = out_refs
        j = pl.program_id(1)
        du = acc_ref[...]
        z = z_ref[...].astype(F32)
        o_val = o_ref[...].astype(F32)
        sig = _sigmoid(z)
        d_o = (du * (z * sig)).astype(BF16)
        do_ref[...] = d_o
        dz_ref[...] = (du * o_val * (sig * (1.0 + z * (1.0 - sig)))).astype(BF16)

        @pl.when(j == 0)
        def _():
            delta_ref[...] = jnp.zeros_like(delta_ref)

        prod = d_o.astype(F32) * o_val
        lane = _iota((tm, LANE), 1)
        delta = delta_ref[...]
        for head in range(tn // HEAD_DIM):
            rows = jnp.sum(prod[:, head * HEAD_DIM:(head + 1) * HEAD_DIM], axis=1, keepdims=True)
            delta = delta + jnp.where(lane == j * (tn // HEAD_DIM) + head, rows, 0.0)
        delta_ref[...] = delta

    (d_o, dz, delta), _ = _matmul(
        name, dy, w_out, nt=True, tm=tm, tn=tn, tk=512, n_cols=n_cols,
        out_shape=[jax.ShapeDtypeStruct((m_rows, n_cols), BF16)] * 2
        + [jax.ShapeDtypeStruct((m_rows, LANE), F32)],
        out_specs=[blk, blk, pl.BlockSpec((tm, LANE), lambda i, j, k: (i, 0))],
        epilogue=epilogue, extra=(z_src, o),
        extra_specs=(pl.BlockSpec((tm, tn), lambda i, j, k: (i, z_blk0 + j)), blk))
    return d_o, dz, delta


def _norm_fwd(name, x, ga, sa, ta, gb=None):
    s_len, d = x.shape
    tr = _tile(s_len, 256)
    two = gb is not None
    row = pl.BlockSpec((tr, d), lambda i: (i, 0))
    vec = pl.BlockSpec((1, d), lambda i: (0, 0))

    def body(x_ref, ga_ref, sa_ref, ta_ref, *rest):
        xv = x_ref[...]
        y = xv * lax.rsqrt(jnp.mean(xv * xv, axis=-1, keepdims=True) + RMS_EPS)
        rest[-2 if two else -1][...] = ((y * ga_ref[...]) * (1.0 + sa_ref[...]) + ta_ref[...]).astype(BF16)
        if two:
            rest[-1][...] = (y * rest[0][...]).astype(BF16)

    ins = [x, ga, sa, ta] + ([gb] if two else [])
    outs = pl.pallas_call(
        body, name=name, grid=(s_len // tr,),
        in_specs=[row] + [vec] * (len(ins) - 1),
        out_specs=[row] * (2 if two else 1),
        out_shape=[jax.ShapeDtypeStruct((s_len, d), BF16)] * (2 if two else 1),
        compiler_params=_params(1))(*ins)
    return outs if two else outs[0]


def _loss_bwd(x2, target, y1, g_final, gate1):
    s_len, d = x2.shape
    tr = _tile(s_len, 128)
    row = pl.BlockSpec((tr, d), lambda i: (i, 0))
    vec = pl.BlockSpec((1, d), lambda i: (0, 0))

    def body(x_ref, t_ref, y_ref, g_ref, gate_ref, loss_ref, dx_ref, dy_ref, sums_ref):
        @pl.when(pl.program_id(0) == 0)
        def _():
            loss_ref[...] = jnp.zeros_like(loss_ref)
            sums_ref[...] = jnp.zeros_like(sums_ref)

        xv = x_ref[...]
        rstd = lax.rsqrt(jnp.mean(xv * xv, axis=-1, keepdims=True) + RMS_EPS)
        xhat = xv * rstd
        g = g_ref[...]
        err = xhat * g - t_ref[...]
        sq = jnp.sum(jnp.sum(err * err, axis=1, keepdims=True), axis=0, keepdims=True)
        loss_ref[...] += sq * (0.5 / d)
        dout = err * (1.0 / d)
        dxhat = dout * g
        dx = rstd * (dxhat - xhat * jnp.mean(dxhat * xhat, axis=-1, keepdims=True))
        dx_ref[...] = dx
        dy_ref[...] = (dx * gate_ref[...]).astype(BF16)
        sums_ref[0:1, :] += jnp.sum(dout * xhat, axis=0, keepdims=True)
        sums_ref[1:2, :] += jnp.sum(dx * y_ref[...], axis=0, keepdims=True)

    return pl.pallas_call(
        body, name="loss_bwd", grid=(s_len // tr,),
        in_specs=[row, row, row, vec, vec],
        out_specs=[pl.BlockSpec((SUBLANE, LANE), lambda i: (0, 0)), row, row,
                   pl.BlockSpec((SUBLANE, d), lambda i: (0, 0))],
        out_shape=[jax.ShapeDtypeStruct((SUBLANE, LANE), F32),
                   jax.ShapeDtypeStruct((s_len, d), F32),
                   jax.ShapeDtypeStruct((s_len, d), BF16),
                   jax.ShapeDtypeStruct((SUBLANE, d), F32)],
        compiler_params=_params(1))(x2, target, y1, g_final, gate1)


def _norm_bwd(name, x, dres, dha, ga, sa, dhb=None, gb=None, y=None, gate=None):
    s_len, d = x.shape
    tr = _tile(s_len, 128)
    has_b, has_y = dhb is not None, y is not None
    row = pl.BlockSpec((tr, d), lambda i: (i, 0))
    vec = pl.BlockSpec((1, d), lambda i: (0, 0))
    ins, specs = [x, dres, dha, ga, sa], [row, row, row, vec, vec]
    if has_b:
        ins += [dhb, gb]
        specs += [row, vec]
    if has_y:
        ins += [y, gate]
        specs += [row, vec]
    n_in = len(ins)

    def body(*refs):
        x_ref, dres_ref, dha_ref, ga_ref, sa_ref = refs[:5]
        pos = 5
        if has_b:
            dhb_ref, gb_ref = refs[pos:pos + 2]
            pos += 2
        if has_y:
            y_ref, gate_ref = refs[pos:pos + 2]
        outs = refs[n_in:]
        dx_ref, sums_ref = outs[0], outs[-1]

        @pl.when(pl.program_id(0) == 0)
        def _():
            sums_ref[...] = jnp.zeros_like(sums_ref)

        xv = x_ref[...]
        rstd = lax.rsqrt(jnp.mean(xv * xv, axis=-1, keepdims=True) + RMS_EPS)
        xhat = xv * rstd
        dha_v = dha_ref[...]
        ga_v, sa_v = ga_ref[...], sa_ref[...]
        dxhat = dha_v * (ga_v * (1.0 + sa_v))
        sums_ref[0:1, :] += jnp.sum(dha_v, axis=0, keepdims=True)
        sums_ref[1:2, :] += jnp.sum(dha_v * (xhat * ga_v), axis=0, keepdims=True)
        sums_ref[2:3, :] += jnp.sum(dha_v * ((1.0 + sa_v) * xhat), axis=0, keepdims=True)
        if has_b:
            dhb_v = dhb_ref[...]
            dxhat = dxhat + dhb_v * gb_ref[...]
            sums_ref[3:4, :] += jnp.sum(dhb_v * xhat, axis=0, keepdims=True)
        dx = dres_ref[...] + rstd * (dxhat - xhat * jnp.mean(dxhat * xhat, axis=-1, keepdims=True))
        dx_ref[...] = dx
        if has_y:
            outs[1][...] = (dx * gate_ref[...]).astype(BF16)
            sums_ref[4:5, :] += jnp.sum(dx * y_ref[...], axis=0, keepdims=True)

    out_shape = [jax.ShapeDtypeStruct((s_len, d), F32)]
    out_specs = [row]
    if has_y:
        out_shape.append(jax.ShapeDtypeStruct((s_len, d), BF16))
        out_specs.append(row)
    out_shape.append(jax.ShapeDtypeStruct((SUBLANE, d), F32))
    out_specs.append(pl.BlockSpec((SUBLANE, d), lambda i: (0, 0)))
    return pl.pallas_call(body, name=name, grid=(s_len // tr,), in_specs=specs,
                          out_specs=out_specs, out_shape=out_shape,
                          compiler_params=_params(1))(*ins)


def _fgate_fwd(f_raw, bias_row):
    s_len = f_raw.shape[0]
    tb = _tile(s_len, 512)
    blk = pl.BlockSpec((tb, LANE), lambda t: (t, 0))

    def body(f_ref, b_ref, out_ref, carry):
        @pl.when(pl.program_id(0) == 0)
        def _():
            carry[...] = jnp.zeros_like(carry)

        u = f_ref[...] + b_ref[...]
        logf = jnp.minimum(u, 0.0) - jnp.log1p(jnp.exp(-jnp.abs(u)))
        tri = (_iota((tb, tb), 1) <= _iota((tb, tb), 0)).astype(F32)
        run = jnp.dot(tri, logf, precision=HIGHEST, preferred_element_type=F32) + carry[...]
        out_ref[...] = run
        carry[...] = run[tb - 1:tb, :]

    return pl.pallas_call(
        body, name="fgate_fwd", grid=(s_len // tb,),
        in_specs=[blk, pl.BlockSpec((1, LANE), lambda t: (0, 0))], out_specs=blk,
        out_shape=jax.ShapeDtypeStruct((s_len, LANE), F32),
        scratch_shapes=[pltpu.VMEM((1, LANE), F32)],
        compiler_params=_params(1))(f_raw, bias_row)


def _fgate_bwd(df_a, df_b, f_raw, bias_row):
    s_len = f_raw.shape[0]
    tb = _tile(s_len, 512)
    nb = s_len // tb
    blk = pl.BlockSpec((tb, LANE), lambda t: (nb - 1 - t, 0))

    def body(a_ref, b2_ref, f_ref, b_ref, df_ref, sums_ref, carry):
        @pl.when(pl.program_id(0) == 0)
        def _():
            carry[...] = jnp.zeros_like(carry)
            sums_ref[...] = jnp.zeros_like(sums_ref)

        d_run = a_ref[...] + b2_ref[...]
        tri = (_iota((tb, tb), 1) >= _iota((tb, tb), 0)).astype(F32)
        dlogf = jnp.dot(tri, d_run, precision=HIGHEST, preferred_element_type=F32) + carry[...]
        carry[...] = dlogf[0:1, :]
        u = f_ref[...] + b_ref[...]
        df = dlogf * _sigmoid(-u)
        df_ref[...] = df.astype(BF16)
        sums_ref[...] += jnp.sum(df, axis=0, keepdims=True)

    return pl.pallas_call(
        body, name="fgate_bwd", grid=(nb,),
        in_specs=[blk, blk, blk, pl.BlockSpec((1, LANE), lambda t: (0, 0))],
        out_specs=[blk, pl.BlockSpec((SUBLANE, LANE), lambda t: (0, 0))],
        out_shape=[jax.ShapeDtypeStruct((s_len, LANE), BF16),
                   jax.ShapeDtypeStruct((SUBLANE, LANE), F32)],
        scratch_shapes=[pltpu.VMEM((1, LANE), F32)],
        compiler_params=_params(1))(df_a, df_b, f_raw, bias_row)


def _fox_fwd(proj, f_nat, f_t, heads):
    s_len = proj.shape[0]
    d = heads * HEAD_DIM
    t = _tile(s_len, 512)
    nq = s_len // t
    scale = HEAD_DIM ** -0.5

    def body(q_ref, k_ref, v_ref, z_ref, fn_ref, ft_ref, o_ref, u_ref, lse_ref,
             m_scr, l_scr, acc_scr, fq_scr):
        i, h, j = pl.program_id(0), pl.program_id(1), pl.program_id(2)

        @pl.when(j == 0)
        def _():
            m_scr[...] = jnp.full_like(m_scr, NEG)
            l_scr[...] = jnp.zeros_like(l_scr)
            acc_scr[...] = jnp.zeros_like(acc_scr)
            fq_scr[...] = _pick_lane(fn_ref[...], h)

        @pl.when((j == 0) & (h == 0))
        def _():
            lse_ref[...] = jnp.zeros_like(lse_ref)

        @pl.when(j <= i)
        def _():
            s = _dot_nt(q_ref[...], k_ref[...]) * scale + (fq_scr[...] - ft_ref[...])
            causal = (_iota((t, t), 1) + j * t) <= (_iota((t, t), 0) + i * t)
            s = jnp.where(causal, s, NEG)
            m_prev = m_scr[...]
            m_new = jnp.maximum(m_prev, jnp.max(s, axis=1, keepdims=True))
            alpha = jnp.exp(m_prev - m_new)
            p = jnp.exp(s - m_new)
            l_scr[...] = alpha * l_scr[...] + jnp.sum(p, axis=1, keepdims=True)
            acc_scr[...] = alpha * acc_scr[...] + _dot(p.astype(BF16), v_ref[...])
            m_scr[...] = m_new

        @pl.when(j == i)
        def _():
            l_sum = l_scr[...]
            o_val = acc_scr[...] / l_sum
            o_ref[...] = o_val.astype(BF16)
            z = z_ref[...].astype(F32)
            u_ref[...] = (o_val * (z * _sigmoid(z))).astype(BF16)
            lse = m_scr[...] + jnp.log(l_sum)
            lse_ref[...] += jnp.where(_iota((t, LANE), 1) == h, lse, 0.0)

    col = lambda base: pl.BlockSpec((t, HEAD_DIM), lambda i, h, j: (i, base + h))
    kv = lambda base: pl.BlockSpec((t, HEAD_DIM), lambda i, h, j: (jnp.minimum(j, i), base + h))
    nat = pl.BlockSpec((t, LANE), lambda i, h, j: (i, 0))
    out_blk = pl.BlockSpec((t, HEAD_DIM), lambda i, h, j: (i, h))
    return pl.pallas_call(
        body, name="fox_fwd", grid=(nq, heads, nq),
        in_specs=[col(0), kv(heads), kv(2 * heads), col(3 * heads), nat,
                  pl.BlockSpec((None, 1, t), lambda i, h, j: (h, 0, jnp.minimum(j, i)))],
        out_specs=[out_blk, out_blk, nat],
        out_shape=[jax.ShapeDtypeStruct((s_len, d), BF16), jax.ShapeDtypeStruct((s_len, d), BF16),
                   jax.ShapeDtypeStruct((s_len, LANE), F32)],
        scratch_shapes=[pltpu.VMEM((t, 1), F32), pltpu.VMEM((t, 1), F32),
                        pltpu.VMEM((t, HEAD_DIM), F32), pltpu.VMEM((t, 1), F32)],
        compiler_params=_params(3))(proj, proj, proj, proj, f_nat, f_t)


def _fox_bwd(proj, k_t, d_o, f_nat, f_t, lse_t, delta_t, heads):
    s_len = proj.shape[0]
    d = heads * HEAD_DIM
    t = _tile(s_len, 512)
    nq = s_len // t
    scale = HEAD_DIM ** -0.5

    def body(k_ref, v_ref, kt_ref, q_ref, do_ref, fn_ref, ft_ref, lse_ref, delta_ref,
             dk_ref, dv_ref, dqt_ref, dfq_ref, dfk_ref,
             dk_acc, dv_acc, dq_acc, dfq_acc, dfk_acc, fk_scr):
        h, j, i = pl.program_id(0), pl.program_id(1), pl.program_id(2)
        head_start = (j == 0) & (i == 0)

        @pl.when(head_start)
        def _():
            dq_acc[...] = jnp.zeros_like(dq_acc)
            dfq_acc[...] = jnp.zeros_like(dfq_acc)

        @pl.when(head_start & (h == 0))
        def _():
            dfk_ref[...] = jnp.zeros_like(dfk_ref)

        @pl.when(i == j)
        def _():
            dk_acc[...] = jnp.zeros_like(dk_acc)
            dv_acc[...] = jnp.zeros_like(dv_acc)
            dfk_acc[...] = jnp.zeros_like(dfk_acc)
            fk_scr[...] = _pick_lane(fn_ref[...], h)

        @pl.when(i >= j)
        def _():
            q = q_ref[...]
            d_out = do_ref[...]
            s_t = _dot_nt(k_ref[...], q) * scale + (ft_ref[...] - fk_scr[...])
            causal = (_iota((t, t), 0) + j * t) <= (_iota((t, t), 1) + i * t)
            p_t = jnp.exp(jnp.where(causal, s_t - lse_ref[...], NEG))
            dp_t = _dot_nt(v_ref[...], d_out)
            ds_t = p_t * (dp_t - delta_ref[...])
            ds_b = ds_t.astype(BF16)
            dv_acc[...] += _dot(p_t.astype(BF16), d_out)
            dk_acc[...] += _dot(ds_b, q)
            dq_acc[i] += _dot(kt_ref[...], ds_b)
            dfq_acc[i] += jnp.sum(ds_t, axis=0, keepdims=True)
            dfk_acc[...] += jnp.sum(ds_t, axis=1, keepdims=True)

        @pl.when(i == nq - 1)
        def _():
            dk_ref[...] = (dk_acc[...] * scale).astype(BF16)
            dv_ref[...] = dv_acc[...].astype(BF16)
            rows = pl.ds(pl.multiple_of(j * t, t), t)
            dfk_ref[rows, :] += jnp.where(_iota((t, LANE), 1) == h, -dfk_acc[...], 0.0)

        @pl.when((i == nq - 1) & (j == nq - 1))
        def _():
            for blk in range(nq):
                cols = slice(blk * t, (blk + 1) * t)
                dqt_ref[:, cols] = (dq_acc[blk] * scale).astype(BF16)
                dfq_ref[:, cols] = dfq_acc[blk]

    key_col = lambda base: pl.BlockSpec((t, HEAD_DIM), lambda h, j, i: (j, base + h))
    qry = pl.BlockSpec((t, HEAD_DIM), lambda h, j, i: (jnp.maximum(i, j), h))
    qry_row = pl.BlockSpec((None, 1, t), lambda h, j, i: (h, 0, jnp.maximum(i, j)))
    kv_out = pl.BlockSpec((t, HEAD_DIM), lambda h, j, i: (j, h))
    return pl.pallas_call(
        body, name="fox_bwd", grid=(heads, nq, nq),
        in_specs=[key_col(heads), key_col(2 * heads),
                  pl.BlockSpec((HEAD_DIM, t), lambda h, j, i: (h, j)),
                  qry, qry, pl.BlockSpec((t, LANE), lambda h, j, i: (j, 0)),
                  qry_row, qry_row, qry_row],
        out_specs=[kv_out, kv_out,
                   pl.BlockSpec((HEAD_DIM, s_len), lambda h, j, i: (h, 0)),
                   pl.BlockSpec((None, 1, s_len), lambda h, j, i: (h, 0, 0)),
                   pl.BlockSpec((s_len, LANE), lambda h, j, i: (0, 0))],
        out_shape=[jax.ShapeDtypeStruct((s_len, d), BF16), jax.ShapeDtypeStruct((s_len, d), BF16),
                   jax.ShapeDtypeStruct((d, s_len), BF16),
                   jax.ShapeDtypeStruct((heads, 1, s_len), F32),
                   jax.ShapeDtypeStruct((s_len, LANE), F32)],
        scratch_shapes=[pltpu.VMEM((t, HEAD_DIM), F32), pltpu.VMEM((t, HEAD_DIM), F32),
                        pltpu.VMEM((nq, HEAD_DIM, t), F32), pltpu.VMEM((nq, 1, t), F32),
                        pltpu.VMEM((t, 1), F32), pltpu.VMEM((t, 1), F32)],
        compiler_params=_params(3))(proj, proj, k_t, proj, d_o, f_nat, f_t, lse_t, delta_t)


def _swa_specs(heads, kv_heads):
    group = heads // kv_heads
    blk = lambda fn: pl.BlockSpec((SWA_BLOCK, HEAD_DIM), fn)
    prev = lambda base: blk(lambda n, h: (jnp.maximum(n - 1, 0), base + h // group))
    cur = lambda base: blk(lambda n, h: (n, base + h // group))
    return blk, prev, cur


def _swa_scores(q, kp, kc, n, scale):
    r, c = _iota((SWA_BLOCK, SWA_BLOCK), 0), _iota((SWA_BLOCK, SWA_BLOCK), 1)
    sp = jnp.where((c > r) & (n > 0), _dot_nt(q, kp) * scale, NEG)
    sc = jnp.where(c <= r, _dot_nt(q, kc) * scale, NEG)
    return sp, sc


def _swa_fwd(proj, kv, sinks_row, heads, kv_heads):
    s_len = proj.shape[0]
    d = heads * HEAD_DIM
    scale = HEAD_DIM ** -0.5
    blk, prev, cur = _swa_specs(heads, kv_heads)

    def body(q_ref, z_ref, kp_ref, kc_ref, vp_ref, vc_ref, sink_ref, o_ref, u_ref, lse_ref):
        n, h = pl.program_id(0), pl.program_id(1)
        sp, sc = _swa_scores(q_ref[...], kp_ref[...], kc_ref[...], n, scale)
        sink = _pick_lane(sink_ref[...], h)
        m = jnp.maximum(jnp.maximum(jnp.max(sp, axis=1, keepdims=True),
                                    jnp.max(sc, axis=1, keepdims=True)), sink)
        pp, pc = jnp.exp(sp - m), jnp.exp(sc - m)
        den = (jnp.sum(pp, axis=1, keepdims=True) + jnp.sum(pc, axis=1, keepdims=True)
               + jnp.exp(sink - m))
        o_val = (_dot(pp.astype(BF16), vp_ref[...]) + _dot(pc.astype(BF16), vc_ref[...])) / den
        o_ref[...] = o_val.astype(BF16)
        z = z_ref[...].astype(F32)
        u_ref[...] = (o_val * (z * _sigmoid(z))).astype(BF16)

        @pl.when(h == 0)
        def _():
            lse_ref[...] = jnp.zeros_like(lse_ref)

        lse_ref[...] += jnp.where(_iota((SWA_BLOCK, LANE), 1) == h, m + jnp.log(den), 0.0)

    out_blk = blk(lambda n, h: (n, h))
    nat = pl.BlockSpec((SWA_BLOCK, LANE), lambda n, h: (n, 0))
    return pl.pallas_call(
        body, name="swa_fwd", grid=(s_len // SWA_BLOCK, heads),
        in_specs=[blk(lambda n, h: (n, h)), blk(lambda n, h: (n, heads + h)),
                  prev(0), cur(0), prev(kv_heads), cur(kv_heads),
                  pl.BlockSpec((1, LANE), lambda n, h: (0, 0))],
        out_specs=[out_blk, out_blk, nat],
        out_shape=[jax.ShapeDtypeStruct((s_len, d), BF16), jax.ShapeDtypeStruct((s_len, d), BF16),
                   jax.ShapeDtypeStruct((s_len, LANE), F32)],
        compiler_params=_params(2))(proj, proj, kv, kv, kv, kv, sinks_row)


def _swa_bwd_q(proj, kv, d_o, lse, delta, sinks_row, cos, sin, heads, kv_heads):
    s_len = proj.shape[0]
    d = heads * HEAD_DIM
    scale = HEAD_DIM ** -0.5
    blk, prev, cur = _swa_specs(heads, kv_heads)

    def body(q_ref, kp_ref, kc_ref, vp_ref, vc_ref, do_ref, lse_ref, delta_ref, sink_ref,
             cos_ref, sin_ref, dq_ref, dsink_ref):
        n, h = pl.program_id(0), pl.program_id(1)

        @pl.when((n == 0) & (h == 0))
        def _():
            dsink_ref[...] = jnp.zeros_like(dsink_ref)

        kp, kc = kp_ref[...], kc_ref[...]
        sp, sc = _swa_scores(q_ref[...], kp, kc, n, scale)
        lse_col = _pick_lane(lse_ref[...], h)
        delta_col = _pick_lane(delta_ref[...], h)
        pp, pc = jnp.exp(sp - lse_col), jnp.exp(sc - lse_col)
        p_sink = jnp.exp(_pick_lane(sink_ref[...], h) - lse_col)
        d_out = do_ref[...]
        dsp = pp * (_dot_nt(d_out, vp_ref[...]) - delta_col)
        dsc = pc * (_dot_nt(d_out, vc_ref[...]) - delta_col)
        dq = (_dot(dsp.astype(BF16), kp) + _dot(dsc.astype(BF16), kc)) * scale
        dq_ref[...] = (dq * cos_ref[...]
                       - pltpu.roll(dq, HEAD_DIM // 2, 1) * sin_ref[...]).astype(BF16)
        d_sink = jnp.sum(-p_sink * delta_col, axis=0, keepdims=True)
        dsink_ref[...] += jnp.where(_iota((SUBLANE, LANE), 1) == h, d_sink, 0.0)

    own = blk(lambda n, h: (n, h))
    nat = pl.BlockSpec((SWA_BLOCK, LANE), lambda n, h: (n, 0))
    return pl.pallas_call(
        body, name="swa_bwd_q", grid=(s_len // SWA_BLOCK, heads),
        in_specs=[own, prev(0), cur(0), prev(kv_heads), cur(kv_heads), own, nat, nat,
                  pl.BlockSpec((1, LANE), lambda n, h: (0, 0)), nat, nat],
        out_specs=[own, pl.BlockSpec((SUBLANE, LANE), lambda n, h: (0, 0))],
        out_shape=[jax.ShapeDtypeStruct((s_len, d), BF16),
                   jax.ShapeDtypeStruct((SUBLANE, LANE), F32)],
        compiler_params=_params(2))(proj, kv, kv, kv, kv, d_o, lse, delta, sinks_row, cos, sin)


def _swa_bwd_kv(proj, kv, d_o, lse_t, delta_t, cos, sin, heads, kv_heads):
    s_len = proj.shape[0]
    nb = s_len // SWA_BLOCK
    group = heads // kv_heads
    scale = HEAD_DIM ** -0.5

    def body(k_ref, v_ref, qm_ref, qn_ref, dom_ref, don_ref, lsem_ref, lsen_ref,
             deltam_ref, deltan_ref, cos_ref, sin_ref, dk_ref, dv_ref, dk_acc, dv_acc):
        m, hh = pl.program_id(1), pl.program_id(2)

        @pl.when(hh == 0)
        def _():
            dk_acc[...] = jnp.zeros_like(dk_acc)
            dv_acc[...] = jnp.zeros_like(dv_acc)

        k, v = k_ref[...], v_ref[...]
        key, qry = _iota((SWA_BLOCK, SWA_BLOCK), 0), _iota((SWA_BLOCK, SWA_BLOCK), 1)

        def accumulate(q, d_out, lse_row, delta_row, valid):
            s_t = _dot_nt(k, q) * scale
            p_t = jnp.exp(jnp.where(valid, s_t - lse_row, NEG))
            ds_t = p_t * (_dot_nt(v, d_out) - delta_row)
            dv_acc[...] += _dot(p_t.astype(BF16), d_out)
            dk_acc[...] += _dot(ds_t.astype(BF16), q)

        accumulate(qm_ref[...], dom_ref[...], lsem_ref[...], deltam_ref[...], key <= qry)

        @pl.when(m + 1 < nb)
        def _():
            accumulate(qn_ref[...], don_ref[...], lsen_ref[...], deltan_ref[...], key > qry)

        @pl.when(hh == group - 1)
        def _():
            dk = dk_acc[...] * scale
            dk_ref[...] = (dk * cos_ref[...]
                           - pltpu.roll(dk, HEAD_DIM // 2, 1) * sin_ref[...]).astype(BF16)
            dv_ref[...] = dv_acc[...].astype(BF16)

    blk = lambda fn: pl.BlockSpec((SWA_BLOCK, HEAD_DIM), fn)
    nxt = lambda m: jnp.minimum(m + 1, nb - 1)
    row = lambda fn: pl.BlockSpec((None, 1, SWA_BLOCK), fn)
    q_m = blk(lambda g, m, hh: (m, g * group + hh))
    q_n = blk(lambda g, m, hh: (nxt(m), g * group + hh))
    r_m = row(lambda g, m, hh: (g * group + hh, 0, m))
    r_n = row(lambda g, m, hh: (g * group + hh, 0, nxt(m)))
    nat = pl.BlockSpec((SWA_BLOCK, LANE), lambda g, m, hh: (m, 0))
    out_blk = blk(lambda g, m, hh: (m, g))
    width = kv_heads * HEAD_DIM
    return pl.pallas_call(
        body, name="swa_bwd_kv", grid=(kv_heads, nb, group),
        in_specs=[blk(lambda g, m, hh: (m, g)), blk(lambda g, m, hh: (m, kv_heads + g)),
                  q_m, q_n, q_m, q_n, r_m, r_n, r_m, r_n, nat, nat],
        out_specs=[out_blk, out_blk],
        out_shape=[jax.ShapeDtypeStruct((s_len, width), BF16)] * 2,
        scratch_shapes=[pltpu.VMEM((SWA_BLOCK, HEAD_DIM), F32)] * 2,
        compiler_params=_params(3))(kv, kv, proj, proj, d_o, d_o, lse_t, lse_t,
                                    delta_t, delta_t, cos, sin)


def _ada_fwd(c_rows, ada_w, bias_loc):
    n_layers, d, cols = ada_w.shape
    rows = c_rows.shape[0]
    tk = _tile(d, 512)
    n_k = d // tk

    def body(c_ref, w_ref, b_ref, mod_ref, sc_ref, acc_ref):
        k = pl.program_id(1)

        @pl.when(k == 0)
        def _():
            acc_ref[...] = jnp.zeros_like(acc_ref)

        cv = c_ref[...]
        sc = cv * _sigmoid(cv)
        sc_ref[...] = sc
        acc_ref[...] += _dot(sc.astype(BF16), w_ref[...].astype(BF16))

        @pl.when(k == n_k - 1)
        def _():
            mod_ref[...] = acc_ref[...] + b_ref[...]

    return pl.pallas_call(
        body, name="ada_fwd", grid=(n_layers, n_k),
        in_specs=[pl.BlockSpec((rows, tk), lambda l, k: (0, k)),
                  pl.BlockSpec((None, tk, cols), lambda l, k: (l, k, 0)),
                  pl.BlockSpec((None, 1, cols), lambda l, k: (l, 0, 0))],
        out_specs=[pl.BlockSpec((None, rows, cols), lambda l, k: (l, 0, 0)),
                   pl.BlockSpec((None, rows, tk), lambda l, k: (l, 0, k))],
        out_shape=[jax.ShapeDtypeStruct((n_layers, rows, cols), F32),
                   jax.ShapeDtypeStruct((n_layers, rows, d), F32)],
        scratch_shapes=[pltpu.VMEM((rows, cols), F32)],
        compiler_params=_params(2))(c_rows, ada_w, bias_loc)


def _ada_update(sc_t, dmod, w, m, v):
    n_layers, d, cols = w.shape
    tr = _tile(d, 256)
    big = pl.BlockSpec((None, tr, cols), lambda l, i: (l, i, 0))

    def body(sc_ref, dm_ref, w_ref, m_ref, v_ref, g_out, d_out, m_out, v_out):
        g = _dot(sc_ref[...], dm_ref[...])
        delta, m_new, v_new = _adamw(w_ref[...], g, m_ref[...], v_ref[...])
        g_out[...] = g
        d_out[...] = delta
        m_out[...] = m_new
        v_out[...] = v_new

    return pl.pallas_call(
        body, name="ada_update", grid=(n_layers, d // tr),
        in_specs=[pl.BlockSpec((tr, LANE), lambda l, i: (i, 0)),
                  pl.BlockSpec((None, LANE, cols), lambda l, i: (l, 0, 0)), big, big, big],
        out_specs=[big] * 4, out_shape=[jax.ShapeDtypeStruct(w.shape, F32)] * 4,
        compiler_params=_params(2))(sc_t, dmod, w, m, v)


def _shard_update(name, slabs, w, m, v):
    rows, cols = w.shape
    bytes_per_row = 2 * cols * (2 * N_DEV + 4 * 7)
    tr = SUBLANE * 2
    while tr * 2 <= rows and rows % (tr * 2) == 0 and tr * 2 * bytes_per_row <= 24 * 2 ** 20:
        tr *= 2
    tr = _tile(rows, tr)
    blk = pl.BlockSpec((tr, cols), lambda i: (i, 0))

    def body(s_ref, w_ref, m_ref, v_ref, g_out, d_out, m_out, v_out):
        g = s_ref[0].astype(F32)
        for dev in range(1, N_DEV):
            g = g + s_ref[dev].astype(F32)
        delta, m_new, v_new = _adamw(w_ref[...], g, m_ref[...], v_ref[...])
        g_out[...] = g
        d_out[...] = delta
        m_out[...] = m_new
        v_out[...] = v_new

    return pl.pallas_call(
        body, name=name, grid=(rows // tr,),
        in_specs=[pl.BlockSpec((N_DEV, tr, cols), lambda i: (0, i, 0)), blk, blk, blk],
        out_specs=[blk] * 4, out_shape=[jax.ShapeDtypeStruct((rows, cols), F32)] * 4,
        compiler_params=_params(1))(slabs, w, m, v)


def _small_update(gathered, w, m, v):
    shape = jax.ShapeDtypeStruct(w.shape, F32)

    def body(g_ref, w_ref, m_ref, v_ref, g_out, d_out, m_out, v_out):
        g = g_ref[0]
        for dev in range(1, N_DEV):
            g = g + g_ref[dev]
        delta, m_new, v_new = _adamw(w_ref[...], g, m_ref[...], v_ref[...])
        g_out[...] = g
        d_out[...] = delta
        m_out[...] = m_new
        v_out[...] = v_new

    return pl.pallas_call(body, name="small_update", out_shape=[shape] * 4,
                          compiler_params=pltpu.CompilerParams(vmem_limit_bytes=VMEM_LIMIT),
                          )(gathered, w, m, v)


def _rope_tables(s_len):
    half = HEAD_DIM // 2
    inv = ROPE_THETA ** (-jnp.arange(half, dtype=F32) / half)
    ang = jnp.arange(s_len, dtype=F32)[:, None] * inv[None, :]
    cos, sin = jnp.cos(ang), jnp.sin(ang)
    return jnp.concatenate([cos, cos], axis=1), jnp.concatenate([-sin, sin], axis=1)


def _pad_lanes(a):
    return jnp.pad(a, ((0, 0), (0, LANE - a.shape[1])))


def _rows_of(nat, heads):
    return jnp.transpose(nat[:, :heads])[:, None, :]


def _pack(parts):
    tile = SUBLANE * LANE
    flat = []
    for p in parts:
        p = p.reshape(-1)
        flat.append(jnp.pad(p, (0, (-p.shape[0]) % tile)))
    return jnp.concatenate(flat).reshape(-1, LANE)


def _unpack(packed, shapes):
    tile = SUBLANE * LANE
    flat = packed.reshape(-1)
    out, pos = [], 0
    for shape in shapes:
        size = 1
        for dim in shape:
            size *= dim
        out.append(flat[pos:pos + size].reshape(shape))
        pos += size + (-size) % tile
    return out


def kernel(x, c, norm_g, ada_w, ada_b, a_w_in, a_b_f, a_w_out, kv_norm_g, kv_w, b_w_in, b_sinks, b_w_out, final_norm_g, loss_target, m_norm_g, m_ada_w, m_ada_b, m_a_w_in, m_a_b_f, m_a_w_out, m_kv_norm_g, m_kv_w, m_b_w_in, m_b_sinks, m_b_w_out, m_final_norm_g, v_norm_g, v_ada_w, v_ada_b, v_a_w_in, v_a_b_f, v_a_w_out, v_kv_norm_g, v_kv_w, v_b_w_in, v_b_sinks, v_b_w_out, v_final_norm_g):
    s_len, d = x.shape[1], x.shape[2]
    heads = d // HEAD_DIM
    kv_heads = kv_w.shape[1] // (2 * HEAD_DIM)
    kv_width = kv_heads * HEAD_DIM
    ada_cols = ada_w.shape[2]
    a_in_cols = a_w_in.shape[2]
    assert heads <= LANE and a_in_cols * N_DEV == 4 * d + heads
    me = _slot(_mesh_pos())
    x0 = x[0]
    target = loss_target[0]
    vec = lambda a: a.reshape(1, d)

    g_a_in, g_a_out, g_kv, g_b_in, g_b_out, c_all = _all_gather(
        "gather_weights",
        [a_w_in[0].astype(BF16), a_w_out[0].astype(BF16), kv_w.astype(BF16),
         b_w_in[0].astype(BF16), b_w_out[0].astype(BF16), c])
    w_a_full = jnp.transpose(g_a_in, (1, 0, 2)).reshape(d, N_DEV * a_in_cols)
    w_a_main = w_a_full[:, :4 * d]
    w_a_f = _pad_lanes(w_a_full[:, 4 * d:])
    w_a_out = g_a_out.reshape(d, d)
    w_kv = g_kv.reshape(d, 2 * kv_width)
    w_b_in = g_b_in
    w_b_out = g_b_out.reshape(d, d)
    b_in_cols = w_b_in.shape[2]

    c_rows = jnp.pad(c_all.reshape(N_DEV, d), ((0, 2 * SUBLANE - N_DEV), (0, 0)))
    bias_loc = lax.dynamic_slice_in_dim(ada_b, me * ada_cols, ada_cols, axis=1)[:, None, :]
    mod_part, sc_rows = _ada_fwd(c_rows, ada_w, bias_loc)
    (mod_recv,) = _all_to_all("exchange_mod", [jnp.transpose(mod_part[:, :N_DEV], (1, 0, 2))])
    mod = jnp.transpose(mod_recv, (1, 0, 2)).reshape(2, 3 * d)
    shift0, scale0, gate0 = vec(mod[0, :d]), vec(mod[0, d:2 * d]), vec(mod[0, 2 * d:])
    shift1, scale1, gate1 = vec(mod[1, :d]), vec(mod[1, d:2 * d]), vec(mod[1, 2 * d:])
    g0, g1, g_kvn, g_fin = vec(norm_g[0]), vec(norm_g[1]), vec(kv_norm_g), vec(final_norm_g)

    cos, sin = _rope_tables(s_len)
    bias_f = _pad_lanes(a_b_f)
    sinks_row = _pad_lanes(b_sinks)

    h0 = _norm_fwd("norm0", x0, g0, scale0, shift0)
    proj0 = _mm_plain("proj0", h0, w_a_main)
    f_raw = _mm_plain("proj0_f", h0, w_a_f, out_dtype=F32, tn=LANE)
    f_nat = _fgate_fwd(f_raw, bias_f)
    f_t = _rows_of(f_nat, heads)
    o0, u0, lse0 = _fox_fwd(proj0, f_nat, f_t, heads)
    y0, x1 = _mm_residual("out0", u0, w_a_out, x0, gate0)

    h1, hk = _norm_fwd("norm1", x1, g1, scale1, shift1, gb=g_kvn)
    kv = _mm_rope("kv_proj", hk, w_kv, cos, sin, n_cols=2 * kv_width, rope_cols=kv_width,
                  tn=kv_width)
    w_b_in_spec = pl.BlockSpec((None, _tile(d, 512), b_in_cols), lambda i, j, k: (j, k, 0))
    proj1 = _mm_rope("proj1", h1, w_b_in, cos, sin, n_cols=2 * d, rope_cols=d, tn=b_in_cols,
                     b_spec=w_b_in_spec)
    o1, u1, lse1 = _swa_fwd(proj1, kv, sinks_row, heads, kv_heads)
    y1, x2 = _mm_residual("out1", u1, w_b_out, x1, gate1)

    loss_part, dx2, dy1, sums_f = _loss_bwd(x2, target, y1, g_fin, gate1)

    do1, dz1, delta1 = _mm_gate_bwd("out1_bwd", dy1, w_b_out, proj1, d, o1)
    gw_b_out = _mm_plain("out1_wgrad", jnp.transpose(u1), dy1)
    dq1, dsinks = _swa_bwd_q(proj1, kv, do1, lse1, delta1, sinks_row, cos, sin, heads, kv_heads)
    dk1, dv1 = _swa_bwd_kv(proj1, kv, do1, _rows_of(lse1, heads), _rows_of(delta1, heads),
                           cos, sin, heads, kv_heads)
    dproj1 = jnp.concatenate([dq1, dz1], axis=1)
    tk_b = _tile(2 * d, b_in_cols)
    dh1 = _mm_plain("proj1_bwd", dproj1, w_b_in, nt=True, n_cols=d, out_dtype=F32, tk=tk_b,
                    b_spec=pl.BlockSpec((None, _tile(d, 1024), tk_b),
                                        lambda i, j, k: (k * tk_b // b_in_cols, j, 0)))
    gw_b_in = _mm_plain("proj1_wgrad", jnp.transpose(h1), dproj1, tn=b_in_cols,
                        out_3d=(N_DEV, b_in_cols))
    dkv = jnp.concatenate([dk1, dv1], axis=1)
    dhk = _mm_plain("kv_bwd", dkv, w_kv, nt=True, out_dtype=F32)
    gw_kv = _mm_plain("kv_wgrad", jnp.transpose(hk), dkv)
    dx1, dy0, sums1 = _norm_bwd("norm1_bwd", x1, dx2, dh1, g1, scale1, dhb=dhk, gb=g_kvn,
                                y=y0, gate=gate0)

    do0, dz0, delta0 = _mm_gate_bwd("out0_bwd", dy0, w_a_out, proj0, 3 * d, o0)
    gw_a_out = _mm_plain("out0_wgrad", jnp.transpose(u0), dy0)
    k0_t = jnp.transpose(proj0[:, d:2 * d])
    dk0, dv0, dq0_t, dfq_t, dfk_nat = _fox_bwd(proj0, k0_t, do0, f_nat, f_t,
                                               _rows_of(lse0, heads), _rows_of(delta0, heads), heads)
    dfq_nat = _pad_lanes(jnp.transpose(dfq_t[:, 0, :]))
    df, sums_bf = _fgate_bwd(dfq_nat, dfk_nat, f_raw, bias_f)
    dproj0 = jnp.concatenate([jnp.transpose(dq0_t), dk0, dv0, dz0], axis=1)
    dh0_f = _mm_plain("proj0_f_bwd", df, w_a_f, nt=True, out_dtype=F32)
    dh0 = _mm_plain("proj0_bwd", dproj0, w_a_main, nt=True, out_dtype=F32, init=dh0_f)
    h0_t = jnp.transpose(h0)
    gw_a_main = _mm_plain("proj0_wgrad", h0_t, dproj0)
    gw_a_f = _mm_plain("proj0_f_wgrad", h0_t, df, tn=LANE)
    grad_x, sums0 = _norm_bwd("norm0_bwd", x0, dx1, dh0, g0, scale0)

    gw_a_in = jnp.concatenate([gw_a_main, gw_a_f[:, :heads]], axis=1)
    slabs_a_in = jnp.transpose(gw_a_in.reshape(d, N_DEV, a_in_cols), (1, 0, 2))
    r_a_in, r_a_out, r_kv, r_b_in, r_b_out = _all_to_all(
        "scatter_grads",
        [slabs_a_in, gw_a_out.reshape(N_DEV, d // N_DEV, d),
         gw_kv.reshape(N_DEV, d // N_DEV, 2 * kv_width), gw_b_in,
         gw_b_out.reshape(N_DEV, d // N_DEV, d)])
    up_a_in = _shard_update("update_a_w_in", r_a_in, a_w_in[0], m_a_w_in[0], v_a_w_in[0])
    up_a_out = _shard_update("update_a_w_out", r_a_out, a_w_out[0], m_a_w_out[0], v_a_w_out[0])
    up_kv = _shard_update("update_kv_w", r_kv, kv_w, m_kv_w, v_kv_w)
    up_b_in = _shard_update("update_b_w_in", r_b_in, b_w_in[0], m_b_w_in[0], v_b_w_in[0])
    up_b_out = _shard_update("update_b_w_out", r_b_out, b_w_out[0], m_b_w_out[0], v_b_w_out[0])

    dmod = jnp.stack([jnp.concatenate([sums0[0], sums0[1], sums1[4]]),
                      jnp.concatenate([sums1[0], sums1[1], sums_f[1]])])
    small_shapes = [(2, 3 * d), (2, d), (1, heads), (d,), (1, heads), (d,), (1,)]
    small_grads = [dmod, jnp.stack([sums0[2], sums1[2]]), sums_bf[0:1, :heads], sums1[3],
                   dsinks[0:1, :heads], sums_f[0], loss_part[0, 0:1]]
    (small_all,) = _all_gather("gather_small", [_pack(small_grads)])
    zero = jnp.zeros((1,), F32)
    small = _small_update(
        small_all,
        _pack([ada_b, norm_g, a_b_f, kv_norm_g, b_sinks, final_norm_g, zero]),
        _pack([m_ada_b, m_norm_g, m_a_b_f, m_kv_norm_g, m_b_sinks, m_final_norm_g, zero]),
        _pack([v_ada_b, v_norm_g, v_a_b_f, v_kv_norm_g, v_b_sinks, v_final_norm_g, zero]))
    s_grad, s_delta, s_m, s_v = [_unpack(p, small_shapes) for p in small]
    loss = s_grad[6][0]

    dmod_all = small_all.reshape(N_DEV, -1)[:, :2 * 3 * d].reshape(N_DEV, 2, 3 * d)
    dmod_loc = lax.dynamic_slice_in_dim(dmod_all, me * ada_cols, ada_cols, axis=2)
    dmod_loc = jnp.pad(jnp.transpose(dmod_loc, (1, 0, 2)), ((0, 0), (0, LANE - N_DEV), (0, 0)))
    sc_t = jnp.pad(jnp.transpose(sc_rows[0, :N_DEV]), ((0, 0), (0, LANE - N_DEV)))
    up_ada = _ada_update(sc_t.astype(BF16), dmod_loc.astype(BF16), ada_w, m_ada_w, v_ada_w)

    lead = lambda a: a[None]
    per_kind = []
    for kind in range(4):
        sm = (s_grad, s_delta, s_m, s_v)[kind]
        per_kind.append([
            sm[1], up_ada[kind], sm[0], lead(up_a_in[kind]), sm[2], lead(up_a_out[kind]),
            sm[3], up_kv[kind], lead(up_b_in[kind]), sm[4], lead(up_b_out[kind]), sm[5]])
    return (loss, grad_x[None], *per_kind[0], *per_kind[1], *per_kind[2], *per_kind[3])
```

```python
import jax
import jax.numpy as jnp
from jax import lax
from jax.experimental import pallas as pl
from jax.experimental.pallas import tpu as pltpu

F32 = jnp.float32
BF16 = jnp.bfloat16
LANE = 128
SUBLANE = 8
HEAD_DIM = 128
SWA_BLOCK = 128
N_DEV = 8
N_PEER = N_DEV - 1
RMS_EPS = 1e-6
ROPE_THETA = 10000.0
NEG = -1e30
VMEM_LIMIT = 56 * 2 ** 20
MM_RESERVE = 10 * 2 ** 20
MESH = pl.DeviceIdType.MESH
HIGHEST = lax.Precision.HIGHEST

ADAM_LR = 0.001
ADAM_B1 = 0.9
ADAM_B2 = 0.999
ADAM_EPS = 1e-08
ADAM_WD = 0.01
ADAM_STEP = 10


def _tile(dim, pref):
    return pref if dim % pref == 0 else dim


def _params(n_axes):
    return pltpu.CompilerParams(dimension_semantics=("arbitrary",) * n_axes,
                                vmem_limit_bytes=VMEM_LIMIT)


def _dot(a, b):
    return jnp.dot(a, b, preferred_element_type=F32)


def _dot_nt(a, b):
    return lax.dot_general(a, b, (((1,), (1,)), ((), ())), preferred_element_type=F32)


def _sigmoid(z):
    return 1.0 / (1.0 + jnp.exp(-z))


def _iota(shape, dim):
    return lax.broadcasted_iota(jnp.int32, shape, dim)


def _pick_lane(block, lane_index):
    lane = _iota(block.shape, 1)
    return jnp.sum(jnp.where(lane == lane_index, block, 0.0), axis=1, keepdims=True)


def _adamw(w, g, m, v):
    m = ADAM_B1 * m + (1.0 - ADAM_B1) * g
    v = ADAM_B2 * v + (1.0 - ADAM_B2) * (g * g)
    m_hat = m / (1.0 - ADAM_B1 ** ADAM_STEP)
    v_hat = v / (1.0 - ADAM_B2 ** ADAM_STEP)
    delta = -ADAM_LR * (m_hat / (jnp.sqrt(v_hat) + ADAM_EPS) + ADAM_WD * w)
    return delta, m, v


def _mesh_pos():
    return lax.axis_index("x"), lax.axis_index("y"), lax.axis_index("c")


def _slot(pos):
    return 4 * pos[0] + 2 * pos[1] + pos[2]


def _all_gather(name, arrays):
    n = len(arrays)

    def body(*refs):
        ins, outs = refs[:n], refs[n:2 * n]
        send_sems, recv_sems, local_sems = refs[2 * n:]
        x, y, c = _mesh_pos()
        me, sibling = (x, y, c), (x, y, 1 - c)
        chips = [(1 - x, y), (x, 1 - y), (1 - x, 1 - y)]

        def copy(a, k, block, to, src=None):
            dst = outs[a].at[_slot(block)]
            return pltpu.make_async_remote_copy(
                src_ref=dst if src is None else src, dst_ref=dst,
                send_sem=send_sems.at[N_PEER * a + k], recv_sem=recv_sems.at[N_PEER * a + k],
                device_id=to, device_id_type=MESH)

        local, first, passed = [], [], []
        for a in range(n):
            cp = pltpu.make_async_copy(ins[a], outs[a].at[_slot(me)], local_sems.at[a])
            cp.start()
            local.append(cp)
            sends = [copy(a, 0, me, sibling, src=ins[a])]
            sends += [copy(a, 1 + j, me, (*chip, c), src=ins[a]) for j, chip in enumerate(chips)]
            for cp in sends:
                cp.start()
            first += sends
        for a in range(n):
            for j, chip in enumerate(chips):
                copy(a, 1 + j, (*chip, c), me).wait_recv()
                cp = copy(a, 4 + j, (*chip, c), sibling)
                cp.start()
                passed.append(cp)
        for a in range(n):
            copy(a, 0, sibling, me).wait_recv()
            for j, chip in enumerate(chips):
                copy(a, 4 + j, (*chip, 1 - c), me).wait_recv()
        for cp in first + passed:
            cp.wait_send()
        for cp in local:
            cp.wait()

    any_spec = pl.BlockSpec(memory_space=pl.ANY)
    return pl.pallas_call(
        body, name=name,
        out_shape=[jax.ShapeDtypeStruct((N_DEV,) + a.shape, a.dtype) for a in arrays],
        in_specs=[any_spec] * n, out_specs=[any_spec] * n,
        scratch_shapes=[pltpu.SemaphoreType.DMA((N_PEER * n,)),
                        pltpu.SemaphoreType.DMA((N_PEER * n,)),
                        pltpu.SemaphoreType.DMA((n,))],
    )(*arrays)


def _all_to_all(name, arrays):
    n = len(arrays)

    def body(*refs):
        ins, outs = refs[:n], refs[n:2 * n]
        send_sems, recv_sems, local_sems = refs[2 * n:]
        x, y, c = _mesh_pos()
        me = _slot((x, y, c))
        local, sends, recvs = [], [], []
        for a in range(n):
            cp = pltpu.make_async_copy(ins[a].at[me], outs[a].at[me], local_sems.at[a])
            cp.start()
            local.append(cp)
        for k in range(1, N_DEV):
            peer = (1 - x if k & 4 else x, 1 - y if k & 2 else y, 1 - c if k & 1 else c)
            ps = _slot(peer)
            for a in range(n):
                sem = N_PEER * a + k - 1
                cp = pltpu.make_async_remote_copy(
                    src_ref=ins[a].at[ps], dst_ref=outs[a].at[me],
                    send_sem=send_sems.at[sem], recv_sem=recv_sems.at[sem],
                    device_id=peer, device_id_type=MESH)
                cp.start()
                sends.append(cp)
                recvs.append(pltpu.make_async_remote_copy(
                    src_ref=ins[a].at[ps], dst_ref=outs[a].at[ps],
                    send_sem=send_sems.at[sem], recv_sem=recv_sems.at[sem],
                    device_id=peer, device_id_type=MESH))
        for cp in recvs:
            cp.wait_recv()
        for cp in sends:
            cp.wait_send()
        for cp in local:
            cp.wait()

    any_spec = pl.BlockSpec(memory_space=pl.ANY)
    return pl.pallas_call(
        body, name=name,
        out_shape=[jax.ShapeDtypeStruct(a.shape, a.dtype) for a in arrays],
        in_specs=[any_spec] * n, out_specs=[any_spec] * n,
        scratch_shapes=[pltpu.SemaphoreType.DMA((N_PEER * n,)),
                        pltpu.SemaphoreType.DMA((N_PEER * n,)),
                        pltpu.SemaphoreType.DMA((n,))],
    )(*arrays)


HBM_SPEC = pl.BlockSpec(memory_space=pltpu.HBM)
SEM_SPEC = pl.BlockSpec(memory_space=pltpu.SEMAPHORE)
EFFECT = pltpu.SideEffectType.DATAFLOW_SIDE_EFFECTING


def _exchange_copies(ins, lands, send_sems, recv_sems, gather):
    x, y, c = _mesh_pos()
    me = _slot((x, y, c))
    copies = []
    for k in range(1, N_DEV):
        peer = (1 - x if k & 4 else x, 1 - y if k & 2 else y, 1 - c if k & 1 else c)
        ps = _slot(peer)
        for a in range(len(ins)):
            sem = N_PEER * a + k - 1
            src = ins[a] if gather else ins[a].at[ps]
            both = dict(send_sem=send_sems.at[sem], recv_sem=recv_sems.at[sem],
                        device_id=peer, device_id_type=MESH)
            copies.append((pltpu.make_async_remote_copy(src_ref=src, dst_ref=lands[a].at[me], **both),
                           pltpu.make_async_remote_copy(src_ref=src, dst_ref=lands[a].at[ps], **both)))
    return copies


def _exchange_start(name, arrays, gather):
    n = len(arrays)
    lands = [lax.empty((N_DEV,) + a.shape if gather else a.shape, a.dtype) for a in arrays]

    def body(*refs):
        ins, zones = refs[:n], refs[n:2 * n]
        send_sems, recv_sems = refs[2 * n], refs[2 * n + 1]
        token, local_sems = refs[4 * n + 2], refs[4 * n + 3]
        me = _slot(_mesh_pos())
        for send, _ in _exchange_copies(ins, zones, send_sems, recv_sems, gather):
            send.start()
        own = [pltpu.make_async_copy(ins[a] if gather else ins[a].at[me], zones[a].at[me],
                                     local_sems.at[a]) for a in range(n)]
        for cp in own:
            cp.start()
        for cp in own:
            cp.wait()
        token[...] = jnp.zeros_like(token)

    hbm = lambda a: pltpu.HBM(a.shape, a.dtype)
    out = pl.pallas_call(
        body, name=name,
        out_shape=(pltpu.SemaphoreType.DMA((N_PEER * n,)), pltpu.SemaphoreType.DMA((N_PEER * n,)),
                   *[hbm(a) for a in arrays], *[hbm(z) for z in lands],
                   jax.ShapeDtypeStruct((SUBLANE, LANE), F32)),
        in_specs=[HBM_SPEC] * (2 * n),
        out_specs=(SEM_SPEC, SEM_SPEC, *[HBM_SPEC] * (2 * n),
                   pl.BlockSpec(memory_space=pltpu.VMEM)),
        input_output_aliases={i: 2 + i for i in range(2 * n)},
        scratch_shapes=[pltpu.SemaphoreType.DMA((n,))],
        compiler_params=pltpu.CompilerParams(has_side_effects=EFFECT),
    )(*[pltpu.with_memory_space_constraint(a, pltpu.HBM) for a in list(arrays) + lands])
    return out[0], out[1], list(out[2:2 + n]), list(out[2 + n:2 + 2 * n]), out[2 + 2 * n]


def _exchange_wait(name, started, after, gather):
    send_sems, recv_sems, arrays, lands, _ = started
    n = len(arrays)

    def body(*refs):
        ins, zones = refs[:n], refs[n:2 * n]
        for send, recv in _exchange_copies(ins, zones, refs[2 * n], refs[2 * n + 1], gather):
            send.wait_send()
            recv.wait_recv()

    hbm = lambda a: pltpu.HBM(a.shape, a.dtype)
    out = pl.pallas_call(
        body, name=name,
        out_shape=tuple(hbm(a) for a in arrays + lands),
        in_specs=[HBM_SPEC] * (2 * n) + [SEM_SPEC, SEM_SPEC, pl.BlockSpec(memory_space=pl.ANY)],
        out_specs=tuple([HBM_SPEC] * (2 * n)),
        input_output_aliases={i: i for i in range(2 * n)},
        compiler_params=pltpu.CompilerParams(has_side_effects=EFFECT),
    )(*arrays, *lands, send_sems, recv_sems, after)
    return list(out[n:])


def _after(value, token):
    return lax.optimization_barrier((value, token))[0]


def _k_tile(k_dim, tm, tn, fixed_bytes):
    budget = VMEM_LIMIT - MM_RESERVE - fixed_bytes
    tk = k_dim
    while tk % 2 == 0 and tk > 512 and (
            4 * (tm + tn) * tk + (4 * tm * tn if tk < k_dim else 0) > budget):
        tk //= 2
    return tk


def _matmul(name, a, b, *, nt, tm, tn, n_cols, out_shape, out_specs, epilogue,
            fixed_bytes, tk=None, b_spec=None, extra=(), extra_specs=()):
    m_rows, k_dim = a.shape
    tm, tn = _tile(m_rows, tm), _tile(n_cols, tn)
    tk = _k_tile(k_dim, tm, tn, fixed_bytes) if tk is None else _tile(k_dim, tk)
    grid = (m_rows // tm, n_cols // tn, k_dim // tk)
    n_k = grid[2]
    a_spec = pl.BlockSpec((tm, tk), lambda i, j, k: (i, k))
    if b_spec is not None:
        b_blk = b_spec(tk)
    elif nt:
        b_blk = pl.BlockSpec((tn, tk), lambda i, j, k: (j, k))
    else:
        b_blk = pl.BlockSpec((tk, tn), lambda i, j, k: (k, j))
    n_extra, n_out = len(extra), len(out_shape)
    product = _dot_nt if nt else _dot

    def body(a_ref, b_ref, *rest):
        extra_refs = rest[:n_extra]
        out_refs = rest[n_extra:n_extra + n_out]
        if n_k == 1:
            epilogue(product(a_ref[...], b_ref[...]), extra_refs, out_refs)
            return
        acc_ref = rest[n_extra + n_out]
        k = pl.program_id(2)

        @pl.when(k == 0)
        def _():
            acc_ref[...] = jnp.zeros_like(acc_ref)

        acc_ref[...] += product(a_ref[...], b_ref[...])

        @pl.when(k == n_k - 1)
        def _():
            epilogue(acc_ref[...], extra_refs, out_refs)

    return pl.pallas_call(
        body, name=name, grid=grid,
        in_specs=[a_spec, b_blk, *extra_specs], out_specs=out_specs, out_shape=out_shape,
        scratch_shapes=[pltpu.VMEM((tm, tn), F32)] if n_k > 1 else [],
        compiler_params=_params(3),
    )(a, b, *extra)


def _mm_plain(name, a, b, *, nt=False, n_cols=None, out_dtype=BF16, init=None,
              tm=1024, tn=1024, tk=None, b_spec=None, out_3d=None):
    m_rows = a.shape[0]
    if n_cols is None:
        n_cols = b.shape[0] if nt else b.shape[1]
    tm, tn = _tile(m_rows, tm), _tile(n_cols, tn)
    fixed = 2 * tm * tn * (jnp.dtype(out_dtype).itemsize + (4 if init is not None else 0))
    if out_3d is None:
        shape = jax.ShapeDtypeStruct((m_rows, n_cols), out_dtype)
        spec = pl.BlockSpec((tm, tn), lambda i, j, k: (i, j))
    else:
        slabs, width = out_3d
        assert tn == width and slabs * width == n_cols
        shape = jax.ShapeDtypeStruct((slabs, m_rows, width), out_dtype)
        spec = pl.BlockSpec((None, tm, width), lambda i, j, k: (j, i, 0))
    extra, extra_specs = (), ()
    if init is not None:
        extra = (init,)
        extra_specs = (pl.BlockSpec((tm, tn), lambda i, j, k: (i, j)),)

    def epilogue(acc, extra_refs, out_refs):
        if init is not None:
            acc = acc + extra_refs[0][...]
        out_refs[0][...] = acc.astype(out_dtype)

    (out,) = _matmul(name, a, b, nt=nt, tm=tm, tn=tn, tk=tk, n_cols=n_cols,
                     out_shape=[shape], out_specs=[spec], epilogue=epilogue, fixed_bytes=fixed,
                     b_spec=b_spec, extra=extra, extra_specs=extra_specs)
    return out


def _mm_rope(name, a, b, cos, sin, *, n_cols, rope_cols, tn, b_spec=None):
    m_rows = a.shape[0]
    tm = _tile(m_rows, 1024)
    tn = _tile(n_cols, tn)
    assert rope_cols % tn == 0 and tn % HEAD_DIM == 0
    rope_blocks = rope_cols // tn
    table_spec = pl.BlockSpec((tm, LANE), lambda i, j, k: (i, 0))

    def epilogue(acc, extra_refs, out_refs):
        cos_ref, sin_ref = extra_refs
        j = pl.program_id(1)

        @pl.when(j < rope_blocks)
        def _():
            for head in range(tn // HEAD_DIM):
                cols = slice(head * HEAD_DIM, (head + 1) * HEAD_DIM)
                blk = acc[:, cols]
                rot = pltpu.roll(blk, HEAD_DIM // 2, 1)
                out_refs[0][:, cols] = (blk * cos_ref[...] + rot * sin_ref[...]).astype(BF16)

        @pl.when(j >= rope_blocks)
        def _():
            out_refs[0][...] = acc.astype(BF16)

    (out,) = _matmul(name, a, b, nt=False, tm=tm, tn=tn, n_cols=n_cols,
                     out_shape=[jax.ShapeDtypeStruct((m_rows, n_cols), BF16)],
                     out_specs=[pl.BlockSpec((tm, tn), lambda i, j, k: (i, j))],
                     epilogue=epilogue, fixed_bytes=4 * tm * tn + 16 * tm * LANE, b_spec=b_spec,
                     extra=(cos, sin), extra_specs=(table_spec, table_spec))
    return out


def _mm_residual(name, u, w, x_in, gate):
    m_rows, n_cols = x_in.shape
    tm, tn = _tile(m_rows, 512), _tile(n_cols, 1024)
    blk = pl.BlockSpec((tm, tn), lambda i, j, k: (i, j))

    def epilogue(acc, extra_refs, out_refs):
        x_ref, gate_ref = extra_refs
        out_refs[0][...] = acc
        out_refs[1][...] = x_ref[...] + gate_ref[...] * acc

    y, x_out = _matmul(
        name, u, w, nt=False, tm=tm, tn=tn, n_cols=n_cols,
        out_shape=[jax.ShapeDtypeStruct((m_rows, n_cols), F32)] * 2, out_specs=[blk, blk],
        epilogue=epilogue, fixed_bytes=3 * 8 * tm * tn, extra=(x_in, gate),
        extra_specs=(blk, pl.BlockSpec((1, tn), lambda i, j, k: (0, j))))
    return y, x_out


def _mm_gate_bwd(name, dy, w_out, z_src, z_col0, o):
    m_rows = dy.shape[0]
    n_cols = w_out.shape[0]
    tm, tn = _tile(m_rows, 1024), _tile(n_cols, 1024)
    assert z_col0 % tn == 0 and tn % HEAD_DIM == 0 and n_cols // HEAD_DIM <= LANE
    z_blk0 = z_col0 // tn
    blk = pl.BlockSpec((tm, tn), lambda i, j, k: (i, j))

    def epilogue(du, extra_refs, out_refs):
        z_ref, o_ref = extra_refs
        do_ref, dz_ref, delta_ref = out_refs
        j = pl.program_id(1)
        z = z_ref[...].astype(F32)
        o_val = o_ref[...].astype(F32)
        sig = _sigmoid(z)
        d_o = (du * (z * sig)).astype(BF16)
        do_ref[...] = d_o
        dz_ref[...] = (du * o_val * (sig * (1.0 + z * (1.0 - sig)))).astype(BF16)

        @pl.when(j == 0)
        def _():
            delta_ref[...] = jnp.zeros_like(delta_ref)

        prod = d_o.astype(F32) * o_val
        lane = _iota((tm, LANE), 1)
        delta = delta_ref[...]
        for head in range(tn // HEAD_DIM):
            rows = jnp.sum(prod[:, head * HEAD_DIM:(head + 1) * HEAD_DIM], axis=1, keepdims=True)
            delta = delta + jnp.where(lane == j * (tn // HEAD_DIM) + head, rows, 0.0)
        delta_ref[...] = delta

    d_o, dz, delta = _matmul(
        name, dy, w_out, nt=True, tm=tm, tn=tn, n_cols=n_cols,
        out_shape=[jax.ShapeDtypeStruct((m_rows, n_cols), BF16)] * 2
        + [jax.ShapeDtypeStruct((m_rows, LANE), F32)],
        out_specs=[blk, blk, pl.BlockSpec((tm, LANE), lambda i, j, k: (i, 0))],
        epilogue=epilogue, fixed_bytes=4 * 4 * tm * tn + 8 * tm * LANE, extra=(z_src, o),
        extra_specs=(pl.BlockSpec((tm, tn), lambda i, j, k: (i, z_blk0 + j)), blk))
    return d_o, dz, delta


def _norm_fwd(name, x, ga, sa, ta, gb=None):
    s_len, d = x.shape
    tr = _tile(s_len, 256)
    two = gb is not None
    row = pl.BlockSpec((tr, d), lambda i: (i, 0))
    vec = pl.BlockSpec((1, d), lambda i: (0, 0))

    def body(x_ref, ga_ref, sa_ref, ta_ref, *rest):
        xv = x_ref[...]
        y = xv * lax.rsqrt(jnp.mean(xv * xv, axis=-1, keepdims=True) + RMS_EPS)
        rest[-2 if two else -1][...] = ((y * ga_ref[...]) * (1.0 + sa_ref[...]) + ta_ref[...]).astype(BF16)
        if two:
            rest[-1][...] = (y * rest[0][...]).astype(BF16)

    ins = [x, ga, sa, ta] + ([gb] if two else [])
    outs = pl.pallas_call(
        body, name=name, grid=(s_len // tr,),
        in_specs=[row] + [vec] * (len(ins) - 1),
        out_specs=[row] * (2 if two else 1),
        out_shape=[jax.ShapeDtypeStruct((s_len, d), BF16)] * (2 if two else 1),
        compiler_params=_params(1))(*ins)
    return outs if two else outs[0]


def _loss_bwd(x2, target, y1, g_final, gate1):
    s_len, d = x2.shape
    tr = _tile(s_len, 128)
    row = pl.BlockSpec((tr, d), lambda i: (i, 0))
    vec = pl.BlockSpec((1, d), lambda i: (0, 0))

    def body(x_ref, t_ref, y_ref, g_ref, gate_ref, loss_ref, dx_ref, dy_ref, sums_ref):
        @pl.when(pl.program_id(0) == 0)
        def _():
            loss_ref[...] = jnp.zeros_like(loss_ref)
            sums_ref[...] = jnp.zeros_like(sums_ref)

        xv = x_ref[...]
        rstd = lax.rsqrt(jnp.mean(xv * xv, axis=-1, keepdims=True) + RMS_EPS)
        xhat = xv * rstd
        g = g_ref[...]
        err = xhat * g - t_ref[...]
        sq = jnp.sum(jnp.sum(err * err, axis=1, keepdims=True), axis=0, keepdims=True)
        loss_ref[...] += sq * (0.5 / d)
        dout = err * (1.0 / d)
        dxhat = dout * g
        dx = rstd * (dxhat - xhat * jnp.mean(dxhat * xhat, axis=-1, keepdims=True))
        dx_ref[...] = dx
        dy_ref[...] = (dx * gate_ref[...]).astype(BF16)
        sums_ref[0:1, :] += jnp.sum(dout * xhat, axis=0, keepdims=True)
        sums_ref[1:2, :] += jnp.sum(dx * y_ref[...], axis=0, keepdims=True)

    return pl.pallas_call(
        body, name="loss_bwd", grid=(s_len // tr,),
        in_specs=[row, row, row, vec, vec],
        out_specs=[pl.BlockSpec((SUBLANE, LANE), lambda i: (0, 0)), row, row,
                   pl.BlockSpec((SUBLANE, d), lambda i: (0, 0))],
        out_shape=[jax.ShapeDtypeStruct((SUBLANE, LANE), F32),
                   jax.ShapeDtypeStruct((s_len, d), F32),
                   jax.ShapeDtypeStruct((s_len, d), BF16),
                   jax.ShapeDtypeStruct((SUBLANE, d), F32)],
        compiler_params=_params(1))(x2, target, y1, g_final, gate1)


def _norm_bwd(name, x, dres, dha, ga, sa, dhb=None, gb=None, y=None, gate=None):
    s_len, d = x.shape
    tr = _tile(s_len, 128)
    has_b, has_y = dhb is not None, y is not None
    row = pl.BlockSpec((tr, d), lambda i: (i, 0))
    vec = pl.BlockSpec((1, d), lambda i: (0, 0))
    ins, specs = [x, dres, dha, ga, sa], [row, row, row, vec, vec]
    if has_b:
        ins += [dhb, gb]
        specs += [row, vec]
    if has_y:
        ins += [y, gate]
        specs += [row, vec]
    n_in = len(ins)

    def body(*refs):
        x_ref, dres_ref, dha_ref, ga_ref, sa_ref = refs[:5]
        pos = 5
        if has_b:
            dhb_ref, gb_ref = refs[pos:pos + 2]
            pos += 2
        if has_y:
            y_ref, gate_ref = refs[pos:pos + 2]
        outs = refs[n_in:]
        dx_ref, sums_ref = outs[0], outs[-1]

        @pl.when(pl.program_id(0) == 0)
        def _():
            sums_ref[...] = jnp.zeros_like(sums_ref)

        xv = x_ref[...]
        rstd = lax.rsqrt(jnp.mean(xv * xv, axis=-1, keepdims=True) + RMS_EPS)
        xhat = xv * rstd
        dha_v = dha_ref[...]
        ga_v, sa_v = ga_ref[...], sa_ref[...]
        dxhat = dha_v * (ga_v * (1.0 + sa_v))
        sums_ref[0:1, :] += jnp.sum(dha_v, axis=0, keepdims=True)
        sums_ref[1:2, :] += jnp.sum(dha_v * (xhat * ga_v), axis=0, keepdims=True)
        sums_ref[2:3, :] += jnp.sum(dha_v * ((1.0 + sa_v) * xhat), axis=0, keepdims=True)
        if has_b:
            dhb_v = dhb_ref[...]
            dxhat = dxhat + dhb_v * gb_ref[...]
            sums_ref[3:4, :] += jnp.sum(dhb_v * xhat, axis=0, keepdims=True)
        dx = dres_ref[...] + rstd * (dxhat - xhat * jnp.mean(dxhat * xhat, axis=-1, keepdims=True))
        dx_ref[...] = dx
        if has_y:
            outs[1][...] = (dx * gate_ref[...]).astype(BF16)
            sums_ref[4:5, :] += jnp.sum(dx * y_ref[...], axis=0, keepdims=True)

    out_shape = [jax.ShapeDtypeStruct((s_len, d), F32)]
    out_specs = [row]
    if has_y:
        out_shape.append(jax.ShapeDtypeStruct((s_len, d), BF16))
        out_specs.append(row)
    out_shape.append(jax.ShapeDtypeStruct((SUBLANE, d), F32))
    out_specs.append(pl.BlockSpec((SUBLANE, d), lambda i: (0, 0)))
    return pl.pallas_call(body, name=name, grid=(s_len // tr,), in_specs=specs,
                          out_specs=out_specs, out_shape=out_shape,
                          compiler_params=_params(1))(*ins)


def _fgate_fwd(f_raw, bias_row):
    s_len = f_raw.shape[0]
    tb = _tile(s_len, 512)
    blk = pl.BlockSpec((tb, LANE), lambda t: (t, 0))

    def body(f_ref, b_ref, out_ref, carry):
        @pl.when(pl.program_id(0) == 0)
        def _():
            carry[...] = jnp.zeros_like(carry)

        u = f_ref[...] + b_ref[...]
        logf = jnp.minimum(u, 0.0) - jnp.log1p(jnp.exp(-jnp.abs(u)))
        tri = (_iota((tb, tb), 1) <= _iota((tb, tb), 0)).astype(F32)
        run = jnp.dot(tri, logf, precision=HIGHEST, preferred_element_type=F32) + carry[...]
        out_ref[...] = run
        carry[...] = run[tb - 1:tb, :]

    return pl.pallas_call(
        body, name="fgate_fwd", grid=(s_len // tb,),
        in_specs=[blk, pl.BlockSpec((1, LANE), lambda t: (0, 0))], out_specs=blk,
        out_shape=jax.ShapeDtypeStruct((s_len, LANE), F32),
        scratch_shapes=[pltpu.VMEM((1, LANE), F32)],
        compiler_params=_params(1))(f_raw, bias_row)


def _fgate_bwd(df_a, df_b, f_raw, bias_row):
    s_len = f_raw.shape[0]
    tb = _tile(s_len, 512)
    nb = s_len // tb
    blk = pl.BlockSpec((tb, LANE), lambda t: (nb - 1 - t, 0))

    def body(a_ref, b2_ref, f_ref, b_ref, df_ref, sums_ref, carry):
        @pl.when(pl.program_id(0) == 0)
        def _():
            carry[...] = jnp.zeros_like(carry)
            sums_ref[...] = jnp.zeros_like(sums_ref)

        d_run = a_ref[...] + b2_ref[...]
        tri = (_iota((tb, tb), 1) >= _iota((tb, tb), 0)).astype(F32)
        dlogf = jnp.dot(tri, d_run, precision=HIGHEST, preferred_element_type=F32) + carry[...]
        carry[...] = dlogf[0:1, :]
        u = f_ref[...] + b_ref[...]
        df = dlogf * _sigmoid(-u)
        df_ref[...] = df.astype(BF16)
        sums_ref[...] += jnp.sum(df, axis=0, keepdims=True)

    return pl.pallas_call(
        body, name="fgate_bwd", grid=(nb,),
        in_specs=[blk, blk, blk, pl.BlockSpec((1, LANE), lambda t: (0, 0))],
        out_specs=[blk, pl.BlockSpec((SUBLANE, LANE), lambda t: (0, 0))],
        out_shape=[jax.ShapeDtypeStruct((s_len, LANE), BF16),
                   jax.ShapeDtypeStruct((SUBLANE, LANE), F32)],
        scratch_shapes=[pltpu.VMEM((1, LANE), F32)],
        compiler_params=_params(1))(df_a, df_b, f_raw, bias_row)


def _fox_fwd(proj, f_nat, f_t, heads):
    s_len = proj.shape[0]
    d = heads * HEAD_DIM
    t = _tile(s_len, 512)
    nq = s_len // t
    scale = HEAD_DIM ** -0.5

    def body(q_ref, k_ref, v_ref, z_ref, fn_ref, ft_ref, o_ref, u_ref, lse_ref,
             m_scr, l_scr, acc_scr, fq_scr):
        i, h, j = pl.program_id(0), pl.program_id(1), pl.program_id(2)

        @pl.when(j == 0)
        def _():
            m_scr[...] = jnp.full_like(m_scr, NEG)
            l_scr[...] = jnp.zeros_like(l_scr)
            acc_scr[...] = jnp.zeros_like(acc_scr)
            fq_scr[...] = _pick_lane(fn_ref[...], h)

        @pl.when((j == 0) & (h == 0))
        def _():
            lse_ref[...] = jnp.zeros_like(lse_ref)

        def step(diagonal):
            s = _dot_nt(q_ref[...], k_ref[...]) * scale + (fq_scr[...] - ft_ref[...])
            if diagonal:
                s = jnp.where(_iota((t, t), 1) <= _iota((t, t), 0), s, NEG)
            m_prev = m_scr[...]
            m_new = jnp.maximum(m_prev, jnp.max(s, axis=1, keepdims=True))
            alpha = jnp.exp(m_prev - m_new)
            p = jnp.exp(s - m_new)
            l_scr[...] = alpha * l_scr[...] + jnp.sum(p, axis=1, keepdims=True)
            acc_scr[...] = alpha * acc_scr[...] + _dot(p.astype(BF16), v_ref[...])
            m_scr[...] = m_new

        @pl.when(j < i)
        def _():
            step(False)

        @pl.when(j == i)
        def _():
            step(True)
            l_sum = l_scr[...]
            o_val = acc_scr[...] / l_sum
            o_ref[...] = o_val.astype(BF16)
            z = z_ref[...].astype(F32)
            u_ref[...] = (o_val * (z * _sigmoid(z))).astype(BF16)
            lse = m_scr[...] + jnp.log(l_sum)
            lse_ref[...] += jnp.where(_iota((t, LANE), 1) == h, lse, 0.0)

    col = lambda base: pl.BlockSpec((t, HEAD_DIM), lambda i, h, j: (i, base + h))
    kv = lambda base: pl.BlockSpec((t, HEAD_DIM), lambda i, h, j: (jnp.minimum(j, i), base + h))
    nat = pl.BlockSpec((t, LANE), lambda i, h, j: (i, 0))
    out_blk = pl.BlockSpec((t, HEAD_DIM), lambda i, h, j: (i, h))
    return pl.pallas_call(
        body, name="fox_fwd", grid=(nq, heads, nq),
        in_specs=[col(0), kv(heads), kv(2 * heads), col(3 * heads), nat,
                  pl.BlockSpec((None, 1, t), lambda i, h, j: (h, 0, jnp.minimum(j, i)))],
        out_specs=[out_blk, out_blk, nat],
        out_shape=[jax.ShapeDtypeStruct((s_len, d), BF16), jax.ShapeDtypeStruct((s_len, d), BF16),
                   jax.ShapeDtypeStruct((s_len, LANE), F32)],
        scratch_shapes=[pltpu.VMEM((t, 1), F32), pltpu.VMEM((t, 1), F32),
                        pltpu.VMEM((t, HEAD_DIM), F32), pltpu.VMEM((t, 1), F32)],
        compiler_params=_params(3))(proj, proj, proj, proj, f_nat, f_t)


def _fox_bwd(proj, k_t, d_o, f_nat, f_t, lse_t, delta_t, heads):
    s_len = proj.shape[0]
    d = heads * HEAD_DIM
    t = _tile(s_len, 512)
    nq = s_len // t
    scale = HEAD_DIM ** -0.5

    def body(k_ref, v_ref, kt_ref, q_ref, do_ref, fn_ref, ft_ref, lse_ref, delta_ref,
             dk_ref, dv_ref, dqt_ref, dfq_ref, dfk_ref,
             dk_acc, dv_acc, dq_acc, dfq_acc, dfk_acc, fk_scr):
        h, j, i = pl.program_id(0), pl.program_id(1), pl.program_id(2)
        head_start = (j == 0) & (i == 0)

        @pl.when(head_start)
        def _():
            dq_acc[...] = jnp.zeros_like(dq_acc)
            dfq_acc[...] = jnp.zeros_like(dfq_acc)

        @pl.when(head_start & (h == 0))
        def _():
            dfk_ref[...] = jnp.zeros_like(dfk_ref)

        def step(diagonal):
            q = q_ref[...]
            d_out = do_ref[...]
            s_t = _dot_nt(k_ref[...], q) * scale + (ft_ref[...] - fk_scr[...]) - lse_ref[...]
            if diagonal:
                s_t = jnp.where(_iota((t, t), 0) <= _iota((t, t), 1), s_t, NEG)
            p_t = jnp.exp(s_t)
            dp_t = _dot_nt(v_ref[...], d_out)
            ds_t = p_t * (dp_t - delta_ref[...])
            ds_b = ds_t.astype(BF16)
            dv_acc[...] += _dot(p_t.astype(BF16), d_out)
            dk_acc[...] += _dot(ds_b, q)
            dq_acc[i] += _dot(kt_ref[...], ds_b)
            dfq_acc[i] += jnp.sum(ds_t, axis=0, keepdims=True)
            dfk_acc[...] += jnp.sum(ds_t, axis=1, keepdims=True)

        @pl.when(i == j)
        def _():
            dk_acc[...] = jnp.zeros_like(dk_acc)
            dv_acc[...] = jnp.zeros_like(dv_acc)
            dfk_acc[...] = jnp.zeros_like(dfk_acc)
            fk_scr[...] = _pick_lane(fn_ref[...], h)
            step(True)

        @pl.when(i > j)
        def _():
            step(False)

        @pl.when(i == nq - 1)
        def _():
            dk_ref[...] = (dk_acc[...] * scale).astype(BF16)
            dv_ref[...] = dv_acc[...].astype(BF16)
            rows = pl.ds(pl.multiple_of(j * t, t), t)
            dfk_ref[rows, :] += jnp.where(_iota((t, LANE), 1) == h, -dfk_acc[...], 0.0)

        @pl.when((i == nq - 1) & (j == nq - 1))
        def _():
            for blk in range(nq):
                cols = slice(blk * t, (blk + 1) * t)
                dqt_ref[:, cols] = (dq_acc[blk] * scale).astype(BF16)
                dfq_ref[:, cols] = dfq_acc[blk]

    key_col = lambda base: pl.BlockSpec((t, HEAD_DIM), lambda h, j, i: (j, base + h))
    qry = pl.BlockSpec((t, HEAD_DIM), lambda h, j, i: (jnp.maximum(i, j), h))
    qry_row = pl.BlockSpec((None, 1, t), lambda h, j, i: (h, 0, jnp.maximum(i, j)))
    kv_out = pl.BlockSpec((t, HEAD_DIM), lambda h, j, i: (j, h))
    return pl.pallas_call(
        body, name="fox_bwd", grid=(heads, nq, nq),
        in_specs=[key_col(heads), key_col(2 * heads),
                  pl.BlockSpec((HEAD_DIM, t), lambda h, j, i: (h, j)),
                  qry, qry, pl.BlockSpec((t, LANE), lambda h, j, i: (j, 0)),
                  qry_row, qry_row, qry_row],
        out_specs=[kv_out, kv_out,
                   pl.BlockSpec((HEAD_DIM, s_len), lambda h, j, i: (h, 0)),
                   pl.BlockSpec((None, 1, s_len), lambda h, j, i: (h, 0, 0)),
                   pl.BlockSpec((s_len, LANE), lambda h, j, i: (0, 0))],
        out_shape=[jax.ShapeDtypeStruct((s_len, d), BF16), jax.ShapeDtypeStruct((s_len, d), BF16),
                   jax.ShapeDtypeStruct((d, s_len), BF16),
                   jax.ShapeDtypeStruct((heads, 1, s_len), F32),
                   jax.ShapeDtypeStruct((s_len, LANE), F32)],
        scratch_shapes=[pltpu.VMEM((t, HEAD_DIM), F32), pltpu.VMEM((t, HEAD_DIM), F32),
                        pltpu.VMEM((nq, HEAD_DIM, t), F32), pltpu.VMEM((nq, 1, t), F32),
                        pltpu.VMEM((t, 1), F32), pltpu.VMEM((t, 1), F32)],
        compiler_params=_params(3))(proj, proj, k_t, proj, d_o, f_nat, f_t, lse_t, delta_t)


def _swa_specs(heads, kv_heads):
    width = heads // kv_heads * HEAD_DIM
    wide = lambda base: pl.BlockSpec((SWA_BLOCK, width), lambda n, g: (n, base + g))
    blk = lambda fn: pl.BlockSpec((SWA_BLOCK, HEAD_DIM), fn)
    prev = lambda base: blk(lambda n, g: (jnp.maximum(n - 1, 0), base + g))
    cur = lambda base: blk(lambda n, g: (n, base + g))
    return wide, prev, cur


def _swa_scores(q, kp, kc, n, scale):
    r, c = _iota((SWA_BLOCK, SWA_BLOCK), 0), _iota((SWA_BLOCK, SWA_BLOCK), 1)
    sp = jnp.where((c > r) & (n > 0), _dot_nt(q, kp) * scale, NEG)
    sc = jnp.where(c <= r, _dot_nt(q, kc) * scale, NEG)
    return sp, sc


def _swa_fwd(proj, kv, sinks_row, heads, kv_heads):
    s_len = proj.shape[0]
    d = heads * HEAD_DIM
    scale = HEAD_DIM ** -0.5
    group = heads // kv_heads
    wide, prev, cur = _swa_specs(heads, kv_heads)

    def body(q_ref, z_ref, kp_ref, kc_ref, vp_ref, vc_ref, sink_ref, o_ref, u_ref, lse_ref):
        n, g = pl.program_id(0), pl.program_id(1)
        kp, kc, vp, vc = kp_ref[...], kc_ref[...], vp_ref[...], vc_ref[...]
        lane = _iota((SWA_BLOCK, LANE), 1)
        lse_all = jnp.zeros((SWA_BLOCK, LANE), F32)
        for hh in range(group):
            cols = slice(hh * HEAD_DIM, (hh + 1) * HEAD_DIM)
            head = g * group + hh
            sp, sc = _swa_scores(q_ref[:, cols], kp, kc, n, scale)
            sink = _pick_lane(sink_ref[...], head)
            m = jnp.maximum(jnp.maximum(jnp.max(sp, axis=1, keepdims=True),
                                        jnp.max(sc, axis=1, keepdims=True)), sink)
            pp, pc = jnp.exp(sp - m), jnp.exp(sc - m)
            den = (jnp.sum(pp, axis=1, keepdims=True) + jnp.sum(pc, axis=1, keepdims=True)
                   + jnp.exp(sink - m))
            o_val = (_dot(pp.astype(BF16), vp) + _dot(pc.astype(BF16), vc)) / den
            o_ref[:, cols] = o_val.astype(BF16)
            z = z_ref[:, cols].astype(F32)
            u_ref[:, cols] = (o_val * (z * _sigmoid(z))).astype(BF16)
            lse_all = lse_all + jnp.where(lane == head, m + jnp.log(den), 0.0)

        @pl.when(g == 0)
        def _():
            lse_ref[...] = lse_all

        @pl.when(g > 0)
        def _():
            lse_ref[...] += lse_all

    nat = pl.BlockSpec((SWA_BLOCK, LANE), lambda n, g: (n, 0))
    return pl.pallas_call(
        body, name="swa_fwd", grid=(s_len // SWA_BLOCK, kv_heads),
        in_specs=[wide(0), wide(kv_heads), prev(0), cur(0), prev(kv_heads), cur(kv_heads),
                  pl.BlockSpec((1, LANE), lambda n, g: (0, 0))],
        out_specs=[wide(0), wide(0), nat],
        out_shape=[jax.ShapeDtypeStruct((s_len, d), BF16), jax.ShapeDtypeStruct((s_len, d), BF16),
                   jax.ShapeDtypeStruct((s_len, LANE), F32)],
        compiler_params=_params(2))(proj, proj, kv, kv, kv, kv, sinks_row)


def _swa_bwd_q(proj, kv, d_o, lse, delta, sinks_row, cos, sin, heads, kv_heads):
    s_len = proj.shape[0]
    d = heads * HEAD_DIM
    scale = HEAD_DIM ** -0.5
    group = heads // kv_heads
    wide, prev, cur = _swa_specs(heads, kv_heads)

    def body(q_ref, kp_ref, kc_ref, vp_ref, vc_ref, do_ref, lse_ref, delta_ref, sink_ref,
             cos_ref, sin_ref, dq_ref, dsink_ref):
        n, g = pl.program_id(0), pl.program_id(1)

        @pl.when((n == 0) & (g == 0))
        def _():
            dsink_ref[...] = jnp.zeros_like(dsink_ref)

        kp, kc, vp, vc = kp_ref[...], kc_ref[...], vp_ref[...], vc_ref[...]
        lse_blk, delta_blk = lse_ref[...], delta_ref[...]
        cos_v, sin_v = cos_ref[...], sin_ref[...]
        lane = _iota((SUBLANE, LANE), 1)
        dsink_all = jnp.zeros((SUBLANE, LANE), F32)
        for hh in range(group):
            cols = slice(hh * HEAD_DIM, (hh + 1) * HEAD_DIM)
            head = g * group + hh
            sp, sc = _swa_scores(q_ref[:, cols], kp, kc, n, scale)
            lse_col = _pick_lane(lse_blk, head)
            delta_col = _pick_lane(delta_blk, head)
            pp, pc = jnp.exp(sp - lse_col), jnp.exp(sc - lse_col)
            p_sink = jnp.exp(_pick_lane(sink_ref[...], head) - lse_col)
            d_out = do_ref[:, cols]
            dsp = pp * (_dot_nt(d_out, vp) - delta_col)
            dsc = pc * (_dot_nt(d_out, vc) - delta_col)
            dq = (_dot(dsp.astype(BF16), kp) + _dot(dsc.astype(BF16), kc)) * scale
            dq_ref[:, cols] = (dq * cos_v - pltpu.roll(dq, HEAD_DIM // 2, 1) * sin_v).astype(BF16)
            d_sink = jnp.sum(-p_sink * delta_col, axis=0, keepdims=True)
            dsink_all = dsink_all + jnp.where(lane == head, d_sink, 0.0)
        dsink_ref[...] += dsink_all

    own = wide(0)
    nat = pl.BlockSpec((SWA_BLOCK, LANE), lambda n, g: (n, 0))
    return pl.pallas_call(
        body, name="swa_bwd_q", grid=(s_len // SWA_BLOCK, kv_heads),
        in_specs=[own, prev(0), cur(0), prev(kv_heads), cur(kv_heads), own, nat, nat,
                  pl.BlockSpec((1, LANE), lambda n, g: (0, 0)), nat, nat],
        out_specs=[own, pl.BlockSpec((SUBLANE, LANE), lambda n, g: (0, 0))],
        out_shape=[jax.ShapeDtypeStruct((s_len, d), BF16),
                   jax.ShapeDtypeStruct((SUBLANE, LANE), F32)],
        compiler_params=_params(2))(proj, kv, kv, kv, kv, d_o, lse, delta, sinks_row, cos, sin)


def _swa_bwd_kv(proj, kv, d_o, lse_t, delta_t, cos, sin, heads, kv_heads):
    s_len = proj.shape[0]
    nb = s_len // SWA_BLOCK
    group = heads // kv_heads
    scale = HEAD_DIM ** -0.5

    def body(k_ref, v_ref, qm_ref, qn_ref, dom_ref, don_ref, lsem_ref, lsen_ref,
             deltam_ref, deltan_ref, cos_ref, sin_ref, dk_ref, dv_ref):
        m = pl.program_id(1)
        k, v = k_ref[...], v_ref[...]
        key, qry = _iota((SWA_BLOCK, SWA_BLOCK), 0), _iota((SWA_BLOCK, SWA_BLOCK), 1)
        own_valid = key <= qry
        next_valid = (key > qry) & (m + 1 < nb)
        dk = jnp.zeros((SWA_BLOCK, HEAD_DIM), F32)
        dv = jnp.zeros((SWA_BLOCK, HEAD_DIM), F32)
        for hh in range(group):
            cols = slice(hh * HEAD_DIM, (hh + 1) * HEAD_DIM)
            for q_ref, do_ref, lse_ref, delta_ref, valid in (
                    (qm_ref, dom_ref, lsem_ref, deltam_ref, own_valid),
                    (qn_ref, don_ref, lsen_ref, deltan_ref, next_valid)):
                q, d_out = q_ref[:, cols], do_ref[:, cols]
                s_t = _dot_nt(k, q) * scale
                p_t = jnp.exp(jnp.where(valid, s_t - lse_ref[hh], NEG))
                ds_t = p_t * (_dot_nt(v, d_out) - delta_ref[hh])
                dv = dv + _dot(p_t.astype(BF16), d_out)
                dk = dk + _dot(ds_t.astype(BF16), q)
        dk = dk * scale
        dk_ref[...] = (dk * cos_ref[...]
                       - pltpu.roll(dk, HEAD_DIM // 2, 1) * sin_ref[...]).astype(BF16)
        dv_ref[...] = dv.astype(BF16)

    blk = lambda fn: pl.BlockSpec((SWA_BLOCK, HEAD_DIM), fn)
    nxt = lambda m: jnp.minimum(m + 1, nb - 1)
    wide = lambda fn: pl.BlockSpec((SWA_BLOCK, group * HEAD_DIM), fn)
    rows = lambda fn: pl.BlockSpec((group, 1, SWA_BLOCK), fn)
    q_m, q_n = wide(lambda g, m: (m, g)), wide(lambda g, m: (nxt(m), g))
    r_m, r_n = rows(lambda g, m: (g, 0, m)), rows(lambda g, m: (g, 0, nxt(m)))
    nat = pl.BlockSpec((SWA_BLOCK, LANE), lambda g, m: (m, 0))
    out_blk = blk(lambda g, m: (m, g))
    width = kv_heads * HEAD_DIM
    return pl.pallas_call(
        body, name="swa_bwd_kv", grid=(kv_heads, nb),
        in_specs=[blk(lambda g, m: (m, g)), blk(lambda g, m: (m, kv_heads + g)),
                  q_m, q_n, q_m, q_n, r_m, r_n, r_m, r_n, nat, nat],
        out_specs=[out_blk, out_blk],
        out_shape=[jax.ShapeDtypeStruct((s_len, width), BF16)] * 2,
        compiler_params=_params(2))(kv, kv, proj, proj, d_o, d_o, lse_t, lse_t,
                                    delta_t, delta_t, cos, sin)


def _ada_fwd(c_rows, ada_w, bias_loc):
    n_layers, d, cols = ada_w.shape
    rows = c_rows.shape[0]
    tk = _tile(d, 512)
    n_k = d // tk

    def body(c_ref, w_ref, b_ref, mod_ref, sc_ref, acc_ref):
        k = pl.program_id(1)

        @pl.when(k == 0)
        def _():
            acc_ref[...] = jnp.zeros_like(acc_ref)

        cv = c_ref[...]
        sc = cv * _sigmoid(cv)
        sc_ref[...] = sc
        acc_ref[...] += _dot(sc.astype(BF16), w_ref[...].astype(BF16))

        @pl.when(k == n_k - 1)
        def _():
            mod_ref[...] = acc_ref[...] + b_ref[...]

    return pl.pallas_call(
        body, name="ada_fwd", grid=(n_layers, n_k),
        in_specs=[pl.BlockSpec((rows, tk), lambda l, k: (0, k)),
                  pl.BlockSpec((None, tk, cols), lambda l, k: (l, k, 0)),
                  pl.BlockSpec((None, 1, cols), lambda l, k: (l, 0, 0))],
        out_specs=[pl.BlockSpec((None, rows, cols), lambda l, k: (l, 0, 0)),
                   pl.BlockSpec((None, rows, tk), lambda l, k: (l, 0, k))],
        out_shape=[jax.ShapeDtypeStruct((n_layers, rows, cols), F32),
                   jax.ShapeDtypeStruct((n_layers, rows, d), F32)],
        scratch_shapes=[pltpu.VMEM((rows, cols), F32)],
        compiler_params=_params(2))(c_rows, ada_w, bias_loc)


def _ada_update(sc_t, dmod, w, m, v):
    n_layers, d, cols = w.shape
    tr = _tile(d, 256)
    big = pl.BlockSpec((None, tr, cols), lambda l, i: (l, i, 0))

    def body(sc_ref, dm_ref, w_ref, m_ref, v_ref, g_out, d_out, m_out, v_out):
        g = _dot(sc_ref[...], dm_ref[...])
        delta, m_new, v_new = _adamw(w_ref[...], g, m_ref[...], v_ref[...])
        g_out[...] = g
        d_out[...] = delta
        m_out[...] = m_new
        v_out[...] = v_new

    return pl.pallas_call(
        body, name="ada_update", grid=(n_layers, d // tr),
        in_specs=[pl.BlockSpec((tr, LANE), lambda l, i: (i, 0)),
                  pl.BlockSpec((None, LANE, cols), lambda l, i: (l, 0, 0)), big, big, big],
        out_specs=[big] * 4, out_shape=[jax.ShapeDtypeStruct(w.shape, F32)] * 4,
        compiler_params=_params(2))(sc_t, dmod, w, m, v)


def _shard_update(name, slabs, w, m, v):
    rows, cols = w.shape
    bytes_per_row = 2 * cols * (2 * N_DEV + 4 * 7)
    tr = SUBLANE * 2
    while tr * 2 <= rows and rows % (tr * 2) == 0 and tr * 2 * bytes_per_row <= 24 * 2 ** 20:
        tr *= 2
    tr = _tile(rows, tr)
    blk = pl.BlockSpec((tr, cols), lambda i: (i, 0))

    def body(s_ref, w_ref, m_ref, v_ref, g_out, d_out, m_out, v_out):
        g = s_ref[0].astype(F32)
        for dev in range(1, N_DEV):
            g = g + s_ref[dev].astype(F32)
        delta, m_new, v_new = _adamw(w_ref[...], g, m_ref[...], v_ref[...])
        g_out[...] = g
        d_out[...] = delta
        m_out[...] = m_new
        v_out[...] = v_new

    return pl.pallas_call(
        body, name=name, grid=(rows // tr,),
        in_specs=[pl.BlockSpec((N_DEV, tr, cols), lambda i: (0, i, 0)), blk, blk, blk],
        out_specs=[blk] * 4, out_shape=[jax.ShapeDtypeStruct((rows, cols), F32)] * 4,
        compiler_params=_params(1))(slabs, w, m, v)


def _small_update(gathered, w, m, v):
    shape = jax.ShapeDtypeStruct(w.shape, F32)

    def body(g_ref, w_ref, m_ref, v_ref, g_out, d_out, m_out, v_out):
        g = g_ref[0]
        for dev in range(1, N_DEV):
            g = g + g_ref[dev]
        delta, m_new, v_new = _adamw(w_ref[...], g, m_ref[...], v_ref[...])
        g_out[...] = g
        d_out[...] = delta
        m_out[...] = m_new
        v_out[...] = v_new

    return pl.pallas_call(body, name="small_update", out_shape=[shape] * 4,
                          compiler_params=pltpu.CompilerParams(vmem_limit_bytes=VMEM_LIMIT),
                          )(gathered, w, m, v)


def _rope_tables(s_len):
    half = HEAD_DIM // 2
    inv = ROPE_THETA ** (-jnp.arange(half, dtype=F32) / half)
    ang = jnp.arange(s_len, dtype=F32)[:, None] * inv[None, :]
    cos, sin = jnp.cos(ang), jnp.sin(ang)
    return jnp.concatenate([cos, cos], axis=1), jnp.concatenate([-sin, sin], axis=1)


def _pad_lanes(a):
    return jnp.pad(a, ((0, 0), (0, LANE - a.shape[1])))


def _rows_of(nat, heads):
    return jnp.transpose(nat[:, :heads])[:, None, :]


def _pack(parts):
    tile = SUBLANE * LANE
    flat = []
    for p in parts:
        p = p.reshape(-1)
        flat.append(jnp.pad(p, (0, (-p.shape[0]) % tile)))
    return jnp.concatenate(flat).reshape(-1, LANE)


def _unpack(packed, shapes):
    tile = SUBLANE * LANE
    flat = packed.reshape(-1)
    out, pos = [], 0
    for shape in shapes:
        size = 1
        for dim in shape:
            size *= dim
        out.append(flat[pos:pos + size].reshape(shape))
        pos += size + (-size) % tile
    return out


def kernel(x, c, norm_g, ada_w, ada_b, a_w_in, a_b_f, a_w_out, kv_norm_g, kv_w, b_w_in, b_sinks, b_w_out, final_norm_g, loss_target, m_norm_g, m_ada_w, m_ada_b, m_a_w_in, m_a_b_f, m_a_w_out, m_kv_norm_g, m_kv_w, m_b_w_in, m_b_sinks, m_b_w_out, m_final_norm_g, v_norm_g, v_ada_w, v_ada_b, v_a_w_in, v_a_b_f, v_a_w_out, v_kv_norm_g, v_kv_w, v_b_w_in, v_b_sinks, v_b_w_out, v_final_norm_g):
    s_len, d = x.shape[1], x.shape[2]
    heads = d // HEAD_DIM
    kv_heads = kv_w.shape[1] // (2 * HEAD_DIM)
    kv_width = kv_heads * HEAD_DIM
    ada_cols = ada_w.shape[2]
    a_in_cols = a_w_in.shape[2]
    assert heads <= LANE and a_in_cols * N_DEV == 4 * d + heads
    me = _slot(_mesh_pos())
    x0 = x[0]
    target = loss_target[0]
    vec = lambda a: a.reshape(1, d)

    g_a_in, c_all = _all_gather("gather_a_w_in", [a_w_in[0].astype(BF16), c])
    later = [a_w_out[0].astype(BF16), kv_w.astype(BF16), b_w_in[0].astype(BF16),
             b_w_out[0].astype(BF16)]
    weights_started = _exchange_start("gather_rest_start", _after(later, g_a_in), gather=True)
    w_a_full = jnp.transpose(g_a_in, (1, 0, 2)).reshape(d, N_DEV * a_in_cols)
    w_a_main = w_a_full[:, :4 * d]
    w_a_f = _pad_lanes(w_a_full[:, 4 * d:])
    b_in_cols = b_w_in.shape[2]

    c_rows = jnp.pad(c_all.reshape(N_DEV, d), ((0, 2 * SUBLANE - N_DEV), (0, 0)))
    bias_loc = lax.dynamic_slice_in_dim(ada_b, me * ada_cols, ada_cols, axis=1)[:, None, :]
    mod_part, sc_rows = _ada_fwd(c_rows, ada_w, bias_loc)
    (mod_recv,) = _all_to_all("exchange_mod", [jnp.transpose(mod_part[:, :N_DEV], (1, 0, 2))])
    mod = jnp.transpose(mod_recv, (1, 0, 2)).reshape(2, 3 * d)
    shift0, scale0, gate0 = vec(mod[0, :d]), vec(mod[0, d:2 * d]), vec(mod[0, 2 * d:])
    shift1, scale1, gate1 = vec(mod[1, :d]), vec(mod[1, d:2 * d]), vec(mod[1, 2 * d:])
    g0, g1, g_kvn, g_fin = vec(norm_g[0]), vec(norm_g[1]), vec(kv_norm_g), vec(final_norm_g)

    cos, sin = _rope_tables(s_len)
    bias_f = _pad_lanes(a_b_f)
    sinks_row = _pad_lanes(b_sinks)

    h0 = _after(_norm_fwd("norm0", x0, g0, scale0, shift0), weights_started[4])
    proj0 = _mm_plain("proj0", h0, w_a_main)
    f_raw = _mm_plain("proj0_f", h0, w_a_f, out_dtype=F32, tn=LANE)
    f_nat = _fgate_fwd(f_raw, bias_f)
    f_t = _rows_of(f_nat, heads)
    o0, u0, lse0 = _fox_fwd(proj0, f_nat, f_t, heads)
    g_a_out, g_kv, w_b_in, g_b_out = _exchange_wait("gather_rest_wait", weights_started, u0,
                                                    gather=True)
    w_a_out = g_a_out.reshape(d, d)
    w_kv = g_kv.reshape(d, 2 * kv_width)
    w_b_out = g_b_out.reshape(d, d)
    y0, x1 = _mm_residual("out0", u0, w_a_out, x0, gate0)

    h1, hk = _norm_fwd("norm1", x1, g1, scale1, shift1, gb=g_kvn)
    kv = _mm_rope("kv_proj", hk, w_kv, cos, sin, n_cols=2 * kv_width, rope_cols=kv_width,
                  tn=kv_width)
    proj1 = _mm_rope("proj1", h1, w_b_in, cos, sin, n_cols=2 * d, rope_cols=d, tn=b_in_cols,
                     b_spec=lambda tk: pl.BlockSpec((None, tk, b_in_cols),
                                                    lambda i, j, k: (j, k, 0)))
    o1, u1, lse1 = _swa_fwd(proj1, kv, sinks_row, heads, kv_heads)
    y1, x2 = _mm_residual("out1", u1, w_b_out, x1, gate1)

    loss_part, dx2, dy1, sums_f = _loss_bwd(x2, target, y1, g_fin, gate1)

    do1, dz1, delta1 = _mm_gate_bwd("out1_bwd", dy1, w_b_out, proj1, d, o1)
    gw_b_out = _mm_plain("out1_wgrad", jnp.transpose(u1), dy1)
    dq1, dsinks = _swa_bwd_q(proj1, kv, do1, lse1, delta1, sinks_row, cos, sin, heads, kv_heads)
    dk1, dv1 = _swa_bwd_kv(proj1, kv, do1, _rows_of(lse1, heads), _rows_of(delta1, heads),
                           cos, sin, heads, kv_heads)
    dproj1 = jnp.concatenate([dq1, dz1], axis=1)
    tk_b = _tile(2 * d, b_in_cols)
    dh1 = _mm_plain("proj1_bwd", dproj1, w_b_in, nt=True, n_cols=d, out_dtype=F32, tk=tk_b,
                    b_spec=lambda tk: pl.BlockSpec((None, _tile(d, 1024), tk),
                                                   lambda i, j, k: (k * tk // b_in_cols, j, 0)))
    gw_b_in = _mm_plain("proj1_wgrad", jnp.transpose(h1), dproj1, tn=b_in_cols,
                        out_3d=(N_DEV, b_in_cols))
    dkv = jnp.concatenate([dk1, dv1], axis=1)
    dhk = _mm_plain("kv_bwd", dkv, w_kv, nt=True, out_dtype=F32)
    gw_kv = _mm_plain("kv_wgrad", jnp.transpose(hk), dkv)
    grads1_started = _exchange_start(
        "scatter_grads1_start",
        [gw_b_out.reshape(N_DEV, d // N_DEV, d), gw_b_in,
         gw_kv.reshape(N_DEV, d // N_DEV, 2 * kv_width)], gather=False)
    dx1, dy0, sums1 = _norm_bwd("norm1_bwd", x1, dx2, dh1, g1, scale1,
                                dhb=_after(dhk, grads1_started[4]), gb=g_kvn, y=y0, gate=gate0)

    do0, dz0, delta0 = _mm_gate_bwd("out0_bwd", dy0, w_a_out, proj0, 3 * d, o0)
    gw_a_out = _mm_plain("out0_wgrad", jnp.transpose(u0), dy0)
    grads2_started = _exchange_start("scatter_grads2_start",
                                     [gw_a_out.reshape(N_DEV, d // N_DEV, d)], gather=False)
    k0_t = jnp.transpose(proj0[:, d:2 * d])
    dk0, dv0, dq0_t, dfq_t, dfk_nat = _fox_bwd(proj0, k0_t, _after(do0, grads2_started[4]),
                                               f_nat, f_t, _rows_of(lse0, heads),
                                               _rows_of(delta0, heads), heads)
    dfq_nat = _pad_lanes(jnp.transpose(dfq_t[:, 0, :]))
    df, sums_bf = _fgate_bwd(dfq_nat, dfk_nat, f_raw, bias_f)
    dproj0 = jnp.concatenate([jnp.transpose(dq0_t), dk0, dv0, dz0], axis=1)
    h0_t = jnp.transpose(h0)
    gw_a_main = _mm_plain("proj0_wgrad", h0_t, dproj0)
    gw_a_f = _mm_plain("proj0_f_wgrad", h0_t, df, tn=LANE)
    gw_a_in = jnp.concatenate([gw_a_main, gw_a_f[:, :heads]], axis=1)
    slabs_a_in = jnp.transpose(gw_a_in.reshape(d, N_DEV, a_in_cols), (1, 0, 2))
    grads3_started = _exchange_start("scatter_grads3_start", [slabs_a_in], gather=False)
    dh0_f = _mm_plain("proj0_f_bwd", _after(df, grads3_started[4]), w_a_f, nt=True, out_dtype=F32)
    dh0 = _mm_plain("proj0_bwd", dproj0, w_a_main, nt=True, out_dtype=F32, init=dh0_f)
    grad_x, sums0 = _norm_bwd("norm0_bwd", x0, dx1, dh0, g0, scale0)

    r_b_out, r_b_in, r_kv = _exchange_wait("scatter_grads1_wait", grads1_started, grad_x,
                                           gather=False)
    up_b_out = _shard_update("update_b_w_out", r_b_out, b_w_out[0], m_b_w_out[0], v_b_w_out[0])
    up_b_in = _shard_update("update_b_w_in", r_b_in, b_w_in[0], m_b_w_in[0], v_b_w_in[0])
    up_kv = _shard_update("update_kv_w", r_kv, kv_w, m_kv_w, v_kv_w)
    (r_a_out,) = _exchange_wait("scatter_grads2_wait", grads2_started, up_kv[0], gather=False)
    up_a_out = _shard_update("update_a_w_out", r_a_out, a_w_out[0], m_a_w_out[0], v_a_w_out[0])

    dmod = jnp.stack([jnp.concatenate([sums0[0], sums0[1], sums1[4]]),
                      jnp.concatenate([sums1[0], sums1[1], sums_f[1]])])
    small_shapes = [(2, 3 * d), (2, d), (1, heads), (d,), (1, heads), (d,), (1,)]
    small_grads = [dmod, jnp.stack([sums0[2], sums1[2]]), sums_bf[0:1, :heads], sums1[3],
                   dsinks[0:1, :heads], sums_f[0], loss_part[0, 0:1]]
    (small_all,) = _all_gather("gather_small", [_pack(small_grads)])
    zero = jnp.zeros((1,), F32)
    small = _small_update(
        small_all,
        _pack([ada_b, norm_g, a_b_f, kv_norm_g, b_sinks, final_norm_g, zero]),
        _pack([m_ada_b, m_norm_g, m_a_b_f, m_kv_norm_g, m_b_sinks, m_final_norm_g, zero]),
        _pack([v_ada_b, v_norm_g, v_a_b_f, v_kv_norm_g, v_b_sinks, v_final_norm_g, zero]))
    s_grad, s_delta, s_m, s_v = [_unpack(p, small_shapes) for p in small]
    loss = s_grad[6][0]

    dmod_all = small_all.reshape(N_DEV, -1)[:, :2 * 3 * d].reshape(N_DEV, 2, 3 * d)
    dmod_loc = lax.dynamic_slice_in_dim(dmod_all, me * ada_cols, ada_cols, axis=2)
    dmod_loc = jnp.pad(jnp.transpose(dmod_loc, (1, 0, 2)), ((0, 0), (0, LANE - N_DEV), (0, 0)))
    sc_t = jnp.pad(jnp.transpose(sc_rows[0, :N_DEV]), ((0, 0), (0, LANE - N_DEV)))
    up_ada = _ada_update(sc_t.astype(BF16), dmod_loc.astype(BF16), ada_w, m_ada_w, v_ada_w)

    last = _after(up_ada[0][0, :SUBLANE], (up_a_out[0], up_b_out[0], up_b_in[0], small[0]))
    (r_a_in,) = _exchange_wait("scatter_grads3_wait", grads3_started, last, gather=False)
    up_a_in = _shard_update("update_a_w_in", r_a_in, a_w_in[0], m_a_w_in[0], v_a_w_in[0])

    lead = lambda a: a[None]
    per_kind = []
    for kind in range(4):
        sm = (s_grad, s_delta, s_m, s_v)[kind]
        per_kind.append([
            sm[1], up_ada[kind], sm[0], lead(up_a_in[kind]), sm[2], lead(up_a_out[kind]),
            sm[3], up_kv[kind], lead(up_b_in[kind]), sm[4], lead(up_b_out[kind]), sm[5]])
    return (loss, grad_x[None], *per_kind[0], *per_kind[1], *per_kind[2], *per_kind[3])
```

```python
import jax
import jax.numpy as jnp
from jax import lax
from jax.experimental import pallas as pl
from jax.experimental.pallas import tpu as pltpu
from jax.experimental.pallas import tpu_sc as plsc

F32 = jnp.float32
BF16 = jnp.bfloat16
LANE = 128
SUBLANE = 8
HEAD_DIM = 128
SWA_BLOCK = 128
N_DEV = 8
N_PEER = N_DEV - 1
RMS_EPS = 1e-6
ROPE_THETA = 10000.0
NEG = -1e30
VMEM_LIMIT = 56 * 2 ** 20
MM_RESERVE = 10 * 2 ** 20
MESH = pl.DeviceIdType.MESH
HIGHEST = lax.Precision.HIGHEST

ADAM_LR = 0.001
ADAM_B1 = 0.9
ADAM_B2 = 0.999
ADAM_EPS = 1e-08
ADAM_WD = 0.01
ADAM_STEP = 10


def _tile(dim, pref):
    return pref if dim % pref == 0 else dim


def _params(n_axes):
    return pltpu.CompilerParams(dimension_semantics=("arbitrary",) * n_axes,
                                vmem_limit_bytes=VMEM_LIMIT)


def _dot(a, b):
    return jnp.dot(a, b, preferred_element_type=F32)


def _dot_nt(a, b):
    return lax.dot_general(a, b, (((1,), (1,)), ((), ())), preferred_element_type=F32)


def _sigmoid(z):
    return 1.0 / (1.0 + jnp.exp(-z))


def _iota(shape, dim):
    return lax.broadcasted_iota(jnp.int32, shape, dim)


def _pick_lane(block, lane_index):
    lane = _iota(block.shape, 1)
    return jnp.sum(jnp.where(lane == lane_index, block, 0.0), axis=1, keepdims=True)


def _adamw(w, g, m, v):
    m = ADAM_B1 * m + (1.0 - ADAM_B1) * g
    v = ADAM_B2 * v + (1.0 - ADAM_B2) * (g * g)
    m_hat = m / (1.0 - ADAM_B1 ** ADAM_STEP)
    v_hat = v / (1.0 - ADAM_B2 ** ADAM_STEP)
    delta = -ADAM_LR * (m_hat / (jnp.sqrt(v_hat) + ADAM_EPS) + ADAM_WD * w)
    return delta, m, v


def _mesh_pos():
    return lax.axis_index("x"), lax.axis_index("y"), lax.axis_index("c")


def _slot(pos):
    return 4 * pos[0] + 2 * pos[1] + pos[2]


def _handshake(peers):
    barrier = pltpu.get_barrier_semaphore()
    for peer in peers:
        pl.semaphore_signal(barrier, inc=1, device_id=peer, device_id_type=MESH)
    pl.semaphore_wait(barrier, len(peers))


def _launch(name, body, arrays, out_shape, sequencer_id):
    n = len(arrays)
    scratch = [pltpu.SemaphoreType.DMA((N_PEER * n,)), pltpu.SemaphoreType.DMA((N_PEER * n,)),
               pltpu.SemaphoreType.DMA((n,))]
    if sequencer_id is None:
        any_spec = pl.BlockSpec(memory_space=pl.ANY)
        return pl.pallas_call(body, name=name, out_shape=out_shape, in_specs=[any_spec] * n,
                              out_specs=[any_spec] * n, scratch_shapes=scratch)(*arrays)
    return pl.kernel(body, name=name, out_type=out_shape,
                     mesh=plsc.ScalarSubcoreMesh(axis_name="sequencer", num_cores=1),
                     scratch_types=scratch,
                     compiler_params=pltpu.CompilerParams(collective_id=sequencer_id))(*arrays)


def _all_gather(name, arrays, sequencer_id=None):
    n = len(arrays)

    def body(*refs):
        ins, outs = refs[:n], refs[n:2 * n]
        send_sems, recv_sems, local_sems = refs[2 * n:]
        x, y, c = _mesh_pos()
        me, sibling = (x, y, c), (x, y, 1 - c)
        chips = [(1 - x, y), (x, 1 - y), (1 - x, 1 - y)]
        if sequencer_id is not None:
            _handshake([sibling] + [(*chip, c) for chip in chips])

        def copy(a, k, block, to, src=None):
            dst = outs[a].at[_slot(block)]
            return pltpu.make_async_remote_copy(
                src_ref=dst if src is None else src, dst_ref=dst,
                send_sem=send_sems.at[N_PEER * a + k], recv_sem=recv_sems.at[N_PEER * a + k],
                device_id=to, device_id_type=MESH)

        local, first, passed = [], [], []
        for a in range(n):
            cp = pltpu.make_async_copy(ins[a], outs[a].at[_slot(me)], local_sems.at[a])
            cp.start()
            local.append(cp)
            sends = [copy(a, 0, me, sibling, src=ins[a])]
            sends += [copy(a, 1 + j, me, (*chip, c), src=ins[a]) for j, chip in enumerate(chips)]
            for cp in sends:
                cp.start()
            first += sends
        for a in range(n):
            for j, chip in enumerate(chips):
                copy(a, 1 + j, (*chip, c), me).wait_recv()
                cp = copy(a, 4 + j, (*chip, c), sibling)
                cp.start()
                passed.append(cp)
        for a in range(n):
            copy(a, 0, sibling, me).wait_recv()
            for j, chip in enumerate(chips):
                copy(a, 4 + j, (*chip, 1 - c), me).wait_recv()
        for cp in first + passed:
            cp.wait_send()
        for cp in local:
            cp.wait()

    out_shape = [jax.ShapeDtypeStruct((N_DEV,) + a.shape, a.dtype) for a in arrays]
    return _launch(name, body, arrays, out_shape, sequencer_id)


def _all_to_all(name, arrays, sequencer_id=None):
    n = len(arrays)

    def body(*refs):
        ins, outs = refs[:n], refs[n:2 * n]
        send_sems, recv_sems, local_sems = refs[2 * n:]
        x, y, c = _mesh_pos()
        me = _slot((x, y, c))
        if sequencer_id is not None:
            _handshake([(1 - x if k & 4 else x, 1 - y if k & 2 else y, 1 - c if k & 1 else c)
                        for k in range(1, N_DEV)])
        local, sends, recvs = [], [], []
        for a in range(n):
            cp = pltpu.make_async_copy(ins[a].at[me], outs[a].at[me], local_sems.at[a])
            cp.start()
            local.append(cp)
        for k in range(1, N_DEV):
            peer = (1 - x if k & 4 else x, 1 - y if k & 2 else y, 1 - c if k & 1 else c)
            ps = _slot(peer)
            for a in range(n):
                sem = N_PEER * a + k - 1
                cp = pltpu.make_async_remote_copy(
                    src_ref=ins[a].at[ps], dst_ref=outs[a].at[me],
                    send_sem=send_sems.at[sem], recv_sem=recv_sems.at[sem],
                    device_id=peer, device_id_type=MESH)
                cp.start()
                sends.append(cp)
                recvs.append(pltpu.make_async_remote_copy(
                    src_ref=ins[a].at[ps], dst_ref=outs[a].at[ps],
                    send_sem=send_sems.at[sem], recv_sem=recv_sems.at[sem],
                    device_id=peer, device_id_type=MESH))
        for cp in recvs:
            cp.wait_recv()
        for cp in sends:
            cp.wait_send()
        for cp in local:
            cp.wait()

    out_shape = [jax.ShapeDtypeStruct(a.shape, a.dtype) for a in arrays]
    return _launch(name, body, arrays, out_shape, sequencer_id)


def _after(value, token):
    return lax.optimization_barrier((value, token))[0]


def _k_tile(k_dim, tm, tn, fixed_bytes):
    budget = VMEM_LIMIT - MM_RESERVE - fixed_bytes
    tk = k_dim
    while tk % 2 == 0 and tk > 512 and (
            4 * (tm + tn) * tk + (4 * tm * tn if tk < k_dim else 0) > budget):
        tk //= 2
    return tk


def _matmul(name, a, b, *, nt, tm, tn, n_cols, out_shape, out_specs, epilogue,
            fixed_bytes, tk=None, b_spec=None, extra=(), extra_specs=()):
    m_rows, k_dim = a.shape
    tm, tn = _tile(m_rows, tm), _tile(n_cols, tn)
    tk = _k_tile(k_dim, tm, tn, fixed_bytes) if tk is None else _tile(k_dim, tk)
    grid = (m_rows // tm, n_cols // tn, k_dim // tk)
    n_k = grid[2]
    a_spec = pl.BlockSpec((tm, tk), lambda i, j, k: (i, k))
    if b_spec is not None:
        b_blk = b_spec(tk)
    elif nt:
        b_blk = pl.BlockSpec((tn, tk), lambda i, j, k: (j, k))
    else:
        b_blk = pl.BlockSpec((tk, tn), lambda i, j, k: (k, j))
    n_extra, n_out = len(extra), len(out_shape)
    product = _dot_nt if nt else _dot

    def body(a_ref, b_ref, *rest):
        extra_refs = rest[:n_extra]
        out_refs = rest[n_extra:n_extra + n_out]
        if n_k == 1:
            epilogue(product(a_ref[...], b_ref[...]), extra_refs, out_refs)
            return
        acc_ref = rest[n_extra + n_out]
        k = pl.program_id(2)

        @pl.when(k == 0)
        def _():
            acc_ref[...] = jnp.zeros_like(acc_ref)

        acc_ref[...] += product(a_ref[...], b_ref[...])

        @pl.when(k == n_k - 1)
        def _():
            epilogue(acc_ref[...], extra_refs, out_refs)

    return pl.pallas_call(
        body, name=name, grid=grid,
        in_specs=[a_spec, b_blk, *extra_specs], out_specs=out_specs, out_shape=out_shape,
        scratch_shapes=[pltpu.VMEM((tm, tn), F32)] if n_k > 1 else [],
        compiler_params=_params(3),
    )(a, b, *extra)


def _mm_plain(name, a, b, *, nt=False, n_cols=None, out_dtype=BF16, init=None,
              tm=1024, tn=1024, tk=None, b_spec=None, out_3d=None):
    m_rows = a.shape[0]
    if n_cols is None:
        n_cols = b.shape[0] if nt else b.shape[1]
    tm, tn = _tile(m_rows, tm), _tile(n_cols, tn)
    fixed = 2 * tm * tn * (jnp.dtype(out_dtype).itemsize + (4 if init is not None else 0))
    if out_3d is None:
        shape = jax.ShapeDtypeStruct((m_rows, n_cols), out_dtype)
        spec = pl.BlockSpec((tm, tn), lambda i, j, k: (i, j))
    else:
        slabs, width = out_3d
        assert tn == width and slabs * width == n_cols
        shape = jax.ShapeDtypeStruct((slabs, m_rows, width), out_dtype)
        spec = pl.BlockSpec((None, tm, width), lambda i, j, k: (j, i, 0))
    extra, extra_specs = (), ()
    if init is not None:
        extra = (init,)
        extra_specs = (pl.BlockSpec((tm, tn), lambda i, j, k: (i, j)),)

    def epilogue(acc, extra_refs, out_refs):
        if init is not None:
            acc = acc + extra_refs[0][...]
        out_refs[0][...] = acc.astype(out_dtype)

    (out,) = _matmul(name, a, b, nt=nt, tm=tm, tn=tn, tk=tk, n_cols=n_cols,
                     out_shape=[shape], out_specs=[spec], epilogue=epilogue, fixed_bytes=fixed,
                     b_spec=b_spec, extra=extra, extra_specs=extra_specs)
    return out


def _mm_rope(name, a, b, cos, sin, *, n_cols, rope_cols, tn, b_spec=None):
    m_rows = a.shape[0]
    tm = _tile(m_rows, 1024)
    tn = _tile(n_cols, tn)
    assert rope_cols % tn == 0 and tn % HEAD_DIM == 0
    rope_blocks = rope_cols // tn
    table_spec = pl.BlockSpec((tm, LANE), lambda i, j, k: (i, 0))

    def epilogue(acc, extra_refs, out_refs):
        cos_ref, sin_ref = extra_refs
        j = pl.program_id(1)

        @pl.when(j < rope_blocks)
        def _():
            for head in range(tn // HEAD_DIM):
                cols = slice(head * HEAD_DIM, (head + 1) * HEAD_DIM)
                blk = acc[:, cols]
                rot = pltpu.roll(blk, HEAD_DIM // 2, 1)
                out_refs[0][:, cols] = (blk * cos_ref[...] + rot * sin_ref[...]).astype(BF16)

        @pl.when(j >= rope_blocks)
        def _():
            out_refs[0][...] = acc.astype(BF16)

    (out,) = _matmul(name, a, b, nt=False, tm=tm, tn=tn, n_cols=n_cols,
                     out_shape=[jax.ShapeDtypeStruct((m_rows, n_cols), BF16)],
                     out_specs=[pl.BlockSpec((tm, tn), lambda i, j, k: (i, j))],
                     epilogue=epilogue, fixed_bytes=4 * tm * tn + 16 * tm * LANE, b_spec=b_spec,
                     extra=(cos, sin), extra_specs=(table_spec, table_spec))
    return out


def _mm_residual(name, u, w, x_in, gate):
    m_rows, n_cols = x_in.shape
    tm, tn = _tile(m_rows, 512), _tile(n_cols, 1024)
    blk = pl.BlockSpec((tm, tn), lambda i, j, k: (i, j))

    def epilogue(acc, extra_refs, out_refs):
        x_ref, gate_ref = extra_refs
        out_refs[0][...] = acc
        out_refs[1][...] = x_ref[...] + gate_ref[...] * acc

    y, x_out = _matmul(
        name, u, w, nt=False, tm=tm, tn=tn, n_cols=n_cols,
        out_shape=[jax.ShapeDtypeStruct((m_rows, n_cols), F32)] * 2, out_specs=[blk, blk],
        epilogue=epilogue, fixed_bytes=3 * 8 * tm * tn, extra=(x_in, gate),
        extra_specs=(blk, pl.BlockSpec((1, tn), lambda i, j, k: (0, j))))
    return y, x_out


def _mm_gate_bwd(name, dy, w_out, z_src, z_col0, o):
    m_rows = dy.shape[0]
    n_cols = w_out.shape[0]
    tm, tn = _tile(m_rows, 1024), _tile(n_cols, 1024)
    assert z_col0 % tn == 0 and tn % HEAD_DIM == 0 and n_cols // HEAD_DIM <= LANE
    z_blk0 = z_col0 // tn
    blk = pl.BlockSpec((tm, tn), lambda i, j, k: (i, j))

    def epilogue(du, extra_refs, out_refs):
        z_ref, o_ref = extra_refs
        do_ref, dz_ref, delta_ref = out_refs
        j = pl.program_id(1)
        z = z_ref[...].astype(F32)
        o_val = o_ref[...].astype(F32)
        sig = _sigmoid(z)
        d_o = (du * (z * sig)).astype(BF16)
        do_ref[...] = d_o
        dz_ref[...] = (du * o_val * (sig * (1.0 + z * (1.0 - sig)))).astype(BF16)

        @pl.when(j == 0)
        def _():
            delta_ref[...] = jnp.zeros_like(delta_ref)

        prod = d_o.astype(F32) * o_val
        lane = _iota((tm, LANE), 1)
        delta = delta_ref[...]
        for head in range(tn // HEAD_DIM):
            rows = jnp.sum(prod[:, head * HEAD_DIM:(head + 1) * HEAD_DIM], axis=1, keepdims=True)
            delta = delta + jnp.where(lane == j * (tn // HEAD_DIM) + head, rows, 0.0)
        delta_ref[...] = delta

    d_o, dz, delta = _matmul(
        name, dy, w_out, nt=True, tm=tm, tn=tn, n_cols=n_cols,
        out_shape=[jax.ShapeDtypeStruct((m_rows, n_cols), BF16)] * 2
        + [jax.ShapeDtypeStruct((m_rows, LANE), F32)],
        out_specs=[blk, blk, pl.BlockSpec((tm, LANE), lambda i, j, k: (i, 0))],
        epilogue=epilogue, fixed_bytes=4 * 4 * tm * tn + 8 * tm * LANE, extra=(z_src, o),
        extra_specs=(pl.BlockSpec((tm, tn), lambda i, j, k: (i, z_blk0 + j)), blk))
    return d_o, dz, delta


def _norm_fwd(name, x, ga, sa, ta, gb=None):
    s_len, d = x.shape
    tr = _tile(s_len, 256)
    two = gb is not None
    row = pl.BlockSpec((tr, d), lambda i: (i, 0))
    vec = pl.BlockSpec((1, d), lambda i: (0, 0))

    def body(x_ref, ga_ref, sa_ref, ta_ref, *rest):
        xv = x_ref[...]
        y = xv * lax.rsqrt(jnp.mean(xv * xv, axis=-1, keepdims=True) + RMS_EPS)
        rest[-2 if two else -1][...] = ((y * ga_ref[...]) * (1.0 + sa_ref[...]) + ta_ref[...]).astype(BF16)
        if two:
            rest[-1][...] = (y * rest[0][...]).astype(BF16)

    ins = [x, ga, sa, ta] + ([gb] if two else [])
    outs = pl.pallas_call(
        body, name=name, grid=(s_len // tr,),
        in_specs=[row] + [vec] * (len(ins) - 1),
        out_specs=[row] * (2 if two else 1),
        out_shape=[jax.ShapeDtypeStruct((s_len, d), BF16)] * (2 if two else 1),
        compiler_params=_params(1))(*ins)
    return outs if two else outs[0]


def _loss_bwd(x2, target, y1, g_final, gate1):
    s_len, d = x2.shape
    tr = _tile(s_len, 128)
    row = pl.BlockSpec((tr, d), lambda i: (i, 0))
    vec = pl.BlockSpec((1, d), lambda i: (0, 0))

    def body(x_ref, t_ref, y_ref, g_ref, gate_ref, loss_ref, dx_ref, dy_ref, sums_ref):
        @pl.when(pl.program_id(0) == 0)
        def _():
            loss_ref[...] = jnp.zeros_like(loss_ref)
            sums_ref[...] = jnp.zeros_like(sums_ref)

        xv = x_ref[...]
        rstd = lax.rsqrt(jnp.mean(xv * xv, axis=-1, keepdims=True) + RMS_EPS)
        xhat = xv * rstd
        g = g_ref[...]
        err = xhat * g - t_ref[...]
        sq = jnp.sum(jnp.sum(err * err, axis=1, keepdims=True), axis=0, keepdims=True)
        loss_ref[...] += sq * (0.5 / d)
        dout = err * (1.0 / d)
        dxhat = dout * g
        dx = rstd * (dxhat - xhat * jnp.mean(dxhat * xhat, axis=-1, keepdims=True))
        dx_ref[...] = dx
        dy_ref[...] = (dx * gate_ref[...]).astype(BF16)
        sums_ref[0:1, :] += jnp.sum(dout * xhat, axis=0, keepdims=True)
        sums_ref[1:2, :] += jnp.sum(dx * y_ref[...], axis=0, keepdims=True)

    return pl.pallas_call(
        body, name="loss_bwd", grid=(s_len // tr,),
        in_specs=[row, row, row, vec, vec],
        out_specs=[pl.BlockSpec((SUBLANE, LANE), lambda i: (0, 0)), row, row,
                   pl.BlockSpec((SUBLANE, d), lambda i: (0, 0))],
        out_shape=[jax.ShapeDtypeStruct((SUBLANE, LANE), F32),
                   jax.ShapeDtypeStruct((s_len, d), F32),
                   jax.ShapeDtypeStruct((s_len, d), BF16),
                   jax.ShapeDtypeStruct((SUBLANE, d), F32)],
        compiler_params=_params(1))(x2, target, y1, g_final, gate1)


def _norm_bwd(name, x, dres, dha, ga, sa, dhb=None, gb=None, y=None, gate=None):
    s_len, d = x.shape
    tr = _tile(s_len, 128)
    has_b, has_y = dhb is not None, y is not None
    row = pl.BlockSpec((tr, d), lambda i: (i, 0))
    vec = pl.BlockSpec((1, d), lambda i: (0, 0))
    ins, specs = [x, dres, dha, ga, sa], [row, row, row, vec, vec]
    if has_b:
        ins += [dhb, gb]
        specs += [row, vec]
    if has_y:
        ins += [y, gate]
        specs += [row, vec]
    n_in = len(ins)

    def body(*refs):
        x_ref, dres_ref, dha_ref, ga_ref, sa_ref = refs[:5]
        pos = 5
        if has_b:
            dhb_ref, gb_ref = refs[pos:pos + 2]
            pos += 2
        if has_y:
            y_ref, gate_ref = refs[pos:pos + 2]
        outs = refs[n_in:]
        dx_ref, sums_ref = outs[0], outs[-1]

        @pl.when(pl.program_id(0) == 0)
        def _():
            sums_ref[...] = jnp.zeros_like(sums_ref)

        xv = x_ref[...]
        rstd = lax.rsqrt(jnp.mean(xv * xv, axis=-1, keepdims=True) + RMS_EPS)
        xhat = xv * rstd
        dha_v = dha_ref[...]
        ga_v, sa_v = ga_ref[...], sa_ref[...]
        dxhat = dha_v * (ga_v * (1.0 + sa_v))
        sums_ref[0:1, :] += jnp.sum(dha_v, axis=0, keepdims=True)
        sums_ref[1:2, :] += jnp.sum(dha_v * (xhat * ga_v), axis=0, keepdims=True)
        sums_ref[2:3, :] += jnp.sum(dha_v * ((1.0 + sa_v) * xhat), axis=0, keepdims=True)
        if has_b:
            dhb_v = dhb_ref[...]
            dxhat = dxhat + dhb_v * gb_ref[...]
            sums_ref[3:4, :] += jnp.sum(dhb_v * xhat, axis=0, keepdims=True)
        dx = dres_ref[...] + rstd * (dxhat - xhat * jnp.mean(dxhat * xhat, axis=-1, keepdims=True))
        dx_ref[...] = dx
        if has_y:
            outs[1][...] = (dx * gate_ref[...]).astype(BF16)
            sums_ref[4:5, :] += jnp.sum(dx * y_ref[...], axis=0, keepdims=True)

    out_shape = [jax.ShapeDtypeStruct((s_len, d), F32)]
    out_specs = [row]
    if has_y:
        out_shape.append(jax.ShapeDtypeStruct((s_len, d), BF16))
        out_specs.append(row)
    out_shape.append(jax.ShapeDtypeStruct((SUBLANE, d), F32))
    out_specs.append(pl.BlockSpec((SUBLANE, d), lambda i: (0, 0)))
    return pl.pallas_call(body, name=name, grid=(s_len // tr,), in_specs=specs,
                          out_specs=out_specs, out_shape=out_shape,
                          compiler_params=_params(1))(*ins)


def _fgate_fwd(f_raw, bias_row):
    s_len = f_raw.shape[0]
    tb = _tile(s_len, 512)
    blk = pl.BlockSpec((tb, LANE), lambda t: (t, 0))

    def body(f_ref, b_ref, out_ref, carry):
        @pl.when(pl.program_id(0) == 0)
        def _():
            carry[...] = jnp.zeros_like(carry)

        u = f_ref[...] + b_ref[...]
        logf = jnp.minimum(u, 0.0) - jnp.log1p(jnp.exp(-jnp.abs(u)))
        tri = (_iota((tb, tb), 1) <= _iota((tb, tb), 0)).astype(F32)
        run = jnp.dot(tri, logf, precision=HIGHEST, preferred_element_type=F32) + carry[...]
        out_ref[...] = run
        carry[...] = run[tb - 1:tb, :]

    return pl.pallas_call(
        body, name="fgate_fwd", grid=(s_len // tb,),
        in_specs=[blk, pl.BlockSpec((1, LANE), lambda t: (0, 0))], out_specs=blk,
        out_shape=jax.ShapeDtypeStruct((s_len, LANE), F32),
        scratch_shapes=[pltpu.VMEM((1, LANE), F32)],
        compiler_params=_params(1))(f_raw, bias_row)


def _fgate_bwd(df_a, df_b, f_raw, bias_row):
    s_len = f_raw.shape[0]
    tb = _tile(s_len, 512)
    nb = s_len // tb
    blk = pl.BlockSpec((tb, LANE), lambda t: (nb - 1 - t, 0))

    def body(a_ref, b2_ref, f_ref, b_ref, df_ref, sums_ref, carry):
        @pl.when(pl.program_id(0) == 0)
        def _():
            carry[...] = jnp.zeros_like(carry)
            sums_ref[...] = jnp.zeros_like(sums_ref)

        d_run = a_ref[...] + b2_ref[...]
        tri = (_iota((tb, tb), 1) >= _iota((tb, tb), 0)).astype(F32)
        dlogf = jnp.dot(tri, d_run, precision=HIGHEST, preferred_element_type=F32) + carry[...]
        carry[...] = dlogf[0:1, :]
        u = f_ref[...] + b_ref[...]
        df = dlogf * _sigmoid(-u)
        df_ref[...] = df.astype(BF16)
        sums_ref[...] += jnp.sum(df, axis=0, keepdims=True)

    return pl.pallas_call(
        body, name="fgate_bwd", grid=(nb,),
        in_specs=[blk, blk, blk, pl.BlockSpec((1, LANE), lambda t: (0, 0))],
        out_specs=[blk, pl.BlockSpec((SUBLANE, LANE), lambda t: (0, 0))],
        out_shape=[jax.ShapeDtypeStruct((s_len, LANE), BF16),
                   jax.ShapeDtypeStruct((SUBLANE, LANE), F32)],
        scratch_shapes=[pltpu.VMEM((1, LANE), F32)],
        compiler_params=_params(1))(df_a, df_b, f_raw, bias_row)


def _fox_fwd(proj, v_t, f_nat, f_t, heads):
    s_len = proj.shape[0]
    d = heads * HEAD_DIM
    t = _tile(s_len, 512)
    nq = s_len // t
    scale = HEAD_DIM ** -0.5

    def body(k_ref, q_ref, vt_ref, fn_ref, ft_ref, ot_ref, lse_ref,
             acc_scr, m_scr, l_scr, fk_scr):
        j, i = pl.program_id(1), pl.program_id(2)
        h = pl.program_id(0)

        @pl.when((j == 0) & (i == 0))
        def _():
            m_scr[...] = jnp.full_like(m_scr, NEG)
            l_scr[...] = jnp.zeros_like(l_scr)
            acc_scr[...] = jnp.zeros_like(acc_scr)

        def step(diagonal):
            s_t = _dot_nt(k_ref[...], q_ref[...]) * scale + (ft_ref[...] - fk_scr[...])
            if diagonal:
                s_t = jnp.where(_iota((t, t), 0) <= _iota((t, t), 1), s_t, NEG)
            m_prev = m_scr[i]
            m_new = jnp.maximum(m_prev, jnp.max(s_t, axis=0, keepdims=True))
            alpha = jnp.exp(m_prev - m_new)
            p_t = jnp.exp(s_t - m_new)
            l_scr[i] = alpha * l_scr[i] + jnp.sum(p_t, axis=0, keepdims=True)
            acc_scr[i] = alpha * acc_scr[i] + _dot(vt_ref[...], p_t.astype(BF16))
            m_scr[i] = m_new

        @pl.when(i == j)
        def _():
            fk_scr[...] = _pick_lane(fn_ref[...], h)
            step(True)

        @pl.when(i > j)
        def _():
            step(False)

        @pl.when((i == nq - 1) & (j == nq - 1))
        def _():
            for blk in range(nq):
                cols = slice(blk * t, (blk + 1) * t)
                l_sum = l_scr[blk]
                ot_ref[:, cols] = acc_scr[blk] / l_sum
                lse_ref[:, cols] = m_scr[blk] + jnp.log(l_sum)

    qry = pl.BlockSpec((t, HEAD_DIM), lambda h, j, i: (jnp.maximum(i, j), h))
    return pl.pallas_call(
        body, name="fox_fwd", grid=(heads, nq, nq),
        in_specs=[pl.BlockSpec((t, HEAD_DIM), lambda h, j, i: (j, heads + h)), qry,
                  pl.BlockSpec((HEAD_DIM, t), lambda h, j, i: (h, j)),
                  pl.BlockSpec((t, LANE), lambda h, j, i: (j, 0)),
                  pl.BlockSpec((None, 1, t), lambda h, j, i: (h, 0, jnp.maximum(i, j)))],
        out_specs=[pl.BlockSpec((HEAD_DIM, s_len), lambda h, j, i: (h, 0)),
                   pl.BlockSpec((None, 1, s_len), lambda h, j, i: (h, 0, 0))],
        out_shape=[jax.ShapeDtypeStruct((d, s_len), F32),
                   jax.ShapeDtypeStruct((heads, 1, s_len), F32)],
        scratch_shapes=[pltpu.VMEM((nq, HEAD_DIM, t), F32), pltpu.VMEM((nq, 1, t), F32),
                        pltpu.VMEM((nq, 1, t), F32), pltpu.VMEM((t, 1), F32)],
        compiler_params=_params(3))(proj, proj, v_t, f_nat, f_t)


def _gate_fwd(o_t, proj, z_blk0, heads):
    d, s_len = o_t.shape
    t = _tile(s_len, 512)

    def body(ot_ref, z_ref, o_ref, u_ref):
        o_val = jnp.transpose(ot_ref[...])
        o_ref[...] = o_val.astype(BF16)
        z = z_ref[...].astype(F32)
        u_ref[...] = (o_val * (z * _sigmoid(z))).astype(BF16)

    out_blk = pl.BlockSpec((t, HEAD_DIM), lambda i, h: (i, h))
    return pl.pallas_call(
        body, name="gate_fwd", grid=(s_len // t, heads),
        in_specs=[pl.BlockSpec((HEAD_DIM, t), lambda i, h: (h, i)),
                  pl.BlockSpec((t, HEAD_DIM), lambda i, h: (i, z_blk0 + h))],
        out_specs=[out_blk, out_blk],
        out_shape=[jax.ShapeDtypeStruct((s_len, d), BF16)] * 2,
        compiler_params=_params(2))(o_t, proj)


def _fox_bwd(proj, k_t, d_o, f_nat, f_t, lse_t, delta_t, heads):
    s_len = proj.shape[0]
    d = heads * HEAD_DIM
    t = _tile(s_len, 512)
    nq = s_len // t
    scale = HEAD_DIM ** -0.5

    def body(k_ref, v_ref, kt_ref, q_ref, do_ref, fn_ref, ft_ref, lse_ref, delta_ref,
             dk_ref, dv_ref, dqt_ref, dfq_ref, dfk_ref,
             dk_acc, dv_acc, dq_acc, dfq_acc, dfk_acc, fk_scr):
        h, j, i = pl.program_id(0), pl.program_id(1), pl.program_id(2)
        head_start = (j == 0) & (i == 0)

        @pl.when(head_start)
        def _():
            dq_acc[...] = jnp.zeros_like(dq_acc)
            dfq_acc[...] = jnp.zeros_like(dfq_acc)

        @pl.when(head_start & (h == 0))
        def _():
            dfk_ref[...] = jnp.zeros_like(dfk_ref)

        def step(diagonal):
            q = q_ref[...]
            d_out = do_ref[...]
            s_t = _dot_nt(k_ref[...], q) * scale + (ft_ref[...] - fk_scr[...]) - lse_ref[...]
            if diagonal:
                s_t = jnp.where(_iota((t, t), 0) <= _iota((t, t), 1), s_t, NEG)
            p_t = jnp.exp(s_t)
            dp_t = _dot_nt(v_ref[...], d_out)
            ds_t = p_t * (dp_t - delta_ref[...])
            ds_b = ds_t.astype(BF16)
            dv_acc[...] += _dot(p_t.astype(BF16), d_out)
            dk_acc[...] += _dot(ds_b, q)
            dq_acc[i] += _dot(kt_ref[...], ds_b)
            dfq_acc[i] += jnp.sum(ds_t, axis=0, keepdims=True)
            dfk_acc[...] += jnp.sum(ds_t, axis=1, keepdims=True)

        @pl.when(i == j)
        def _():
            dk_acc[...] = jnp.zeros_like(dk_acc)
            dv_acc[...] = jnp.zeros_like(dv_acc)
            dfk_acc[...] = jnp.zeros_like(dfk_acc)
            fk_scr[...] = _pick_lane(fn_ref[...], h)
            step(True)

        @pl.when(i > j)
        def _():
            step(False)

        @pl.when(i == nq - 1)
        def _():
            dk_ref[...] = (dk_acc[...] * scale).astype(BF16)
            dv_ref[...] = dv_acc[...].astype(BF16)
            rows = pl.ds(pl.multiple_of(j * t, t), t)
            dfk_ref[rows, :] += jnp.where(_iota((t, LANE), 1) == h, -dfk_acc[...], 0.0)

        @pl.when((i == nq - 1) & (j == nq - 1))
        def _():
            for blk in range(nq):
                cols = slice(blk * t, (blk + 1) * t)
                dqt_ref[:, cols] = (dq_acc[blk] * scale).astype(BF16)
                dfq_ref[:, cols] = dfq_acc[blk]

    key_col = lambda base: pl.BlockSpec((t, HEAD_DIM), lambda h, j, i: (j, base + h))
    qry = pl.BlockSpec((t, HEAD_DIM), lambda h, j, i: (jnp.maximum(i, j), h))
    qry_row = pl.BlockSpec((None, 1, t), lambda h, j, i: (h, 0, jnp.maximum(i, j)))
    kv_out = pl.BlockSpec((t, HEAD_DIM), lambda h, j, i: (j, h))
    return pl.pallas_call(
        body, name="fox_bwd", grid=(heads, nq, nq),
        in_specs=[key_col(heads), key_col(2 * heads),
                  pl.BlockSpec((HEAD_DIM, t), lambda h, j, i: (h, j)),
                  qry, qry, pl.BlockSpec((t, LANE), lambda h, j, i: (j, 0)),
                  qry_row, qry_row, qry_row],
        out_specs=[kv_out, kv_out,
                   pl.BlockSpec((HEAD_DIM, s_len), lambda h, j, i: (h, 0)),
                   pl.BlockSpec((None, 1, s_len), lambda h, j, i: (h, 0, 0)),
                   pl.BlockSpec((s_len, LANE), lambda h, j, i: (0, 0))],
        out_shape=[jax.ShapeDtypeStruct((s_len, d), BF16), jax.ShapeDtypeStruct((s_len, d), BF16),
                   jax.ShapeDtypeStruct((d, s_len), BF16),
                   jax.ShapeDtypeStruct((heads, 1, s_len), F32),
                   jax.ShapeDtypeStruct((s_len, LANE), F32)],
        scratch_shapes=[pltpu.VMEM((t, HEAD_DIM), F32), pltpu.VMEM((t, HEAD_DIM), F32),
                        pltpu.VMEM((nq, HEAD_DIM, t), F32), pltpu.VMEM((nq, 1, t), F32),
                        pltpu.VMEM((t, 1), F32), pltpu.VMEM((t, 1), F32)],
        compiler_params=_params(3))(proj, proj, k_t, proj, d_o, f_nat, f_t, lse_t, delta_t)


def _swa_specs(heads, kv_heads):
    width = heads // kv_heads * HEAD_DIM
    wide = lambda base: pl.BlockSpec((SWA_BLOCK, width), lambda n, g: (n, base + g))
    blk = lambda fn: pl.BlockSpec((SWA_BLOCK, HEAD_DIM), fn)
    prev = lambda base: blk(lambda n, g: (jnp.maximum(n - 1, 0), base + g))
    cur = lambda base: blk(lambda n, g: (n, base + g))
    return wide, prev, cur


def _swa_scores(q, kp, kc, n, scale):
    r, c = _iota((SWA_BLOCK, SWA_BLOCK), 0), _iota((SWA_BLOCK, SWA_BLOCK), 1)
    sp = jnp.where((c > r) & (n > 0), _dot_nt(q, kp) * scale, NEG)
    sc = jnp.where(c <= r, _dot_nt(q, kc) * scale, NEG)
    return sp, sc


def _swa_fwd(proj, kv, sinks_row, heads, kv_heads):
    s_len = proj.shape[0]
    d = heads * HEAD_DIM
    scale = HEAD_DIM ** -0.5
    group = heads // kv_heads
    wide, prev, cur = _swa_specs(heads, kv_heads)

    def body(q_ref, z_ref, kp_ref, kc_ref, vp_ref, vc_ref, sink_ref, o_ref, u_ref, lse_ref):
        n, g = pl.program_id(0), pl.program_id(1)
        kp, kc, vp, vc = kp_ref[...], kc_ref[...], vp_ref[...], vc_ref[...]
        lane = _iota((SWA_BLOCK, LANE), 1)
        lse_all = jnp.zeros((SWA_BLOCK, LANE), F32)
        for hh in range(group):
            cols = slice(hh * HEAD_DIM, (hh + 1) * HEAD_DIM)
            head = g * group + hh
            sp, sc = _swa_scores(q_ref[:, cols], kp, kc, n, scale)
            sink = _pick_lane(sink_ref[...], head)
            m = jnp.maximum(jnp.maximum(jnp.max(sp, axis=1, keepdims=True),
                                        jnp.max(sc, axis=1, keepdims=True)), sink)
            pp, pc = jnp.exp(sp - m), jnp.exp(sc - m)
            den = (jnp.sum(pp, axis=1, keepdims=True) + jnp.sum(pc, axis=1, keepdims=True)
                   + jnp.exp(sink - m))
            o_val = (_dot(pp.astype(BF16), vp) + _dot(pc.astype(BF16), vc)) / den
            o_ref[:, cols] = o_val.astype(BF16)
            z = z_ref[:, cols].astype(F32)
            u_ref[:, cols] = (o_val * (z * _sigmoid(z))).astype(BF16)
            lse_all = lse_all + jnp.where(lane == head, m + jnp.log(den), 0.0)

        @pl.when(g == 0)
        def _():
            lse_ref[...] = lse_all

        @pl.when(g > 0)
        def _():
            lse_ref[...] += lse_all

    nat = pl.BlockSpec((SWA_BLOCK, LANE), lambda n, g: (n, 0))
    return pl.pallas_call(
        body, name="swa_fwd", grid=(s_len // SWA_BLOCK, kv_heads),
        in_specs=[wide(0), wide(kv_heads), prev(0), cur(0), prev(kv_heads), cur(kv_heads),
                  pl.BlockSpec((1, LANE), lambda n, g: (0, 0))],
        out_specs=[wide(0), wide(0), nat],
        out_shape=[jax.ShapeDtypeStruct((s_len, d), BF16), jax.ShapeDtypeStruct((s_len, d), BF16),
                   jax.ShapeDtypeStruct((s_len, LANE), F32)],
        compiler_params=_params(2))(proj, proj, kv, kv, kv, kv, sinks_row)


def _swa_bwd_q(proj, kv, d_o, lse, delta, sinks_row, cos, sin, heads, kv_heads):
    s_len = proj.shape[0]
    d = heads * HEAD_DIM
    scale = HEAD_DIM ** -0.5
    group = heads // kv_heads
    wide, prev, cur = _swa_specs(heads, kv_heads)

    def body(q_ref, kp_ref, kc_ref, vp_ref, vc_ref, do_ref, lse_ref, delta_ref, sink_ref,
             cos_ref, sin_ref, dq_ref, dsink_ref):
        n, g = pl.program_id(0), pl.program_id(1)

        @pl.when((n == 0) & (g == 0))
        def _():
            dsink_ref[...] = jnp.zeros_like(dsink_ref)

        kp, kc, vp, vc = kp_ref[...], kc_ref[...], vp_ref[...], vc_ref[...]
        lse_blk, delta_blk = lse_ref[...], delta_ref[...]
        cos_v, sin_v = cos_ref[...], sin_ref[...]
        lane = _iota((SUBLANE, LANE), 1)
        dsink_all = jnp.zeros((SUBLANE, LANE), F32)
        for hh in range(group):
            cols = slice(hh * HEAD_DIM, (hh + 1) * HEAD_DIM)
            head = g * group + hh
            sp, sc = _swa_scores(q_ref[:, cols], kp, kc, n, scale)
            lse_col = _pick_lane(lse_blk, head)
            delta_col = _pick_lane(delta_blk, head)
            pp, pc = jnp.exp(sp - lse_col), jnp.exp(sc - lse_col)
            p_sink = jnp.exp(_pick_lane(sink_ref[...], head) - lse_col)
            d_out = do_ref[:, cols]
            dsp = pp * (_dot_nt(d_out, vp) - delta_col)
            dsc = pc * (_dot_nt(d_out, vc) - delta_col)
            dq = (_dot(dsp.astype(BF16), kp) + _dot(dsc.astype(BF16), kc)) * scale
            dq_ref[:, cols] = (dq * cos_v - pltpu.roll(dq, HEAD_DIM // 2, 1) * sin_v).astype(BF16)
            d_sink = jnp.sum(-p_sink * delta_col, axis=0, keepdims=True)
            dsink_all = dsink_all + jnp.where(lane == head, d_sink, 0.0)
        dsink_ref[...] += dsink_all

    own = wide(0)
    nat = pl.BlockSpec((SWA_BLOCK, LANE), lambda n, g: (n, 0))
    return pl.pallas_call(
        body, name="swa_bwd_q", grid=(s_len // SWA_BLOCK, kv_heads),
        in_specs=[own, prev(0), cur(0), prev(kv_heads), cur(kv_heads), own, nat, nat,
                  pl.BlockSpec((1, LANE), lambda n, g: (0, 0)), nat, nat],
        out_specs=[own, pl.BlockSpec((SUBLANE, LANE), lambda n, g: (0, 0))],
        out_shape=[jax.ShapeDtypeStruct((s_len, d), BF16),
                   jax.ShapeDtypeStruct((SUBLANE, LANE), F32)],
        compiler_params=_params(2))(proj, kv, kv, kv, kv, d_o, lse, delta, sinks_row, cos, sin)


def _swa_bwd_kv(proj, kv, d_o, lse_t, delta_t, cos, sin, heads, kv_heads):
    s_len = proj.shape[0]
    nb = s_len // SWA_BLOCK
    group = heads // kv_heads
    scale = HEAD_DIM ** -0.5

    def body(k_ref, v_ref, qm_ref, qn_ref, dom_ref, don_ref, lsem_ref, lsen_ref,
             deltam_ref, deltan_ref, cos_ref, sin_ref, dk_ref, dv_ref):
        m = pl.program_id(1)
        k, v = k_ref[...], v_ref[...]
        key, qry = _iota((SWA_BLOCK, SWA_BLOCK), 0), _iota((SWA_BLOCK, SWA_BLOCK), 1)
        own_valid = key <= qry
        next_valid = (key > qry) & (m + 1 < nb)
        dk = jnp.zeros((SWA_BLOCK, HEAD_DIM), F32)
        dv = jnp.zeros((SWA_BLOCK, HEAD_DIM), F32)
        for hh in range(group):
            cols = slice(hh * HEAD_DIM, (hh + 1) * HEAD_DIM)
            for q_ref, do_ref, lse_ref, delta_ref, valid in (
                    (qm_ref, dom_ref, lsem_ref, deltam_ref, own_valid),
                    (qn_ref, don_ref, lsen_ref, deltan_ref, next_valid)):
                q, d_out = q_ref[:, cols], do_ref[:, cols]
                s_t = _dot_nt(k, q) * scale
                p_t = jnp.exp(jnp.where(valid, s_t - lse_ref[hh], NEG))
                ds_t = p_t * (_dot_nt(v, d_out) - delta_ref[hh])
                dv = dv + _dot(p_t.astype(BF16), d_out)
                dk = dk + _dot(ds_t.astype(BF16), q)
        dk = dk * scale
        dk_ref[...] = (dk * cos_ref[...]
                       - pltpu.roll(dk, HEAD_DIM // 2, 1) * sin_ref[...]).astype(BF16)
        dv_ref[...] = dv.astype(BF16)

    blk = lambda fn: pl.BlockSpec((SWA_BLOCK, HEAD_DIM), fn)
    nxt = lambda m: jnp.minimum(m + 1, nb - 1)
    wide = lambda fn: pl.BlockSpec((SWA_BLOCK, group * HEAD_DIM), fn)
    rows = lambda fn: pl.BlockSpec((group, 1, SWA_BLOCK), fn)
    q_m, q_n = wide(lambda g, m: (m, g)), wide(lambda g, m: (nxt(m), g))
    r_m, r_n = rows(lambda g, m: (g, 0, m)), rows(lambda g, m: (g, 0, nxt(m)))
    nat = pl.BlockSpec((SWA_BLOCK, LANE), lambda g, m: (m, 0))
    out_blk = blk(lambda g, m: (m, g))
    width = kv_heads * HEAD_DIM
    return pl.pallas_call(
        body, name="swa_bwd_kv", grid=(kv_heads, nb),
        in_specs=[blk(lambda g, m: (m, g)), blk(lambda g, m: (m, kv_heads + g)),
                  q_m, q_n, q_m, q_n, r_m, r_n, r_m, r_n, nat, nat],
        out_specs=[out_blk, out_blk],
        out_shape=[jax.ShapeDtypeStruct((s_len, width), BF16)] * 2,
        compiler_params=_params(2))(kv, kv, proj, proj, d_o, d_o, lse_t, lse_t,
                                    delta_t, delta_t, cos, sin)


def _ada_fwd(c_rows, ada_w, bias_loc):
    n_layers, d, cols = ada_w.shape
    rows = c_rows.shape[0]
    tk = _tile(d, 512)
    n_k = d // tk

    def body(c_ref, w_ref, b_ref, mod_ref, sc_ref, acc_ref):
        k = pl.program_id(1)

        @pl.when(k == 0)
        def _():
            acc_ref[...] = jnp.zeros_like(acc_ref)

        cv = c_ref[...]
        sc = cv * _sigmoid(cv)
        sc_ref[...] = sc
        acc_ref[...] += _dot(sc.astype(BF16), w_ref[...].astype(BF16))

        @pl.when(k == n_k - 1)
        def _():
            mod_ref[...] = acc_ref[...] + b_ref[...]

    return pl.pallas_call(
        body, name="ada_fwd", grid=(n_layers, n_k),
        in_specs=[pl.BlockSpec((rows, tk), lambda l, k: (0, k)),
                  pl.BlockSpec((None, tk, cols), lambda l, k: (l, k, 0)),
                  pl.BlockSpec((None, 1, cols), lambda l, k: (l, 0, 0))],
        out_specs=[pl.BlockSpec((None, rows, cols), lambda l, k: (l, 0, 0)),
                   pl.BlockSpec((None, rows, tk), lambda l, k: (l, 0, k))],
        out_shape=[jax.ShapeDtypeStruct((n_layers, rows, cols), F32),
                   jax.ShapeDtypeStruct((n_layers, rows, d), F32)],
        scratch_shapes=[pltpu.VMEM((rows, cols), F32)],
        compiler_params=_params(2))(c_rows, ada_w, bias_loc)


def _ada_update(sc_t, dmod, w, m, v):
    n_layers, d, cols = w.shape
    tr = _tile(d, 256)
    big = pl.BlockSpec((None, tr, cols), lambda l, i: (l, i, 0))

    def body(sc_ref, dm_ref, w_ref, m_ref, v_ref, g_out, d_out, m_out, v_out):
        g = _dot(sc_ref[...], dm_ref[...])
        delta, m_new, v_new = _adamw(w_ref[...], g, m_ref[...], v_ref[...])
        g_out[...] = g
        d_out[...] = delta
        m_out[...] = m_new
        v_out[...] = v_new

    return pl.pallas_call(
        body, name="ada_update", grid=(n_layers, d // tr),
        in_specs=[pl.BlockSpec((tr, LANE), lambda l, i: (i, 0)),
                  pl.BlockSpec((None, LANE, cols), lambda l, i: (l, 0, 0)), big, big, big],
        out_specs=[big] * 4, out_shape=[jax.ShapeDtypeStruct(w.shape, F32)] * 4,
        compiler_params=_params(2))(sc_t, dmod, w, m, v)


def _shard_update(name, slabs, w, m, v):
    rows, cols = w.shape
    bytes_per_row = 2 * cols * (2 * N_DEV + 4 * 7)
    tr = SUBLANE * 2
    while tr * 2 <= rows and rows % (tr * 2) == 0 and tr * 2 * bytes_per_row <= 24 * 2 ** 20:
        tr *= 2
    tr = _tile(rows, tr)
    blk = pl.BlockSpec((tr, cols), lambda i: (i, 0))

    def body(s_ref, w_ref, m_ref, v_ref, g_out, d_out, m_out, v_out):
        g = s_ref[0].astype(F32)
        for dev in range(1, N_DEV):
            g = g + s_ref[dev].astype(F32)
        delta, m_new, v_new = _adamw(w_ref[...], g, m_ref[...], v_ref[...])
        g_out[...] = g
        d_out[...] = delta
        m_out[...] = m_new
        v_out[...] = v_new

    return pl.pallas_call(
        body, name=name, grid=(rows // tr,),
        in_specs=[pl.BlockSpec((N_DEV, tr, cols), lambda i: (0, i, 0)), blk, blk, blk],
        out_specs=[blk] * 4, out_shape=[jax.ShapeDtypeStruct((rows, cols), F32)] * 4,
        compiler_params=_params(1))(slabs, w, m, v)


def _small_update(gathered, w, m, v):
    shape = jax.ShapeDtypeStruct(w.shape, F32)

    def body(g_ref, w_ref, m_ref, v_ref, g_out, d_out, m_out, v_out):
        g = g_ref[0]
        for dev in range(1, N_DEV):
            g = g + g_ref[dev]
        delta, m_new, v_new = _adamw(w_ref[...], g, m_ref[...], v_ref[...])
        g_out[...] = g
        d_out[...] = delta
        m_out[...] = m_new
        v_out[...] = v_new

    return pl.pallas_call(body, name="small_update", out_shape=[shape] * 4,
                          compiler_params=pltpu.CompilerParams(vmem_limit_bytes=VMEM_LIMIT),
                          )(gathered, w, m, v)


def _rope_tables(s_len):
    half = HEAD_DIM // 2
    inv = ROPE_THETA ** (-jnp.arange(half, dtype=F32) / half)
    ang = jnp.arange(s_len, dtype=F32)[:, None] * inv[None, :]
    cos, sin = jnp.cos(ang), jnp.sin(ang)
    return jnp.concatenate([cos, cos], axis=1), jnp.concatenate([-sin, sin], axis=1)


def _pad_lanes(a):
    return jnp.pad(a, ((0, 0), (0, LANE - a.shape[1])))


def _rows_of(nat, heads):
    return jnp.transpose(nat[:, :heads])[:, None, :]


def _pack(parts):
    tile = SUBLANE * LANE
    flat = []
    for p in parts:
        p = p.reshape(-1)
        flat.append(jnp.pad(p, (0, (-p.shape[0]) % tile)))
    return jnp.concatenate(flat).reshape(-1, LANE)


def _unpack(packed, shapes):
    tile = SUBLANE * LANE
    flat = packed.reshape(-1)
    out, pos = [], 0
    for shape in shapes:
        size = 1
        for dim in shape:
            size *= dim
        out.append(flat[pos:pos + size].reshape(shape))
        pos += size + (-size) % tile
    return out


def kernel(x, c, norm_g, ada_w, ada_b, a_w_in, a_b_f, a_w_out, kv_norm_g, kv_w, b_w_in, b_sinks, b_w_out, final_norm_g, loss_target, m_norm_g, m_ada_w, m_ada_b, m_a_w_in, m_a_b_f, m_a_w_out, m_kv_norm_g, m_kv_w, m_b_w_in, m_b_sinks, m_b_w_out, m_final_norm_g, v_norm_g, v_ada_w, v_ada_b, v_a_w_in, v_a_b_f, v_a_w_out, v_kv_norm_g, v_kv_w, v_b_w_in, v_b_sinks, v_b_w_out, v_final_norm_g):
    s_len, d = x.shape[1], x.shape[2]
    heads = d // HEAD_DIM
    kv_heads = kv_w.shape[1] // (2 * HEAD_DIM)
    kv_width = kv_heads * HEAD_DIM
    ada_cols = ada_w.shape[2]
    a_in_cols = a_w_in.shape[2]
    assert heads <= LANE and a_in_cols * N_DEV == 4 * d + heads
    me = _slot(_mesh_pos())
    x0 = x[0]
    target = loss_target[0]
    vec = lambda a: a.reshape(1, d)

    g_a_in, c_all = _all_gather("gather_a_w_in", [a_w_in[0].astype(BF16), c])
    later = [a_w_out[0].astype(BF16), kv_w.astype(BF16), b_w_in[0].astype(BF16),
             b_w_out[0].astype(BF16)]
    g_a_out, g_kv, w_b_in, g_b_out = _all_gather("gather_rest", _after(later, g_a_in),
                                                 sequencer_id=1)
    w_a_out = g_a_out.reshape(d, d)
    w_kv = g_kv.reshape(d, 2 * kv_width)
    w_b_out = g_b_out.reshape(d, d)
    w_a_full = jnp.transpose(g_a_in, (1, 0, 2)).reshape(d, N_DEV * a_in_cols)
    w_a_main = w_a_full[:, :4 * d]
    w_a_f = _pad_lanes(w_a_full[:, 4 * d:])
    b_in_cols = b_w_in.shape[2]

    c_rows = jnp.pad(c_all.reshape(N_DEV, d), ((0, 2 * SUBLANE - N_DEV), (0, 0)))
    bias_loc = lax.dynamic_slice_in_dim(ada_b, me * ada_cols, ada_cols, axis=1)[:, None, :]
    mod_part, sc_rows = _ada_fwd(c_rows, ada_w, bias_loc)
    (mod_recv,) = _all_to_all("exchange_mod", [jnp.transpose(mod_part[:, :N_DEV], (1, 0, 2))])
    mod = jnp.transpose(mod_recv, (1, 0, 2)).reshape(2, 3 * d)
    shift0, scale0, gate0 = vec(mod[0, :d]), vec(mod[0, d:2 * d]), vec(mod[0, 2 * d:])
    shift1, scale1, gate1 = vec(mod[1, :d]), vec(mod[1, d:2 * d]), vec(mod[1, 2 * d:])
    g0, g1, g_kvn, g_fin = vec(norm_g[0]), vec(norm_g[1]), vec(kv_norm_g), vec(final_norm_g)

    cos, sin = _rope_tables(s_len)
    bias_f = _pad_lanes(a_b_f)
    sinks_row = _pad_lanes(b_sinks)

    h0 = _norm_fwd("norm0", x0, g0, scale0, shift0)
    proj0 = _mm_plain("proj0", h0, w_a_main)
    f_raw = _mm_plain("proj0_f", h0, w_a_f, out_dtype=F32, tn=LANE)
    f_nat = _fgate_fwd(f_raw, bias_f)
    f_t = _rows_of(f_nat, heads)
    o0_t, lse0_t = _fox_fwd(proj0, jnp.transpose(proj0[:, 2 * d:3 * d]), f_nat, f_t, heads)
    o0, u0 = _gate_fwd(o0_t, proj0, 3 * heads, heads)
    y0, x1 = _mm_residual("out0", u0, w_a_out, x0, gate0)

    h1, hk = _norm_fwd("norm1", x1, g1, scale1, shift1, gb=g_kvn)
    kv = _mm_rope("kv_proj", hk, w_kv, cos, sin, n_cols=2 * kv_width, rope_cols=kv_width,
                  tn=kv_width)
    proj1 = _mm_rope("proj1", h1, w_b_in, cos, sin, n_cols=2 * d, rope_cols=d, tn=b_in_cols,
                     b_spec=lambda tk: pl.BlockSpec((None, tk, b_in_cols),
                                                    lambda i, j, k: (j, k, 0)))
    o1, u1, lse1 = _swa_fwd(proj1, kv, sinks_row, heads, kv_heads)
    y1, x2 = _mm_residual("out1", u1, w_b_out, x1, gate1)

    loss_part, dx2, dy1, sums_f = _loss_bwd(x2, target, y1, g_fin, gate1)

    do1, dz1, delta1 = _mm_gate_bwd("out1_bwd", dy1, w_b_out, proj1, d, o1)
    gw_b_out = _mm_plain("out1_wgrad", jnp.transpose(u1), dy1)
    do1 = _after(do1, gw_b_out)
    dq1, dsinks = _swa_bwd_q(proj1, kv, do1, lse1, delta1, sinks_row, cos, sin, heads, kv_heads)
    dk1, dv1 = _swa_bwd_kv(proj1, kv, do1, _rows_of(lse1, heads), _rows_of(delta1, heads),
                           cos, sin, heads, kv_heads)
    dproj1 = jnp.concatenate([dq1, dz1], axis=1)
    tk_b = _tile(2 * d, b_in_cols)
    dh1 = _mm_plain("proj1_bwd", dproj1, w_b_in, nt=True, n_cols=d, out_dtype=F32, tk=tk_b,
                    b_spec=lambda tk: pl.BlockSpec((None, _tile(d, 1024), tk),
                                                   lambda i, j, k: (k * tk // b_in_cols, j, 0)))
    gw_b_in = _mm_plain("proj1_wgrad", jnp.transpose(h1), dproj1, tn=b_in_cols,
                        out_3d=(N_DEV, b_in_cols))
    dkv = jnp.concatenate([dk1, dv1], axis=1)
    dhk = _mm_plain("kv_bwd", dkv, w_kv, nt=True, out_dtype=F32)
    gw_kv = _mm_plain("kv_wgrad", jnp.transpose(hk), dkv)
    dx1, dy0, sums1 = _norm_bwd("norm1_bwd", x1, dx2, dh1, g1, scale1,
                                dhb=_after(dhk, (gw_b_in, gw_kv)), gb=g_kvn, y=y0, gate=gate0)

    do0, dz0, delta0 = _mm_gate_bwd("out0_bwd", dy0, w_a_out, proj0, 3 * d, o0)
    gw_a_out = _mm_plain("out0_wgrad", jnp.transpose(u0), dy0)
    r_b_out, r_b_in, r_kv, r_a_out = _all_to_all(
        "scatter_grads_early",
        [gw_b_out.reshape(N_DEV, d // N_DEV, d), gw_b_in,
         gw_kv.reshape(N_DEV, d // N_DEV, 2 * kv_width),
         gw_a_out.reshape(N_DEV, d // N_DEV, d)], sequencer_id=2)
    k0_t = jnp.transpose(proj0[:, d:2 * d])
    dk0, dv0, dq0_t, dfq_t, dfk_nat = _fox_bwd(proj0, k0_t, _after(do0, gw_a_out), f_nat, f_t,
                                               lse0_t, _rows_of(delta0, heads), heads)
    r_b_out, r_b_in, r_kv, r_a_out = _after([r_b_out, r_b_in, r_kv, r_a_out], dk0)
    up_b_out = _shard_update("update_b_w_out", r_b_out, b_w_out[0], m_b_w_out[0], v_b_w_out[0])
    up_b_in = _shard_update("update_b_w_in", r_b_in, b_w_in[0], m_b_w_in[0], v_b_w_in[0])
    up_kv = _shard_update("update_kv_w", r_kv, kv_w, m_kv_w, v_kv_w)
    up_a_out = _shard_update("update_a_w_out", r_a_out, a_w_out[0], m_a_w_out[0], v_a_w_out[0])
    dk0 = _after(dk0, (up_b_out[0], up_b_in[0], up_kv[0], up_a_out[0]))
    dfq_nat = _pad_lanes(jnp.transpose(dfq_t[:, 0, :]))
    df, sums_bf = _fgate_bwd(dfq_nat, dfk_nat, f_raw, bias_f)
    dproj0 = jnp.concatenate([jnp.transpose(dq0_t), dk0, dv0, dz0], axis=1)
    h0_t = jnp.transpose(h0)
    gw_a_main = _mm_plain("proj0_wgrad", h0_t, dproj0)
    gw_a_f = _mm_plain("proj0_f_wgrad", h0_t, df, tn=LANE)
    gw_a_in = jnp.concatenate([gw_a_main, gw_a_f[:, :heads]], axis=1)
    slabs_a_in = jnp.transpose(gw_a_in.reshape(d, N_DEV, a_in_cols), (1, 0, 2))
    (r_a_in,) = _all_to_all("scatter_grads_a_w_in", [slabs_a_in], sequencer_id=3)
    dh0_f = _mm_plain("proj0_f_bwd", _after(df, slabs_a_in), w_a_f, nt=True, out_dtype=F32)
    dh0 = _mm_plain("proj0_bwd", dproj0, w_a_main, nt=True, out_dtype=F32, init=dh0_f)
    grad_x, sums0 = _norm_bwd("norm0_bwd", x0, dx1, dh0, g0, scale0)

    dmod = jnp.stack([jnp.concatenate([sums0[0], sums0[1], sums1[4]]),
                      jnp.concatenate([sums1[0], sums1[1], sums_f[1]])])
    small_shapes = [(2, 3 * d), (2, d), (1, heads), (d,), (1, heads), (d,), (1,)]
    small_grads = [dmod, jnp.stack([sums0[2], sums1[2]]), sums_bf[0:1, :heads], sums1[3],
                   dsinks[0:1, :heads], sums_f[0], loss_part[0, 0:1]]
    (small_all,) = _all_gather("gather_small", [_pack(small_grads)])
    zero = jnp.zeros((1,), F32)
    small = _small_update(
        small_all,
        _pack([ada_b, norm_g, a_b_f, kv_norm_g, b_sinks, final_norm_g, zero]),
        _pack([m_ada_b, m_norm_g, m_a_b_f, m_kv_norm_g, m_b_sinks, m_final_norm_g, zero]),
        _pack([v_ada_b, v_norm_g, v_a_b_f, v_kv_norm_g, v_b_sinks, v_final_norm_g, zero]))
    s_grad, s_delta, s_m, s_v = [_unpack(p, small_shapes) for p in small]
    loss = s_grad[6][0]

    dmod_all = small_all.reshape(N_DEV, -1)[:, :2 * 3 * d].reshape(N_DEV, 2, 3 * d)
    dmod_loc = lax.dynamic_slice_in_dim(dmod_all, me * ada_cols, ada_cols, axis=2)
    dmod_loc = jnp.pad(jnp.transpose(dmod_loc, (1, 0, 2)), ((0, 0), (0, LANE - N_DEV), (0, 0)))
    sc_t = jnp.pad(jnp.transpose(sc_rows[0, :N_DEV]), ((0, 0), (0, LANE - N_DEV)))
    up_ada = _ada_update(sc_t.astype(BF16), dmod_loc.astype(BF16), ada_w, m_ada_w, v_ada_w)

    r_a_in = _after(r_a_in, (up_ada[0], small[0]))
    up_a_in = _shard_update("update_a_w_in", r_a_in, a_w_in[0], m_a_w_in[0], v_a_w_in[0])

    lead = lambda a: a[None]
    per_kind = []
    for kind in range(4):
        sm = (s_grad, s_delta, s_m, s_v)[kind]
        per_kind.append([
            sm[1], up_ada[kind], sm[0], lead(up_a_in[kind]), sm[2], lead(up_a_out[kind]),
            sm[3], up_kv[kind], lead(up_b_in[kind]), sm[4], lead(up_b_out[kind]), sm[5]])
    return (loss, grad_x[None], *per_kind[0], *per_kind[1], *per_kind[2], *per_kind[3])
```

```python
import jax
import jax.numpy as jnp
from jax import lax
from jax.experimental import pallas as pl
from jax.experimental.pallas import tpu as pltpu
from jax.experimental.pallas import tpu_sc as plsc

F32 = jnp.float32
BF16 = jnp.bfloat16
LANE = 128
SUBLANE = 8
HEAD_DIM = 128
SWA_BLOCK = 128
N_DEV = 8
N_PEER = N_DEV - 1
RMS_EPS = 1e-6
ROPE_THETA = 10000.0
NEG = -1e30
VMEM_LIMIT = 56 * 2 ** 20
MM_RESERVE = 10 * 2 ** 20
MESH = pl.DeviceIdType.MESH
HIGHEST = lax.Precision.HIGHEST

ADAM_LR = 0.001
ADAM_B1 = 0.9
ADAM_B2 = 0.999
ADAM_EPS = 1e-08
ADAM_WD = 0.01
ADAM_STEP = 10


def _tile(dim, pref):
    return pref if dim % pref == 0 else dim


def _params(n_axes):
    return pltpu.CompilerParams(dimension_semantics=("arbitrary",) * n_axes,
                                vmem_limit_bytes=VMEM_LIMIT)


def _dot(a, b):
    return jnp.dot(a, b, preferred_element_type=F32)


def _dot_nt(a, b):
    return lax.dot_general(a, b, (((1,), (1,)), ((), ())), preferred_element_type=F32)


def _dot_tn(a, b):
    return lax.dot_general(a, b, (((0,), (0,)), ((), ())), preferred_element_type=F32)


def _sigmoid(z):
    return 1.0 / (1.0 + jnp.exp(-z))


def _iota(shape, dim):
    return lax.broadcasted_iota(jnp.int32, shape, dim)


def _pick_lane(block, lane_index):
    lane = _iota(block.shape, 1)
    return jnp.sum(jnp.where(lane == lane_index, block, 0.0), axis=1, keepdims=True)


def _adamw(w, g, m, v):
    m = ADAM_B1 * m + (1.0 - ADAM_B1) * g
    v = ADAM_B2 * v + (1.0 - ADAM_B2) * (g * g)
    m_hat = m / (1.0 - ADAM_B1 ** ADAM_STEP)
    v_hat = v / (1.0 - ADAM_B2 ** ADAM_STEP)
    delta = -ADAM_LR * (m_hat / (jnp.sqrt(v_hat) + ADAM_EPS) + ADAM_WD * w)
    return delta, m, v


def _mesh_pos():
    return lax.axis_index("x"), lax.axis_index("y"), lax.axis_index("c")


def _slot(pos):
    return 4 * pos[0] + 2 * pos[1] + pos[2]


def _handshake(peers):
    barrier = pltpu.get_barrier_semaphore()
    for peer in peers:
        pl.semaphore_signal(barrier, inc=1, device_id=peer, device_id_type=MESH)
    pl.semaphore_wait(barrier, len(peers))


def _launch(name, body, arrays, out_shape, sequencer_id):
    n = len(arrays)
    scratch = [pltpu.SemaphoreType.DMA((N_PEER * n,)), pltpu.SemaphoreType.DMA((N_PEER * n,)),
               pltpu.SemaphoreType.DMA((n,))]
    if sequencer_id is None:
        any_spec = pl.BlockSpec(memory_space=pl.ANY)
        return pl.pallas_call(body, name=name, out_shape=out_shape, in_specs=[any_spec] * n,
                              out_specs=[any_spec] * n, scratch_shapes=scratch)(*arrays)
    return pl.kernel(body, name=name, out_type=out_shape,
                     mesh=plsc.ScalarSubcoreMesh(axis_name="sequencer", num_cores=1),
                     scratch_types=scratch,
                     compiler_params=pltpu.CompilerParams(collective_id=sequencer_id))(*arrays)


def _all_gather(name, arrays, sequencer_id=None):
    n = len(arrays)

    def body(*refs):
        ins, outs = refs[:n], refs[n:2 * n]
        send_sems, recv_sems, local_sems = refs[2 * n:]
        x, y, c = _mesh_pos()
        me, sibling = (x, y, c), (x, y, 1 - c)
        chips = [(1 - x, y), (x, 1 - y), (1 - x, 1 - y)]
        if sequencer_id is not None:
            _handshake([sibling] + [(*chip, c) for chip in chips])

        def copy(a, k, block, to, src=None):
            dst = outs[a].at[_slot(block)]
            return pltpu.make_async_remote_copy(
                src_ref=dst if src is None else src, dst_ref=dst,
                send_sem=send_sems.at[N_PEER * a + k], recv_sem=recv_sems.at[N_PEER * a + k],
                device_id=to, device_id_type=MESH)

        local, first, passed = [], [], []
        for a in range(n):
            cp = pltpu.make_async_copy(ins[a], outs[a].at[_slot(me)], local_sems.at[a])
            cp.start()
            local.append(cp)
            sends = [copy(a, 0, me, sibling, src=ins[a])]
            sends += [copy(a, 1 + j, me, (*chip, c), src=ins[a]) for j, chip in enumerate(chips)]
            for cp in sends:
                cp.start()
            first += sends
        for a in range(n):
            for j, chip in enumerate(chips):
                copy(a, 1 + j, (*chip, c), me).wait_recv()
                cp = copy(a, 4 + j, (*chip, c), sibling)
                cp.start()
                passed.append(cp)
        for a in range(n):
            copy(a, 0, sibling, me).wait_recv()
            for j, chip in enumerate(chips):
                copy(a, 4 + j, (*chip, 1 - c), me).wait_recv()
        for cp in first + passed:
            cp.wait_send()
        for cp in local:
            cp.wait()

    out_shape = [jax.ShapeDtypeStruct((N_DEV,) + a.shape, a.dtype) for a in arrays]
    return _launch(name, body, arrays, out_shape, sequencer_id)


def _all_to_all(name, arrays, sequencer_id=None):
    n = len(arrays)

    def body(*refs):
        ins, outs = refs[:n], refs[n:2 * n]
        send_sems, recv_sems, local_sems = refs[2 * n:]
        x, y, c = _mesh_pos()
        me = _slot((x, y, c))
        if sequencer_id is not None:
            _handshake([(1 - x if k & 4 else x, 1 - y if k & 2 else y, 1 - c if k & 1 else c)
                        for k in range(1, N_DEV)])
        local, sends, recvs = [], [], []
        for a in range(n):
            cp = pltpu.make_async_copy(ins[a].at[me], outs[a].at[me], local_sems.at[a])
            cp.start()
            local.append(cp)
        for k in range(1, N_DEV):
            peer = (1 - x if k & 4 else x, 1 - y if k & 2 else y, 1 - c if k & 1 else c)
            ps = _slot(peer)
            for a in range(n):
                sem = N_PEER * a + k - 1
                cp = pltpu.make_async_remote_copy(
                    src_ref=ins[a].at[ps], dst_ref=outs[a].at[me],
                    send_sem=send_sems.at[sem], recv_sem=recv_sems.at[sem],
                    device_id=peer, device_id_type=MESH)
                cp.start()
                sends.append(cp)
                recvs.append(pltpu.make_async_remote_copy(
                    src_ref=ins[a].at[ps], dst_ref=outs[a].at[ps],
                    send_sem=send_sems.at[sem], recv_sem=recv_sems.at[sem],
                    device_id=peer, device_id_type=MESH))
        for cp in recvs:
            cp.wait_recv()
        for cp in sends:
            cp.wait_send()
        for cp in local:
            cp.wait()

    out_shape = [jax.ShapeDtypeStruct(a.shape, a.dtype) for a in arrays]
    return _launch(name, body, arrays, out_shape, sequencer_id)


def _after(value, token):
    return lax.optimization_barrier((value, token))[0]


def _k_tile(k_dim, tm, tn, fixed_bytes):
    budget = VMEM_LIMIT - MM_RESERVE - fixed_bytes
    tk = k_dim
    while tk % 2 == 0 and tk > 512 and (
            4 * (tm + tn) * tk + (4 * tm * tn if tk < k_dim else 0) > budget):
        tk //= 2
    return tk


def _matmul(name, a, b, *, nt, tm, tn, n_cols, out_shape, out_specs, epilogue,
            fixed_bytes, ta=False, tk=None, b_spec=None, extra=(), extra_specs=()):
    assert not (ta and nt)
    k_dim, m_rows = a.shape if ta else a.shape[::-1]
    tm, tn = _tile(m_rows, tm), _tile(n_cols, tn)
    tk = _k_tile(k_dim, tm, tn, fixed_bytes) if tk is None else _tile(k_dim, tk)
    grid = (m_rows // tm, n_cols // tn, k_dim // tk)
    n_k = grid[2]
    if ta:
        a_spec = pl.BlockSpec((tk, tm), lambda i, j, k: (k, i))
    else:
        a_spec = pl.BlockSpec((tm, tk), lambda i, j, k: (i, k))
    if b_spec is not None:
        b_blk = b_spec(tk)
    elif nt:
        b_blk = pl.BlockSpec((tn, tk), lambda i, j, k: (j, k))
    else:
        b_blk = pl.BlockSpec((tk, tn), lambda i, j, k: (k, j))
    n_extra, n_out = len(extra), len(out_shape)
    product = _dot_tn if ta else _dot_nt if nt else _dot

    def body(a_ref, b_ref, *rest):
        extra_refs = rest[:n_extra]
        out_refs = rest[n_extra:n_extra + n_out]
        if n_k == 1:
            epilogue(product(a_ref[...], b_ref[...]), extra_refs, out_refs)
            return
        acc_ref = rest[n_extra + n_out]
        k = pl.program_id(2)

        @pl.when(k == 0)
        def _():
            acc_ref[...] = jnp.zeros_like(acc_ref)

        acc_ref[...] += product(a_ref[...], b_ref[...])

        @pl.when(k == n_k - 1)
        def _():
            epilogue(acc_ref[...], extra_refs, out_refs)

    return pl.pallas_call(
        body, name=name, grid=grid,
        in_specs=[a_spec, b_blk, *extra_specs], out_specs=out_specs, out_shape=out_shape,
        scratch_shapes=[pltpu.VMEM((tm, tn), F32)] if n_k > 1 else [],
        compiler_params=_params(3),
    )(a, b, *extra)


def _mm_plain(name, a, b, *, nt=False, ta=False, n_cols=None, out_dtype=BF16, init=None,
              tm=1024, tn=1024, tk=None, b_spec=None, out_3d=None):
    m_rows = a.shape[1] if ta else a.shape[0]
    if n_cols is None:
        n_cols = b.shape[0] if nt else b.shape[1]
    tm, tn = _tile(m_rows, tm), _tile(n_cols, tn)
    fixed = 2 * tm * tn * (jnp.dtype(out_dtype).itemsize + (4 if init is not None else 0))
    if out_3d is None:
        shape = jax.ShapeDtypeStruct((m_rows, n_cols), out_dtype)
        spec = pl.BlockSpec((tm, tn), lambda i, j, k: (i, j))
    else:
        slabs, width = out_3d
        assert width % tn == 0 and slabs * width == n_cols
        per = width // tn
        shape = jax.ShapeDtypeStruct((slabs, m_rows, width), out_dtype)
        spec = pl.BlockSpec((None, tm, tn), lambda i, j, k: (j // per, i, j % per))
    extra, extra_specs = (), ()
    if init is not None:
        extra = (init,)
        extra_specs = (pl.BlockSpec((tm, tn), lambda i, j, k: (i, j)),)

    def epilogue(acc, extra_refs, out_refs):
        if init is not None:
            acc = acc + extra_refs[0][...]
        out_refs[0][...] = acc.astype(out_dtype)

    (out,) = _matmul(name, a, b, nt=nt, ta=ta, tm=tm, tn=tn, tk=tk, n_cols=n_cols,
                     out_shape=[shape], out_specs=[spec], epilogue=epilogue, fixed_bytes=fixed,
                     b_spec=b_spec, extra=extra, extra_specs=extra_specs)
    return out


def _mm_rope(name, a, b, cos, sin, *, n_cols, rope_cols, tn, b_spec=None):
    m_rows = a.shape[0]
    tm = _tile(m_rows, 1024)
    tn = _tile(n_cols, tn)
    assert rope_cols % tn == 0 and tn % HEAD_DIM == 0
    rope_blocks = rope_cols // tn
    table_spec = pl.BlockSpec((tm, LANE), lambda i, j, k: (i, 0))

    def epilogue(acc, extra_refs, out_refs):
        cos_ref, sin_ref = extra_refs
        j = pl.program_id(1)

        @pl.when(j < rope_blocks)
        def _():
            for head in range(tn // HEAD_DIM):
                cols = slice(head * HEAD_DIM, (head + 1) * HEAD_DIM)
                blk = acc[:, cols]
                rot = pltpu.roll(blk, HEAD_DIM // 2, 1)
                out_refs[0][:, cols] = (blk * cos_ref[...] + rot * sin_ref[...]).astype(BF16)

        @pl.when(j >= rope_blocks)
        def _():
            out_refs[0][...] = acc.astype(BF16)

    (out,) = _matmul(name, a, b, nt=False, tm=tm, tn=tn, n_cols=n_cols,
                     out_shape=[jax.ShapeDtypeStruct((m_rows, n_cols), BF16)],
                     out_specs=[pl.BlockSpec((tm, tn), lambda i, j, k: (i, j))],
                     epilogue=epilogue, fixed_bytes=4 * tm * tn + 16 * tm * LANE, b_spec=b_spec,
                     extra=(cos, sin), extra_specs=(table_spec, table_spec))
    return out


def _mm_residual(name, u, w, x_in, gate):
    m_rows, n_cols = x_in.shape
    tm, tn = _tile(m_rows, 512), _tile(n_cols, 1024)
    blk = pl.BlockSpec((tm, tn), lambda i, j, k: (i, j))

    def epilogue(acc, extra_refs, out_refs):
        x_ref, gate_ref = extra_refs
        out_refs[0][...] = acc
        out_refs[1][...] = x_ref[...] + gate_ref[...] * acc

    y, x_out = _matmul(
        name, u, w, nt=False, tm=tm, tn=tn, n_cols=n_cols,
        out_shape=[jax.ShapeDtypeStruct((m_rows, n_cols), F32)] * 2, out_specs=[blk, blk],
        epilogue=epilogue, fixed_bytes=3 * 8 * tm * tn, extra=(x_in, gate),
        extra_specs=(blk, pl.BlockSpec((1, tn), lambda i, j, k: (0, j))))
    return y, x_out


def _mm_gate_bwd(name, dy, w_out, z_src, z_col0, o):
    m_rows = dy.shape[0]
    n_cols = w_out.shape[0]
    tm, tn = _tile(m_rows, 1024), _tile(n_cols, 1024)
    assert z_col0 % tn == 0 and tn % HEAD_DIM == 0 and n_cols // HEAD_DIM <= LANE
    z_blk0 = z_col0 // tn
    blk = pl.BlockSpec((tm, tn), lambda i, j, k: (i, j))

    def epilogue(du, extra_refs, out_refs):
        z_ref, o_ref = extra_refs
        do_ref, dz_ref, delta_ref = out_refs
        j = pl.program_id(1)
        z = z_ref[...].astype(F32)
        o_val = o_ref[...].astype(F32)
        sig = _sigmoid(z)
        d_o = (du * (z * sig)).astype(BF16)
        do_ref[...] = d_o
        dz_ref[...] = (du * o_val * (sig * (1.0 + z * (1.0 - sig)))).astype(BF16)

        @pl.when(j == 0)
        def _():
            delta_ref[...] = jnp.zeros_like(delta_ref)

        prod = d_o.astype(F32) * o_val
        lane = _iota((tm, LANE), 1)
        delta = delta_ref[...]
        for head in range(tn // HEAD_DIM):
            rows = jnp.sum(prod[:, head * HEAD_DIM:(head + 1) * HEAD_DIM], axis=1, keepdims=True)
            delta = delta + jnp.where(lane == j * (tn // HEAD_DIM) + head, rows, 0.0)
        delta_ref[...] = delta

    d_o, dz, delta = _matmul(
        name, dy, w_out, nt=True, tm=tm, tn=tn, n_cols=n_cols,
        out_shape=[jax.ShapeDtypeStruct((m_rows, n_cols), BF16)] * 2
        + [jax.ShapeDtypeStruct((m_rows, LANE), F32)],
        out_specs=[blk, blk, pl.BlockSpec((tm, LANE), lambda i, j, k: (i, 0))],
        epilogue=epilogue, fixed_bytes=4 * 4 * tm * tn + 8 * tm * LANE, extra=(z_src, o),
        extra_specs=(pl.BlockSpec((tm, tn), lambda i, j, k: (i, z_blk0 + j)), blk))
    return d_o, dz, delta


def _norm_fwd(name, x, ga, sa, ta, gb=None):
    s_len, d = x.shape
    tr = _tile(s_len, 256)
    two = gb is not None
    row = pl.BlockSpec((tr, d), lambda i: (i, 0))
    vec = pl.BlockSpec((1, d), lambda i: (0, 0))

    def body(x_ref, ga_ref, sa_ref, ta_ref, *rest):
        xv = x_ref[...]
        y = xv * lax.rsqrt(jnp.mean(xv * xv, axis=-1, keepdims=True) + RMS_EPS)
        rest[-2 if two else -1][...] = ((y * ga_ref[...]) * (1.0 + sa_ref[...]) + ta_ref[...]).astype(BF16)
        if two:
            rest[-1][...] = (y * rest[0][...]).astype(BF16)

    ins = [x, ga, sa, ta] + ([gb] if two else [])
    outs = pl.pallas_call(
        body, name=name, grid=(s_len // tr,),
        in_specs=[row] + [vec] * (len(ins) - 1),
        out_specs=[row] * (2 if two else 1),
        out_shape=[jax.ShapeDtypeStruct((s_len, d), BF16)] * (2 if two else 1),
        compiler_params=_params(1))(*ins)
    return outs if two else outs[0]


def _loss_bwd(x2, target, y1, g_final, gate1):
    s_len, d = x2.shape
    tr = _tile(s_len, 128)
    row = pl.BlockSpec((tr, d), lambda i: (i, 0))
    vec = pl.BlockSpec((1, d), lambda i: (0, 0))

    def body(x_ref, t_ref, y_ref, g_ref, gate_ref, loss_ref, dx_ref, dy_ref, sums_ref):
        @pl.when(pl.program_id(0) == 0)
        def _():
            loss_ref[...] = jnp.zeros_like(loss_ref)
            sums_ref[...] = jnp.zeros_like(sums_ref)

        xv = x_ref[...]
        rstd = lax.rsqrt(jnp.mean(xv * xv, axis=-1, keepdims=True) + RMS_EPS)
        xhat = xv * rstd
        g = g_ref[...]
        err = xhat * g - t_ref[...]
        sq = jnp.sum(jnp.sum(err * err, axis=1, keepdims=True), axis=0, keepdims=True)
        loss_ref[...] += sq * (0.5 / d)
        dout = err * (1.0 / d)
        dxhat = dout * g
        dx = rstd * (dxhat - xhat * jnp.mean(dxhat * xhat, axis=-1, keepdims=True))
        dx_ref[...] = dx
        dy_ref[...] = (dx * gate_ref[...]).astype(BF16)
        sums_ref[0:1, :] += jnp.sum(dout * xhat, axis=0, keepdims=True)
        sums_ref[1:2, :] += jnp.sum(dx * y_ref[...], axis=0, keepdims=True)

    return pl.pallas_call(
        body, name="loss_bwd", grid=(s_len // tr,),
        in_specs=[row, row, row, vec, vec],
        out_specs=[pl.BlockSpec((SUBLANE, LANE), lambda i: (0, 0)), row, row,
                   pl.BlockSpec((SUBLANE, d), lambda i: (0, 0))],
        out_shape=[jax.ShapeDtypeStruct((SUBLANE, LANE), F32),
                   jax.ShapeDtypeStruct((s_len, d), F32),
                   jax.ShapeDtypeStruct((s_len, d), BF16),
                   jax.ShapeDtypeStruct((SUBLANE, d), F32)],
        compiler_params=_params(1))(x2, target, y1, g_final, gate1)


def _norm_bwd(name, x, dres, dha, ga, sa, dhb=None, gb=None, y=None, gate=None):
    s_len, d = x.shape
    tr = _tile(s_len, 128)
    has_b, has_y = dhb is not None, y is not None
    row = pl.BlockSpec((tr, d), lambda i: (i, 0))
    vec = pl.BlockSpec((1, d), lambda i: (0, 0))
    ins, specs = [x, dres, dha, ga, sa], [row, row, row, vec, vec]
    if has_b:
        ins += [dhb, gb]
        specs += [row, vec]
    if has_y:
        ins += [y, gate]
        specs += [row, vec]
    n_in = len(ins)

    def body(*refs):
        x_ref, dres_ref, dha_ref, ga_ref, sa_ref = refs[:5]
        pos = 5
        if has_b:
            dhb_ref, gb_ref = refs[pos:pos + 2]
            pos += 2
        if has_y:
            y_ref, gate_ref = refs[pos:pos + 2]
        outs = refs[n_in:]
        dx_ref, sums_ref = outs[0], outs[-1]

        @pl.when(pl.program_id(0) == 0)
        def _():
            sums_ref[...] = jnp.zeros_like(sums_ref)

        xv = x_ref[...]
        rstd = lax.rsqrt(jnp.mean(xv * xv, axis=-1, keepdims=True) + RMS_EPS)
        xhat = xv * rstd
        dha_v = dha_ref[...]
        ga_v, sa_v = ga_ref[...], sa_ref[...]
        dxhat = dha_v * (ga_v * (1.0 + sa_v))
        sums_ref[0:1, :] += jnp.sum(dha_v, axis=0, keepdims=True)
        sums_ref[1:2, :] += jnp.sum(dha_v * (xhat * ga_v), axis=0, keepdims=True)
        sums_ref[2:3, :] += jnp.sum(dha_v * ((1.0 + sa_v) * xhat), axis=0, keepdims=True)
        if has_b:
            dhb_v = dhb_ref[...]
            dxhat = dxhat + dhb_v * gb_ref[...]
            sums_ref[3:4, :] += jnp.sum(dhb_v * xhat, axis=0, keepdims=True)
        dx = dres_ref[...] + rstd * (dxhat - xhat * jnp.mean(dxhat * xhat, axis=-1, keepdims=True))
        dx_ref[...] = dx
        if has_y:
            outs[1][...] = (dx * gate_ref[...]).astype(BF16)
            sums_ref[4:5, :] += jnp.sum(dx * y_ref[...], axis=0, keepdims=True)

    out_shape = [jax.ShapeDtypeStruct((s_len, d), F32)]
    out_specs = [row]
    if has_y:
        out_shape.append(jax.ShapeDtypeStruct((s_len, d), BF16))
        out_specs.append(row)
    out_shape.append(jax.ShapeDtypeStruct((SUBLANE, d), F32))
    out_specs.append(pl.BlockSpec((SUBLANE, d), lambda i: (0, 0)))
    return pl.pallas_call(body, name=name, grid=(s_len // tr,), in_specs=specs,
                          out_specs=out_specs, out_shape=out_shape,
                          compiler_params=_params(1))(*ins)


def _fgate_fwd(f_raw, bias_row):
    s_len = f_raw.shape[0]
    tb = _tile(s_len, 512)
    blk = pl.BlockSpec((tb, LANE), lambda t: (t, 0))

    def body(f_ref, b_ref, out_ref, carry):
        @pl.when(pl.program_id(0) == 0)
        def _():
            carry[...] = jnp.zeros_like(carry)

        u = f_ref[...] + b_ref[...]
        logf = jnp.minimum(u, 0.0) - jnp.log1p(jnp.exp(-jnp.abs(u)))
        tri = (_iota((tb, tb), 1) <= _iota((tb, tb), 0)).astype(F32)
        run = jnp.dot(tri, logf, precision=HIGHEST, preferred_element_type=F32) + carry[...]
        out_ref[...] = run
        carry[...] = run[tb - 1:tb, :]

    return pl.pallas_call(
        body, name="fgate_fwd", grid=(s_len // tb,),
        in_specs=[blk, pl.BlockSpec((1, LANE), lambda t: (0, 0))], out_specs=blk,
        out_shape=jax.ShapeDtypeStruct((s_len, LANE), F32),
        scratch_shapes=[pltpu.VMEM((1, LANE), F32)],
        compiler_params=_params(1))(f_raw, bias_row)


def _fgate_bwd(df_a, df_b, f_raw, bias_row):
    s_len = f_raw.shape[0]
    tb = _tile(s_len, 512)
    nb = s_len // tb
    blk = pl.BlockSpec((tb, LANE), lambda t: (nb - 1 - t, 0))

    def body(a_ref, b2_ref, f_ref, b_ref, df_ref, sums_ref, carry):
        @pl.when(pl.program_id(0) == 0)
        def _():
            carry[...] = jnp.zeros_like(carry)
            sums_ref[...] = jnp.zeros_like(sums_ref)

        d_run = a_ref[...] + b2_ref[...]
        tri = (_iota((tb, tb), 1) >= _iota((tb, tb), 0)).astype(F32)
        dlogf = jnp.dot(tri, d_run, precision=HIGHEST, preferred_element_type=F32) + carry[...]
        carry[...] = dlogf[0:1, :]
        u = f_ref[...] + b_ref[...]
        df = dlogf * _sigmoid(-u)
        df_ref[...] = df.astype(BF16)
        sums_ref[...] += jnp.sum(df, axis=0, keepdims=True)

    return pl.pallas_call(
        body, name="fgate_bwd", grid=(nb,),
        in_specs=[blk, blk, blk, pl.BlockSpec((1, LANE), lambda t: (0, 0))],
        out_specs=[blk, pl.BlockSpec((SUBLANE, LANE), lambda t: (0, 0))],
        out_shape=[jax.ShapeDtypeStruct((s_len, LANE), BF16),
                   jax.ShapeDtypeStruct((SUBLANE, LANE), F32)],
        scratch_shapes=[pltpu.VMEM((1, LANE), F32)],
        compiler_params=_params(1))(df_a, df_b, f_raw, bias_row)


def _fox_fwd(proj, v_t, f_nat, f_t, heads):
    s_len = proj.shape[0]
    d = heads * HEAD_DIM
    t = _tile(s_len, 512)
    nq = s_len // t
    scale = HEAD_DIM ** -0.5

    def body(k_ref, q_ref, vt_ref, fn_ref, ft_ref, ot_ref, lse_ref,
             acc_scr, m_scr, l_scr, fk_scr):
        j, i = pl.program_id(1), pl.program_id(2)
        h = pl.program_id(0)

        @pl.when((j == 0) & (i == 0))
        def _():
            m_scr[...] = jnp.full_like(m_scr, NEG)
            l_scr[...] = jnp.zeros_like(l_scr)
            acc_scr[...] = jnp.zeros_like(acc_scr)

        def step(diagonal):
            s_t = _dot_nt(k_ref[...], q_ref[...]) * scale + (ft_ref[...] - fk_scr[...])
            if diagonal:
                s_t = jnp.where(_iota((t, t), 0) <= _iota((t, t), 1), s_t, NEG)
            m_prev = m_scr[i]
            m_new = jnp.maximum(m_prev, jnp.max(s_t, axis=0, keepdims=True))
            alpha = jnp.exp(m_prev - m_new)
            p_t = jnp.exp(s_t - m_new)
            l_scr[i] = alpha * l_scr[i] + jnp.sum(p_t, axis=0, keepdims=True)
            acc_scr[i] = alpha * acc_scr[i] + _dot(vt_ref[...], p_t.astype(BF16))
            m_scr[i] = m_new

        @pl.when(i == j)
        def _():
            fk_scr[...] = _pick_lane(fn_ref[...], h)
            step(True)

        @pl.when(i > j)
        def _():
            step(False)

        @pl.when((i == nq - 1) & (j == nq - 1))
        def _():
            for blk in range(nq):
                cols = slice(blk * t, (blk + 1) * t)
                l_sum = l_scr[blk]
                ot_ref[:, cols] = acc_scr[blk] / l_sum
                lse_ref[:, cols] = m_scr[blk] + jnp.log(l_sum)

    qry = pl.BlockSpec((t, HEAD_DIM), lambda h, j, i: (jnp.maximum(i, j), h))
    return pl.pallas_call(
        body, name="fox_fwd", grid=(heads, nq, nq),
        in_specs=[pl.BlockSpec((t, HEAD_DIM), lambda h, j, i: (j, heads + h)), qry,
                  pl.BlockSpec((HEAD_DIM, t), lambda h, j, i: (h, j)),
                  pl.BlockSpec((t, LANE), lambda h, j, i: (j, 0)),
                  pl.BlockSpec((None, 1, t), lambda h, j, i: (h, 0, jnp.maximum(i, j)))],
        out_specs=[pl.BlockSpec((HEAD_DIM, s_len), lambda h, j, i: (h, 0)),
                   pl.BlockSpec((None, 1, s_len), lambda h, j, i: (h, 0, 0))],
        out_shape=[jax.ShapeDtypeStruct((d, s_len), F32),
                   jax.ShapeDtypeStruct((heads, 1, s_len), F32)],
        scratch_shapes=[pltpu.VMEM((nq, HEAD_DIM, t), F32), pltpu.VMEM((nq, 1, t), F32),
                        pltpu.VMEM((nq, 1, t), F32), pltpu.VMEM((t, 1), F32)],
        compiler_params=_params(3))(proj, proj, v_t, f_nat, f_t)


def _gate_fwd(o_t, proj, z_blk0, heads):
    d, s_len = o_t.shape
    t = _tile(s_len, 512)

    def body(ot_ref, z_ref, o_ref, u_ref):
        o_val = jnp.transpose(ot_ref[...])
        o_ref[...] = o_val.astype(BF16)
        z = z_ref[...].astype(F32)
        u_ref[...] = (o_val * (z * _sigmoid(z))).astype(BF16)

    out_blk = pl.BlockSpec((t, HEAD_DIM), lambda i, h: (i, h))
    return pl.pallas_call(
        body, name="gate_fwd", grid=(s_len // t, heads),
        in_specs=[pl.BlockSpec((HEAD_DIM, t), lambda i, h: (h, i)),
                  pl.BlockSpec((t, HEAD_DIM), lambda i, h: (i, z_blk0 + h))],
        out_specs=[out_blk, out_blk],
        out_shape=[jax.ShapeDtypeStruct((s_len, d), BF16)] * 2,
        compiler_params=_params(2))(o_t, proj)


def _fox_bwd(proj, k_t, d_o, f_nat, f_t, lse_t, delta_t, heads):
    s_len = proj.shape[0]
    d = heads * HEAD_DIM
    t = _tile(s_len, 512)
    nq = s_len // t
    scale = HEAD_DIM ** -0.5

    def body(k_ref, v_ref, kt_ref, q_ref, do_ref, fn_ref, ft_ref, lse_ref, delta_ref,
             dk_ref, dv_ref, dqt_ref, dfq_ref, dfk_ref,
             dk_acc, dv_acc, dq_acc, dfq_acc, dfk_acc, fk_scr):
        h, j, i = pl.program_id(0), pl.program_id(1), pl.program_id(2)
        head_start = (j == 0) & (i == 0)

        @pl.when(head_start)
        def _():
            dq_acc[...] = jnp.zeros_like(dq_acc)
            dfq_acc[...] = jnp.zeros_like(dfq_acc)

        @pl.when(head_start & (h == 0))
        def _():
            dfk_ref[...] = jnp.zeros_like(dfk_ref)

        def step(diagonal):
            q = q_ref[...]
            d_out = do_ref[...]
            s_t = _dot_nt(k_ref[...], q) * scale + (ft_ref[...] - fk_scr[...]) - lse_ref[...]
            if diagonal:
                s_t = jnp.where(_iota((t, t), 0) <= _iota((t, t), 1), s_t, NEG)
            p_t = jnp.exp(s_t)
            dp_t = _dot_nt(v_ref[...], d_out)
            ds_t = p_t * (dp_t - delta_ref[...])
            ds_b = ds_t.astype(BF16)
            dv_acc[...] += _dot(p_t.astype(BF16), d_out)
            dk_acc[...] += _dot(ds_b, q)
            dq_acc[i] += _dot(kt_ref[...], ds_b)
            dfq_acc[i] += jnp.sum(ds_t, axis=0, keepdims=True)
            dfk_acc[...] += jnp.sum(ds_t, axis=1, keepdims=True)

        @pl.when(i == j)
        def _():
            dk_acc[...] = jnp.zeros_like(dk_acc)
            dv_acc[...] = jnp.zeros_like(dv_acc)
            dfk_acc[...] = jnp.zeros_like(dfk_acc)
            fk_scr[...] = _pick_lane(fn_ref[...], h)
            step(True)

        @pl.when(i > j)
        def _():
            step(False)

        @pl.when(i == nq - 1)
        def _():
            dk_ref[...] = (dk_acc[...] * scale).astype(BF16)
            dv_ref[...] = dv_acc[...].astype(BF16)
            rows = pl.ds(pl.multiple_of(j * t, t), t)
            dfk_ref[rows, :] += jnp.where(_iota((t, LANE), 1) == h, -dfk_acc[...], 0.0)

        @pl.when((i == nq - 1) & (j == nq - 1))
        def _():
            for blk in range(nq):
                cols = slice(blk * t, (blk + 1) * t)
                dqt_ref[:, cols] = (dq_acc[blk] * scale).astype(BF16)
                dfq_ref[:, cols] = dfq_acc[blk]

    key_col = lambda base: pl.BlockSpec((t, HEAD_DIM), lambda h, j, i: (j, base + h))
    qry = pl.BlockSpec((t, HEAD_DIM), lambda h, j, i: (jnp.maximum(i, j), h))
    qry_row = pl.BlockSpec((None, 1, t), lambda h, j, i: (h, 0, jnp.maximum(i, j)))
    kv_out = pl.BlockSpec((t, HEAD_DIM), lambda h, j, i: (j, h))
    return pl.pallas_call(
        body, name="fox_bwd", grid=(heads, nq, nq),
        in_specs=[key_col(heads), key_col(2 * heads),
                  pl.BlockSpec((HEAD_DIM, t), lambda h, j, i: (h, j)),
                  qry, qry, pl.BlockSpec((t, LANE), lambda h, j, i: (j, 0)),
                  qry_row, qry_row, qry_row],
        out_specs=[kv_out, kv_out,
                   pl.BlockSpec((HEAD_DIM, s_len), lambda h, j, i: (h, 0)),
                   pl.BlockSpec((None, 1, s_len), lambda h, j, i: (h, 0, 0)),
                   pl.BlockSpec((s_len, LANE), lambda h, j, i: (0, 0))],
        out_shape=[jax.ShapeDtypeStruct((s_len, d), BF16), jax.ShapeDtypeStruct((s_len, d), BF16),
                   jax.ShapeDtypeStruct((d, s_len), BF16),
                   jax.ShapeDtypeStruct((heads, 1, s_len), F32),
                   jax.ShapeDtypeStruct((s_len, LANE), F32)],
        scratch_shapes=[pltpu.VMEM((t, HEAD_DIM), F32), pltpu.VMEM((t, HEAD_DIM), F32),
                        pltpu.VMEM((nq, HEAD_DIM, t), F32), pltpu.VMEM((nq, 1, t), F32),
                        pltpu.VMEM((t, 1), F32), pltpu.VMEM((t, 1), F32)],
        compiler_params=_params(3))(proj, proj, k_t, proj, d_o, f_nat, f_t, lse_t, delta_t)


def _swa_specs(heads, kv_heads):
    width = heads // kv_heads * HEAD_DIM
    wide = lambda base: pl.BlockSpec((SWA_BLOCK, width), lambda n, g: (n, base + g))
    blk = lambda fn: pl.BlockSpec((SWA_BLOCK, HEAD_DIM), fn)
    prev = lambda base: blk(lambda n, g: (jnp.maximum(n - 1, 0), base + g))
    cur = lambda base: blk(lambda n, g: (n, base + g))
    return wide, prev, cur


def _swa_scores(q, kp, kc, n, scale):
    r, c = _iota((SWA_BLOCK, SWA_BLOCK), 0), _iota((SWA_BLOCK, SWA_BLOCK), 1)
    sp = jnp.where((c > r) & (n > 0), _dot_nt(q, kp) * scale, NEG)
    sc = jnp.where(c <= r, _dot_nt(q, kc) * scale, NEG)
    return sp, sc


def _swa_fwd(proj, kv, sinks_row, heads, kv_heads):
    s_len = proj.shape[0]
    d = heads * HEAD_DIM
    scale = HEAD_DIM ** -0.5
    group = heads // kv_heads
    wide, prev, cur = _swa_specs(heads, kv_heads)

    def body(q_ref, z_ref, kp_ref, kc_ref, vp_ref, vc_ref, sink_ref, o_ref, u_ref, lse_ref):
        n, g = pl.program_id(0), pl.program_id(1)
        kp, kc, vp, vc = kp_ref[...], kc_ref[...], vp_ref[...], vc_ref[...]
        lane = _iota((SWA_BLOCK, LANE), 1)
        lse_all = jnp.zeros((SWA_BLOCK, LANE), F32)
        for hh in range(group):
            cols = slice(hh * HEAD_DIM, (hh + 1) * HEAD_DIM)
            head = g * group + hh
            sp, sc = _swa_scores(q_ref[:, cols], kp, kc, n, scale)
            sink = _pick_lane(sink_ref[...], head)
            m = jnp.maximum(jnp.maximum(jnp.max(sp, axis=1, keepdims=True),
                                        jnp.max(sc, axis=1, keepdims=True)), sink)
            pp, pc = jnp.exp(sp - m), jnp.exp(sc - m)
            den = (jnp.sum(pp, axis=1, keepdims=True) + jnp.sum(pc, axis=1, keepdims=True)
                   + jnp.exp(sink - m))
            o_val = (_dot(pp.astype(BF16), vp) + _dot(pc.astype(BF16), vc)) / den
            o_ref[:, cols] = o_val.astype(BF16)
            z = z_ref[:, cols].astype(F32)
            u_ref[:, cols] = (o_val * (z * _sigmoid(z))).astype(BF16)
            lse_all = lse_all + jnp.where(lane == head, m + jnp.log(den), 0.0)

        @pl.when(g == 0)
        def _():
            lse_ref[...] = lse_all

        @pl.when(g > 0)
        def _():
            lse_ref[...] += lse_all

    nat = pl.BlockSpec((SWA_BLOCK, LANE), lambda n, g: (n, 0))
    return pl.pallas_call(
        body, name="swa_fwd", grid=(s_len // SWA_BLOCK, kv_heads),
        in_specs=[wide(0), wide(kv_heads), prev(0), cur(0), prev(kv_heads), cur(kv_heads),
                  pl.BlockSpec((1, LANE), lambda n, g: (0, 0))],
        out_specs=[wide(0), wide(0), nat],
        out_shape=[jax.ShapeDtypeStruct((s_len, d), BF16), jax.ShapeDtypeStruct((s_len, d), BF16),
                   jax.ShapeDtypeStruct((s_len, LANE), F32)],
        compiler_params=_params(2))(proj, proj, kv, kv, kv, kv, sinks_row)


def _swa_bwd_q(proj, kv, d_o, lse, delta, sinks_row, cos, sin, heads, kv_heads):
    s_len = proj.shape[0]
    d = heads * HEAD_DIM
    scale = HEAD_DIM ** -0.5
    group = heads // kv_heads
    wide, prev, cur = _swa_specs(heads, kv_heads)

    def body(q_ref, kp_ref, kc_ref, vp_ref, vc_ref, do_ref, lse_ref, delta_ref, sink_ref,
             cos_ref, sin_ref, dq_ref, dsink_ref):
        n, g = pl.program_id(0), pl.program_id(1)

        @pl.when((n == 0) & (g == 0))
        def _():
            dsink_ref[...] = jnp.zeros_like(dsink_ref)

        kp, kc, vp, vc = kp_ref[...], kc_ref[...], vp_ref[...], vc_ref[...]
        lse_blk, delta_blk = lse_ref[...], delta_ref[...]
        cos_v, sin_v = cos_ref[...], sin_ref[...]
        lane = _iota((SUBLANE, LANE), 1)
        dsink_all = jnp.zeros((SUBLANE, LANE), F32)
        for hh in range(group):
            cols = slice(hh * HEAD_DIM, (hh + 1) * HEAD_DIM)
            head = g * group + hh
            sp, sc = _swa_scores(q_ref[:, cols], kp, kc, n, scale)
            lse_col = _pick_lane(lse_blk, head)
            delta_col = _pick_lane(delta_blk, head)
            pp, pc = jnp.exp(sp - lse_col), jnp.exp(sc - lse_col)
            p_sink = jnp.exp(_pick_lane(sink_ref[...], head) - lse_col)
            d_out = do_ref[:, cols]
            dsp = pp * (_dot_nt(d_out, vp) - delta_col)
            dsc = pc * (_dot_nt(d_out, vc) - delta_col)
            dq = (_dot(dsp.astype(BF16), kp) + _dot(dsc.astype(BF16), kc)) * scale
            dq_ref[:, cols] = (dq * cos_v - pltpu.roll(dq, HEAD_DIM // 2, 1) * sin_v).astype(BF16)
            d_sink = jnp.sum(-p_sink * delta_col, axis=0, keepdims=True)
            dsink_all = dsink_all + jnp.where(lane == head, d_sink, 0.0)
        dsink_ref[...] += dsink_all

    own = wide(0)
    nat = pl.BlockSpec((SWA_BLOCK, LANE), lambda n, g: (n, 0))
    return pl.pallas_call(
        body, name="swa_bwd_q", grid=(s_len // SWA_BLOCK, kv_heads),
        in_specs=[own, prev(0), cur(0), prev(kv_heads), cur(kv_heads), own, nat, nat,
                  pl.BlockSpec((1, LANE), lambda n, g: (0, 0)), nat, nat],
        out_specs=[own, pl.BlockSpec((SUBLANE, LANE), lambda n, g: (0, 0))],
        out_shape=[jax.ShapeDtypeStruct((s_len, d), BF16),
                   jax.ShapeDtypeStruct((SUBLANE, LANE), F32)],
        compiler_params=_params(2))(proj, kv, kv, kv, kv, d_o, lse, delta, sinks_row, cos, sin)


def _swa_bwd_kv(proj, kv, d_o, lse_t, delta_t, cos, sin, heads, kv_heads):
    s_len = proj.shape[0]
    nb = s_len // SWA_BLOCK
    group = heads // kv_heads
    scale = HEAD_DIM ** -0.5

    def body(k_ref, v_ref, qm_ref, qn_ref, dom_ref, don_ref, lsem_ref, lsen_ref,
             deltam_ref, deltan_ref, cos_ref, sin_ref, dk_ref, dv_ref):
        m = pl.program_id(1)
        k, v = k_ref[...], v_ref[...]
        key, qry = _iota((SWA_BLOCK, SWA_BLOCK), 0), _iota((SWA_BLOCK, SWA_BLOCK), 1)
        own_valid = key <= qry
        next_valid = (key > qry) & (m + 1 < nb)
        dk = jnp.zeros((SWA_BLOCK, HEAD_DIM), F32)
        dv = jnp.zeros((SWA_BLOCK, HEAD_DIM), F32)
        for hh in range(group):
            cols = slice(hh * HEAD_DIM, (hh + 1) * HEAD_DIM)
            for q_ref, do_ref, lse_ref, delta_ref, valid in (
                    (qm_ref, dom_ref, lsem_ref, deltam_ref, own_valid),
                    (qn_ref, don_ref, lsen_ref, deltan_ref, next_valid)):
                q, d_out = q_ref[:, cols], do_ref[:, cols]
                s_t = _dot_nt(k, q) * scale
                p_t = jnp.exp(jnp.where(valid, s_t - lse_ref[hh], NEG))
                ds_t = p_t * (_dot_nt(v, d_out) - delta_ref[hh])
                dv = dv + _dot(p_t.astype(BF16), d_out)
                dk = dk + _dot(ds_t.astype(BF16), q)
        dk = dk * scale
        dk_ref[...] = (dk * cos_ref[...]
                       - pltpu.roll(dk, HEAD_DIM // 2, 1) * sin_ref[...]).astype(BF16)
        dv_ref[...] = dv.astype(BF16)

    blk = lambda fn: pl.BlockSpec((SWA_BLOCK, HEAD_DIM), fn)
    nxt = lambda m: jnp.minimum(m + 1, nb - 1)
    wide = lambda fn: pl.BlockSpec((SWA_BLOCK, group * HEAD_DIM), fn)
    rows = lambda fn: pl.BlockSpec((group, 1, SWA_BLOCK), fn)
    q_m, q_n = wide(lambda g, m: (m, g)), wide(lambda g, m: (nxt(m), g))
    r_m, r_n = rows(lambda g, m: (g, 0, m)), rows(lambda g, m: (g, 0, nxt(m)))
    nat = pl.BlockSpec((SWA_BLOCK, LANE), lambda g, m: (m, 0))
    out_blk = blk(lambda g, m: (m, g))
    width = kv_heads * HEAD_DIM
    return pl.pallas_call(
        body, name="swa_bwd_kv", grid=(kv_heads, nb),
        in_specs=[blk(lambda g, m: (m, g)), blk(lambda g, m: (m, kv_heads + g)),
                  q_m, q_n, q_m, q_n, r_m, r_n, r_m, r_n, nat, nat],
        out_specs=[out_blk, out_blk],
        out_shape=[jax.ShapeDtypeStruct((s_len, width), BF16)] * 2,
        compiler_params=_params(2))(kv, kv, proj, proj, d_o, d_o, lse_t, lse_t,
                                    delta_t, delta_t, cos, sin)


def _ada_fwd(c_rows, ada_w, bias_loc):
    n_layers, d, cols = ada_w.shape
    rows = c_rows.shape[0]
    tk = _tile(d, 512)
    n_k = d // tk

    def body(c_ref, w_ref, b_ref, mod_ref, sc_ref, acc_ref):
        k = pl.program_id(1)

        @pl.when(k == 0)
        def _():
            acc_ref[...] = jnp.zeros_like(acc_ref)

        cv = c_ref[...]
        sc = cv * _sigmoid(cv)
        sc_ref[...] = sc
        acc_ref[...] += _dot(sc.astype(BF16), w_ref[...].astype(BF16))

        @pl.when(k == n_k - 1)
        def _():
            mod_ref[...] = acc_ref[...] + b_ref[...]

    return pl.pallas_call(
        body, name="ada_fwd", grid=(n_layers, n_k),
        in_specs=[pl.BlockSpec((rows, tk), lambda l, k: (0, k)),
                  pl.BlockSpec((None, tk, cols), lambda l, k: (l, k, 0)),
                  pl.BlockSpec((None, 1, cols), lambda l, k: (l, 0, 0))],
        out_specs=[pl.BlockSpec((None, rows, cols), lambda l, k: (l, 0, 0)),
                   pl.BlockSpec((None, rows, tk), lambda l, k: (l, 0, k))],
        out_shape=[jax.ShapeDtypeStruct((n_layers, rows, cols), F32),
                   jax.ShapeDtypeStruct((n_layers, rows, d), F32)],
        scratch_shapes=[pltpu.VMEM((rows, cols), F32)],
        compiler_params=_params(2))(c_rows, ada_w, bias_loc)


def _ada_update(sc_t, dmod, w, m, v):
    n_layers, d, cols = w.shape
    tr = _tile(d, 256)
    big = pl.BlockSpec((None, tr, cols), lambda l, i: (l, i, 0))

    def body(sc_ref, dm_ref, w_ref, m_ref, v_ref, g_out, d_out, m_out, v_out):
        g = _dot(sc_ref[...], dm_ref[...])
        delta, m_new, v_new = _adamw(w_ref[...], g, m_ref[...], v_ref[...])
        g_out[...] = g
        d_out[...] = delta
        m_out[...] = m_new
        v_out[...] = v_new

    return pl.pallas_call(
        body, name="ada_update", grid=(n_layers, d // tr),
        in_specs=[pl.BlockSpec((tr, LANE), lambda l, i: (i, 0)),
                  pl.BlockSpec((None, LANE, cols), lambda l, i: (l, 0, 0)), big, big, big],
        out_specs=[big] * 4, out_shape=[jax.ShapeDtypeStruct(w.shape, F32)] * 4,
        compiler_params=_params(2))(sc_t, dmod, w, m, v)


def _row_tile(rows, bytes_per_row):
    tr = SUBLANE * 2
    while tr * 2 <= rows and rows % (tr * 2) == 0 and tr * 2 * bytes_per_row <= 24 * 2 ** 20:
        tr *= 2
    return _tile(rows, tr)


def _slab_sum(name, arrays):
    rows = arrays[0].shape[1]
    per_row = sum(2 * a.shape[2] * (a.shape[0] * a.dtype.itemsize + 4) for a in arrays)
    tr = _row_tile(rows, per_row)
    n = len(arrays)

    def body(*refs):
        for s_ref, out_ref in zip(refs[:n], refs[n:]):
            total = s_ref[0].astype(F32)
            for slot in range(1, s_ref.shape[0]):
                total = total + s_ref[slot].astype(F32)
            out_ref[...] = total

    return pl.pallas_call(
        body, name=name, grid=(rows // tr,),
        in_specs=[pl.BlockSpec((a.shape[0], tr, a.shape[2]), lambda i: (0, i, 0)) for a in arrays],
        out_specs=[pl.BlockSpec((tr, a.shape[2]), lambda i: (i, 0)) for a in arrays],
        out_shape=[jax.ShapeDtypeStruct(a.shape[1:], F32) for a in arrays],
        compiler_params=_params(1))(*arrays)


def _shard_update(name, slabs, w, m, v):
    rows, cols = w.shape
    n_slabs = slabs.shape[0]
    tr = _row_tile(rows, 2 * cols * (slabs.dtype.itemsize * n_slabs + 4 * 7))
    blk = pl.BlockSpec((tr, cols), lambda i: (i, 0))

    def body(s_ref, w_ref, m_ref, v_ref, g_out, d_out, m_out, v_out):
        g = s_ref[0].astype(F32)
        for slot in range(1, n_slabs):
            g = g + s_ref[slot].astype(F32)
        delta, m_new, v_new = _adamw(w_ref[...], g, m_ref[...], v_ref[...])
        g_out[...] = g
        d_out[...] = delta
        m_out[...] = m_new
        v_out[...] = v_new

    return pl.pallas_call(
        body, name=name, grid=(rows // tr,),
        in_specs=[pl.BlockSpec((n_slabs, tr, cols), lambda i: (0, i, 0)), blk, blk, blk],
        out_specs=[blk] * 4, out_shape=[jax.ShapeDtypeStruct((rows, cols), F32)] * 4,
        compiler_params=_params(1))(slabs, w, m, v)


def _small_update(gathered, w, m, v):
    shape = jax.ShapeDtypeStruct(w.shape, F32)

    def body(g_ref, w_ref, m_ref, v_ref, g_out, d_out, m_out, v_out):
        g = g_ref[0]
        for dev in range(1, N_DEV):
            g = g + g_ref[dev]
        delta, m_new, v_new = _adamw(w_ref[...], g, m_ref[...], v_ref[...])
        g_out[...] = g
        d_out[...] = delta
        m_out[...] = m_new
        v_out[...] = v_new

    return pl.pallas_call(body, name="small_update", out_shape=[shape] * 4,
                          compiler_params=pltpu.CompilerParams(vmem_limit_bytes=VMEM_LIMIT),
                          )(gathered, w, m, v)


def _rope_tables(s_len):
    half = HEAD_DIM // 2
    inv = ROPE_THETA ** (-jnp.arange(half, dtype=F32) / half)
    ang = jnp.arange(s_len, dtype=F32)[:, None] * inv[None, :]
    cos, sin = jnp.cos(ang), jnp.sin(ang)
    return jnp.concatenate([cos, cos], axis=1), jnp.concatenate([-sin, sin], axis=1)


def _pad_lanes(a):
    return jnp.pad(a, ((0, 0), (0, LANE - a.shape[1])))


def _rows_of(nat, heads):
    return jnp.transpose(nat[:, :heads])[:, None, :]


def _pack(parts):
    tile = SUBLANE * LANE
    flat = []
    for p in parts:
        p = p.reshape(-1)
        flat.append(jnp.pad(p, (0, (-p.shape[0]) % tile)))
    return jnp.concatenate(flat).reshape(-1, LANE)


def _unpack(packed, shapes):
    tile = SUBLANE * LANE
    flat = packed.reshape(-1)
    out, pos = [], 0
    for shape in shapes:
        size = 1
        for dim in shape:
            size *= dim
        out.append(flat[pos:pos + size].reshape(shape))
        pos += size + (-size) % tile
    return out


def kernel(x, c, norm_g, ada_w, ada_b, a_w_in, a_b_f, a_w_out, kv_norm_g, kv_w, b_w_in, b_sinks, b_w_out, final_norm_g, loss_target, m_norm_g, m_ada_w, m_ada_b, m_a_w_in, m_a_b_f, m_a_w_out, m_kv_norm_g, m_kv_w, m_b_w_in, m_b_sinks, m_b_w_out, m_final_norm_g, v_norm_g, v_ada_w, v_ada_b, v_a_w_in, v_a_b_f, v_a_w_out, v_kv_norm_g, v_kv_w, v_b_w_in, v_b_sinks, v_b_w_out, v_final_norm_g):
    s_len, d = x.shape[1], x.shape[2]
    heads = d // HEAD_DIM
    kv_heads = kv_w.shape[1] // (2 * HEAD_DIM)
    kv_width = kv_heads * HEAD_DIM
    ada_cols = ada_w.shape[2]
    assert heads <= LANE and heads % N_DEV == 0 and a_w_in.shape[2] * N_DEV == 4 * d + heads
    me = _slot(_mesh_pos())
    x0 = x[0]
    target = loss_target[0]
    vec = lambda a: a.reshape(1, d)

    sup, extra = 4 * d // N_DEV, heads // N_DEV
    padded = jnp.pad(a_w_in[0].astype(BF16), ((0, 0), (heads, LANE)))
    big_loc = lax.dynamic_slice_in_dim(padded, heads - extra * me, sup, axis=1)
    small_loc = lax.dynamic_slice_in_dim(padded, heads + sup - extra * me, LANE, axis=1)
    g_big, g_small, c_all = _all_gather("gather_a_w_in", [big_loc, small_loc, c])
    lane_id = jnp.arange(LANE)[None, :]
    patch = jnp.stack([jnp.where(lane_id < extra * s, g_small[s - 1], g_big[s, :, :LANE])
                       for s in range(1, N_DEV)])
    w_a_main = g_big.at[1:, :, :LANE].set(patch)
    w_a_f = g_small[N_DEV - 1]
    b_in_cols = b_w_in.shape[2]
    a_tn = _tile(sup, 1024)
    w_a_main_spec = lambda tk: pl.BlockSpec(
        (None, tk, a_tn), lambda i, j, k: (j // (sup // a_tn), k, j % (sup // a_tn)))

    c_rows = jnp.pad(c_all.reshape(N_DEV, d), ((0, 2 * SUBLANE - N_DEV), (0, 0)))
    bias_loc = lax.dynamic_slice_in_dim(ada_b, me * ada_cols, ada_cols, axis=1)[:, None, :]
    mod_part, sc_rows = _ada_fwd(c_rows, ada_w, bias_loc)
    (mod_recv,) = _all_to_all("exchange_mod", [jnp.transpose(mod_part[:, :N_DEV], (1, 0, 2))])
    mod = jnp.transpose(mod_recv, (1, 0, 2)).reshape(2, 3 * d)
    later = [a_w_out[0].astype(BF16), kv_w.astype(BF16), b_w_in[0].astype(BF16),
             b_w_out[0].astype(BF16)]
    g_a_out, g_kv, w_b_in, g_b_out = _all_gather("gather_rest", _after(later, mod_recv),
                                                 sequencer_id=1)
    w_a_out = g_a_out.reshape(d, d)
    w_kv = g_kv.reshape(d, 2 * kv_width)
    w_b_out = g_b_out.reshape(d, d)
    shift0, scale0, gate0 = vec(mod[0, :d]), vec(mod[0, d:2 * d]), vec(mod[0, 2 * d:])
    shift1, scale1, gate1 = vec(mod[1, :d]), vec(mod[1, d:2 * d]), vec(mod[1, 2 * d:])
    g0, g1, g_kvn, g_fin = vec(norm_g[0]), vec(norm_g[1]), vec(kv_norm_g), vec(final_norm_g)

    cos, sin = _rope_tables(s_len)
    bias_f = _pad_lanes(a_b_f)
    sinks_row = _pad_lanes(b_sinks)

    h0 = _norm_fwd("norm0", x0, g0, scale0, shift0)
    proj0 = _mm_plain("proj0", h0, w_a_main, n_cols=4 * d, tn=a_tn, b_spec=w_a_main_spec)
    f_raw = _mm_plain("proj0_f", h0, w_a_f, out_dtype=F32, tn=LANE)
    f_nat = _fgate_fwd(f_raw, bias_f)
    f_t = _rows_of(f_nat, heads)
    o0_t, lse0_t = _fox_fwd(proj0, jnp.transpose(proj0[:, 2 * d:3 * d]), f_nat, f_t, heads)
    o0, u0 = _gate_fwd(o0_t, proj0, 3 * heads, heads)
    y0, x1 = _mm_residual("out0", u0, w_a_out, x0, gate0)

    h1, hk = _norm_fwd("norm1", x1, g1, scale1, shift1, gb=g_kvn)
    kv = _mm_rope("kv_proj", hk, w_kv, cos, sin, n_cols=2 * kv_width, rope_cols=kv_width,
                  tn=kv_width)
    proj1 = _mm_rope("proj1", h1, w_b_in, cos, sin, n_cols=2 * d, rope_cols=d, tn=b_in_cols,
                     b_spec=lambda tk: pl.BlockSpec((None, tk, b_in_cols),
                                                    lambda i, j, k: (j, k, 0)))
    o1, u1, lse1 = _swa_fwd(proj1, kv, sinks_row, heads, kv_heads)
    y1, x2 = _mm_residual("out1", u1, w_b_out, x1, gate1)

    loss_part, dx2, dy1, sums_f = _loss_bwd(x2, target, y1, g_fin, gate1)

    do1, dz1, delta1 = _mm_gate_bwd("out1_bwd", dy1, w_b_out, proj1, d, o1)
    gw_b_out = _mm_plain("out1_wgrad", u1, dy1, ta=True)
    do1 = _after(do1, gw_b_out)
    dq1, dsinks = _swa_bwd_q(proj1, kv, do1, lse1, delta1, sinks_row, cos, sin, heads, kv_heads)
    dk1, dv1 = _swa_bwd_kv(proj1, kv, do1, _rows_of(lse1, heads), _rows_of(delta1, heads),
                           cos, sin, heads, kv_heads)
    dproj1 = jnp.concatenate([dq1, dz1], axis=1)
    tk_b = _tile(2 * d, b_in_cols)
    dh1 = _mm_plain("proj1_bwd", dproj1, w_b_in, nt=True, n_cols=d, out_dtype=F32, tk=tk_b,
                    b_spec=lambda tk: pl.BlockSpec((None, _tile(d, 1024), tk),
                                                   lambda i, j, k: (k * tk // b_in_cols, j, 0)))
    gw_b_in = _mm_plain("proj1_wgrad", h1, dproj1, ta=True, tn=b_in_cols,
                        out_3d=(N_DEV, b_in_cols))
    dkv = jnp.concatenate([dk1, dv1], axis=1)
    dhk = _mm_plain("kv_bwd", dkv, w_kv, nt=True, out_dtype=F32)
    gw_kv = _mm_plain("kv_wgrad", hk, dkv, ta=True)
    dx1, dy0, sums1 = _norm_bwd("norm1_bwd", x1, dx2, dh1, g1, scale1,
                                dhb=_after(dhk, (gw_b_in, gw_kv)), gb=g_kvn, y=y0, gate=gate0)

    do0, dz0, delta0 = _mm_gate_bwd("out0_bwd", dy0, w_a_out, proj0, 3 * d, o0)
    gw_a_out = _mm_plain("out0_wgrad", u0, dy0, ta=True)
    r_b_out, r_b_in, r_kv, r_a_out = _all_to_all(
        "scatter_grads_early",
        [gw_b_out.reshape(N_DEV, d // N_DEV, d), gw_b_in,
         gw_kv.reshape(N_DEV, d // N_DEV, 2 * kv_width),
         gw_a_out.reshape(N_DEV, d // N_DEV, d)], sequencer_id=2)
    k0_t = jnp.transpose(proj0[:, d:2 * d])
    dk0, dv0, dq0_t, dfq_t, dfk_nat = _fox_bwd(proj0, k0_t, _after(do0, gw_a_out), f_nat, f_t,
                                               lse0_t, _rows_of(delta0, heads), heads)
    dfq_nat = _pad_lanes(jnp.transpose(dfq_t[:, 0, :]))
    df, sums_bf = _fgate_bwd(dfq_nat, dfk_nat, f_raw, bias_f)
    dproj0 = jnp.concatenate([jnp.transpose(dq0_t), dk0, dv0, dz0], axis=1)
    gw_a_big = _mm_plain("proj0_wgrad", h0, dproj0, ta=True, tn=a_tn, out_3d=(N_DEV, sup))
    gw_a_f = _mm_plain("proj0_f_wgrad", h0, df, ta=True, tn=LANE)
    gw_a_small = jnp.concatenate([gw_a_big[1:, :, :LANE], gw_a_f[None]], axis=0)
    r_a_big, r_a_small = _all_to_all("scatter_grads_a_w_in", [gw_a_big, gw_a_small],
                                     sequencer_id=3)
    r_b_out, r_b_in, r_kv, r_a_out = _after([r_b_out, r_b_in, r_kv, r_a_out], gw_a_big)
    up_b_out = _shard_update("update_b_w_out", r_b_out, b_w_out[0], m_b_w_out[0], v_b_w_out[0])
    up_b_in = _shard_update("update_b_w_in", r_b_in, b_w_in[0], m_b_w_in[0], v_b_w_in[0])
    up_kv = _shard_update("update_kv_w", r_kv, kv_w, m_kv_w, v_kv_w)
    up_a_out = _shard_update("update_a_w_out", r_a_out, a_w_out[0], m_a_w_out[0], v_a_w_out[0])
    df = _after(df, (gw_a_big, up_b_out[0], up_b_in[0], up_kv[0], up_a_out[0]))
    dh0_f = _mm_plain("proj0_f_bwd", df, w_a_f, nt=True, out_dtype=F32)
    dh0 = _mm_plain("proj0_bwd", dproj0, w_a_main, nt=True, n_cols=d, out_dtype=F32, init=dh0_f,
                    tk=sup, b_spec=lambda tk: pl.BlockSpec((None, _tile(d, 1024), tk),
                                                           lambda i, j, k: (k, j, 0)))
    grad_x, sums0 = _norm_bwd("norm0_bwd", x0, dx1, dh0, g0, scale0)

    dmod = jnp.stack([jnp.concatenate([sums0[0], sums0[1], sums1[4]]),
                      jnp.concatenate([sums1[0], sums1[1], sums_f[1]])])
    small_shapes = [(2, 3 * d), (2, d), (1, heads), (d,), (1, heads), (d,), (1,)]
    small_grads = [dmod, jnp.stack([sums0[2], sums1[2]]), sums_bf[0:1, :heads], sums1[3],
                   dsinks[0:1, :heads], sums_f[0], loss_part[0, 0:1]]
    (small_all,) = _all_gather("gather_small", [_pack(small_grads)])
    zero = jnp.zeros((1,), F32)
    small = _small_update(
        small_all,
        _pack([ada_b, norm_g, a_b_f, kv_norm_g, b_sinks, final_norm_g, zero]),
        _pack([m_ada_b, m_norm_g, m_a_b_f, m_kv_norm_g, m_b_sinks, m_final_norm_g, zero]),
        _pack([v_ada_b, v_norm_g, v_a_b_f, v_kv_norm_g, v_b_sinks, v_final_norm_g, zero]))
    s_grad, s_delta, s_m, s_v = [_unpack(p, small_shapes) for p in small]
    loss = s_grad[6][0]

    dmod_all = small_all.reshape(N_DEV, -1)[:, :2 * 3 * d].reshape(N_DEV, 2, 3 * d)
    dmod_loc = lax.dynamic_slice_in_dim(dmod_all, me * ada_cols, ada_cols, axis=2)
    dmod_loc = jnp.pad(jnp.transpose(dmod_loc, (1, 0, 2)), ((0, 0), (0, LANE - N_DEV), (0, 0)))
    sc_t = jnp.pad(jnp.transpose(sc_rows[0, :N_DEV]), ((0, 0), (0, LANE - N_DEV)))
    up_ada = _ada_update(sc_t.astype(BF16), dmod_loc.astype(BF16), ada_w, m_ada_w, v_ada_w)

    r_a_big, r_a_small = _after((r_a_big, r_a_small), (up_ada[0], small[0]))
    ga_big, ga_small = _slab_sum("sum_a_w_in", [r_a_big, r_a_small])
    ga_shard = lax.dynamic_slice_in_dim(jnp.concatenate([ga_big, ga_small], axis=1),
                                        extra * me, sup + extra, axis=1)
    up_a_in = _shard_update("update_a_w_in", ga_shard[None], a_w_in[0], m_a_w_in[0], v_a_w_in[0])

    lead = lambda a: a[None]
    per_kind = []
    for kind in range(4):
        sm = (s_grad, s_delta, s_m, s_v)[kind]
        per_kind.append([
            sm[1], up_ada[kind], sm[0], lead(up_a_in[kind]), sm[2], lead(up_a_out[kind]),
            sm[3], up_kv[kind], lead(up_b_in[kind]), sm[4], lead(up_b_out[kind]), sm[5]])
    return (loss, grad_x[None], *per_kind[0], *per_kind[1], *per_kind[2], *per_kind[3])
```

```python
import jax
import jax.numpy as jnp
from jax import lax
from jax.experimental import pallas as pl
from jax.experimental.pallas import tpu as pltpu
from jax.experimental.pallas import tpu_sc as plsc

F32 = jnp.float32
BF16 = jnp.bfloat16
LANE = 128
SUBLANE = 8
HEAD_DIM = 128
SWA_BLOCK = 128
N_DEV = 8
N_PEER = N_DEV - 1
RMS_EPS = 1e-6
ROPE_THETA = 10000.0
NEG = -1e30
VMEM_LIMIT = 56 * 2 ** 20
MM_RESERVE = 10 * 2 ** 20
MESH = pl.DeviceIdType.MESH
HIGHEST = lax.Precision.HIGHEST

ADAM_LR = 0.001
ADAM_B1 = 0.9
ADAM_B2 = 0.999
ADAM_EPS = 1e-08
ADAM_WD = 0.01
ADAM_STEP = 10


def _tile(dim, pref):
    return pref if dim % pref == 0 else dim


def _params(n_axes):
    return pltpu.CompilerParams(dimension_semantics=("arbitrary",) * n_axes,
                                vmem_limit_bytes=VMEM_LIMIT)


def _dot(a, b):
    return jnp.dot(a, b, preferred_element_type=F32)


def _dot_nt(a, b):
    return lax.dot_general(a, b, (((1,), (1,)), ((), ())), preferred_element_type=F32)


def _dot_tn(a, b):
    return lax.dot_general(a, b, (((0,), (0,)), ((), ())), preferred_element_type=F32)


def _sigmoid(z):
    return 1.0 / (1.0 + jnp.exp(-z))


def _iota(shape, dim):
    return lax.broadcasted_iota(jnp.int32, shape, dim)


def _pick_lane(block, lane_index):
    lane = _iota(block.shape, 1)
    return jnp.sum(jnp.where(lane == lane_index, block, 0.0), axis=1, keepdims=True)


def _adamw(w, g, m, v):
    m = ADAM_B1 * m + (1.0 - ADAM_B1) * g
    v = ADAM_B2 * v + (1.0 - ADAM_B2) * (g * g)
    m_hat = m / (1.0 - ADAM_B1 ** ADAM_STEP)
    v_hat = v / (1.0 - ADAM_B2 ** ADAM_STEP)
    delta = -ADAM_LR * (m_hat / (jnp.sqrt(v_hat) + ADAM_EPS) + ADAM_WD * w)
    return delta, m, v


def _mesh_pos():
    return lax.axis_index("x"), lax.axis_index("y"), lax.axis_index("c")


def _slot(pos):
    return 4 * pos[0] + 2 * pos[1] + pos[2]


def _handshake(peers):
    barrier = pltpu.get_barrier_semaphore()
    for peer in peers:
        pl.semaphore_signal(barrier, inc=1, device_id=peer, device_id_type=MESH)
    pl.semaphore_wait(barrier, len(peers))


def _launch(name, body, arrays, out_shape, sequencer_id):
    n = len(arrays)
    scratch = [pltpu.SemaphoreType.DMA((N_PEER * n,)), pltpu.SemaphoreType.DMA((N_PEER * n,)),
               pltpu.SemaphoreType.DMA((n,))]
    if sequencer_id is None:
        any_spec = pl.BlockSpec(memory_space=pl.ANY)
        return pl.pallas_call(body, name=name, out_shape=out_shape, in_specs=[any_spec] * n,
                              out_specs=[any_spec] * n, scratch_shapes=scratch)(*arrays)
    return pl.kernel(body, name=name, out_type=out_shape,
                     mesh=plsc.ScalarSubcoreMesh(axis_name="sequencer", num_cores=1),
                     scratch_types=scratch,
                     compiler_params=pltpu.CompilerParams(collective_id=sequencer_id))(*arrays)


def _all_gather(name, arrays, sequencer_id=None):
    n = len(arrays)

    def body(*refs):
        ins, outs = refs[:n], refs[n:2 * n]
        send_sems, recv_sems, local_sems = refs[2 * n:]
        x, y, c = _mesh_pos()
        me, sibling = (x, y, c), (x, y, 1 - c)
        chips = [(1 - x, y), (x, 1 - y), (1 - x, 1 - y)]
        if sequencer_id is not None:
            _handshake([sibling] + [(*chip, c) for chip in chips])

        def copy(a, k, block, to, src=None):
            dst = outs[a].at[_slot(block)]
            return pltpu.make_async_remote_copy(
                src_ref=dst if src is None else src, dst_ref=dst,
                send_sem=send_sems.at[N_PEER * a + k], recv_sem=recv_sems.at[N_PEER * a + k],
                device_id=to, device_id_type=MESH)

        local, first, passed = [], [], []
        for a in range(n):
            cp = pltpu.make_async_copy(ins[a], outs[a].at[_slot(me)], local_sems.at[a])
            cp.start()
            local.append(cp)
            sends = [copy(a, 0, me, sibling, src=ins[a])]
            sends += [copy(a, 1 + j, me, (*chip, c), src=ins[a]) for j, chip in enumerate(chips)]
            for cp in sends:
                cp.start()
            first += sends
        for a in range(n):
            for j, chip in enumerate(chips):
                copy(a, 1 + j, (*chip, c), me).wait_recv()
                cp = copy(a, 4 + j, (*chip, c), sibling)
                cp.start()
                passed.append(cp)
        for a in range(n):
            copy(a, 0, sibling, me).wait_recv()
            for j, chip in enumerate(chips):
                copy(a, 4 + j, (*chip, 1 - c), me).wait_recv()
        for cp in first + passed:
            cp.wait_send()
        for cp in local:
            cp.wait()

    out_shape = [jax.ShapeDtypeStruct((N_DEV,) + a.shape, a.dtype) for a in arrays]
    return _launch(name, body, arrays, out_shape, sequencer_id)


def _all_to_all(name, arrays, sequencer_id=None):
    n = len(arrays)

    def body(*refs):
        ins, outs = refs[:n], refs[n:2 * n]
        send_sems, recv_sems, local_sems = refs[2 * n:]
        x, y, c = _mesh_pos()
        me = _slot((x, y, c))
        if sequencer_id is not None:
            _handshake([(1 - x if k & 4 else x, 1 - y if k & 2 else y, 1 - c if k & 1 else c)
                        for k in range(1, N_DEV)])
        local, sends, recvs = [], [], []
        for a in range(n):
            cp = pltpu.make_async_copy(ins[a].at[me], outs[a].at[me], local_sems.at[a])
            cp.start()
            local.append(cp)
        for k in range(1, N_DEV):
            peer = (1 - x if k & 4 else x, 1 - y if k & 2 else y, 1 - c if k & 1 else c)
            ps = _slot(peer)
            for a in range(n):
                sem = N_PEER * a + k - 1
                cp = pltpu.make_async_remote_copy(
                    src_ref=ins[a].at[ps], dst_ref=outs[a].at[me],
                    send_sem=send_sems.at[sem], recv_sem=recv_sems.at[sem],
                    device_id=peer, device_id_type=MESH)
                cp.start()
                sends.append(cp)
                recvs.append(pltpu.make_async_remote_copy(
                    src_ref=ins[a].at[ps], dst_ref=outs[a].at[ps],
                    send_sem=send_sems.at[sem], recv_sem=recv_sems.at[sem],
                    device_id=peer, device_id_type=MESH))
        for cp in recvs:
            cp.wait_recv()
        for cp in sends:
            cp.wait_send()
        for cp in local:
            cp.wait()

    out_shape = [jax.ShapeDtypeStruct(a.shape, a.dtype) for a in arrays]
    return _launch(name, body, arrays, out_shape, sequencer_id)


def _pair_exchange(name, arrays, sequencer_id=None):
    n = len(arrays)
    chips = N_DEV // 2

    def body(*refs):
        ins, outs = refs[:n], refs[n:2 * n]
        send_sems, recv_sems = refs[2 * n], refs[2 * n + 1]
        x, y, c = _mesh_pos()
        sibling = (x, y, 1 - c)
        if sequencer_id is not None:
            _handshake([sibling])
        copies = [pltpu.make_async_remote_copy(
            src_ref=ins[a].at[2 * q + 1 - c], dst_ref=outs[a].at[q],
            send_sem=send_sems.at[chips * a + q], recv_sem=recv_sems.at[chips * a + q],
            device_id=sibling, device_id_type=MESH) for a in range(n) for q in range(chips)]
        for cp in copies:
            cp.start()
        for cp in copies:
            cp.wait()

    out_shape = [jax.ShapeDtypeStruct((chips,) + a.shape[1:], a.dtype) for a in arrays]
    return _launch(name, body, arrays, out_shape, sequencer_id)


def _chip_exchange(name, arrays, sequencer_id=None):
    n = len(arrays)
    chips = N_DEV // 2

    def body(*refs):
        ins, outs = refs[:n], refs[n:2 * n]
        send_sems, recv_sems, local_sems = refs[2 * n:]
        x, y, c = _mesh_pos()
        mine = 2 * x + y
        others = [(1 - x, y), (x, 1 - y), (1 - x, 1 - y)]
        if sequencer_id is not None:
            _handshake([(*chip, c) for chip in others])
        local = [pltpu.make_async_copy(ins[a].at[mine], outs[a].at[mine], local_sems.at[a])
                 for a in range(n)]
        for cp in local:
            cp.start()
        sends, recvs = [], []
        for j, chip in enumerate(others):
            theirs = 2 * chip[0] + chip[1]
            for a in range(n):
                both = dict(send_sem=send_sems.at[3 * a + j], recv_sem=recv_sems.at[3 * a + j],
                            device_id=(*chip, c), device_id_type=MESH)
                sends.append(pltpu.make_async_remote_copy(
                    src_ref=ins[a].at[theirs], dst_ref=outs[a].at[mine], **both))
                recvs.append(pltpu.make_async_remote_copy(
                    src_ref=ins[a].at[theirs], dst_ref=outs[a].at[theirs], **both))
        for cp in sends:
            cp.start()
        for cp in recvs:
            cp.wait_recv()
        for cp in sends:
            cp.wait_send()
        for cp in local:
            cp.wait()

    out_shape = [jax.ShapeDtypeStruct(a.shape, a.dtype) for a in arrays]
    return _launch(name, body, arrays, out_shape, sequencer_id)


def _pair_add(name, mine, theirs):
    chips, rows = theirs[0].shape[0], theirs[0].shape[1]
    per_row = sum(2 * 3 * a.shape[2] * a.dtype.itemsize for a in theirs)
    tr = _row_tile(rows, per_row)
    n = len(theirs)
    core = lax.axis_index("c").astype(jnp.int32).reshape(1)

    def body(core_ref, *refs):
        for a in range(n):
            refs[2 * n + a][...] = (refs[a][...].astype(F32)
                                    + refs[n + a][...].astype(F32)).astype(refs[2 * n + a].dtype)

    blk = lambda a, fn: pl.BlockSpec((None, tr, a.shape[2]), fn)
    grid_spec = pltpu.PrefetchScalarGridSpec(
        num_scalar_prefetch=1, grid=(chips, rows // tr),
        in_specs=[blk(a, lambda q, i, core_ref: (2 * q + core_ref[0], i, 0)) for a in mine]
        + [blk(a, lambda q, i, core_ref: (q, i, 0)) for a in theirs],
        out_specs=[blk(a, lambda q, i, core_ref: (q, i, 0)) for a in theirs])
    return pl.pallas_call(
        body, name=name, grid_spec=grid_spec,
        out_shape=[jax.ShapeDtypeStruct(a.shape, a.dtype) for a in theirs],
        compiler_params=_params(2))(core, *mine, *theirs)


def _after(value, token):
    return lax.optimization_barrier((value, token))[0]


def _k_tile(k_dim, tm, tn, fixed_bytes):
    budget = VMEM_LIMIT - MM_RESERVE - fixed_bytes
    tk = k_dim
    while tk % 2 == 0 and tk > 512 and (
            4 * (tm + tn) * tk + (4 * tm * tn if tk < k_dim else 0) > budget):
        tk //= 2
    return tk


def _matmul(name, a, b, *, nt, tm, tn, n_cols, out_shape, out_specs, epilogue,
            fixed_bytes, ta=False, tk=None, b_spec=None, extra=(), extra_specs=()):
    assert not (ta and nt)
    k_dim, m_rows = a.shape if ta else a.shape[::-1]
    tm, tn = _tile(m_rows, tm), _tile(n_cols, tn)
    tk = _k_tile(k_dim, tm, tn, fixed_bytes) if tk is None else _tile(k_dim, tk)
    grid = (m_rows // tm, n_cols // tn, k_dim // tk)
    n_k = grid[2]
    if ta:
        a_spec = pl.BlockSpec((tk, tm), lambda i, j, k: (k, i))
    else:
        a_spec = pl.BlockSpec((tm, tk), lambda i, j, k: (i, k))
    if b_spec is not None:
        b_blk = b_spec(tk)
    elif nt:
        b_blk = pl.BlockSpec((tn, tk), lambda i, j, k: (j, k))
    else:
        b_blk = pl.BlockSpec((tk, tn), lambda i, j, k: (k, j))
    n_extra, n_out = len(extra), len(out_shape)
    product = _dot_tn if ta else _dot_nt if nt else _dot

    def body(a_ref, b_ref, *rest):
        extra_refs = rest[:n_extra]
        out_refs = rest[n_extra:n_extra + n_out]
        if n_k == 1:
            epilogue(product(a_ref[...], b_ref[...]), extra_refs, out_refs)
            return
        acc_ref = rest[n_extra + n_out]
        k = pl.program_id(2)

        @pl.when(k == 0)
        def _():
            acc_ref[...] = jnp.zeros_like(acc_ref)

        acc_ref[...] += product(a_ref[...], b_ref[...])

        @pl.when(k == n_k - 1)
        def _():
            epilogue(acc_ref[...], extra_refs, out_refs)

    return pl.pallas_call(
        body, name=name, grid=grid,
        in_specs=[a_spec, b_blk, *extra_specs], out_specs=out_specs, out_shape=out_shape,
        scratch_shapes=[pltpu.VMEM((tm, tn), F32)] if n_k > 1 else [],
        compiler_params=_params(3),
    )(a, b, *extra)


def _mm_plain(name, a, b, *, nt=False, ta=False, n_cols=None, out_dtype=BF16, init=None,
              tm=1024, tn=1024, tk=None, b_spec=None, out_3d=None):
    m_rows = a.shape[1] if ta else a.shape[0]
    if n_cols is None:
        n_cols = b.shape[0] if nt else b.shape[1]
    tm, tn = _tile(m_rows, tm), _tile(n_cols, tn)
    fixed = 2 * tm * tn * (jnp.dtype(out_dtype).itemsize + (4 if init is not None else 0))
    if out_3d is None:
        shape = jax.ShapeDtypeStruct((m_rows, n_cols), out_dtype)
        spec = pl.BlockSpec((tm, tn), lambda i, j, k: (i, j))
    else:
        slabs, width = out_3d
        assert width % tn == 0 and slabs * width == n_cols
        per = width // tn
        shape = jax.ShapeDtypeStruct((slabs, m_rows, width), out_dtype)
        spec = pl.BlockSpec((None, tm, tn), lambda i, j, k: (j // per, i, j % per))
    extra, extra_specs = (), ()
    if init is not None:
        extra = (init,)
        extra_specs = (pl.BlockSpec((tm, tn), lambda i, j, k: (i, j)),)

    def epilogue(acc, extra_refs, out_refs):
        if init is not None:
            acc = acc + extra_refs[0][...]
        out_refs[0][...] = acc.astype(out_dtype)

    (out,) = _matmul(name, a, b, nt=nt, ta=ta, tm=tm, tn=tn, tk=tk, n_cols=n_cols,
                     out_shape=[shape], out_specs=[spec], epilogue=epilogue, fixed_bytes=fixed,
                     b_spec=b_spec, extra=extra, extra_specs=extra_specs)
    return out


def _mm_rope(name, a, b, cos, sin, *, n_cols, rope_cols, tn, b_spec=None):
    m_rows = a.shape[0]
    tm = _tile(m_rows, 1024)
    tn = _tile(n_cols, tn)
    assert rope_cols % tn == 0 and tn % HEAD_DIM == 0
    rope_blocks = rope_cols // tn
    table_spec = pl.BlockSpec((tm, LANE), lambda i, j, k: (i, 0))

    def epilogue(acc, extra_refs, out_refs):
        cos_ref, sin_ref = extra_refs
        j = pl.program_id(1)

        @pl.when(j < rope_blocks)
        def _():
            for head in range(tn // HEAD_DIM):
                cols = slice(head * HEAD_DIM, (head + 1) * HEAD_DIM)
                blk = acc[:, cols]
                rot = pltpu.roll(blk, HEAD_DIM // 2, 1)
                out_refs[0][:, cols] = (blk * cos_ref[...] + rot * sin_ref[...]).astype(BF16)

        @pl.when(j >= rope_blocks)
        def _():
            out_refs[0][...] = acc.astype(BF16)

    (out,) = _matmul(name, a, b, nt=False, tm=tm, tn=tn, n_cols=n_cols,
                     out_shape=[jax.ShapeDtypeStruct((m_rows, n_cols), BF16)],
                     out_specs=[pl.BlockSpec((tm, tn), lambda i, j, k: (i, j))],
                     epilogue=epilogue, fixed_bytes=4 * tm * tn + 16 * tm * LANE, b_spec=b_spec,
                     extra=(cos, sin), extra_specs=(table_spec, table_spec))
    return out


def _mm_residual(name, u, w, x_in, gate):
    m_rows, n_cols = x_in.shape
    tm, tn = _tile(m_rows, 512), _tile(n_cols, 1024)
    blk = pl.BlockSpec((tm, tn), lambda i, j, k: (i, j))

    def epilogue(acc, extra_refs, out_refs):
        x_ref, gate_ref = extra_refs
        out_refs[0][...] = acc
        out_refs[1][...] = x_ref[...] + gate_ref[...] * acc

    y, x_out = _matmul(
        name, u, w, nt=False, tm=tm, tn=tn, n_cols=n_cols,
        out_shape=[jax.ShapeDtypeStruct((m_rows, n_cols), F32)] * 2, out_specs=[blk, blk],
        epilogue=epilogue, fixed_bytes=3 * 8 * tm * tn, extra=(x_in, gate),
        extra_specs=(blk, pl.BlockSpec((1, tn), lambda i, j, k: (0, j))))
    return y, x_out


def _mm_gate_bwd(name, dy, w_out, z_src, z_col0, o):
    m_rows = dy.shape[0]
    n_cols = w_out.shape[0]
    tm, tn = _tile(m_rows, 1024), _tile(n_cols, 1024)
    assert z_col0 % tn == 0 and tn % HEAD_DIM == 0 and n_cols // HEAD_DIM <= LANE
    z_blk0 = z_col0 // tn
    blk = pl.BlockSpec((tm, tn), lambda i, j, k: (i, j))

    def epilogue(du, extra_refs, out_refs):
        z_ref, o_ref = extra_refs
        do_ref, dz_ref, delta_ref = out_refs
        j = pl.program_id(1)
        z = z_ref[...].astype(F32)
        o_val = o_ref[...].astype(F32)
        sig = _sigmoid(z)
        d_o = (du * (z * sig)).astype(BF16)
        do_ref[...] = d_o
        dz_ref[...] = (du * o_val * (sig * (1.0 + z * (1.0 - sig)))).astype(BF16)

        @pl.when(j == 0)
        def _():
            delta_ref[...] = jnp.zeros_like(delta_ref)

        prod = d_o.astype(F32) * o_val
        lane = _iota((tm, LANE), 1)
        delta = delta_ref[...]
        for head in range(tn // HEAD_DIM):
            rows = jnp.sum(prod[:, head * HEAD_DIM:(head + 1) * HEAD_DIM], axis=1, keepdims=True)
            delta = delta + jnp.where(lane == j * (tn // HEAD_DIM) + head, rows, 0.0)
        delta_ref[...] = delta

    d_o, dz, delta = _matmul(
        name, dy, w_out, nt=True, tm=tm, tn=tn, n_cols=n_cols,
        out_shape=[jax.ShapeDtypeStruct((m_rows, n_cols), BF16)] * 2
        + [jax.ShapeDtypeStruct((m_rows, LANE), F32)],
        out_specs=[blk, blk, pl.BlockSpec((tm, LANE), lambda i, j, k: (i, 0))],
        epilogue=epilogue, fixed_bytes=4 * 4 * tm * tn + 8 * tm * LANE, extra=(z_src, o),
        extra_specs=(pl.BlockSpec((tm, tn), lambda i, j, k: (i, z_blk0 + j)), blk))
    return d_o, dz, delta


def _norm_fwd(name, x, ga, sa, ta, gb=None):
    s_len, d = x.shape
    tr = _tile(s_len, 256)
    two = gb is not None
    row = pl.BlockSpec((tr, d), lambda i: (i, 0))
    vec = pl.BlockSpec((1, d), lambda i: (0, 0))

    def body(x_ref, ga_ref, sa_ref, ta_ref, *rest):
        xv = x_ref[...]
        y = xv * lax.rsqrt(jnp.mean(xv * xv, axis=-1, keepdims=True) + RMS_EPS)
        rest[-2 if two else -1][...] = ((y * ga_ref[...]) * (1.0 + sa_ref[...]) + ta_ref[...]).astype(BF16)
        if two:
            rest[-1][...] = (y * rest[0][...]).astype(BF16)

    ins = [x, ga, sa, ta] + ([gb] if two else [])
    outs = pl.pallas_call(
        body, name=name, grid=(s_len // tr,),
        in_specs=[row] + [vec] * (len(ins) - 1),
        out_specs=[row] * (2 if two else 1),
        out_shape=[jax.ShapeDtypeStruct((s_len, d), BF16)] * (2 if two else 1),
        compiler_params=_params(1))(*ins)
    return outs if two else outs[0]


def _loss_bwd(x2, target, y1, g_final, gate1):
    s_len, d = x2.shape
    tr = _tile(s_len, 128)
    row = pl.BlockSpec((tr, d), lambda i: (i, 0))
    vec = pl.BlockSpec((1, d), lambda i: (0, 0))

    def body(x_ref, t_ref, y_ref, g_ref, gate_ref, loss_ref, dx_ref, dy_ref, sums_ref):
        @pl.when(pl.program_id(0) == 0)
        def _():
            loss_ref[...] = jnp.zeros_like(loss_ref)
            sums_ref[...] = jnp.zeros_like(sums_ref)

        xv = x_ref[...]
        rstd = lax.rsqrt(jnp.mean(xv * xv, axis=-1, keepdims=True) + RMS_EPS)
        xhat = xv * rstd
        g = g_ref[...]
        err = xhat * g - t_ref[...]
        sq = jnp.sum(jnp.sum(err * err, axis=1, keepdims=True), axis=0, keepdims=True)
        loss_ref[...] += sq * (0.5 / d)
        dout = err * (1.0 / d)
        dxhat = dout * g
        dx = rstd * (dxhat - xhat * jnp.mean(dxhat * xhat, axis=-1, keepdims=True))
        dx_ref[...] = dx
        dy_ref[...] = (dx * gate_ref[...]).astype(BF16)
        sums_ref[0:1, :] += jnp.sum(dout * xhat, axis=0, keepdims=True)
        sums_ref[1:2, :] += jnp.sum(dx * y_ref[...], axis=0, keepdims=True)

    return pl.pallas_call(
        body, name="loss_bwd", grid=(s_len // tr,),
        in_specs=[row, row, row, vec, vec],
        out_specs=[pl.BlockSpec((SUBLANE, LANE), lambda i: (0, 0)), row, row,
                   pl.BlockSpec((SUBLANE, d), lambda i: (0, 0))],
        out_shape=[jax.ShapeDtypeStruct((SUBLANE, LANE), F32),
                   jax.ShapeDtypeStruct((s_len, d), F32),
                   jax.ShapeDtypeStruct((s_len, d), BF16),
                   jax.ShapeDtypeStruct((SUBLANE, d), F32)],
        compiler_params=_params(1))(x2, target, y1, g_final, gate1)


def _norm_bwd(name, x, dres, dha, ga, sa, dhb=None, gb=None, y=None, gate=None):
    s_len, d = x.shape
    tr = _tile(s_len, 128)
    has_b, has_y = dhb is not None, y is not None
    row = pl.BlockSpec((tr, d), lambda i: (i, 0))
    vec = pl.BlockSpec((1, d), lambda i: (0, 0))
    ins, specs = [x, dres, dha, ga, sa], [row, row, row, vec, vec]
    if has_b:
        ins += [dhb, gb]
        specs += [row, vec]
    if has_y:
        ins += [y, gate]
        specs += [row, vec]
    n_in = len(ins)

    def body(*refs):
        x_ref, dres_ref, dha_ref, ga_ref, sa_ref = refs[:5]
        pos = 5
        if has_b:
            dhb_ref, gb_ref = refs[pos:pos + 2]
            pos += 2
        if has_y:
            y_ref, gate_ref = refs[pos:pos + 2]
        outs = refs[n_in:]
        dx_ref, sums_ref = outs[0], outs[-1]

        @pl.when(pl.program_id(0) == 0)
        def _():
            sums_ref[...] = jnp.zeros_like(sums_ref)

        xv = x_ref[...]
        rstd = lax.rsqrt(jnp.mean(xv * xv, axis=-1, keepdims=True) + RMS_EPS)
        xhat = xv * rstd
        dha_v = dha_ref[...]
        ga_v, sa_v = ga_ref[...], sa_ref[...]
        dxhat = dha_v * (ga_v * (1.0 + sa_v))
        sums_ref[0:1, :] += jnp.sum(dha_v, axis=0, keepdims=True)
        sums_ref[1:2, :] += jnp.sum(dha_v * (xhat * ga_v), axis=0, keepdims=True)
        sums_ref[2:3, :] += jnp.sum(dha_v * ((1.0 + sa_v) * xhat), axis=0, keepdims=True)
        if has_b:
            dhb_v = dhb_ref[...]
            dxhat = dxhat + dhb_v * gb_ref[...]
            sums_ref[3:4, :] += jnp.sum(dhb_v * xhat, axis=0, keepdims=True)
        dx = dres_ref[...] + rstd * (dxhat - xhat * jnp.mean(dxhat * xhat, axis=-1, keepdims=True))
        dx_ref[...] = dx
        if has_y:
            outs[1][...] = (dx * gate_ref[...]).astype(BF16)
            sums_ref[4:5, :] += jnp.sum(dx * y_ref[...], axis=0, keepdims=True)

    out_shape = [jax.ShapeDtypeStruct((s_len, d), F32)]
    out_specs = [row]
    if has_y:
        out_shape.append(jax.ShapeDtypeStruct((s_len, d), BF16))
        out_specs.append(row)
    out_shape.append(jax.ShapeDtypeStruct((SUBLANE, d), F32))
    out_specs.append(pl.BlockSpec((SUBLANE, d), lambda i: (0, 0)))
    return pl.pallas_call(body, name=name, grid=(s_len // tr,), in_specs=specs,
                          out_specs=out_specs, out_shape=out_shape,
                          compiler_params=_params(1))(*ins)


def _fgate_fwd(f_raw, bias_row):
    s_len = f_raw.shape[0]
    tb = _tile(s_len, 512)
    blk = pl.BlockSpec((tb, LANE), lambda t: (t, 0))

    def body(f_ref, b_ref, out_ref, carry):
        @pl.when(pl.program_id(0) == 0)
        def _():
            carry[...] = jnp.zeros_like(carry)

        u = f_ref[...] + b_ref[...]
        logf = jnp.minimum(u, 0.0) - jnp.log1p(jnp.exp(-jnp.abs(u)))
        tri = (_iota((tb, tb), 1) <= _iota((tb, tb), 0)).astype(F32)
        run = jnp.dot(tri, logf, precision=HIGHEST, preferred_element_type=F32) + carry[...]
        out_ref[...] = run
        carry[...] = run[tb - 1:tb, :]

    return pl.pallas_call(
        body, name="fgate_fwd", grid=(s_len // tb,),
        in_specs=[blk, pl.BlockSpec((1, LANE), lambda t: (0, 0))], out_specs=blk,
        out_shape=jax.ShapeDtypeStruct((s_len, LANE), F32),
        scratch_shapes=[pltpu.VMEM((1, LANE), F32)],
        compiler_params=_params(1))(f_raw, bias_row)


def _fgate_bwd(df_a, df_b, f_raw, bias_row):
    s_len = f_raw.shape[0]
    tb = _tile(s_len, 512)
    nb = s_len // tb
    blk = pl.BlockSpec((tb, LANE), lambda t: (nb - 1 - t, 0))

    def body(a_ref, b2_ref, f_ref, b_ref, df_ref, sums_ref, carry):
        @pl.when(pl.program_id(0) == 0)
        def _():
            carry[...] = jnp.zeros_like(carry)
            sums_ref[...] = jnp.zeros_like(sums_ref)

        d_run = a_ref[...] + b2_ref[...]
        tri = (_iota((tb, tb), 1) >= _iota((tb, tb), 0)).astype(F32)
        dlogf = jnp.dot(tri, d_run, precision=HIGHEST, preferred_element_type=F32) + carry[...]
        carry[...] = dlogf[0:1, :]
        u = f_ref[...] + b_ref[...]
        df = dlogf * _sigmoid(-u)
        df_ref[...] = df.astype(BF16)
        sums_ref[...] += jnp.sum(df, axis=0, keepdims=True)

    return pl.pallas_call(
        body, name="fgate_bwd", grid=(nb,),
        in_specs=[blk, blk, blk, pl.BlockSpec((1, LANE), lambda t: (0, 0))],
        out_specs=[blk, pl.BlockSpec((SUBLANE, LANE), lambda t: (0, 0))],
        out_shape=[jax.ShapeDtypeStruct((s_len, LANE), BF16),
                   jax.ShapeDtypeStruct((SUBLANE, LANE), F32)],
        scratch_shapes=[pltpu.VMEM((1, LANE), F32)],
        compiler_params=_params(1))(df_a, df_b, f_raw, bias_row)


def _fox_fwd(proj, v_t, f_nat, f_t, heads):
    s_len = proj.shape[0]
    d = heads * HEAD_DIM
    t = _tile(s_len, 512)
    nq = s_len // t
    scale = HEAD_DIM ** -0.5

    def body(k_ref, q_ref, vt_ref, fn_ref, ft_ref, ot_ref, lse_ref,
             acc_scr, m_scr, l_scr, fk_scr):
        j, i = pl.program_id(1), pl.program_id(2)
        h = pl.program_id(0)

        @pl.when((j == 0) & (i == 0))
        def _():
            m_scr[...] = jnp.full_like(m_scr, NEG)
            l_scr[...] = jnp.zeros_like(l_scr)
            acc_scr[...] = jnp.zeros_like(acc_scr)

        def step(diagonal):
            s_t = _dot_nt(k_ref[...], q_ref[...]) * scale + (ft_ref[...] - fk_scr[...])
            if diagonal:
                s_t = jnp.where(_iota((t, t), 0) <= _iota((t, t), 1), s_t, NEG)
            m_prev = m_scr[i]
            m_new = jnp.maximum(m_prev, jnp.max(s_t, axis=0, keepdims=True))
            alpha = jnp.exp(m_prev - m_new)
            p_t = jnp.exp(s_t - m_new)
            l_scr[i] = alpha * l_scr[i] + jnp.sum(p_t, axis=0, keepdims=True)
            acc_scr[i] = alpha * acc_scr[i] + _dot(vt_ref[...], p_t.astype(BF16))
            m_scr[i] = m_new

        @pl.when(i == j)
        def _():
            fk_scr[...] = _pick_lane(fn_ref[...], h)
            step(True)

        @pl.when(i > j)
        def _():
            step(False)

        @pl.when((i == nq - 1) & (j == nq - 1))
        def _():
            for blk in range(nq):
                cols = slice(blk * t, (blk + 1) * t)
                l_sum = l_scr[blk]
                ot_ref[:, cols] = acc_scr[blk] / l_sum
                lse_ref[:, cols] = m_scr[blk] + jnp.log(l_sum)

    qry = pl.BlockSpec((t, HEAD_DIM), lambda h, j, i: (jnp.maximum(i, j), h))
    return pl.pallas_call(
        body, name="fox_fwd", grid=(heads, nq, nq),
        in_specs=[pl.BlockSpec((t, HEAD_DIM), lambda h, j, i: (j, heads + h)), qry,
                  pl.BlockSpec((HEAD_DIM, t), lambda h, j, i: (h, j)),
                  pl.BlockSpec((t, LANE), lambda h, j, i: (j, 0)),
                  pl.BlockSpec((None, 1, t), lambda h, j, i: (h, 0, jnp.maximum(i, j)))],
        out_specs=[pl.BlockSpec((HEAD_DIM, s_len), lambda h, j, i: (h, 0)),
                   pl.BlockSpec((None, 1, s_len), lambda h, j, i: (h, 0, 0))],
        out_shape=[jax.ShapeDtypeStruct((d, s_len), F32),
                   jax.ShapeDtypeStruct((heads, 1, s_len), F32)],
        scratch_shapes=[pltpu.VMEM((nq, HEAD_DIM, t), F32), pltpu.VMEM((nq, 1, t), F32),
                        pltpu.VMEM((nq, 1, t), F32), pltpu.VMEM((t, 1), F32)],
        compiler_params=_params(3))(proj, proj, v_t, f_nat, f_t)


def _gate_fwd(o_t, proj, z_blk0, heads):
    d, s_len = o_t.shape
    t = _tile(s_len, 512)

    def body(ot_ref, z_ref, o_ref, u_ref):
        o_val = jnp.transpose(ot_ref[...])
        o_ref[...] = o_val.astype(BF16)
        z = z_ref[...].astype(F32)
        u_ref[...] = (o_val * (z * _sigmoid(z))).astype(BF16)

    out_blk = pl.BlockSpec((t, HEAD_DIM), lambda i, h: (i, h))
    return pl.pallas_call(
        body, name="gate_fwd", grid=(s_len // t, heads),
        in_specs=[pl.BlockSpec((HEAD_DIM, t), lambda i, h: (h, i)),
                  pl.BlockSpec((t, HEAD_DIM), lambda i, h: (i, z_blk0 + h))],
        out_specs=[out_blk, out_blk],
        out_shape=[jax.ShapeDtypeStruct((s_len, d), BF16)] * 2,
        compiler_params=_params(2))(o_t, proj)


def _fox_bwd(proj, k_t, d_o, f_nat, f_t, lse_t, delta_t, heads):
    s_len = proj.shape[0]
    d = heads * HEAD_DIM
    t = _tile(s_len, 512)
    nq = s_len // t
    scale = HEAD_DIM ** -0.5

    def body(k_ref, v_ref, kt_ref, q_ref, do_ref, fn_ref, ft_ref, lse_ref, delta_ref,
             dk_ref, dv_ref, dqt_ref, dfq_ref, dfk_ref,
             dk_acc, dv_acc, dq_acc, dfq_acc, dfk_acc, fk_scr):
        h, j, i = pl.program_id(0), pl.program_id(1), pl.program_id(2)
        head_start = (j == 0) & (i == 0)

        @pl.when(head_start)
        def _():
            dq_acc[...] = jnp.zeros_like(dq_acc)
            dfq_acc[...] = jnp.zeros_like(dfq_acc)

        @pl.when(head_start & (h == 0))
        def _():
            dfk_ref[...] = jnp.zeros_like(dfk_ref)

        def step(diagonal):
            q = q_ref[...]
            d_out = do_ref[...]
            s_t = _dot_nt(k_ref[...], q) * scale + (ft_ref[...] - fk_scr[...]) - lse_ref[...]
            if diagonal:
                s_t = jnp.where(_iota((t, t), 0) <= _iota((t, t), 1), s_t, NEG)
            p_t = jnp.exp(s_t)
            dp_t = _dot_nt(v_ref[...], d_out)
            ds_t = p_t * (dp_t - delta_ref[...])
            ds_b = ds_t.astype(BF16)
            dv_acc[...] += _dot(p_t.astype(BF16), d_out)
            dk_acc[...] += _dot(ds_b, q)
            dq_acc[i] += _dot(kt_ref[...], ds_b)
            dfq_acc[i] += jnp.sum(ds_t, axis=0, keepdims=True)
            dfk_acc[...] += jnp.sum(ds_t, axis=1, keepdims=True)

        @pl.when(i == j)
        def _():
            dk_acc[...] = jnp.zeros_like(dk_acc)
            dv_acc[...] = jnp.zeros_like(dv_acc)
            dfk_acc[...] = jnp.zeros_like(dfk_acc)
            fk_scr[...] = _pick_lane(fn_ref[...], h)
            step(True)

        @pl.when(i > j)
        def _():
            step(False)

        @pl.when(i == nq - 1)
        def _():
            dk_ref[...] = (dk_acc[...] * scale).astype(BF16)
            dv_ref[...] = dv_acc[...].astype(BF16)
            rows = pl.ds(pl.multiple_of(j * t, t), t)
            dfk_ref[rows, :] += jnp.where(_iota((t, LANE), 1) == h, -dfk_acc[...], 0.0)

        @pl.when((i == nq - 1) & (j == nq - 1))
        def _():
            for blk in range(nq):
                cols = slice(blk * t, (blk + 1) * t)
                dqt_ref[:, cols] = (dq_acc[blk] * scale).astype(BF16)
                dfq_ref[:, cols] = dfq_acc[blk]

    key_col = lambda base: pl.BlockSpec((t, HEAD_DIM), lambda h, j, i: (j, base + h))
    qry = pl.BlockSpec((t, HEAD_DIM), lambda h, j, i: (jnp.maximum(i, j), h))
    qry_row = pl.BlockSpec((None, 1, t), lambda h, j, i: (h, 0, jnp.maximum(i, j)))
    kv_out = pl.BlockSpec((t, HEAD_DIM), lambda h, j, i: (j, h))
    return pl.pallas_call(
        body, name="fox_bwd", grid=(heads, nq, nq),
        in_specs=[key_col(heads), key_col(2 * heads),
                  pl.BlockSpec((HEAD_DIM, t), lambda h, j, i: (h, j)),
                  qry, qry, pl.BlockSpec((t, LANE), lambda h, j, i: (j, 0)),
                  qry_row, qry_row, qry_row],
        out_specs=[kv_out, kv_out,
                   pl.BlockSpec((HEAD_DIM, s_len), lambda h, j, i: (h, 0)),
                   pl.BlockSpec((None, 1, s_len), lambda h, j, i: (h, 0, 0)),
                   pl.BlockSpec((s_len, LANE), lambda h, j, i: (0, 0))],
        out_shape=[jax.ShapeDtypeStruct((s_len, d), BF16), jax.ShapeDtypeStruct((s_len, d), BF16),
                   jax.ShapeDtypeStruct((d, s_len), BF16),
                   jax.ShapeDtypeStruct((heads, 1, s_len), F32),
                   jax.ShapeDtypeStruct((s_len, LANE), F32)],
        scratch_shapes=[pltpu.VMEM((t, HEAD_DIM), F32), pltpu.VMEM((t, HEAD_DIM), F32),
                        pltpu.VMEM((nq, HEAD_DIM, t), F32), pltpu.VMEM((nq, 1, t), F32),
                        pltpu.VMEM((t, 1), F32), pltpu.VMEM((t, 1), F32)],
        compiler_params=_params(3))(proj, proj, k_t, proj, d_o, f_nat, f_t, lse_t, delta_t)


def _swa_specs(heads, kv_heads):
    width = heads // kv_heads * HEAD_DIM
    wide = lambda base: pl.BlockSpec((SWA_BLOCK, width), lambda n, g: (n, base + g))
    blk = lambda fn: pl.BlockSpec((SWA_BLOCK, HEAD_DIM), fn)
    prev = lambda base: blk(lambda n, g: (jnp.maximum(n - 1, 0), base + g))
    cur = lambda base: blk(lambda n, g: (n, base + g))
    return wide, prev, cur


def _swa_scores(q, kp, kc, n, scale):
    r, c = _iota((SWA_BLOCK, SWA_BLOCK), 0), _iota((SWA_BLOCK, SWA_BLOCK), 1)
    sp = jnp.where((c > r) & (n > 0), _dot_nt(q, kp) * scale, NEG)
    sc = jnp.where(c <= r, _dot_nt(q, kc) * scale, NEG)
    return sp, sc


def _swa_fwd(proj, kv, sinks_row, heads, kv_heads):
    s_len = proj.shape[0]
    d = heads * HEAD_DIM
    scale = HEAD_DIM ** -0.5
    group = heads // kv_heads
    wide, prev, cur = _swa_specs(heads, kv_heads)

    def body(q_ref, z_ref, kp_ref, kc_ref, vp_ref, vc_ref, sink_ref, o_ref, u_ref, lse_ref):
        n, g = pl.program_id(0), pl.program_id(1)
        kp, kc, vp, vc = kp_ref[...], kc_ref[...], vp_ref[...], vc_ref[...]
        lane = _iota((SWA_BLOCK, LANE), 1)
        lse_all = jnp.zeros((SWA_BLOCK, LANE), F32)
        for hh in range(group):
            cols = slice(hh * HEAD_DIM, (hh + 1) * HEAD_DIM)
            head = g * group + hh
            sp, sc = _swa_scores(q_ref[:, cols], kp, kc, n, scale)
            sink = _pick_lane(sink_ref[...], head)
            m = jnp.maximum(jnp.maximum(jnp.max(sp, axis=1, keepdims=True),
                                        jnp.max(sc, axis=1, keepdims=True)), sink)
            pp, pc = jnp.exp(sp - m), jnp.exp(sc - m)
            den = (jnp.sum(pp, axis=1, keepdims=True) + jnp.sum(pc, axis=1, keepdims=True)
                   + jnp.exp(sink - m))
            o_val = (_dot(pp.astype(BF16), vp) + _dot(pc.astype(BF16), vc)) / den
            o_ref[:, cols] = o_val.astype(BF16)
            z = z_ref[:, cols].astype(F32)
            u_ref[:, cols] = (o_val * (z * _sigmoid(z))).astype(BF16)
            lse_all = lse_all + jnp.where(lane == head, m + jnp.log(den), 0.0)

        @pl.when(g == 0)
        def _():
            lse_ref[...] = lse_all

        @pl.when(g > 0)
        def _():
            lse_ref[...] += lse_all

    nat = pl.BlockSpec((SWA_BLOCK, LANE), lambda n, g: (n, 0))
    return pl.pallas_call(
        body, name="swa_fwd", grid=(s_len // SWA_BLOCK, kv_heads),
        in_specs=[wide(0), wide(kv_heads), prev(0), cur(0), prev(kv_heads), cur(kv_heads),
                  pl.BlockSpec((1, LANE), lambda n, g: (0, 0))],
        out_specs=[wide(0), wide(0), nat],
        out_shape=[jax.ShapeDtypeStruct((s_len, d), BF16), jax.ShapeDtypeStruct((s_len, d), BF16),
                   jax.ShapeDtypeStruct((s_len, LANE), F32)],
        compiler_params=_params(2))(proj, proj, kv, kv, kv, kv, sinks_row)


def _swa_bwd_q(proj, kv, d_o, lse, delta, sinks_row, cos, sin, heads, kv_heads):
    s_len = proj.shape[0]
    d = heads * HEAD_DIM
    scale = HEAD_DIM ** -0.5
    group = heads // kv_heads
    wide, prev, cur = _swa_specs(heads, kv_heads)

    def body(q_ref, kp_ref, kc_ref, vp_ref, vc_ref, do_ref, lse_ref, delta_ref, sink_ref,
             cos_ref, sin_ref, dq_ref, dsink_ref):
        n, g = pl.program_id(0), pl.program_id(1)

        @pl.when((n == 0) & (g == 0))
        def _():
            dsink_ref[...] = jnp.zeros_like(dsink_ref)

        kp, kc, vp, vc = kp_ref[...], kc_ref[...], vp_ref[...], vc_ref[...]
        lse_blk, delta_blk = lse_ref[...], delta_ref[...]
        cos_v, sin_v = cos_ref[...], sin_ref[...]
        lane = _iota((SUBLANE, LANE), 1)
        dsink_all = jnp.zeros((SUBLANE, LANE), F32)
        for hh in range(group):
            cols = slice(hh * HEAD_DIM, (hh + 1) * HEAD_DIM)
            head = g * group + hh
            sp, sc = _swa_scores(q_ref[:, cols], kp, kc, n, scale)
            lse_col = _pick_lane(lse_blk, head)
            delta_col = _pick_lane(delta_blk, head)
            pp, pc = jnp.exp(sp - lse_col), jnp.exp(sc - lse_col)
            p_sink = jnp.exp(_pick_lane(sink_ref[...], head) - lse_col)
            d_out = do_ref[:, cols]
            dsp = pp * (_dot_nt(d_out, vp) - delta_col)
            dsc = pc * (_dot_nt(d_out, vc) - delta_col)
            dq = (_dot(dsp.astype(BF16), kp) + _dot(dsc.astype(BF16), kc)) * scale
            dq_ref[:, cols] = (dq * cos_v - pltpu.roll(dq, HEAD_DIM // 2, 1) * sin_v).astype(BF16)
            d_sink = jnp.sum(-p_sink * delta_col, axis=0, keepdims=True)
            dsink_all = dsink_all + jnp.where(lane == head, d_sink, 0.0)
        dsink_ref[...] += dsink_all

    own = wide(0)
    nat = pl.BlockSpec((SWA_BLOCK, LANE), lambda n, g: (n, 0))
    return pl.pallas_call(
        body, name="swa_bwd_q", grid=(s_len // SWA_BLOCK, kv_heads),
        in_specs=[own, prev(0), cur(0), prev(kv_heads), cur(kv_heads), own, nat, nat,
                  pl.BlockSpec((1, LANE), lambda n, g: (0, 0)), nat, nat],
        out_specs=[own, pl.BlockSpec((SUBLANE, LANE), lambda n, g: (0, 0))],
        out_shape=[jax.ShapeDtypeStruct((s_len, d), BF16),
                   jax.ShapeDtypeStruct((SUBLANE, LANE), F32)],
        compiler_params=_params(2))(proj, kv, kv, kv, kv, d_o, lse, delta, sinks_row, cos, sin)


def _swa_bwd_kv(proj, kv, d_o, lse_t, delta_t, cos, sin, heads, kv_heads):
    s_len = proj.shape[0]
    nb = s_len // SWA_BLOCK
    group = heads // kv_heads
    scale = HEAD_DIM ** -0.5

    def body(k_ref, v_ref, qm_ref, qn_ref, dom_ref, don_ref, lsem_ref, lsen_ref,
             deltam_ref, deltan_ref, cos_ref, sin_ref, dk_ref, dv_ref):
        m = pl.program_id(1)
        k, v = k_ref[...], v_ref[...]
        key, qry = _iota((SWA_BLOCK, SWA_BLOCK), 0), _iota((SWA_BLOCK, SWA_BLOCK), 1)
        own_valid = key <= qry
        next_valid = (key > qry) & (m + 1 < nb)
        dk = jnp.zeros((SWA_BLOCK, HEAD_DIM), F32)
        dv = jnp.zeros((SWA_BLOCK, HEAD_DIM), F32)
        for hh in range(group):
            cols = slice(hh * HEAD_DIM, (hh + 1) * HEAD_DIM)
            for q_ref, do_ref, lse_ref, delta_ref, valid in (
                    (qm_ref, dom_ref, lsem_ref, deltam_ref, own_valid),
                    (qn_ref, don_ref, lsen_ref, deltan_ref, next_valid)):
                q, d_out = q_ref[:, cols], do_ref[:, cols]
                s_t = _dot_nt(k, q) * scale
                p_t = jnp.exp(jnp.where(valid, s_t - lse_ref[hh], NEG))
                ds_t = p_t * (_dot_nt(v, d_out) - delta_ref[hh])
                dv = dv + _dot(p_t.astype(BF16), d_out)
                dk = dk + _dot(ds_t.astype(BF16), q)
        dk = dk * scale
        dk_ref[...] = (dk * cos_ref[...]
                       - pltpu.roll(dk, HEAD_DIM // 2, 1) * sin_ref[...]).astype(BF16)
        dv_ref[...] = dv.astype(BF16)

    blk = lambda fn: pl.BlockSpec((SWA_BLOCK, HEAD_DIM), fn)
    nxt = lambda m: jnp.minimum(m + 1, nb - 1)
    wide = lambda fn: pl.BlockSpec((SWA_BLOCK, group * HEAD_DIM), fn)
    rows = lambda fn: pl.BlockSpec((group, 1, SWA_BLOCK), fn)
    q_m, q_n = wide(lambda g, m: (m, g)), wide(lambda g, m: (nxt(m), g))
    r_m, r_n = rows(lambda g, m: (g, 0, m)), rows(lambda g, m: (g, 0, nxt(m)))
    nat = pl.BlockSpec((SWA_BLOCK, LANE), lambda g, m: (m, 0))
    out_blk = blk(lambda g, m: (m, g))
    width = kv_heads * HEAD_DIM
    return pl.pallas_call(
        body, name="swa_bwd_kv", grid=(kv_heads, nb),
        in_specs=[blk(lambda g, m: (m, g)), blk(lambda g, m: (m, kv_heads + g)),
                  q_m, q_n, q_m, q_n, r_m, r_n, r_m, r_n, nat, nat],
        out_specs=[out_blk, out_blk],
        out_shape=[jax.ShapeDtypeStruct((s_len, width), BF16)] * 2,
        compiler_params=_params(2))(kv, kv, proj, proj, d_o, d_o, lse_t, lse_t,
                                    delta_t, delta_t, cos, sin)


def _ada_fwd(c_rows, ada_w, bias_loc):
    n_layers, d, cols = ada_w.shape
    rows = c_rows.shape[0]
    tk = _tile(d, 512)
    n_k = d // tk

    def body(c_ref, w_ref, b_ref, mod_ref, sc_ref, acc_ref):
        k = pl.program_id(1)

        @pl.when(k == 0)
        def _():
            acc_ref[...] = jnp.zeros_like(acc_ref)

        cv = c_ref[...]
        sc = cv * _sigmoid(cv)
        sc_ref[...] = sc
        acc_ref[...] += _dot(sc.astype(BF16), w_ref[...].astype(BF16))

        @pl.when(k == n_k - 1)
        def _():
            mod_ref[...] = acc_ref[...] + b_ref[...]

    return pl.pallas_call(
        body, name="ada_fwd", grid=(n_layers, n_k),
        in_specs=[pl.BlockSpec((rows, tk), lambda l, k: (0, k)),
                  pl.BlockSpec((None, tk, cols), lambda l, k: (l, k, 0)),
                  pl.BlockSpec((None, 1, cols), lambda l, k: (l, 0, 0))],
        out_specs=[pl.BlockSpec((None, rows, cols), lambda l, k: (l, 0, 0)),
                   pl.BlockSpec((None, rows, tk), lambda l, k: (l, 0, k))],
        out_shape=[jax.ShapeDtypeStruct((n_layers, rows, cols), F32),
                   jax.ShapeDtypeStruct((n_layers, rows, d), F32)],
        scratch_shapes=[pltpu.VMEM((rows, cols), F32)],
        compiler_params=_params(2))(c_rows, ada_w, bias_loc)


def _ada_update(sc_t, dmod, w, m, v):
    n_layers, d, cols = w.shape
    tr = _tile(d, 256)
    big = pl.BlockSpec((None, tr, cols), lambda l, i: (l, i, 0))

    def body(sc_ref, dm_ref, w_ref, m_ref, v_ref, g_out, d_out, m_out, v_out):
        g = _dot(sc_ref[...], dm_ref[...])
        delta, m_new, v_new = _adamw(w_ref[...], g, m_ref[...], v_ref[...])
        g_out[...] = g
        d_out[...] = delta
        m_out[...] = m_new
        v_out[...] = v_new

    return pl.pallas_call(
        body, name="ada_update", grid=(n_layers, d // tr),
        in_specs=[pl.BlockSpec((tr, LANE), lambda l, i: (i, 0)),
                  pl.BlockSpec((None, LANE, cols), lambda l, i: (l, 0, 0)), big, big, big],
        out_specs=[big] * 4, out_shape=[jax.ShapeDtypeStruct(w.shape, F32)] * 4,
        compiler_params=_params(2))(sc_t, dmod, w, m, v)


def _row_tile(rows, bytes_per_row):
    tr = SUBLANE * 2
    while tr * 2 <= rows and rows % (tr * 2) == 0 and tr * 2 * bytes_per_row <= 24 * 2 ** 20:
        tr *= 2
    return _tile(rows, tr)


def _slab_sum(name, arrays):
    rows = arrays[0].shape[1]
    per_row = sum(2 * a.shape[2] * (a.shape[0] * a.dtype.itemsize + 4) for a in arrays)
    tr = _row_tile(rows, per_row)
    n = len(arrays)

    def body(*refs):
        for s_ref, out_ref in zip(refs[:n], refs[n:]):
            total = s_ref[0].astype(F32)
            for slot in range(1, s_ref.shape[0]):
                total = total + s_ref[slot].astype(F32)
            out_ref[...] = total

    return pl.pallas_call(
        body, name=name, grid=(rows // tr,),
        in_specs=[pl.BlockSpec((a.shape[0], tr, a.shape[2]), lambda i: (0, i, 0)) for a in arrays],
        out_specs=[pl.BlockSpec((tr, a.shape[2]), lambda i: (i, 0)) for a in arrays],
        out_shape=[jax.ShapeDtypeStruct(a.shape[1:], F32) for a in arrays],
        compiler_params=_params(1))(*arrays)


def _shard_update(name, slabs, w, m, v):
    rows, cols = w.shape
    n_slabs = slabs.shape[0]
    tr = _row_tile(rows, 2 * cols * (slabs.dtype.itemsize * n_slabs + 4 * 7))
    blk = pl.BlockSpec((tr, cols), lambda i: (i, 0))

    def body(s_ref, w_ref, m_ref, v_ref, g_out, d_out, m_out, v_out):
        g = s_ref[0].astype(F32)
        for slot in range(1, n_slabs):
            g = g + s_ref[slot].astype(F32)
        delta, m_new, v_new = _adamw(w_ref[...], g, m_ref[...], v_ref[...])
        g_out[...] = g
        d_out[...] = delta
        m_out[...] = m_new
        v_out[...] = v_new

    return pl.pallas_call(
        body, name=name, grid=(rows // tr,),
        in_specs=[pl.BlockSpec((n_slabs, tr, cols), lambda i: (0, i, 0)), blk, blk, blk],
        out_specs=[blk] * 4, out_shape=[jax.ShapeDtypeStruct((rows, cols), F32)] * 4,
        compiler_params=_params(1))(slabs, w, m, v)


def _small_update(gathered, w, m, v):
    shape = jax.ShapeDtypeStruct(w.shape, F32)

    def body(g_ref, w_ref, m_ref, v_ref, g_out, d_out, m_out, v_out):
        g = g_ref[0]
        for dev in range(1, N_DEV):
            g = g + g_ref[dev]
        delta, m_new, v_new = _adamw(w_ref[...], g, m_ref[...], v_ref[...])
        g_out[...] = g
        d_out[...] = delta
        m_out[...] = m_new
        v_out[...] = v_new

    return pl.pallas_call(body, name="small_update", out_shape=[shape] * 4,
                          compiler_params=pltpu.CompilerParams(vmem_limit_bytes=VMEM_LIMIT),
                          )(gathered, w, m, v)


def _rope_tables(s_len):
    half = HEAD_DIM // 2
    inv = ROPE_THETA ** (-jnp.arange(half, dtype=F32) / half)
    ang = jnp.arange(s_len, dtype=F32)[:, None] * inv[None, :]
    cos, sin = jnp.cos(ang), jnp.sin(ang)
    return jnp.concatenate([cos, cos], axis=1), jnp.concatenate([-sin, sin], axis=1)


def _pad_lanes(a):
    return jnp.pad(a, ((0, 0), (0, LANE - a.shape[1])))


def _rows_of(nat, heads):
    return jnp.transpose(nat[:, :heads])[:, None, :]


def _pack(parts):
    tile = SUBLANE * LANE
    flat = []
    for p in parts:
        p = p.reshape(-1)
        flat.append(jnp.pad(p, (0, (-p.shape[0]) % tile)))
    return jnp.concatenate(flat).reshape(-1, LANE)


def _unpack(packed, shapes):
    tile = SUBLANE * LANE
    flat = packed.reshape(-1)
    out, pos = [], 0
    for shape in shapes:
        size = 1
        for dim in shape:
            size *= dim
        out.append(flat[pos:pos + size].reshape(shape))
        pos += size + (-size) % tile
    return out


def kernel(x, c, norm_g, ada_w, ada_b, a_w_in, a_b_f, a_w_out, kv_norm_g, kv_w, b_w_in, b_sinks, b_w_out, final_norm_g, loss_target, m_norm_g, m_ada_w, m_ada_b, m_a_w_in, m_a_b_f, m_a_w_out, m_kv_norm_g, m_kv_w, m_b_w_in, m_b_sinks, m_b_w_out, m_final_norm_g, v_norm_g, v_ada_w, v_ada_b, v_a_w_in, v_a_b_f, v_a_w_out, v_kv_norm_g, v_kv_w, v_b_w_in, v_b_sinks, v_b_w_out, v_final_norm_g):
    s_len, d = x.shape[1], x.shape[2]
    heads = d // HEAD_DIM
    kv_heads = kv_w.shape[1] // (2 * HEAD_DIM)
    kv_width = kv_heads * HEAD_DIM
    ada_cols = ada_w.shape[2]
    assert heads <= LANE and heads % N_DEV == 0 and a_w_in.shape[2] * N_DEV == 4 * d + heads
    me = _slot(_mesh_pos())
    x0 = x[0]
    target = loss_target[0]
    vec = lambda a: a.reshape(1, d)

    sup, extra = 4 * d // N_DEV, heads // N_DEV
    padded = jnp.pad(a_w_in[0].astype(BF16), ((0, 0), (heads, LANE)))
    big_loc = lax.dynamic_slice_in_dim(padded, heads - extra * me, sup, axis=1)
    small_loc = lax.dynamic_slice_in_dim(padded, heads + sup - extra * me, LANE, axis=1)
    g_big, g_small, c_all = _all_gather("gather_a_w_in", [big_loc, small_loc, c])
    lane_id = jnp.arange(LANE)[None, :]
    patch = jnp.stack([jnp.where(lane_id < extra * s, g_small[s - 1], g_big[s, :, :LANE])
                       for s in range(1, N_DEV)])
    w_a_main = g_big.at[1:, :, :LANE].set(patch)
    w_a_f = g_small[N_DEV - 1]
    b_in_cols = b_w_in.shape[2]
    a_tn = _tile(sup, 1024)
    w_a_main_spec = lambda tk: pl.BlockSpec(
        (None, tk, a_tn), lambda i, j, k: (j // (sup // a_tn), k, j % (sup // a_tn)))

    c_rows = jnp.pad(c_all.reshape(N_DEV, d), ((0, 2 * SUBLANE - N_DEV), (0, 0)))
    bias_loc = lax.dynamic_slice_in_dim(ada_b, me * ada_cols, ada_cols, axis=1)[:, None, :]
    mod_part, sc_rows = _ada_fwd(c_rows, ada_w, bias_loc)
    (mod_recv,) = _all_to_all("exchange_mod", [jnp.transpose(mod_part[:, :N_DEV], (1, 0, 2))])
    mod = jnp.transpose(mod_recv, (1, 0, 2)).reshape(2, 3 * d)
    later = [a_w_out[0].astype(BF16), kv_w.astype(BF16), b_w_in[0].astype(BF16),
             b_w_out[0].astype(BF16)]
    g_a_out, g_kv, w_b_in, g_b_out = _all_gather("gather_rest", _after(later, mod_recv),
                                                 sequencer_id=1)
    w_a_out = g_a_out.reshape(d, d)
    w_kv = g_kv.reshape(d, 2 * kv_width)
    w_b_out = g_b_out.reshape(d, d)
    shift0, scale0, gate0 = vec(mod[0, :d]), vec(mod[0, d:2 * d]), vec(mod[0, 2 * d:])
    shift1, scale1, gate1 = vec(mod[1, :d]), vec(mod[1, d:2 * d]), vec(mod[1, 2 * d:])
    g0, g1, g_kvn, g_fin = vec(norm_g[0]), vec(norm_g[1]), vec(kv_norm_g), vec(final_norm_g)

    cos, sin = _rope_tables(s_len)
    bias_f = _pad_lanes(a_b_f)
    sinks_row = _pad_lanes(b_sinks)

    h0 = _norm_fwd("norm0", x0, g0, scale0, shift0)
    proj0 = _mm_plain("proj0", h0, w_a_main, n_cols=4 * d, tn=a_tn, b_spec=w_a_main_spec)
    f_raw = _mm_plain("proj0_f", h0, w_a_f, out_dtype=F32, tn=LANE)
    f_nat = _fgate_fwd(f_raw, bias_f)
    f_t = _rows_of(f_nat, heads)
    o0_t, lse0_t = _fox_fwd(proj0, jnp.transpose(proj0[:, 2 * d:3 * d]), f_nat, f_t, heads)
    o0, u0 = _gate_fwd(o0_t, proj0, 3 * heads, heads)
    y0, x1 = _mm_residual("out0", u0, w_a_out, x0, gate0)

    h1, hk = _norm_fwd("norm1", x1, g1, scale1, shift1, gb=g_kvn)
    kv = _mm_rope("kv_proj", hk, w_kv, cos, sin, n_cols=2 * kv_width, rope_cols=kv_width,
                  tn=kv_width)
    proj1 = _mm_rope("proj1", h1, w_b_in, cos, sin, n_cols=2 * d, rope_cols=d, tn=b_in_cols,
                     b_spec=lambda tk: pl.BlockSpec((None, tk, b_in_cols),
                                                    lambda i, j, k: (j, k, 0)))
    o1, u1, lse1 = _swa_fwd(proj1, kv, sinks_row, heads, kv_heads)
    y1, x2 = _mm_residual("out1", u1, w_b_out, x1, gate1)

    loss_part, dx2, dy1, sums_f = _loss_bwd(x2, target, y1, g_fin, gate1)

    do1, dz1, delta1 = _mm_gate_bwd("out1_bwd", dy1, w_b_out, proj1, d, o1)
    gw_b_out = _mm_plain("out1_wgrad", u1, dy1, ta=True)
    do1 = _after(do1, gw_b_out)
    dq1, dsinks = _swa_bwd_q(proj1, kv, do1, lse1, delta1, sinks_row, cos, sin, heads, kv_heads)
    dk1, dv1 = _swa_bwd_kv(proj1, kv, do1, _rows_of(lse1, heads), _rows_of(delta1, heads),
                           cos, sin, heads, kv_heads)
    dproj1 = jnp.concatenate([dq1, dz1], axis=1)
    tk_b = _tile(2 * d, b_in_cols)
    dh1 = _mm_plain("proj1_bwd", dproj1, w_b_in, nt=True, n_cols=d, out_dtype=F32, tk=tk_b,
                    b_spec=lambda tk: pl.BlockSpec((None, _tile(d, 1024), tk),
                                                   lambda i, j, k: (k * tk // b_in_cols, j, 0)))
    gw_b_in = _mm_plain("proj1_wgrad", h1, dproj1, ta=True, tn=b_in_cols,
                        out_3d=(N_DEV, b_in_cols))
    dkv = jnp.concatenate([dk1, dv1], axis=1)
    dhk = _mm_plain("kv_bwd", dkv, w_kv, nt=True, out_dtype=F32)
    gw_kv = _mm_plain("kv_wgrad", hk, dkv, ta=True)
    dx1, dy0, sums1 = _norm_bwd("norm1_bwd", x1, dx2, dh1, g1, scale1,
                                dhb=_after(dhk, (gw_b_in, gw_kv)), gb=g_kvn, y=y0, gate=gate0)

    do0, dz0, delta0 = _mm_gate_bwd("out0_bwd", dy0, w_a_out, proj0, 3 * d, o0)
    gw_a_out = _mm_plain("out0_wgrad", u0, dy0, ta=True)
    r_b_out, r_b_in, r_kv, r_a_out = _all_to_all(
        "scatter_grads_early",
        [gw_b_out.reshape(N_DEV, d // N_DEV, d), gw_b_in,
         gw_kv.reshape(N_DEV, d // N_DEV, 2 * kv_width),
         gw_a_out.reshape(N_DEV, d // N_DEV, d)], sequencer_id=2)
    k0_t = jnp.transpose(proj0[:, d:2 * d])
    dk0, dv0, dq0_t, dfq_t, dfk_nat = _fox_bwd(proj0, k0_t, _after(do0, gw_a_out), f_nat, f_t,
                                               lse0_t, _rows_of(delta0, heads), heads)
    dfq_nat = _pad_lanes(jnp.transpose(dfq_t[:, 0, :]))
    df, sums_bf = _fgate_bwd(dfq_nat, dfk_nat, f_raw, bias_f)
    dproj0 = jnp.concatenate([jnp.transpose(dq0_t), dk0, dv0, dz0], axis=1)
    gw_a_big = _mm_plain("proj0_wgrad", h0, dproj0, ta=True, tn=a_tn, out_3d=(N_DEV, sup))
    gw_a_f = _mm_plain("proj0_f_wgrad", h0, df, ta=True, tn=LANE)
    gw_a_small = jnp.concatenate([gw_a_big[1:, :, :LANE], gw_a_f[None]], axis=0)
    sib_big, sib_small = _pair_exchange("reduce_a_w_in_pair", [gw_a_big, gw_a_small],
                                        sequencer_id=3)
    r_b_out, r_b_in, r_kv, r_a_out = _after([r_b_out, r_b_in, r_kv, r_a_out], gw_a_big)
    up_b_out = _shard_update("update_b_w_out", r_b_out, b_w_out[0], m_b_w_out[0], v_b_w_out[0])
    up_b_in = _shard_update("update_b_w_in", r_b_in, b_w_in[0], m_b_w_in[0], v_b_w_in[0])
    up_kv = _shard_update("update_kv_w", r_kv, kv_w, m_kv_w, v_kv_w)
    up_a_out = _shard_update("update_a_w_out", r_a_out, a_w_out[0], m_a_w_out[0], v_a_w_out[0])
    sib_big, sib_small = _after((sib_big, sib_small),
                                (up_b_out[0], up_b_in[0], up_kv[0], up_a_out[0]))
    chip_big, chip_small = _pair_add("reduce_a_w_in_add", [gw_a_big, gw_a_small],
                                     [sib_big, sib_small])
    r_a_big, r_a_small = _chip_exchange("reduce_a_w_in_chips", [chip_big, chip_small],
                                        sequencer_id=4)
    df = _after(df, (chip_big, chip_small))
    dh0_f = _mm_plain("proj0_f_bwd", df, w_a_f, nt=True, out_dtype=F32)
    dh0 = _mm_plain("proj0_bwd", dproj0, w_a_main, nt=True, n_cols=d, out_dtype=F32, init=dh0_f,
                    tk=sup, b_spec=lambda tk: pl.BlockSpec((None, _tile(d, 1024), tk),
                                                           lambda i, j, k: (k, j, 0)))
    grad_x, sums0 = _norm_bwd("norm0_bwd", x0, dx1, dh0, g0, scale0)

    dmod = jnp.stack([jnp.concatenate([sums0[0], sums0[1], sums1[4]]),
                      jnp.concatenate([sums1[0], sums1[1], sums_f[1]])])
    small_shapes = [(2, 3 * d), (2, d), (1, heads), (d,), (1, heads), (d,), (1,)]
    small_grads = [dmod, jnp.stack([sums0[2], sums1[2]]), sums_bf[0:1, :heads], sums1[3],
                   dsinks[0:1, :heads], sums_f[0], loss_part[0, 0:1]]
    (small_all,) = _all_gather("gather_small", [_pack(small_grads)])
    zero = jnp.zeros((1,), F32)
    small = _small_update(
        small_all,
        _pack([ada_b, norm_g, a_b_f, kv_norm_g, b_sinks, final_norm_g, zero]),
        _pack([m_ada_b, m_norm_g, m_a_b_f, m_kv_norm_g, m_b_sinks, m_final_norm_g, zero]),
        _pack([v_ada_b, v_norm_g, v_a_b_f, v_kv_norm_g, v_b_sinks, v_final_norm_g, zero]))
    s_grad, s_delta, s_m, s_v = [_unpack(p, small_shapes) for p in small]
    loss = s_grad[6][0]

    dmod_all = small_all.reshape(N_DEV, -1)[:, :2 * 3 * d].reshape(N_DEV, 2, 3 * d)
    dmod_loc = lax.dynamic_slice_in_dim(dmod_all, me * ada_cols, ada_cols, axis=2)
    dmod_loc = jnp.pad(jnp.transpose(dmod_loc, (1, 0, 2)), ((0, 0), (0, LANE - N_DEV), (0, 0)))
    sc_t = jnp.pad(jnp.transpose(sc_rows[0, :N_DEV]), ((0, 0), (0, LANE - N_DEV)))
    up_ada = _ada_update(sc_t.astype(BF16), dmod_loc.astype(BF16), ada_w, m_ada_w, v_ada_w)

    r_a_big, r_a_small = _after((r_a_big, r_a_small), (up_ada[0], small[0]))
    ga_big, ga_small = _slab_sum("sum_a_w_in", [r_a_big, r_a_small])
    ga_shard = lax.dynamic_slice_in_dim(jnp.concatenate([ga_big, ga_small], axis=1),
                                        extra * me, sup + extra, axis=1)
    up_a_in = _shard_update("update_a_w_in", ga_shard[None], a_w_in[0], m_a_w_in[0], v_a_w_in[0])

    lead = lambda a: a[None]
    per_kind = []
    for kind in range(4):
        sm = (s_grad, s_delta, s_m, s_v)[kind]
        per_kind.append([
            sm[1], up_ada[kind], sm[0], lead(up_a_in[kind]), sm[2], lead(up_a_out[kind]),
            sm[3], up_kv[kind], lead(up_b_in[kind]), sm[4], lead(up_b_out[kind]), sm[5]])
    return (loss, grad_x[None], *per_kind[0], *per_kind[1], *per_kind[2], *per_kind[3])
```

```python
import jax
import jax.numpy as jnp
from jax import lax
from jax.experimental import pallas as pl
from jax.experimental.pallas import tpu as pltpu
from jax.experimental.pallas import tpu_sc as plsc

F32 = jnp.float32
BF16 = jnp.bfloat16
LANE = 128
SUBLANE = 8
HEAD_DIM = 128
SWA_BLOCK = 128
N_DEV = 8
N_PEER = N_DEV - 1
RMS_EPS = 1e-6
ROPE_THETA = 10000.0
NEG = -1e30
VMEM_LIMIT = 56 * 2 ** 20
MM_RESERVE = 10 * 2 ** 20
MESH = pl.DeviceIdType.MESH
HIGHEST = lax.Precision.HIGHEST

ADAM_LR = 0.001
ADAM_B1 = 0.9
ADAM_B2 = 0.999
ADAM_EPS = 1e-08
ADAM_WD = 0.01
ADAM_STEP = 10


def _tile(dim, pref):
    return pref if dim % pref == 0 else dim


def _params(n_axes):
    return pltpu.CompilerParams(dimension_semantics=("arbitrary",) * n_axes,
                                vmem_limit_bytes=VMEM_LIMIT)


def _dot(a, b):
    return jnp.dot(a, b, preferred_element_type=F32)


def _dot_nt(a, b):
    return lax.dot_general(a, b, (((1,), (1,)), ((), ())), preferred_element_type=F32)


def _dot_tn(a, b):
    return lax.dot_general(a, b, (((0,), (0,)), ((), ())), preferred_element_type=F32)


def _sigmoid(z):
    return 1.0 / (1.0 + jnp.exp(-z))


def _iota(shape, dim):
    return lax.broadcasted_iota(jnp.int32, shape, dim)


def _pick_lane(block, lane_index):
    lane = _iota(block.shape, 1)
    return jnp.sum(jnp.where(lane == lane_index, block, 0.0), axis=1, keepdims=True)


def _adamw_decayed(w_decay, g, m_decayed, v_decayed):
    m = m_decayed + (1.0 - ADAM_B1) * g
    v = v_decayed + (1.0 - ADAM_B2) * (g * g)
    m_hat = m / (1.0 - ADAM_B1 ** ADAM_STEP)
    v_hat = v / (1.0 - ADAM_B2 ** ADAM_STEP)
    delta = -ADAM_LR * (m_hat / (jnp.sqrt(v_hat) + ADAM_EPS) + w_decay)
    return delta, m, v


def _adamw(w, g, m, v):
    return _adamw_decayed(ADAM_WD * w, g, ADAM_B1 * m, ADAM_B2 * v)


def _mesh_pos():
    return lax.axis_index("x"), lax.axis_index("y"), lax.axis_index("c")


def _slot(pos):
    return 4 * pos[0] + 2 * pos[1] + pos[2]


def _handshake(peers):
    barrier = pltpu.get_barrier_semaphore()
    for peer in peers:
        pl.semaphore_signal(barrier, inc=1, device_id=peer, device_id_type=MESH)
    pl.semaphore_wait(barrier, len(peers))


def _launch(name, body, arrays, out_shape, sequencer_id):
    n = len(arrays)
    scratch = [pltpu.SemaphoreType.DMA((N_PEER * n,)), pltpu.SemaphoreType.DMA((N_PEER * n,)),
               pltpu.SemaphoreType.DMA((n,))]
    if sequencer_id is None:
        any_spec = pl.BlockSpec(memory_space=pl.ANY)
        return pl.pallas_call(body, name=name, out_shape=out_shape, in_specs=[any_spec] * n,
                              out_specs=[any_spec] * n, scratch_shapes=scratch)(*arrays)
    return pl.kernel(body, name=name, out_type=out_shape,
                     mesh=plsc.ScalarSubcoreMesh(axis_name="sequencer", num_cores=1),
                     scratch_types=scratch,
                     compiler_params=pltpu.CompilerParams(collective_id=sequencer_id))(*arrays)


def _all_gather(name, arrays, sequencer_id=None):
    n = len(arrays)

    def body(*refs):
        ins, outs = refs[:n], refs[n:2 * n]
        send_sems, recv_sems, local_sems = refs[2 * n:]
        x, y, c = _mesh_pos()
        me, sibling = (x, y, c), (x, y, 1 - c)
        chips = [(1 - x, y), (x, 1 - y), (1 - x, 1 - y)]
        if sequencer_id is not None:
            _handshake([sibling] + [(*chip, c) for chip in chips])

        def copy(a, k, block, to, src=None):
            dst = outs[a].at[_slot(block)]
            return pltpu.make_async_remote_copy(
                src_ref=dst if src is None else src, dst_ref=dst,
                send_sem=send_sems.at[N_PEER * a + k], recv_sem=recv_sems.at[N_PEER * a + k],
                device_id=to, device_id_type=MESH)

        local, first, passed = [], [], []
        for a in range(n):
            cp = pltpu.make_async_copy(ins[a], outs[a].at[_slot(me)], local_sems.at[a])
            cp.start()
            local.append(cp)
            sends = [copy(a, 0, me, sibling, src=ins[a])]
            sends += [copy(a, 1 + j, me, (*chip, c), src=ins[a]) for j, chip in enumerate(chips)]
            for cp in sends:
                cp.start()
            first += sends
        for a in range(n):
            for j, chip in enumerate(chips):
                copy(a, 1 + j, (*chip, c), me).wait_recv()
                cp = copy(a, 4 + j, (*chip, c), sibling)
                cp.start()
                passed.append(cp)
        for a in range(n):
            copy(a, 0, sibling, me).wait_recv()
            for j, chip in enumerate(chips):
                copy(a, 4 + j, (*chip, 1 - c), me).wait_recv()
        for cp in first + passed:
            cp.wait_send()
        for cp in local:
            cp.wait()

    out_shape = [jax.ShapeDtypeStruct((N_DEV,) + a.shape, a.dtype) for a in arrays]
    return _launch(name, body, arrays, out_shape, sequencer_id)


def _all_to_all(name, arrays, sequencer_id=None):
    n = len(arrays)

    def body(*refs):
        ins, outs = refs[:n], refs[n:2 * n]
        send_sems, recv_sems, local_sems = refs[2 * n:]
        x, y, c = _mesh_pos()
        me = _slot((x, y, c))
        if sequencer_id is not None:
            _handshake([(1 - x if k & 4 else x, 1 - y if k & 2 else y, 1 - c if k & 1 else c)
                        for k in range(1, N_DEV)])
        local, sends, recvs = [], [], []
        for a in range(n):
            cp = pltpu.make_async_copy(ins[a].at[me], outs[a].at[me], local_sems.at[a])
            cp.start()
            local.append(cp)
        for k in range(1, N_DEV):
            peer = (1 - x if k & 4 else x, 1 - y if k & 2 else y, 1 - c if k & 1 else c)
            ps = _slot(peer)
            for a in range(n):
                sem = N_PEER * a + k - 1
                cp = pltpu.make_async_remote_copy(
                    src_ref=ins[a].at[ps], dst_ref=outs[a].at[me],
                    send_sem=send_sems.at[sem], recv_sem=recv_sems.at[sem],
                    device_id=peer, device_id_type=MESH)
                cp.start()
                sends.append(cp)
                recvs.append(pltpu.make_async_remote_copy(
                    src_ref=ins[a].at[ps], dst_ref=outs[a].at[ps],
                    send_sem=send_sems.at[sem], recv_sem=recv_sems.at[sem],
                    device_id=peer, device_id_type=MESH))
        for cp in recvs:
            cp.wait_recv()
        for cp in sends:
            cp.wait_send()
        for cp in local:
            cp.wait()

    out_shape = [jax.ShapeDtypeStruct(a.shape, a.dtype) for a in arrays]
    return _launch(name, body, arrays, out_shape, sequencer_id)


def _pair_exchange(name, arrays, sequencer_id=None):
    n = len(arrays)
    chips = N_DEV // 2

    def body(*refs):
        ins, outs = refs[:n], refs[n:2 * n]
        send_sems, recv_sems = refs[2 * n], refs[2 * n + 1]
        x, y, c = _mesh_pos()
        sibling = (x, y, 1 - c)
        if sequencer_id is not None:
            _handshake([sibling])
        copies = [pltpu.make_async_remote_copy(
            src_ref=ins[a].at[2 * q + 1 - c], dst_ref=outs[a].at[q],
            send_sem=send_sems.at[chips * a + q], recv_sem=recv_sems.at[chips * a + q],
            device_id=sibling, device_id_type=MESH) for a in range(n) for q in range(chips)]
        for cp in copies:
            cp.start()
        for cp in copies:
            cp.wait()

    out_shape = [jax.ShapeDtypeStruct((chips,) + a.shape[1:], a.dtype) for a in arrays]
    return _launch(name, body, arrays, out_shape, sequencer_id)


def _chip_exchange(name, arrays, sequencer_id=None):
    n = len(arrays)
    chips = N_DEV // 2

    def body(*refs):
        ins, outs = refs[:n], refs[n:2 * n]
        send_sems, recv_sems, local_sems = refs[2 * n:]
        x, y, c = _mesh_pos()
        mine = 2 * x + y
        others = [(1 - x, y), (x, 1 - y), (1 - x, 1 - y)]
        if sequencer_id is not None:
            _handshake([(*chip, c) for chip in others])
        local = [pltpu.make_async_copy(ins[a].at[mine], outs[a].at[mine], local_sems.at[a])
                 for a in range(n)]
        for cp in local:
            cp.start()
        sends, recvs = [], []
        for j, chip in enumerate(others):
            theirs = 2 * chip[0] + chip[1]
            for a in range(n):
                both = dict(send_sem=send_sems.at[3 * a + j], recv_sem=recv_sems.at[3 * a + j],
                            device_id=(*chip, c), device_id_type=MESH)
                sends.append(pltpu.make_async_remote_copy(
                    src_ref=ins[a].at[theirs], dst_ref=outs[a].at[mine], **both))
                recvs.append(pltpu.make_async_remote_copy(
                    src_ref=ins[a].at[theirs], dst_ref=outs[a].at[theirs], **both))
        for cp in sends:
            cp.start()
        for cp in recvs:
            cp.wait_recv()
        for cp in sends:
            cp.wait_send()
        for cp in local:
            cp.wait()

    out_shape = [jax.ShapeDtypeStruct(a.shape, a.dtype) for a in arrays]
    return _launch(name, body, arrays, out_shape, sequencer_id)


def _pair_add(name, mine, theirs):
    chips, rows = theirs[0].shape[0], theirs[0].shape[1]
    per_row = sum(2 * 3 * a.shape[2] * a.dtype.itemsize for a in theirs)
    tr = _row_tile(rows, per_row)
    n = len(theirs)
    core = lax.axis_index("c").astype(jnp.int32).reshape(1)

    def body(core_ref, *refs):
        for a in range(n):
            refs[2 * n + a][...] = (refs[a][...].astype(F32)
                                    + refs[n + a][...].astype(F32)).astype(refs[2 * n + a].dtype)

    blk = lambda a, fn: pl.BlockSpec((None, tr, a.shape[2]), fn)
    grid_spec = pltpu.PrefetchScalarGridSpec(
        num_scalar_prefetch=1, grid=(chips, rows // tr),
        in_specs=[blk(a, lambda q, i, core_ref: (2 * q + core_ref[0], i, 0)) for a in mine]
        + [blk(a, lambda q, i, core_ref: (q, i, 0)) for a in theirs],
        out_specs=[blk(a, lambda q, i, core_ref: (q, i, 0)) for a in theirs])
    return pl.pallas_call(
        body, name=name, grid_spec=grid_spec,
        out_shape=[jax.ShapeDtypeStruct(a.shape, a.dtype) for a in theirs],
        compiler_params=_params(2))(core, *mine, *theirs)


def _after(value, token):
    return lax.optimization_barrier((value, token))[0]


def _k_tile(k_dim, tm, tn, fixed_bytes):
    budget = VMEM_LIMIT - MM_RESERVE - fixed_bytes
    tk = k_dim
    while tk % 2 == 0 and tk > 512 and (
            4 * (tm + tn) * tk + (4 * tm * tn if tk < k_dim else 0) > budget):
        tk //= 2
    return tk


def _matmul(name, a, b, *, nt, tm, tn, n_cols, out_shape, out_specs, epilogue,
            fixed_bytes, ta=False, tk=None, b_spec=None, extra=(), extra_specs=()):
    assert not (ta and nt)
    k_dim, m_rows = a.shape if ta else a.shape[::-1]
    tm, tn = _tile(m_rows, tm), _tile(n_cols, tn)
    tk = _k_tile(k_dim, tm, tn, fixed_bytes) if tk is None else _tile(k_dim, tk)
    grid = (m_rows // tm, n_cols // tn, k_dim // tk)
    n_k = grid[2]
    if ta:
        a_spec = pl.BlockSpec((tk, tm), lambda i, j, k: (k, i))
    else:
        a_spec = pl.BlockSpec((tm, tk), lambda i, j, k: (i, k))
    if b_spec is not None:
        b_blk = b_spec(tk)
    elif nt:
        b_blk = pl.BlockSpec((tn, tk), lambda i, j, k: (j, k))
    else:
        b_blk = pl.BlockSpec((tk, tn), lambda i, j, k: (k, j))
    n_extra, n_out = len(extra), len(out_shape)
    product = _dot_tn if ta else _dot_nt if nt else _dot

    def body(a_ref, b_ref, *rest):
        extra_refs = rest[:n_extra]
        out_refs = rest[n_extra:n_extra + n_out]
        if n_k == 1:
            epilogue(product(a_ref[...], b_ref[...]), extra_refs, out_refs)
            return
        acc_ref = rest[n_extra + n_out]
        k = pl.program_id(2)

        @pl.when(k == 0)
        def _():
            acc_ref[...] = jnp.zeros_like(acc_ref)

        acc_ref[...] += product(a_ref[...], b_ref[...])

        @pl.when(k == n_k - 1)
        def _():
            epilogue(acc_ref[...], extra_refs, out_refs)

    return pl.pallas_call(
        body, name=name, grid=grid,
        in_specs=[a_spec, b_blk, *extra_specs], out_specs=out_specs, out_shape=out_shape,
        scratch_shapes=[pltpu.VMEM((tm, tn), F32)] if n_k > 1 else [],
        compiler_params=_params(3),
    )(a, b, *extra)


def _mm_plain(name, a, b, *, nt=False, ta=False, n_cols=None, out_dtype=BF16, init=None,
              tm=1024, tn=1024, tk=None, b_spec=None, out_3d=None):
    m_rows = a.shape[1] if ta else a.shape[0]
    if n_cols is None:
        n_cols = b.shape[0] if nt else b.shape[1]
    tm, tn = _tile(m_rows, tm), _tile(n_cols, tn)
    fixed = 2 * tm * tn * (jnp.dtype(out_dtype).itemsize + (4 if init is not None else 0))
    if out_3d is None:
        shape = jax.ShapeDtypeStruct((m_rows, n_cols), out_dtype)
        spec = pl.BlockSpec((tm, tn), lambda i, j, k: (i, j))
    else:
        slabs, width = out_3d
        assert width % tn == 0 and slabs * width == n_cols
        per = width // tn
        shape = jax.ShapeDtypeStruct((slabs, m_rows, width), out_dtype)
        spec = pl.BlockSpec((None, tm, tn), lambda i, j, k: (j // per, i, j % per))
    extra, extra_specs = (), ()
    if init is not None:
        extra = (init,)
        extra_specs = (pl.BlockSpec((tm, tn), lambda i, j, k: (i, j)),)

    def epilogue(acc, extra_refs, out_refs):
        if init is not None:
            acc = acc + extra_refs[0][...]
        out_refs[0][...] = acc.astype(out_dtype)

    (out,) = _matmul(name, a, b, nt=nt, ta=ta, tm=tm, tn=tn, tk=tk, n_cols=n_cols,
                     out_shape=[shape], out_specs=[spec], epilogue=epilogue, fixed_bytes=fixed,
                     b_spec=b_spec, extra=extra, extra_specs=extra_specs)
    return out


def _mm_rope(name, a, b, cos, sin, *, n_cols, rope_cols, tn, b_spec=None):
    m_rows = a.shape[0]
    tm = _tile(m_rows, 1024)
    tn = _tile(n_cols, tn)
    assert rope_cols % tn == 0 and tn % HEAD_DIM == 0
    rope_blocks = rope_cols // tn
    table_spec = pl.BlockSpec((tm, LANE), lambda i, j, k: (i, 0))

    def epilogue(acc, extra_refs, out_refs):
        cos_ref, sin_ref = extra_refs
        j = pl.program_id(1)

        @pl.when(j < rope_blocks)
        def _():
            for head in range(tn // HEAD_DIM):
                cols = slice(head * HEAD_DIM, (head + 1) * HEAD_DIM)
                blk = acc[:, cols]
                rot = pltpu.roll(blk, HEAD_DIM // 2, 1)
                out_refs[0][:, cols] = (blk * cos_ref[...] + rot * sin_ref[...]).astype(BF16)

        @pl.when(j >= rope_blocks)
        def _():
            out_refs[0][...] = acc.astype(BF16)

    (out,) = _matmul(name, a, b, nt=False, tm=tm, tn=tn, n_cols=n_cols,
                     out_shape=[jax.ShapeDtypeStruct((m_rows, n_cols), BF16)],
                     out_specs=[pl.BlockSpec((tm, tn), lambda i, j, k: (i, j))],
                     epilogue=epilogue, fixed_bytes=4 * tm * tn + 16 * tm * LANE, b_spec=b_spec,
                     extra=(cos, sin), extra_specs=(table_spec, table_spec))
    return out


def _mm_residual(name, u, w, x_in, gate):
    m_rows, n_cols = x_in.shape
    tm, tn = _tile(m_rows, 512), _tile(n_cols, 1024)
    blk = pl.BlockSpec((tm, tn), lambda i, j, k: (i, j))

    def epilogue(acc, extra_refs, out_refs):
        x_ref, gate_ref = extra_refs
        out_refs[0][...] = acc
        out_refs[1][...] = x_ref[...] + gate_ref[...] * acc

    y, x_out = _matmul(
        name, u, w, nt=False, tm=tm, tn=tn, n_cols=n_cols,
        out_shape=[jax.ShapeDtypeStruct((m_rows, n_cols), F32)] * 2, out_specs=[blk, blk],
        epilogue=epilogue, fixed_bytes=3 * 8 * tm * tn, extra=(x_in, gate),
        extra_specs=(blk, pl.BlockSpec((1, tn), lambda i, j, k: (0, j))))
    return y, x_out


def _mm_gate_bwd(name, dy, w_out, z_src, z_col0, o):
    m_rows = dy.shape[0]
    n_cols = w_out.shape[0]
    tm, tn = _tile(m_rows, 1024), _tile(n_cols, 1024)
    assert z_col0 % tn == 0 and tn % HEAD_DIM == 0 and n_cols // HEAD_DIM <= LANE
    z_blk0 = z_col0 // tn
    blk = pl.BlockSpec((tm, tn), lambda i, j, k: (i, j))

    def epilogue(du, extra_refs, out_refs):
        z_ref, o_ref = extra_refs
        do_ref, dz_ref, delta_ref = out_refs
        j = pl.program_id(1)
        z = z_ref[...].astype(F32)
        o_val = o_ref[...].astype(F32)
        sig = _sigmoid(z)
        d_o = (du * (z * sig)).astype(BF16)
        do_ref[...] = d_o
        dz_ref[...] = (du * o_val * (sig * (1.0 + z * (1.0 - sig)))).astype(BF16)

        @pl.when(j == 0)
        def _():
            delta_ref[...] = jnp.zeros_like(delta_ref)

        prod = d_o.astype(F32) * o_val
        lane = _iota((tm, LANE), 1)
        delta = delta_ref[...]
        for head in range(tn // HEAD_DIM):
            rows = jnp.sum(prod[:, head * HEAD_DIM:(head + 1) * HEAD_DIM], axis=1, keepdims=True)
            delta = delta + jnp.where(lane == j * (tn // HEAD_DIM) + head, rows, 0.0)
        delta_ref[...] = delta

    d_o, dz, delta = _matmul(
        name, dy, w_out, nt=True, tm=tm, tn=tn, n_cols=n_cols,
        out_shape=[jax.ShapeDtypeStruct((m_rows, n_cols), BF16)] * 2
        + [jax.ShapeDtypeStruct((m_rows, LANE), F32)],
        out_specs=[blk, blk, pl.BlockSpec((tm, LANE), lambda i, j, k: (i, 0))],
        epilogue=epilogue, fixed_bytes=4 * 4 * tm * tn + 8 * tm * LANE, extra=(z_src, o),
        extra_specs=(pl.BlockSpec((tm, tn), lambda i, j, k: (i, z_blk0 + j)), blk))
    return d_o, dz, delta


def _norm_fwd(name, x, ga, sa, ta, gb=None):
    s_len, d = x.shape
    tr = _tile(s_len, 256)
    two = gb is not None
    row = pl.BlockSpec((tr, d), lambda i: (i, 0))
    vec = pl.BlockSpec((1, d), lambda i: (0, 0))

    def body(x_ref, ga_ref, sa_ref, ta_ref, *rest):
        xv = x_ref[...]
        y = xv * lax.rsqrt(jnp.mean(xv * xv, axis=-1, keepdims=True) + RMS_EPS)
        rest[-2 if two else -1][...] = ((y * ga_ref[...]) * (1.0 + sa_ref[...]) + ta_ref[...]).astype(BF16)
        if two:
            rest[-1][...] = (y * rest[0][...]).astype(BF16)

    ins = [x, ga, sa, ta] + ([gb] if two else [])
    outs = pl.pallas_call(
        body, name=name, grid=(s_len // tr,),
        in_specs=[row] + [vec] * (len(ins) - 1),
        out_specs=[row] * (2 if two else 1),
        out_shape=[jax.ShapeDtypeStruct((s_len, d), BF16)] * (2 if two else 1),
        compiler_params=_params(1))(*ins)
    return outs if two else outs[0]


def _loss_bwd(x2, target, y1, g_final, gate1):
    s_len, d = x2.shape
    tr = _tile(s_len, 128)
    row = pl.BlockSpec((tr, d), lambda i: (i, 0))
    vec = pl.BlockSpec((1, d), lambda i: (0, 0))

    def body(x_ref, t_ref, y_ref, g_ref, gate_ref, loss_ref, dx_ref, dy_ref, sums_ref):
        @pl.when(pl.program_id(0) == 0)
        def _():
            loss_ref[...] = jnp.zeros_like(loss_ref)
            sums_ref[...] = jnp.zeros_like(sums_ref)

        xv = x_ref[...]
        rstd = lax.rsqrt(jnp.mean(xv * xv, axis=-1, keepdims=True) + RMS_EPS)
        xhat = xv * rstd
        g = g_ref[...]
        err = xhat * g - t_ref[...]
        sq = jnp.sum(jnp.sum(err * err, axis=1, keepdims=True), axis=0, keepdims=True)
        loss_ref[...] += sq * (0.5 / d)
        dout = err * (1.0 / d)
        dxhat = dout * g
        dx = rstd * (dxhat - xhat * jnp.mean(dxhat * xhat, axis=-1, keepdims=True))
        dx_ref[...] = dx
        dy_ref[...] = (dx * gate_ref[...]).astype(BF16)
        sums_ref[0:1, :] += jnp.sum(dout * xhat, axis=0, keepdims=True)
        sums_ref[1:2, :] += jnp.sum(dx * y_ref[...], axis=0, keepdims=True)

    return pl.pallas_call(
        body, name="loss_bwd", grid=(s_len // tr,),
        in_specs=[row, row, row, vec, vec],
        out_specs=[pl.BlockSpec((SUBLANE, LANE), lambda i: (0, 0)), row, row,
                   pl.BlockSpec((SUBLANE, d), lambda i: (0, 0))],
        out_shape=[jax.ShapeDtypeStruct((SUBLANE, LANE), F32),
                   jax.ShapeDtypeStruct((s_len, d), F32),
                   jax.ShapeDtypeStruct((s_len, d), BF16),
                   jax.ShapeDtypeStruct((SUBLANE, d), F32)],
        compiler_params=_params(1))(x2, target, y1, g_final, gate1)


def _norm_bwd(name, x, dres, dha, ga, sa, dhb=None, gb=None, y=None, gate=None):
    s_len, d = x.shape
    tr = _tile(s_len, 128)
    has_b, has_y = dhb is not None, y is not None
    row = pl.BlockSpec((tr, d), lambda i: (i, 0))
    vec = pl.BlockSpec((1, d), lambda i: (0, 0))
    ins, specs = [x, dres, dha, ga, sa], [row, row, row, vec, vec]
    if has_b:
        ins += [dhb, gb]
        specs += [row, vec]
    if has_y:
        ins += [y, gate]
        specs += [row, vec]
    n_in = len(ins)

    def body(*refs):
        x_ref, dres_ref, dha_ref, ga_ref, sa_ref = refs[:5]
        pos = 5
        if has_b:
            dhb_ref, gb_ref = refs[pos:pos + 2]
            pos += 2
        if has_y:
            y_ref, gate_ref = refs[pos:pos + 2]
        outs = refs[n_in:]
        dx_ref, sums_ref = outs[0], outs[-1]

        @pl.when(pl.program_id(0) == 0)
        def _():
            sums_ref[...] = jnp.zeros_like(sums_ref)

        xv = x_ref[...]
        rstd = lax.rsqrt(jnp.mean(xv * xv, axis=-1, keepdims=True) + RMS_EPS)
        xhat = xv * rstd
        dha_v = dha_ref[...]
        ga_v, sa_v = ga_ref[...], sa_ref[...]
        dxhat = dha_v * (ga_v * (1.0 + sa_v))
        sums_ref[0:1, :] += jnp.sum(dha_v, axis=0, keepdims=True)
        sums_ref[1:2, :] += jnp.sum(dha_v * (xhat * ga_v), axis=0, keepdims=True)
        sums_ref[2:3, :] += jnp.sum(dha_v * ((1.0 + sa_v) * xhat), axis=0, keepdims=True)
        if has_b:
            dhb_v = dhb_ref[...]
            dxhat = dxhat + dhb_v * gb_ref[...]
            sums_ref[3:4, :] += jnp.sum(dhb_v * xhat, axis=0, keepdims=True)
        dx = dres_ref[...] + rstd * (dxhat - xhat * jnp.mean(dxhat * xhat, axis=-1, keepdims=True))
        dx_ref[...] = dx
        if has_y:
            outs[1][...] = (dx * gate_ref[...]).astype(BF16)
            sums_ref[4:5, :] += jnp.sum(dx * y_ref[...], axis=0, keepdims=True)

    out_shape = [jax.ShapeDtypeStruct((s_len, d), F32)]
    out_specs = [row]
    if has_y:
        out_shape.append(jax.ShapeDtypeStruct((s_len, d), BF16))
        out_specs.append(row)
    out_shape.append(jax.ShapeDtypeStruct((SUBLANE, d), F32))
    out_specs.append(pl.BlockSpec((SUBLANE, d), lambda i: (0, 0)))
    return pl.pallas_call(body, name=name, grid=(s_len // tr,), in_specs=specs,
                          out_specs=out_specs, out_shape=out_shape,
                          compiler_params=_params(1))(*ins)


def _fgate_fwd(f_raw, bias_row):
    s_len = f_raw.shape[0]
    tb = _tile(s_len, 512)
    blk = pl.BlockSpec((tb, LANE), lambda t: (t, 0))

    def body(f_ref, b_ref, out_ref, carry):
        @pl.when(pl.program_id(0) == 0)
        def _():
            carry[...] = jnp.zeros_like(carry)

        u = f_ref[...] + b_ref[...]
        logf = jnp.minimum(u, 0.0) - jnp.log1p(jnp.exp(-jnp.abs(u)))
        tri = (_iota((tb, tb), 1) <= _iota((tb, tb), 0)).astype(F32)
        run = jnp.dot(tri, logf, precision=HIGHEST, preferred_element_type=F32) + carry[...]
        out_ref[...] = run
        carry[...] = run[tb - 1:tb, :]

    return pl.pallas_call(
        body, name="fgate_fwd", grid=(s_len // tb,),
        in_specs=[blk, pl.BlockSpec((1, LANE), lambda t: (0, 0))], out_specs=blk,
        out_shape=jax.ShapeDtypeStruct((s_len, LANE), F32),
        scratch_shapes=[pltpu.VMEM((1, LANE), F32)],
        compiler_params=_params(1))(f_raw, bias_row)


def _fgate_bwd(df_a, df_b, f_raw, bias_row):
    s_len = f_raw.shape[0]
    tb = _tile(s_len, 512)
    nb = s_len // tb
    blk = pl.BlockSpec((tb, LANE), lambda t: (nb - 1 - t, 0))

    def body(a_ref, b2_ref, f_ref, b_ref, df_ref, sums_ref, carry):
        @pl.when(pl.program_id(0) == 0)
        def _():
            carry[...] = jnp.zeros_like(carry)
            sums_ref[...] = jnp.zeros_like(sums_ref)

        d_run = a_ref[...] + b2_ref[...]
        tri = (_iota((tb, tb), 1) >= _iota((tb, tb), 0)).astype(F32)
        dlogf = jnp.dot(tri, d_run, precision=HIGHEST, preferred_element_type=F32) + carry[...]
        carry[...] = dlogf[0:1, :]
        u = f_ref[...] + b_ref[...]
        df = dlogf * _sigmoid(-u)
        df_ref[...] = df.astype(BF16)
        sums_ref[...] += jnp.sum(df, axis=0, keepdims=True)

    return pl.pallas_call(
        body, name="fgate_bwd", grid=(nb,),
        in_specs=[blk, blk, blk, pl.BlockSpec((1, LANE), lambda t: (0, 0))],
        out_specs=[blk, pl.BlockSpec((SUBLANE, LANE), lambda t: (0, 0))],
        out_shape=[jax.ShapeDtypeStruct((s_len, LANE), BF16),
                   jax.ShapeDtypeStruct((SUBLANE, LANE), F32)],
        scratch_shapes=[pltpu.VMEM((1, LANE), F32)],
        compiler_params=_params(1))(df_a, df_b, f_raw, bias_row)


def _fox_fwd(proj, f_nat, f_t, heads):
    s_len = proj.shape[0]
    d = heads * HEAD_DIM
    t = _tile(s_len, 512)
    nq = s_len // t
    scale = HEAD_DIM ** -0.5

    def body(k_ref, q_ref, v_ref, fn_ref, ft_ref, ot_ref, lse_ref,
             acc_scr, m_scr, l_scr, fk_scr):
        j, i = pl.program_id(1), pl.program_id(2)
        h = pl.program_id(0)

        @pl.when((j == 0) & (i == 0))
        def _():
            m_scr[...] = jnp.full_like(m_scr, NEG)
            l_scr[...] = jnp.zeros_like(l_scr)
            acc_scr[...] = jnp.zeros_like(acc_scr)

        def step(diagonal):
            s_t = _dot_nt(k_ref[...], q_ref[...]) * scale + (ft_ref[...] - fk_scr[...])
            if diagonal:
                s_t = jnp.where(_iota((t, t), 0) <= _iota((t, t), 1), s_t, NEG)
            m_prev = m_scr[i]
            m_new = jnp.maximum(m_prev, jnp.max(s_t, axis=0, keepdims=True))
            alpha = jnp.exp(m_prev - m_new)
            p_t = jnp.exp(s_t - m_new)
            l_scr[i] = alpha * l_scr[i] + jnp.sum(p_t, axis=0, keepdims=True)
            acc_scr[i] = alpha * acc_scr[i] + _dot_tn(v_ref[...], p_t.astype(BF16))
            m_scr[i] = m_new

        @pl.when(i == j)
        def _():
            fk_scr[...] = _pick_lane(fn_ref[...], h)
            step(True)

        @pl.when(i > j)
        def _():
            step(False)

        @pl.when((i == nq - 1) & (j == nq - 1))
        def _():
            for blk in range(nq):
                cols = slice(blk * t, (blk + 1) * t)
                l_sum = l_scr[blk]
                ot_ref[:, cols] = acc_scr[blk] / l_sum
                lse_ref[:, cols] = m_scr[blk] + jnp.log(l_sum)

    qry = pl.BlockSpec((t, HEAD_DIM), lambda h, j, i: (jnp.maximum(i, j), h))
    return pl.pallas_call(
        body, name="fox_fwd", grid=(heads, nq, nq),
        in_specs=[pl.BlockSpec((t, HEAD_DIM), lambda h, j, i: (j, heads + h)), qry,
                  pl.BlockSpec((t, HEAD_DIM), lambda h, j, i: (j, 2 * heads + h)),
                  pl.BlockSpec((t, LANE), lambda h, j, i: (j, 0)),
                  pl.BlockSpec((None, 1, t), lambda h, j, i: (h, 0, jnp.maximum(i, j)))],
        out_specs=[pl.BlockSpec((HEAD_DIM, s_len), lambda h, j, i: (h, 0)),
                   pl.BlockSpec((None, 1, s_len), lambda h, j, i: (h, 0, 0))],
        out_shape=[jax.ShapeDtypeStruct((d, s_len), F32),
                   jax.ShapeDtypeStruct((heads, 1, s_len), F32)],
        scratch_shapes=[pltpu.VMEM((nq, HEAD_DIM, t), F32), pltpu.VMEM((nq, 1, t), F32),
                        pltpu.VMEM((nq, 1, t), F32), pltpu.VMEM((t, 1), F32)],
        compiler_params=_params(3))(proj, proj, proj, f_nat, f_t)


def _gate_fwd(o_t, proj, z_blk0, heads):
    d, s_len = o_t.shape
    t = _tile(s_len, 512)

    def body(ot_ref, z_ref, o_ref, u_ref):
        o_val = jnp.transpose(ot_ref[...])
        o_ref[...] = o_val.astype(BF16)
        z = z_ref[...].astype(F32)
        u_ref[...] = (o_val * (z * _sigmoid(z))).astype(BF16)

    out_blk = pl.BlockSpec((t, HEAD_DIM), lambda i, h: (i, h))
    return pl.pallas_call(
        body, name="gate_fwd", grid=(s_len // t, heads),
        in_specs=[pl.BlockSpec((HEAD_DIM, t), lambda i, h: (h, i)),
                  pl.BlockSpec((t, HEAD_DIM), lambda i, h: (i, z_blk0 + h))],
        out_specs=[out_blk, out_blk],
        out_shape=[jax.ShapeDtypeStruct((s_len, d), BF16)] * 2,
        compiler_params=_params(2))(o_t, proj)


def _fox_bwd(proj, d_o, f_nat, f_t, lse_t, delta_t, heads):
    s_len = proj.shape[0]
    d = heads * HEAD_DIM
    t = _tile(s_len, 512)
    nq = s_len // t
    scale = HEAD_DIM ** -0.5

    def body(k_ref, v_ref, q_ref, do_ref, fn_ref, ft_ref, lse_ref, delta_ref,
             dk_ref, dv_ref, dqt_ref, dfq_ref, dfk_ref,
             dk_acc, dv_acc, dq_acc, dfq_acc, dfk_acc, fk_scr):
        h, j, i = pl.program_id(0), pl.program_id(1), pl.program_id(2)
        head_start = (j == 0) & (i == 0)

        @pl.when(head_start)
        def _():
            dq_acc[...] = jnp.zeros_like(dq_acc)
            dfq_acc[...] = jnp.zeros_like(dfq_acc)

        @pl.when(head_start & (h == 0))
        def _():
            dfk_ref[...] = jnp.zeros_like(dfk_ref)

        def step(diagonal):
            q = q_ref[...]
            d_out = do_ref[...]
            s_t = _dot_nt(k_ref[...], q) * scale + (ft_ref[...] - fk_scr[...]) - lse_ref[...]
            if diagonal:
                s_t = jnp.where(_iota((t, t), 0) <= _iota((t, t), 1), s_t, NEG)
            p_t = jnp.exp(s_t)
            dp_t = _dot_nt(v_ref[...], d_out)
            ds_t = p_t * (dp_t - delta_ref[...])
            ds_b = ds_t.astype(BF16)
            dv_acc[...] += _dot(p_t.astype(BF16), d_out)
            dk_acc[...] += _dot(ds_b, q)
            dq_acc[i] += _dot_tn(k_ref[...], ds_b)
            dfq_acc[i] += jnp.sum(ds_t, axis=0, keepdims=True)
            dfk_acc[...] += jnp.sum(ds_t, axis=1, keepdims=True)

        @pl.when(i == j)
        def _():
            dk_acc[...] = jnp.zeros_like(dk_acc)
            dv_acc[...] = jnp.zeros_like(dv_acc)
            dfk_acc[...] = jnp.zeros_like(dfk_acc)
            fk_scr[...] = _pick_lane(fn_ref[...], h)
            step(True)

        @pl.when(i > j)
        def _():
            step(False)

        @pl.when(i == nq - 1)
        def _():
            dk_ref[...] = (dk_acc[...] * scale).astype(BF16)
            dv_ref[...] = dv_acc[...].astype(BF16)
            rows = pl.ds(pl.multiple_of(j * t, t), t)
            dfk_ref[rows, :] += jnp.where(_iota((t, LANE), 1) == h, -dfk_acc[...], 0.0)

        @pl.when((i == nq - 1) & (j == nq - 1))
        def _():
            for blk in range(nq):
                cols = slice(blk * t, (blk + 1) * t)
                dqt_ref[:, cols] = (dq_acc[blk] * scale).astype(BF16)
                dfq_ref[:, cols] = dfq_acc[blk]

    key_col = lambda base: pl.BlockSpec((t, HEAD_DIM), lambda h, j, i: (j, base + h))
    qry = pl.BlockSpec((t, HEAD_DIM), lambda h, j, i: (jnp.maximum(i, j), h))
    qry_row = pl.BlockSpec((None, 1, t), lambda h, j, i: (h, 0, jnp.maximum(i, j)))
    kv_out = pl.BlockSpec((t, HEAD_DIM), lambda h, j, i: (j, h))
    return pl.pallas_call(
        body, name="fox_bwd", grid=(heads, nq, nq),
        in_specs=[key_col(heads), key_col(2 * heads),
                  qry, qry, pl.BlockSpec((t, LANE), lambda h, j, i: (j, 0)),
                  qry_row, qry_row, qry_row],
        out_specs=[kv_out, kv_out,
                   pl.BlockSpec((HEAD_DIM, s_len), lambda h, j, i: (h, 0)),
                   pl.BlockSpec((None, 1, s_len), lambda h, j, i: (h, 0, 0)),
                   pl.BlockSpec((s_len, LANE), lambda h, j, i: (0, 0))],
        out_shape=[jax.ShapeDtypeStruct((s_len, d), BF16), jax.ShapeDtypeStruct((s_len, d), BF16),
                   jax.ShapeDtypeStruct((d, s_len), BF16),
                   jax.ShapeDtypeStruct((heads, 1, s_len), F32),
                   jax.ShapeDtypeStruct((s_len, LANE), F32)],
        scratch_shapes=[pltpu.VMEM((t, HEAD_DIM), F32), pltpu.VMEM((t, HEAD_DIM), F32),
                        pltpu.VMEM((nq, HEAD_DIM, t), F32), pltpu.VMEM((nq, 1, t), F32),
                        pltpu.VMEM((t, 1), F32), pltpu.VMEM((t, 1), F32)],
        compiler_params=_params(3))(proj, proj, proj, d_o, f_nat, f_t, lse_t, delta_t)


def _swa_specs(heads, kv_heads):
    width = heads // kv_heads * HEAD_DIM
    wide = lambda base: pl.BlockSpec((SWA_BLOCK, width), lambda n, g: (n, base + g))
    blk = lambda fn: pl.BlockSpec((SWA_BLOCK, HEAD_DIM), fn)
    prev = lambda base: blk(lambda n, g: (jnp.maximum(n - 1, 0), base + g))
    cur = lambda base: blk(lambda n, g: (n, base + g))
    return wide, prev, cur


def _stack_heads(ref, group):
    return jnp.concatenate([ref[:, hh * HEAD_DIM:(hh + 1) * HEAD_DIM] for hh in range(group)], axis=0)


def _head_rows(stacked, hh):
    return stacked[hh * SWA_BLOCK:(hh + 1) * SWA_BLOCK]


def _swa_scores(q, kp, kc, n, scale):
    shape = (q.shape[0], SWA_BLOCK)
    r, c = _iota(shape, 0) & (SWA_BLOCK - 1), _iota(shape, 1)
    sp = jnp.where((c > r) & (n > 0), _dot_nt(q, kp) * scale, NEG)
    sc = jnp.where(c <= r, _dot_nt(q, kc) * scale, NEG)
    return sp, sc


def _per_head_column(values_row, first_head, group):
    head_of_row = _iota((group * SWA_BLOCK, 1), 0) // SWA_BLOCK
    col = jnp.zeros((group * SWA_BLOCK, 1), F32)
    for hh in range(group):
        col = jnp.where(head_of_row == hh, _pick_lane(values_row, first_head + hh), col)
    return col


def _swa_fwd(proj, kv, sinks_row, heads, kv_heads):
    s_len = proj.shape[0]
    d = heads * HEAD_DIM
    scale = HEAD_DIM ** -0.5
    group = heads // kv_heads
    wide, prev, cur = _swa_specs(heads, kv_heads)

    def body(q_ref, z_ref, kp_ref, kc_ref, vp_ref, vc_ref, sink_ref, o_ref, u_ref, lse_ref):
        n, g = pl.program_id(0), pl.program_id(1)
        sp, sc = _swa_scores(_stack_heads(q_ref, group), kp_ref[...], kc_ref[...], n, scale)
        sink = _per_head_column(sink_ref[...], g * group, group)
        m = jnp.maximum(jnp.maximum(jnp.max(sp, axis=1, keepdims=True),
                                    jnp.max(sc, axis=1, keepdims=True)), sink)
        pp, pc = jnp.exp(sp - m), jnp.exp(sc - m)
        den = (jnp.sum(pp, axis=1, keepdims=True) + jnp.sum(pc, axis=1, keepdims=True)
               + jnp.exp(sink - m))
        o_all = (_dot(pp.astype(BF16), vp_ref[...]) + _dot(pc.astype(BF16), vc_ref[...])) / den
        lse = m + jnp.log(den)
        lane = _iota((SWA_BLOCK, LANE), 1)
        lse_all = jnp.zeros((SWA_BLOCK, LANE), F32)
        for hh in range(group):
            cols = slice(hh * HEAD_DIM, (hh + 1) * HEAD_DIM)
            o_val = _head_rows(o_all, hh)
            o_ref[:, cols] = o_val.astype(BF16)
            z = z_ref[:, cols].astype(F32)
            u_ref[:, cols] = (o_val * (z * _sigmoid(z))).astype(BF16)
            lse_all = lse_all + jnp.where(lane == g * group + hh, _head_rows(lse, hh), 0.0)

        @pl.when(g == 0)
        def _():
            lse_ref[...] = lse_all

        @pl.when(g > 0)
        def _():
            lse_ref[...] += lse_all

    nat = pl.BlockSpec((SWA_BLOCK, LANE), lambda n, g: (n, 0))
    return pl.pallas_call(
        body, name="swa_fwd", grid=(s_len // SWA_BLOCK, kv_heads),
        in_specs=[wide(0), wide(kv_heads), prev(0), cur(0), prev(kv_heads), cur(kv_heads),
                  pl.BlockSpec((1, LANE), lambda n, g: (0, 0))],
        out_specs=[wide(0), wide(0), nat],
        out_shape=[jax.ShapeDtypeStruct((s_len, d), BF16), jax.ShapeDtypeStruct((s_len, d), BF16),
                   jax.ShapeDtypeStruct((s_len, LANE), F32)],
        compiler_params=_params(2))(proj, proj, kv, kv, kv, kv, sinks_row)


def _swa_bwd_q(proj, kv, d_o, lse, delta, sinks_row, cos, sin, heads, kv_heads):
    s_len = proj.shape[0]
    d = heads * HEAD_DIM
    scale = HEAD_DIM ** -0.5
    group = heads // kv_heads
    wide, prev, cur = _swa_specs(heads, kv_heads)

    def body(q_ref, kp_ref, kc_ref, vp_ref, vc_ref, do_ref, lse_ref, delta_ref, sink_ref,
             cos_ref, sin_ref, dq_ref, dsink_ref):
        n, g = pl.program_id(0), pl.program_id(1)

        @pl.when((n == 0) & (g == 0))
        def _():
            dsink_ref[...] = jnp.zeros_like(dsink_ref)

        kp, kc = kp_ref[...], kc_ref[...]
        first = g * group
        sp, sc = _swa_scores(_stack_heads(q_ref, group), kp, kc, n, scale)
        lse_blk, delta_blk = lse_ref[...], delta_ref[...]
        lse_col = jnp.concatenate([_pick_lane(lse_blk, first + hh) for hh in range(group)], axis=0)
        delta_col = jnp.concatenate([_pick_lane(delta_blk, first + hh) for hh in range(group)],
                                    axis=0)
        pp, pc = jnp.exp(sp - lse_col), jnp.exp(sc - lse_col)
        p_sink = jnp.exp(_per_head_column(sink_ref[...], first, group) - lse_col)
        d_out = _stack_heads(do_ref, group)
        dsp = pp * (_dot_nt(d_out, vp_ref[...]) - delta_col)
        dsc = pc * (_dot_nt(d_out, vc_ref[...]) - delta_col)
        dq = (_dot(dsp.astype(BF16), kp) + _dot(dsc.astype(BF16), kc)) * scale
        cos_v, sin_v = cos_ref[...], sin_ref[...]
        sink_part = -p_sink * delta_col
        lane = _iota((SUBLANE, LANE), 1)
        dsink_all = jnp.zeros((SUBLANE, LANE), F32)
        for hh in range(group):
            cols = slice(hh * HEAD_DIM, (hh + 1) * HEAD_DIM)
            dq_h = _head_rows(dq, hh)
            dq_ref[:, cols] = (dq_h * cos_v
                               - pltpu.roll(dq_h, HEAD_DIM // 2, 1) * sin_v).astype(BF16)
            d_sink = jnp.sum(_head_rows(sink_part, hh), axis=0, keepdims=True)
            dsink_all = dsink_all + jnp.where(lane == first + hh, d_sink, 0.0)
        dsink_ref[...] += dsink_all

    own = wide(0)
    nat = pl.BlockSpec((SWA_BLOCK, LANE), lambda n, g: (n, 0))
    return pl.pallas_call(
        body, name="swa_bwd_q", grid=(s_len // SWA_BLOCK, kv_heads),
        in_specs=[own, prev(0), cur(0), prev(kv_heads), cur(kv_heads), own, nat, nat,
                  pl.BlockSpec((1, LANE), lambda n, g: (0, 0)), nat, nat],
        out_specs=[own, pl.BlockSpec((SUBLANE, LANE), lambda n, g: (0, 0))],
        out_shape=[jax.ShapeDtypeStruct((s_len, d), BF16),
                   jax.ShapeDtypeStruct((SUBLANE, LANE), F32)],
        compiler_params=_params(2))(proj, kv, kv, kv, kv, d_o, lse, delta, sinks_row, cos, sin)


def _swa_bwd_kv(proj, kv, d_o, lse_t, delta_t, cos, sin, heads, kv_heads):
    s_len = proj.shape[0]
    nb = s_len // SWA_BLOCK
    group = heads // kv_heads
    scale = HEAD_DIM ** -0.5

    def body(k_ref, v_ref, qm_ref, qn_ref, dom_ref, don_ref, lsem_ref, lsen_ref,
             deltam_ref, deltan_ref, cos_ref, sin_ref, dk_ref, dv_ref):
        m = pl.program_id(1)
        k, v = k_ref[...], v_ref[...]
        shape = (SWA_BLOCK, group * SWA_BLOCK)
        key, qry = _iota(shape, 0), _iota(shape, 1) & (SWA_BLOCK - 1)
        own_valid = key <= qry
        next_valid = (key > qry) & (m + 1 < nb)
        dk = jnp.zeros((SWA_BLOCK, HEAD_DIM), F32)
        dv = jnp.zeros((SWA_BLOCK, HEAD_DIM), F32)
        for q_ref, do_ref, lse_ref, delta_ref, valid in (
                (qm_ref, dom_ref, lsem_ref, deltam_ref, own_valid),
                (qn_ref, don_ref, lsen_ref, deltan_ref, next_valid)):
            q, d_out = _stack_heads(q_ref, group), _stack_heads(do_ref, group)
            lse_row = jnp.concatenate([lse_ref[hh] for hh in range(group)], axis=1)
            delta_row = jnp.concatenate([delta_ref[hh] for hh in range(group)], axis=1)
            s_t = _dot_nt(k, q) * scale
            p_t = jnp.exp(jnp.where(valid, s_t - lse_row, NEG))
            ds_t = p_t * (_dot_nt(v, d_out) - delta_row)
            dv = dv + _dot(p_t.astype(BF16), d_out)
            dk = dk + _dot(ds_t.astype(BF16), q)
        dk = dk * scale
        dk_ref[...] = (dk * cos_ref[...]
                       - pltpu.roll(dk, HEAD_DIM // 2, 1) * sin_ref[...]).astype(BF16)
        dv_ref[...] = dv.astype(BF16)

    blk = lambda fn: pl.BlockSpec((SWA_BLOCK, HEAD_DIM), fn)
    nxt = lambda m: jnp.minimum(m + 1, nb - 1)
    wide = lambda fn: pl.BlockSpec((SWA_BLOCK, group * HEAD_DIM), fn)
    rows = lambda fn: pl.BlockSpec((group, 1, SWA_BLOCK), fn)
    q_m, q_n = wide(lambda g, m: (m, g)), wide(lambda g, m: (nxt(m), g))
    r_m, r_n = rows(lambda g, m: (g, 0, m)), rows(lambda g, m: (g, 0, nxt(m)))
    nat = pl.BlockSpec((SWA_BLOCK, LANE), lambda g, m: (m, 0))
    out_blk = blk(lambda g, m: (m, g))
    width = kv_heads * HEAD_DIM
    return pl.pallas_call(
        body, name="swa_bwd_kv", grid=(kv_heads, nb),
        in_specs=[blk(lambda g, m: (m, g)), blk(lambda g, m: (m, kv_heads + g)),
                  q_m, q_n, q_m, q_n, r_m, r_n, r_m, r_n, nat, nat],
        out_specs=[out_blk, out_blk],
        out_shape=[jax.ShapeDtypeStruct((s_len, width), BF16)] * 2,
        compiler_params=_params(2))(kv, kv, proj, proj, d_o, d_o, lse_t, lse_t,
                                    delta_t, delta_t, cos, sin)


def _ada_fwd(c_rows, ada_w, bias_loc):
    n_layers, d, cols = ada_w.shape
    rows = c_rows.shape[0]
    tk = _tile(d, 512)
    n_k = d // tk

    def body(c_ref, w_ref, b_ref, mod_ref, sc_ref, acc_ref):
        k = pl.program_id(1)

        @pl.when(k == 0)
        def _():
            acc_ref[...] = jnp.zeros_like(acc_ref)

        cv = c_ref[...]
        sc = cv * _sigmoid(cv)
        sc_ref[...] = sc
        acc_ref[...] += _dot(sc.astype(BF16), w_ref[...].astype(BF16))

        @pl.when(k == n_k - 1)
        def _():
            mod_ref[...] = acc_ref[...] + b_ref[...]

    return pl.pallas_call(
        body, name="ada_fwd", grid=(n_layers, n_k),
        in_specs=[pl.BlockSpec((rows, tk), lambda l, k: (0, k)),
                  pl.BlockSpec((None, tk, cols), lambda l, k: (l, k, 0)),
                  pl.BlockSpec((None, 1, cols), lambda l, k: (l, 0, 0))],
        out_specs=[pl.BlockSpec((None, rows, cols), lambda l, k: (l, 0, 0)),
                   pl.BlockSpec((None, rows, tk), lambda l, k: (l, 0, k))],
        out_shape=[jax.ShapeDtypeStruct((n_layers, rows, cols), F32),
                   jax.ShapeDtypeStruct((n_layers, rows, d), F32)],
        scratch_shapes=[pltpu.VMEM((rows, cols), F32)],
        compiler_params=_params(2))(c_rows, ada_w, bias_loc)


def _ada_update(sc_t, dmod, w, m, v):
    n_layers, d, cols = w.shape
    tr = _tile(d, 256)
    big = pl.BlockSpec((None, tr, cols), lambda l, i: (l, i, 0))

    def body(sc_ref, dm_ref, w_ref, m_ref, v_ref, g_out, d_out, m_out, v_out):
        g = _dot(sc_ref[...], dm_ref[...])
        delta, m_new, v_new = _adamw(w_ref[...], g, m_ref[...], v_ref[...])
        g_out[...] = g
        d_out[...] = delta
        m_out[...] = m_new
        v_out[...] = v_new

    return pl.pallas_call(
        body, name="ada_update", grid=(n_layers, d // tr),
        in_specs=[pl.BlockSpec((tr, LANE), lambda l, i: (i, 0)),
                  pl.BlockSpec((None, LANE, cols), lambda l, i: (l, 0, 0)), big, big, big],
        out_specs=[big] * 4, out_shape=[jax.ShapeDtypeStruct(w.shape, F32)] * 4,
        compiler_params=_params(2))(sc_t, dmod, w, m, v)


def _row_tile(rows, bytes_per_row):
    tr = SUBLANE * 2
    while tr * 2 <= rows and rows % (tr * 2) == 0 and tr * 2 * bytes_per_row <= 24 * 2 ** 20:
        tr *= 2
    return _tile(rows, tr)


def _slab_sum(name, arrays):
    rows = arrays[0].shape[1]
    per_row = sum(2 * a.shape[2] * (a.shape[0] * a.dtype.itemsize + 4) for a in arrays)
    tr = _row_tile(rows, per_row)
    n = len(arrays)

    def body(*refs):
        for s_ref, out_ref in zip(refs[:n], refs[n:]):
            total = s_ref[0].astype(F32)
            for slot in range(1, s_ref.shape[0]):
                total = total + s_ref[slot].astype(F32)
            out_ref[...] = total

    return pl.pallas_call(
        body, name=name, grid=(rows // tr,),
        in_specs=[pl.BlockSpec((a.shape[0], tr, a.shape[2]), lambda i: (0, i, 0)) for a in arrays],
        out_specs=[pl.BlockSpec((tr, a.shape[2]), lambda i: (i, 0)) for a in arrays],
        out_shape=[jax.ShapeDtypeStruct(a.shape[1:], F32) for a in arrays],
        compiler_params=_params(1))(*arrays)


def _decay(name, w, m, v):
    rows, cols = w.shape
    tr = _row_tile(rows, 2 * cols * 4 * 6)
    blk = pl.BlockSpec((tr, cols), lambda i: (i, 0))

    def body(w_ref, m_ref, v_ref, w_out, m_out, v_out):
        w_out[...] = ADAM_WD * w_ref[...]
        m_out[...] = ADAM_B1 * m_ref[...]
        v_out[...] = ADAM_B2 * v_ref[...]

    return pl.pallas_call(body, name=name, grid=(rows // tr,), in_specs=[blk] * 3,
                          out_specs=[blk] * 3, out_shape=[jax.ShapeDtypeStruct(w.shape, F32)] * 3,
                          compiler_params=_params(1))(w, m, v)


def _shard_update(name, slabs, w, m, v, decayed=False):
    rows, cols = w.shape
    n_slabs = slabs.shape[0]
    tr = _row_tile(rows, 2 * cols * (slabs.dtype.itemsize * n_slabs + 4 * 7))
    blk = pl.BlockSpec((tr, cols), lambda i: (i, 0))
    step = _adamw_decayed if decayed else _adamw

    def body(s_ref, w_ref, m_ref, v_ref, g_out, d_out, m_out, v_out):
        g = s_ref[0].astype(F32)
        for slot in range(1, n_slabs):
            g = g + s_ref[slot].astype(F32)
        delta, m_new, v_new = step(w_ref[...], g, m_ref[...], v_ref[...])
        g_out[...] = g
        d_out[...] = delta
        m_out[...] = m_new
        v_out[...] = v_new

    return pl.pallas_call(
        body, name=name, grid=(rows // tr,),
        in_specs=[pl.BlockSpec((n_slabs, tr, cols), lambda i: (0, i, 0)), blk, blk, blk],
        out_specs=[blk] * 4, out_shape=[jax.ShapeDtypeStruct((rows, cols), F32)] * 4,
        compiler_params=_params(1))(slabs, w, m, v)


def _small_update(gathered, w, m, v):
    shape = jax.ShapeDtypeStruct(w.shape, F32)

    def body(g_ref, w_ref, m_ref, v_ref, g_out, d_out, m_out, v_out):
        g = g_ref[0]
        for dev in range(1, N_DEV):
            g = g + g_ref[dev]
        delta, m_new, v_new = _adamw(w_ref[...], g, m_ref[...], v_ref[...])
        g_out[...] = g
        d_out[...] = delta
        m_out[...] = m_new
        v_out[...] = v_new

    return pl.pallas_call(body, name="small_update", out_shape=[shape] * 4,
                          compiler_params=pltpu.CompilerParams(vmem_limit_bytes=VMEM_LIMIT),
                          )(gathered, w, m, v)


def _rope_tables(s_len):
    half = HEAD_DIM // 2
    inv = ROPE_THETA ** (-jnp.arange(half, dtype=F32) / half)
    ang = jnp.arange(s_len, dtype=F32)[:, None] * inv[None, :]
    cos, sin = jnp.cos(ang), jnp.sin(ang)
    return jnp.concatenate([cos, cos], axis=1), jnp.concatenate([-sin, sin], axis=1)


def _pad_lanes(a):
    return jnp.pad(a, ((0, 0), (0, LANE - a.shape[1])))


def _rows_of(nat, heads):
    return jnp.transpose(nat[:, :heads])[:, None, :]


def _pack(parts):
    tile = SUBLANE * LANE
    flat = []
    for p in parts:
        p = p.reshape(-1)
        flat.append(jnp.pad(p, (0, (-p.shape[0]) % tile)))
    return jnp.concatenate(flat).reshape(-1, LANE)


def _unpack(packed, shapes):
    tile = SUBLANE * LANE
    flat = packed.reshape(-1)
    out, pos = [], 0
    for shape in shapes:
        size = 1
        for dim in shape:
            size *= dim
        out.append(flat[pos:pos + size].reshape(shape))
        pos += size + (-size) % tile
    return out


def kernel(x, c, norm_g, ada_w, ada_b, a_w_in, a_b_f, a_w_out, kv_norm_g, kv_w, b_w_in, b_sinks, b_w_out, final_norm_g, loss_target, m_norm_g, m_ada_w, m_ada_b, m_a_w_in, m_a_b_f, m_a_w_out, m_kv_norm_g, m_kv_w, m_b_w_in, m_b_sinks, m_b_w_out, m_final_norm_g, v_norm_g, v_ada_w, v_ada_b, v_a_w_in, v_a_b_f, v_a_w_out, v_kv_norm_g, v_kv_w, v_b_w_in, v_b_sinks, v_b_w_out, v_final_norm_g):
    s_len, d = x.shape[1], x.shape[2]
    heads = d // HEAD_DIM
    kv_heads = kv_w.shape[1] // (2 * HEAD_DIM)
    kv_width = kv_heads * HEAD_DIM
    ada_cols = ada_w.shape[2]
    assert heads <= LANE and heads % N_DEV == 0 and a_w_in.shape[2] * N_DEV == 4 * d + heads
    me = _slot(_mesh_pos())
    x0 = x[0]
    target = loss_target[0]
    vec = lambda a: a.reshape(1, d)

    sup, extra = 4 * d // N_DEV, heads // N_DEV
    padded = jnp.pad(a_w_in[0].astype(BF16), ((0, 0), (heads, LANE)))
    big_loc = lax.dynamic_slice_in_dim(padded, heads - extra * me, sup, axis=1)
    small_loc = lax.dynamic_slice_in_dim(padded, heads + sup - extra * me, LANE, axis=1)

    (c_all,) = _all_gather("gather_c", [c])
    c_rows = jnp.pad(c_all.reshape(N_DEV, d), ((0, 2 * SUBLANE - N_DEV), (0, 0)))
    bias_loc = lax.dynamic_slice_in_dim(ada_b, me * ada_cols, ada_cols, axis=1)[:, None, :]
    mod_part, sc_rows = _ada_fwd(c_rows, ada_w, bias_loc)
    (mod_recv,) = _all_to_all("exchange_mod", [jnp.transpose(mod_part[:, :N_DEV], (1, 0, 2))])
    mod = jnp.transpose(mod_recv, (1, 0, 2)).reshape(2, 3 * d)

    g_big, g_small = _all_gather("gather_a_w_in", _after([big_loc, small_loc], mod_recv),
                                 sequencer_id=5)
    shift0, scale0, gate0 = vec(mod[0, :d]), vec(mod[0, d:2 * d]), vec(mod[0, 2 * d:])
    shift1, scale1, gate1 = vec(mod[1, :d]), vec(mod[1, d:2 * d]), vec(mod[1, 2 * d:])
    g0, g1, g_kvn, g_fin = vec(norm_g[0]), vec(norm_g[1]), vec(kv_norm_g), vec(final_norm_g)
    h0 = _norm_fwd("norm0", x0, g0, scale0, shift0)
    a_in_decayed = _decay("decay_a_w_in", a_w_in[0], m_a_w_in[0], v_a_w_in[0])
    later = [a_w_out[0].astype(BF16), kv_w.astype(BF16), b_w_in[0].astype(BF16),
             b_w_out[0].astype(BF16)]
    g_a_out, g_kv, w_b_in, g_b_out = _all_gather(
        "gather_rest", _after(later, (g_big, h0, *a_in_decayed)), sequencer_id=1)
    w_a_out = g_a_out.reshape(d, d)
    w_kv = g_kv.reshape(d, 2 * kv_width)
    w_b_out = g_b_out.reshape(d, d)
    lane_id = jnp.arange(LANE)[None, :]
    patch = jnp.stack([jnp.where(lane_id < extra * s, g_small[s - 1], g_big[s, :, :LANE])
                       for s in range(1, N_DEV)])
    w_a_main = g_big.at[1:, :, :LANE].set(patch)
    w_a_f = g_small[N_DEV - 1]
    b_in_cols = b_w_in.shape[2]
    a_tn = _tile(sup, 1024)
    w_a_main_spec = lambda tk: pl.BlockSpec(
        (None, tk, a_tn), lambda i, j, k: (j // (sup // a_tn), k, j % (sup // a_tn)))

    cos, sin = _rope_tables(s_len)
    bias_f = _pad_lanes(a_b_f)
    sinks_row = _pad_lanes(b_sinks)

    proj0 = _mm_plain("proj0", h0, w_a_main, n_cols=4 * d, tn=a_tn, b_spec=w_a_main_spec)
    f_raw = _mm_plain("proj0_f", h0, w_a_f, out_dtype=F32, tn=LANE)
    f_nat = _fgate_fwd(f_raw, bias_f)
    f_t = _rows_of(f_nat, heads)
    o0_t, lse0_t = _fox_fwd(proj0, f_nat, f_t, heads)
    o0, u0 = _gate_fwd(o0_t, proj0, 3 * heads, heads)
    y0, x1 = _mm_residual("out0", u0, w_a_out, x0, gate0)

    h1, hk = _norm_fwd("norm1", x1, g1, scale1, shift1, gb=g_kvn)
    kv = _mm_rope("kv_proj", hk, w_kv, cos, sin, n_cols=2 * kv_width, rope_cols=kv_width,
                  tn=kv_width)
    proj1 = _mm_rope("proj1", h1, w_b_in, cos, sin, n_cols=2 * d, rope_cols=d, tn=b_in_cols,
                     b_spec=lambda tk: pl.BlockSpec((None, tk, b_in_cols),
                                                    lambda i, j, k: (j, k, 0)))
    o1, u1, lse1 = _swa_fwd(proj1, kv, sinks_row, heads, kv_heads)
    y1, x2 = _mm_residual("out1", u1, w_b_out, x1, gate1)

    loss_part, dx2, dy1, sums_f = _loss_bwd(x2, target, y1, g_fin, gate1)

    do1, dz1, delta1 = _mm_gate_bwd("out1_bwd", dy1, w_b_out, proj1, d, o1)
    gw_b_out = _mm_plain("out1_wgrad", u1, dy1, ta=True)
    do1 = _after(do1, gw_b_out)
    dq1, dsinks = _swa_bwd_q(proj1, kv, do1, lse1, delta1, sinks_row, cos, sin, heads, kv_heads)
    dk1, dv1 = _swa_bwd_kv(proj1, kv, do1, _rows_of(lse1, heads), _rows_of(delta1, heads),
                           cos, sin, heads, kv_heads)
    dproj1 = jnp.concatenate([dq1, dz1], axis=1)
    tk_b = _tile(2 * d, b_in_cols)
    dh1 = _mm_plain("proj1_bwd", dproj1, w_b_in, nt=True, n_cols=d, out_dtype=F32, tk=tk_b,
                    b_spec=lambda tk: pl.BlockSpec((None, _tile(d, 1024), tk),
                                                   lambda i, j, k: (k * tk // b_in_cols, j, 0)))
    gw_b_in = _mm_plain("proj1_wgrad", h1, dproj1, ta=True, tn=b_in_cols,
                        out_3d=(N_DEV, b_in_cols))
    dkv = jnp.concatenate([dk1, dv1], axis=1)
    dhk = _mm_plain("kv_bwd", dkv, w_kv, nt=True, out_dtype=F32)
    gw_kv = _mm_plain("kv_wgrad", hk, dkv, ta=True)
    dx1, dy0, sums1 = _norm_bwd("norm1_bwd", x1, dx2, dh1, g1, scale1,
                                dhb=_after(dhk, (gw_b_in, gw_kv)), gb=g_kvn, y=y0, gate=gate0)

    do0, dz0, delta0 = _mm_gate_bwd("out0_bwd", dy0, w_a_out, proj0, 3 * d, o0)
    gw_a_out = _mm_plain("out0_wgrad", u0, dy0, ta=True)
    r_b_out, r_b_in, r_kv, r_a_out = _all_to_all(
        "scatter_grads_early",
        [gw_b_out.reshape(N_DEV, d // N_DEV, d), gw_b_in,
         gw_kv.reshape(N_DEV, d // N_DEV, 2 * kv_width),
         gw_a_out.reshape(N_DEV, d // N_DEV, d)], sequencer_id=2)
    dk0, dv0, dq0_t, dfq_t, dfk_nat = _fox_bwd(proj0, _after(do0, gw_a_out), f_nat, f_t,
                                               lse0_t, _rows_of(delta0, heads), heads)
    dfq_nat = _pad_lanes(jnp.transpose(dfq_t[:, 0, :]))
    df, sums_bf = _fgate_bwd(dfq_nat, dfk_nat, f_raw, bias_f)
    dproj0 = jnp.concatenate([jnp.transpose(dq0_t), dk0, dv0, dz0], axis=1)
    gw_a_big = _mm_plain("proj0_wgrad", h0, dproj0, ta=True, tn=a_tn, out_3d=(N_DEV, sup))
    gw_a_f = _mm_plain("proj0_f_wgrad", h0, df, ta=True, tn=LANE)
    gw_a_small = jnp.concatenate([gw_a_big[1:, :, :LANE], gw_a_f[None]], axis=0)
    sib_big, sib_small = _pair_exchange("reduce_a_w_in_pair", [gw_a_big, gw_a_small],
                                        sequencer_id=3)
    r_b_out, r_b_in, r_kv, r_a_out = _after([r_b_out, r_b_in, r_kv, r_a_out], gw_a_big)
    up_b_out = _shard_update("update_b_w_out", r_b_out, b_w_out[0], m_b_w_out[0], v_b_w_out[0])
    up_b_in = _shard_update("update_b_w_in", r_b_in, b_w_in[0], m_b_w_in[0], v_b_w_in[0])
    up_kv = _shard_update("update_kv_w", r_kv, kv_w, m_kv_w, v_kv_w)
    up_a_out = _shard_update("update_a_w_out", r_a_out, a_w_out[0], m_a_w_out[0], v_a_w_out[0])
    sib_big, sib_small = _after((sib_big, sib_small),
                                (up_b_out[0], up_b_in[0], up_kv[0], up_a_out[0]))
    chip_big, chip_small = _pair_add("reduce_a_w_in_add", [gw_a_big, gw_a_small],
                                     [sib_big, sib_small])
    r_a_big, r_a_small = _chip_exchange("reduce_a_w_in_chips", [chip_big, chip_small],
                                        sequencer_id=4)
    df = _after(df, (chip_big, chip_small))
    dh0_f = _mm_plain("proj0_f_bwd", df, w_a_f, nt=True, out_dtype=F32)
    dh0 = _mm_plain("proj0_bwd", dproj0, w_a_main, nt=True, n_cols=d, out_dtype=F32, init=dh0_f,
                    tk=sup, b_spec=lambda tk: pl.BlockSpec((None, _tile(d, 1024), tk),
                                                           lambda i, j, k: (k, j, 0)))
    grad_x, sums0 = _norm_bwd("norm0_bwd", x0, dx1, dh0, g0, scale0)

    dmod = jnp.stack([jnp.concatenate([sums0[0], sums0[1], sums1[4]]),
                      jnp.concatenate([sums1[0], sums1[1], sums_f[1]])])
    small_shapes = [(2, 3 * d), (2, d), (1, heads), (d,), (1, heads), (d,), (1,)]
    small_grads = [dmod, jnp.stack([sums0[2], sums1[2]]), sums_bf[0:1, :heads], sums1[3],
                   dsinks[0:1, :heads], sums_f[0], loss_part[0, 0:1]]
    (small_all,) = _all_gather("gather_small", [_pack(small_grads)])
    zero = jnp.zeros((1,), F32)
    small = _small_update(
        small_all,
        _pack([ada_b, norm_g, a_b_f, kv_norm_g, b_sinks, final_norm_g, zero]),
        _pack([m_ada_b, m_norm_g, m_a_b_f, m_kv_norm_g, m_b_sinks, m_final_norm_g, zero]),
        _pack([v_ada_b, v_norm_g, v_a_b_f, v_kv_norm_g, v_b_sinks, v_final_norm_g, zero]))
    s_grad, s_delta, s_m, s_v = [_unpack(p, small_shapes) for p in small]
    loss = s_grad[6][0]

    dmod_all = small_all.reshape(N_DEV, -1)[:, :2 * 3 * d].reshape(N_DEV, 2, 3 * d)
    dmod_loc = lax.dynamic_slice_in_dim(dmod_all, me * ada_cols, ada_cols, axis=2)
    dmod_loc = jnp.pad(jnp.transpose(dmod_loc, (1, 0, 2)), ((0, 0), (0, LANE - N_DEV), (0, 0)))
    sc_t = jnp.pad(jnp.transpose(sc_rows[0, :N_DEV]), ((0, 0), (0, LANE - N_DEV)))
    up_ada = _ada_update(sc_t.astype(BF16), dmod_loc.astype(BF16), ada_w, m_ada_w, v_ada_w)

    r_a_big, r_a_small = _after((r_a_big, r_a_small), (up_ada[0], small[0]))
    ga_big, ga_small = _slab_sum("sum_a_w_in", [r_a_big, r_a_small])
    ga_shard = lax.dynamic_slice_in_dim(jnp.concatenate([ga_big, ga_small], axis=1),
                                        extra * me, sup + extra, axis=1)
    up_a_in = _shard_update("update_a_w_in", ga_shard[None], *a_in_decayed, decayed=True)

    lead = lambda a: a[None]
    per_kind = []
    for kind in range(4):
        sm = (s_grad, s_delta, s_m, s_v)[kind]
        per_kind.append([
            sm[1], up_ada[kind], sm[0], lead(up_a_in[kind]), sm[2], lead(up_a_out[kind]),
            sm[3], up_kv[kind], lead(up_b_in[kind]), sm[4], lead(up_b_out[kind]), sm[5]])
    return (loss, grad_x[None], *per_kind[0], *per_kind[1], *per_kind[2], *per_kind[3])
```

```python
import jax
import jax.numpy as jnp
from jax import lax
from jax.experimental import pallas as pl
from jax.experimental.pallas import tpu as pltpu
from jax.experimental.pallas import tpu_sc as plsc

F32 = jnp.float32
BF16 = jnp.bfloat16
LANE = 128
SUBLANE = 8
HEAD_DIM = 128
SWA_BLOCK = 128
N_DEV = 8
N_PEER = N_DEV - 1
RMS_EPS = 1e-6
ROPE_THETA = 10000.0
NEG = -1e30
VMEM_LIMIT = 56 * 2 ** 20
MM_RESERVE = 10 * 2 ** 20
MESH = pl.DeviceIdType.MESH
HIGHEST = lax.Precision.HIGHEST

ADAM_LR = 0.001
ADAM_B1 = 0.9
ADAM_B2 = 0.999
ADAM_EPS = 1e-08
ADAM_WD = 0.01
ADAM_STEP = 10


def _tile(dim, pref):
    return pref if dim % pref == 0 else dim


def _params(n_axes):
    return pltpu.CompilerParams(dimension_semantics=("arbitrary",) * n_axes,
                                vmem_limit_bytes=VMEM_LIMIT)


def _dot(a, b):
    return jnp.dot(a, b, preferred_element_type=F32)


def _dot_nt(a, b):
    return lax.dot_general(a, b, (((1,), (1,)), ((), ())), preferred_element_type=F32)


def _dot_tn(a, b):
    return lax.dot_general(a, b, (((0,), (0,)), ((), ())), preferred_element_type=F32)


def _sigmoid(z):
    return 1.0 / (1.0 + jnp.exp(-z))


def _iota(shape, dim):
    return lax.broadcasted_iota(jnp.int32, shape, dim)


def _pick_lane(block, lane_index):
    lane = _iota(block.shape, 1)
    return jnp.sum(jnp.where(lane == lane_index, block, 0.0), axis=1, keepdims=True)


def _adamw_decayed(w_decay, g, m_decayed, v_decayed):
    m = m_decayed + (1.0 - ADAM_B1) * g
    v = v_decayed + (1.0 - ADAM_B2) * (g * g)
    m_hat = m / (1.0 - ADAM_B1 ** ADAM_STEP)
    v_hat = v / (1.0 - ADAM_B2 ** ADAM_STEP)
    delta = -ADAM_LR * (m_hat / (jnp.sqrt(v_hat) + ADAM_EPS) + w_decay)
    return delta, m, v


def _adamw(w, g, m, v):
    return _adamw_decayed(ADAM_WD * w, g, ADAM_B1 * m, ADAM_B2 * v)


def _mesh_pos():
    return lax.axis_index("x"), lax.axis_index("y"), lax.axis_index("c")


def _slot(pos):
    return 4 * pos[0] + 2 * pos[1] + pos[2]


def _handshake(peers):
    barrier = pltpu.get_barrier_semaphore()
    for peer in peers:
        pl.semaphore_signal(barrier, inc=1, device_id=peer, device_id_type=MESH)
    pl.semaphore_wait(barrier, len(peers))


def _launch(name, body, arrays, out_shape, sequencer_id):
    n = len(arrays)
    scratch = [pltpu.SemaphoreType.DMA((N_PEER * n,)), pltpu.SemaphoreType.DMA((N_PEER * n,)),
               pltpu.SemaphoreType.DMA((n,))]
    if sequencer_id is None:
        any_spec = pl.BlockSpec(memory_space=pl.ANY)
        return pl.pallas_call(body, name=name, out_shape=out_shape, in_specs=[any_spec] * n,
                              out_specs=[any_spec] * n, scratch_shapes=scratch)(*arrays)
    return pl.kernel(body, name=name, out_type=out_shape,
                     mesh=plsc.ScalarSubcoreMesh(axis_name="sequencer", num_cores=1),
                     scratch_types=scratch,
                     compiler_params=pltpu.CompilerParams(collective_id=sequencer_id))(*arrays)


def _all_gather(name, arrays, sequencer_id=None):
    n = len(arrays)

    def body(*refs):
        ins, outs = refs[:n], refs[n:2 * n]
        send_sems, recv_sems, local_sems = refs[2 * n:]
        x, y, c = _mesh_pos()
        me, sibling = (x, y, c), (x, y, 1 - c)
        chips = [(1 - x, y), (x, 1 - y), (1 - x, 1 - y)]
        if sequencer_id is not None:
            _handshake([sibling] + [(*chip, c) for chip in chips])

        def copy(a, k, block, to, src=None):
            dst = outs[a].at[_slot(block)]
            return pltpu.make_async_remote_copy(
                src_ref=dst if src is None else src, dst_ref=dst,
                send_sem=send_sems.at[N_PEER * a + k], recv_sem=recv_sems.at[N_PEER * a + k],
                device_id=to, device_id_type=MESH)

        local, first, passed = [], [], []
        for a in range(n):
            cp = pltpu.make_async_copy(ins[a], outs[a].at[_slot(me)], local_sems.at[a])
            cp.start()
            local.append(cp)
            sends = [copy(a, 0, me, sibling, src=ins[a])]
            sends += [copy(a, 1 + j, me, (*chip, c), src=ins[a]) for j, chip in enumerate(chips)]
            for cp in sends:
                cp.start()
            first += sends
        for a in range(n):
            for j, chip in enumerate(chips):
                copy(a, 1 + j, (*chip, c), me).wait_recv()
                cp = copy(a, 4 + j, (*chip, c), sibling)
                cp.start()
                passed.append(cp)
        for a in range(n):
            copy(a, 0, sibling, me).wait_recv()
            for j, chip in enumerate(chips):
                copy(a, 4 + j, (*chip, 1 - c), me).wait_recv()
        for cp in first + passed:
            cp.wait_send()
        for cp in local:
            cp.wait()

    out_shape = [jax.ShapeDtypeStruct((N_DEV,) + a.shape, a.dtype) for a in arrays]
    return _launch(name, body, arrays, out_shape, sequencer_id)


def _all_to_all(name, arrays, sequencer_id=None):
    n = len(arrays)

    def body(*refs):
        ins, outs = refs[:n], refs[n:2 * n]
        send_sems, recv_sems, local_sems = refs[2 * n:]
        x, y, c = _mesh_pos()
        me = _slot((x, y, c))
        if sequencer_id is not None:
            _handshake([(1 - x if k & 4 else x, 1 - y if k & 2 else y, 1 - c if k & 1 else c)
                        for k in range(1, N_DEV)])
        local, sends, recvs = [], [], []
        for a in range(n):
            cp = pltpu.make_async_copy(ins[a].at[me], outs[a].at[me], local_sems.at[a])
            cp.start()
            local.append(cp)
        for k in range(1, N_DEV):
            peer = (1 - x if k & 4 else x, 1 - y if k & 2 else y, 1 - c if k & 1 else c)
            ps = _slot(peer)
            for a in range(n):
                sem = N_PEER * a + k - 1
                cp = pltpu.make_async_remote_copy(
                    src_ref=ins[a].at[ps], dst_ref=outs[a].at[me],
                    send_sem=send_sems.at[sem], recv_sem=recv_sems.at[sem],
                    device_id=peer, device_id_type=MESH)
                cp.start()
                sends.append(cp)
                recvs.append(pltpu.make_async_remote_copy(
                    src_ref=ins[a].at[ps], dst_ref=outs[a].at[ps],
                    send_sem=send_sems.at[sem], recv_sem=recv_sems.at[sem],
                    device_id=peer, device_id_type=MESH))
        for cp in recvs:
            cp.wait_recv()
        for cp in sends:
            cp.wait_send()
        for cp in local:
            cp.wait()

    out_shape = [jax.ShapeDtypeStruct(a.shape, a.dtype) for a in arrays]
    return _launch(name, body, arrays, out_shape, sequencer_id)


def _pair_exchange(name, arrays, sequencer_id=None):
    n = len(arrays)
    chips = N_DEV // 2

    def body(*refs):
        ins, outs = refs[:n], refs[n:2 * n]
        send_sems, recv_sems = refs[2 * n], refs[2 * n + 1]
        x, y, c = _mesh_pos()
        sibling = (x, y, 1 - c)
        if sequencer_id is not None:
            _handshake([sibling])
        copies = [pltpu.make_async_remote_copy(
            src_ref=ins[a].at[2 * q + 1 - c], dst_ref=outs[a].at[q],
            send_sem=send_sems.at[chips * a + q], recv_sem=recv_sems.at[chips * a + q],
            device_id=sibling, device_id_type=MESH) for a in range(n) for q in range(chips)]
        for cp in copies:
            cp.start()
        for cp in copies:
            cp.wait()

    out_shape = [jax.ShapeDtypeStruct((chips,) + a.shape[1:], a.dtype) for a in arrays]
    return _launch(name, body, arrays, out_shape, sequencer_id)


def _chip_exchange(name, arrays, sequencer_id=None):
    n = len(arrays)
    chips = N_DEV // 2

    def body(*refs):
        ins, outs = refs[:n], refs[n:2 * n]
        send_sems, recv_sems, local_sems = refs[2 * n:]
        x, y, c = _mesh_pos()
        mine = 2 * x + y
        others = [(1 - x, y), (x, 1 - y), (1 - x, 1 - y)]
        if sequencer_id is not None:
            _handshake([(*chip, c) for chip in others])
        local = [pltpu.make_async_copy(ins[a].at[mine], outs[a].at[mine], local_sems.at[a])
                 for a in range(n)]
        for cp in local:
            cp.start()
        sends, recvs = [], []
        for j, chip in enumerate(others):
            theirs = 2 * chip[0] + chip[1]
            for a in range(n):
                both = dict(send_sem=send_sems.at[3 * a + j], recv_sem=recv_sems.at[3 * a + j],
                            device_id=(*chip, c), device_id_type=MESH)
                sends.append(pltpu.make_async_remote_copy(
                    src_ref=ins[a].at[theirs], dst_ref=outs[a].at[mine], **both))
                recvs.append(pltpu.make_async_remote_copy(
                    src_ref=ins[a].at[theirs], dst_ref=outs[a].at[theirs], **both))
        for cp in sends:
            cp.start()
        for cp in recvs:
            cp.wait_recv()
        for cp in sends:
            cp.wait_send()
        for cp in local:
            cp.wait()

    out_shape = [jax.ShapeDtypeStruct(a.shape, a.dtype) for a in arrays]
    return _launch(name, body, arrays, out_shape, sequencer_id)


def _pair_add(name, mine, theirs):
    chips, rows = theirs[0].shape[0], theirs[0].shape[1]
    per_row = sum(2 * 3 * a.shape[2] * a.dtype.itemsize for a in theirs)
    tr = _row_tile(rows, per_row)
    n = len(theirs)
    core = lax.axis_index("c").astype(jnp.int32).reshape(1)

    def body(core_ref, *refs):
        for a in range(n):
            refs[2 * n + a][...] = (refs[a][...].astype(F32)
                                    + refs[n + a][...].astype(F32)).astype(refs[2 * n + a].dtype)

    blk = lambda a, fn: pl.BlockSpec((None, tr, a.shape[2]), fn)
    grid_spec = pltpu.PrefetchScalarGridSpec(
        num_scalar_prefetch=1, grid=(chips, rows // tr),
        in_specs=[blk(a, lambda q, i, core_ref: (2 * q + core_ref[0], i, 0)) for a in mine]
        + [blk(a, lambda q, i, core_ref: (q, i, 0)) for a in theirs],
        out_specs=[blk(a, lambda q, i, core_ref: (q, i, 0)) for a in theirs])
    return pl.pallas_call(
        body, name=name, grid_spec=grid_spec,
        out_shape=[jax.ShapeDtypeStruct(a.shape, a.dtype) for a in theirs],
        compiler_params=_params(2))(core, *mine, *theirs)


def _after(value, token):
    return lax.optimization_barrier((value, token))[0]


def _k_tile(k_dim, tm, tn, fixed_bytes):
    budget = VMEM_LIMIT - MM_RESERVE - fixed_bytes
    tk = k_dim
    while tk % 2 == 0 and tk > 512 and (
            4 * (tm + tn) * tk + (4 * tm * tn if tk < k_dim else 0) > budget):
        tk //= 2
    return tk


def _matmul(name, a, b, *, nt, tm, tn, n_cols, out_shape, out_specs, epilogue,
            fixed_bytes, ta=False, tk=None, b_spec=None, extra=(), extra_specs=()):
    assert not (ta and nt)
    k_dim, m_rows = a.shape if ta else a.shape[::-1]
    tm, tn = _tile(m_rows, tm), _tile(n_cols, tn)
    tk = _k_tile(k_dim, tm, tn, fixed_bytes) if tk is None else _tile(k_dim, tk)
    grid = (m_rows // tm, n_cols // tn, k_dim // tk)
    n_k = grid[2]
    if ta:
        a_spec = pl.BlockSpec((tk, tm), lambda i, j, k: (k, i))
    else:
        a_spec = pl.BlockSpec((tm, tk), lambda i, j, k: (i, k))
    if b_spec is not None:
        b_blk = b_spec(tk)
    elif nt:
        b_blk = pl.BlockSpec((tn, tk), lambda i, j, k: (j, k))
    else:
        b_blk = pl.BlockSpec((tk, tn), lambda i, j, k: (k, j))
    n_extra, n_out = len(extra), len(out_shape)
    product = _dot_tn if ta else _dot_nt if nt else _dot

    def body(a_ref, b_ref, *rest):
        extra_refs = rest[:n_extra]
        out_refs = rest[n_extra:n_extra + n_out]
        if n_k == 1:
            epilogue(product(a_ref[...], b_ref[...]), extra_refs, out_refs)
            return
        acc_ref = rest[n_extra + n_out]
        k = pl.program_id(2)

        @pl.when(k == 0)
        def _():
            acc_ref[...] = jnp.zeros_like(acc_ref)

        acc_ref[...] += product(a_ref[...], b_ref[...])

        @pl.when(k == n_k - 1)
        def _():
            epilogue(acc_ref[...], extra_refs, out_refs)

    return pl.pallas_call(
        body, name=name, grid=grid,
        in_specs=[a_spec, b_blk, *extra_specs], out_specs=out_specs, out_shape=out_shape,
        scratch_shapes=[pltpu.VMEM((tm, tn), F32)] if n_k > 1 else [],
        compiler_params=_params(3),
    )(a, b, *extra)


def _mm_plain(name, a, b, *, nt=False, ta=False, n_cols=None, out_dtype=BF16, init=None,
              tm=1024, tn=1024, tk=None, b_spec=None, out_3d=None):
    m_rows = a.shape[1] if ta else a.shape[0]
    if n_cols is None:
        n_cols = b.shape[0] if nt else b.shape[1]
    tm, tn = _tile(m_rows, tm), _tile(n_cols, tn)
    fixed = 2 * tm * tn * (jnp.dtype(out_dtype).itemsize + (4 if init is not None else 0))
    if out_3d is None:
        shape = jax.ShapeDtypeStruct((m_rows, n_cols), out_dtype)
        spec = pl.BlockSpec((tm, tn), lambda i, j, k: (i, j))
    else:
        slabs, width = out_3d
        assert width % tn == 0 and slabs * width == n_cols
        per = width // tn
        shape = jax.ShapeDtypeStruct((slabs, m_rows, width), out_dtype)
        spec = pl.BlockSpec((None, tm, tn), lambda i, j, k: (j // per, i, j % per))
    extra, extra_specs = (), ()
    if init is not None:
        extra = (init,)
        extra_specs = (pl.BlockSpec((tm, tn), lambda i, j, k: (i, j)),)

    def epilogue(acc, extra_refs, out_refs):
        if init is not None:
            acc = acc + extra_refs[0][...]
        out_refs[0][...] = acc.astype(out_dtype)

    (out,) = _matmul(name, a, b, nt=nt, ta=ta, tm=tm, tn=tn, tk=tk, n_cols=n_cols,
                     out_shape=[shape], out_specs=[spec], epilogue=epilogue, fixed_bytes=fixed,
                     b_spec=b_spec, extra=extra, extra_specs=extra_specs)
    return out


def _mm_rope(name, a, b, cos, sin, *, n_cols, rope_cols, tn, b_spec=None):
    m_rows = a.shape[0]
    tm = _tile(m_rows, 1024)
    tn = _tile(n_cols, tn)
    assert rope_cols % tn == 0 and tn % HEAD_DIM == 0
    rope_blocks = rope_cols // tn
    table_spec = pl.BlockSpec((tm, LANE), lambda i, j, k: (i, 0))

    def epilogue(acc, extra_refs, out_refs):
        cos_ref, sin_ref = extra_refs
        j = pl.program_id(1)

        @pl.when(j < rope_blocks)
        def _():
            for head in range(tn // HEAD_DIM):
                cols = slice(head * HEAD_DIM, (head + 1) * HEAD_DIM)
                blk = acc[:, cols]
                rot = pltpu.roll(blk, HEAD_DIM // 2, 1)
                out_refs[0][:, cols] = (blk * cos_ref[...] + rot * sin_ref[...]).astype(BF16)

        @pl.when(j >= rope_blocks)
        def _():
            out_refs[0][...] = acc.astype(BF16)

    (out,) = _matmul(name, a, b, nt=False, tm=tm, tn=tn, n_cols=n_cols,
                     out_shape=[jax.ShapeDtypeStruct((m_rows, n_cols), BF16)],
                     out_specs=[pl.BlockSpec((tm, tn), lambda i, j, k: (i, j))],
                     epilogue=epilogue, fixed_bytes=4 * tm * tn + 16 * tm * LANE, b_spec=b_spec,
                     extra=(cos, sin), extra_specs=(table_spec, table_spec))
    return out


def _mm_residual(name, u, w, x_in, gate):
    m_rows, n_cols = x_in.shape
    tm, tn = _tile(m_rows, 512), _tile(n_cols, 1024)
    blk = pl.BlockSpec((tm, tn), lambda i, j, k: (i, j))

    def epilogue(acc, extra_refs, out_refs):
        x_ref, gate_ref = extra_refs
        out_refs[0][...] = acc
        out_refs[1][...] = x_ref[...] + gate_ref[...] * acc

    y, x_out = _matmul(
        name, u, w, nt=False, tm=tm, tn=tn, n_cols=n_cols,
        out_shape=[jax.ShapeDtypeStruct((m_rows, n_cols), F32)] * 2, out_specs=[blk, blk],
        epilogue=epilogue, fixed_bytes=3 * 8 * tm * tn, extra=(x_in, gate),
        extra_specs=(blk, pl.BlockSpec((1, tn), lambda i, j, k: (0, j))))
    return y, x_out


def _mm_gate_bwd(name, dy, w_out, z_src, z_col0, o):
    m_rows = dy.shape[0]
    n_cols = w_out.shape[0]
    tm, tn = _tile(m_rows, 1024), _tile(n_cols, 1024)
    assert z_col0 % tn == 0 and tn % HEAD_DIM == 0 and n_cols // HEAD_DIM <= LANE
    z_blk0 = z_col0 // tn
    blk = pl.BlockSpec((tm, tn), lambda i, j, k: (i, j))

    def epilogue(du, extra_refs, out_refs):
        z_ref, o_ref = extra_refs
        do_ref, dz_ref, delta_ref = out_refs
        j = pl.program_id(1)
        z = z_ref[...].astype(F32)
        o_val = o_ref[...].astype(F32)
        sig = _sigmoid(z)
        d_o = (du * (z * sig)).astype(BF16)
        do_ref[...] = d_o
        dz_ref[...] = (du * o_val * (sig * (1.0 + z * (1.0 - sig)))).astype(BF16)

        @pl.when(j == 0)
        def _():
            delta_ref[...] = jnp.zeros_like(delta_ref)

        prod = d_o.astype(F32) * o_val
        lane = _iota((tm, LANE), 1)
        delta = delta_ref[...]
        for head in range(tn // HEAD_DIM):
            rows = jnp.sum(prod[:, head * HEAD_DIM:(head + 1) * HEAD_DIM], axis=1, keepdims=True)
            delta = delta + jnp.where(lane == j * (tn // HEAD_DIM) + head, rows, 0.0)
        delta_ref[...] = delta

    d_o, dz, delta = _matmul(
        name, dy, w_out, nt=True, tm=tm, tn=tn, n_cols=n_cols,
        out_shape=[jax.ShapeDtypeStruct((m_rows, n_cols), BF16)] * 2
        + [jax.ShapeDtypeStruct((m_rows, LANE), F32)],
        out_specs=[blk, blk, pl.BlockSpec((tm, LANE), lambda i, j, k: (i, 0))],
        epilogue=epilogue, fixed_bytes=4 * 4 * tm * tn + 8 * tm * LANE, extra=(z_src, o),
        extra_specs=(pl.BlockSpec((tm, tn), lambda i, j, k: (i, z_blk0 + j)), blk))
    return d_o, dz, delta


def _norm_fwd(name, x, ga, sa, ta, gb=None):
    s_len, d = x.shape
    tr = _tile(s_len, 256)
    two = gb is not None
    row = pl.BlockSpec((tr, d), lambda i: (i, 0))
    vec = pl.BlockSpec((1, d), lambda i: (0, 0))

    def body(x_ref, ga_ref, sa_ref, ta_ref, *rest):
        xv = x_ref[...]
        y = xv * lax.rsqrt(jnp.mean(xv * xv, axis=-1, keepdims=True) + RMS_EPS)
        rest[-2 if two else -1][...] = ((y * ga_ref[...]) * (1.0 + sa_ref[...]) + ta_ref[...]).astype(BF16)
        if two:
            rest[-1][...] = (y * rest[0][...]).astype(BF16)

    ins = [x, ga, sa, ta] + ([gb] if two else [])
    outs = pl.pallas_call(
        body, name=name, grid=(s_len // tr,),
        in_specs=[row] + [vec] * (len(ins) - 1),
        out_specs=[row] * (2 if two else 1),
        out_shape=[jax.ShapeDtypeStruct((s_len, d), BF16)] * (2 if two else 1),
        compiler_params=_params(1))(*ins)
    return outs if two else outs[0]


def _loss_bwd(x2, target, y1, g_final, gate1):
    s_len, d = x2.shape
    tr = _tile(s_len, 128)
    row = pl.BlockSpec((tr, d), lambda i: (i, 0))
    vec = pl.BlockSpec((1, d), lambda i: (0, 0))

    def body(x_ref, t_ref, y_ref, g_ref, gate_ref, loss_ref, dx_ref, dy_ref, sums_ref):
        @pl.when(pl.program_id(0) == 0)
        def _():
            loss_ref[...] = jnp.zeros_like(loss_ref)
            sums_ref[...] = jnp.zeros_like(sums_ref)

        xv = x_ref[...]
        rstd = lax.rsqrt(jnp.mean(xv * xv, axis=-1, keepdims=True) + RMS_EPS)
        xhat = xv * rstd
        g = g_ref[...]
        err = xhat * g - t_ref[...]
        sq = jnp.sum(jnp.sum(err * err, axis=1, keepdims=True), axis=0, keepdims=True)
        loss_ref[...] += sq * (0.5 / d)
        dout = err * (1.0 / d)
        dxhat = dout * g
        dx = rstd * (dxhat - xhat * jnp.mean(dxhat * xhat, axis=-1, keepdims=True))
        dx_ref[...] = dx
        dy_ref[...] = (dx * gate_ref[...]).astype(BF16)
        sums_ref[0:1, :] += jnp.sum(dout * xhat, axis=0, keepdims=True)
        sums_ref[1:2, :] += jnp.sum(dx * y_ref[...], axis=0, keepdims=True)

    return pl.pallas_call(
        body, name="loss_bwd", grid=(s_len // tr,),
        in_specs=[row, row, row, vec, vec],
        out_specs=[pl.BlockSpec((SUBLANE, LANE), lambda i: (0, 0)), row, row,
                   pl.BlockSpec((SUBLANE, d), lambda i: (0, 0))],
        out_shape=[jax.ShapeDtypeStruct((SUBLANE, LANE), F32),
                   jax.ShapeDtypeStruct((s_len, d), F32),
                   jax.ShapeDtypeStruct((s_len, d), BF16),
                   jax.ShapeDtypeStruct((SUBLANE, d), F32)],
        compiler_params=_params(1))(x2, target, y1, g_final, gate1)


def _norm_bwd(name, x, dres, dha, ga, sa, dhb=None, gb=None, y=None, gate=None):
    s_len, d = x.shape
    tr = _tile(s_len, 128)
    has_b, has_y = dhb is not None, y is not None
    row = pl.BlockSpec((tr, d), lambda i: (i, 0))
    vec = pl.BlockSpec((1, d), lambda i: (0, 0))
    ins, specs = [x, dres, dha, ga, sa], [row, row, row, vec, vec]
    if has_b:
        ins += [dhb, gb]
        specs += [row, vec]
    if has_y:
        ins += [y, gate]
        specs += [row, vec]
    n_in = len(ins)

    def body(*refs):
        x_ref, dres_ref, dha_ref, ga_ref, sa_ref = refs[:5]
        pos = 5
        if has_b:
            dhb_ref, gb_ref = refs[pos:pos + 2]
            pos += 2
        if has_y:
            y_ref, gate_ref = refs[pos:pos + 2]
        outs = refs[n_in:]
        dx_ref, sums_ref = outs[0], outs[-1]

        @pl.when(pl.program_id(0) == 0)
        def _():
            sums_ref[...] = jnp.zeros_like(sums_ref)

        xv = x_ref[...]
        rstd = lax.rsqrt(jnp.mean(xv * xv, axis=-1, keepdims=True) + RMS_EPS)
        xhat = xv * rstd
        dha_v = dha_ref[...]
        ga_v, sa_v = ga_ref[...], sa_ref[...]
        dxhat = dha_v * (ga_v * (1.0 + sa_v))
        sums_ref[0:1, :] += jnp.sum(dha_v, axis=0, keepdims=True)
        sums_ref[1:2, :] += jnp.sum(dha_v * (xhat * ga_v), axis=0, keepdims=True)
        sums_ref[2:3, :] += jnp.sum(dha_v * ((1.0 + sa_v) * xhat), axis=0, keepdims=True)
        if has_b:
            dhb_v = dhb_ref[...]
            dxhat = dxhat + dhb_v * gb_ref[...]
            sums_ref[3:4, :] += jnp.sum(dhb_v * xhat, axis=0, keepdims=True)
        dx = dres_ref[...] + rstd * (dxhat - xhat * jnp.mean(dxhat * xhat, axis=-1, keepdims=True))
        dx_ref[...] = dx
        if has_y:
            outs[1][...] = (dx * gate_ref[...]).astype(BF16)
            sums_ref[4:5, :] += jnp.sum(dx * y_ref[...], axis=0, keepdims=True)

    out_shape = [jax.ShapeDtypeStruct((s_len, d), F32)]
    out_specs = [row]
    if has_y:
        out_shape.append(jax.ShapeDtypeStruct((s_len, d), BF16))
        out_specs.append(row)
    out_shape.append(jax.ShapeDtypeStruct((SUBLANE, d), F32))
    out_specs.append(pl.BlockSpec((SUBLANE, d), lambda i: (0, 0)))
    return pl.pallas_call(body, name=name, grid=(s_len // tr,), in_specs=specs,
                          out_specs=out_specs, out_shape=out_shape,
                          compiler_params=_params(1))(*ins)


def _fgate_fwd(f_raw, bias_row):
    s_len = f_raw.shape[0]
    tb = _tile(s_len, 512)
    blk = pl.BlockSpec((tb, LANE), lambda t: (t, 0))

    def body(f_ref, b_ref, out_ref, carry):
        @pl.when(pl.program_id(0) == 0)
        def _():
            carry[...] = jnp.zeros_like(carry)

        u = f_ref[...] + b_ref[...]
        logf = jnp.minimum(u, 0.0) - jnp.log1p(jnp.exp(-jnp.abs(u)))
        tri = (_iota((tb, tb), 1) <= _iota((tb, tb), 0)).astype(F32)
        run = jnp.dot(tri, logf, precision=HIGHEST, preferred_element_type=F32) + carry[...]
        out_ref[...] = run
        carry[...] = run[tb - 1:tb, :]

    return pl.pallas_call(
        body, name="fgate_fwd", grid=(s_len // tb,),
        in_specs=[blk, pl.BlockSpec((1, LANE), lambda t: (0, 0))], out_specs=blk,
        out_shape=jax.ShapeDtypeStruct((s_len, LANE), F32),
        scratch_shapes=[pltpu.VMEM((1, LANE), F32)],
        compiler_params=_params(1))(f_raw, bias_row)


def _fgate_bwd(df_a, df_b, f_raw, bias_row):
    s_len = f_raw.shape[0]
    tb = _tile(s_len, 512)
    nb = s_len // tb
    blk = pl.BlockSpec((tb, LANE), lambda t: (nb - 1 - t, 0))

    def body(a_ref, b2_ref, f_ref, b_ref, df_ref, sums_ref, carry):
        @pl.when(pl.program_id(0) == 0)
        def _():
            carry[...] = jnp.zeros_like(carry)
            sums_ref[...] = jnp.zeros_like(sums_ref)

        d_run = a_ref[...] + b2_ref[...]
        tri = (_iota((tb, tb), 1) >= _iota((tb, tb), 0)).astype(F32)
        dlogf = jnp.dot(tri, d_run, precision=HIGHEST, preferred_element_type=F32) + carry[...]
        carry[...] = dlogf[0:1, :]
        u = f_ref[...] + b_ref[...]
        df = dlogf * _sigmoid(-u)
        df_ref[...] = df.astype(BF16)
        sums_ref[...] += jnp.sum(df, axis=0, keepdims=True)

    return pl.pallas_call(
        body, name="fgate_bwd", grid=(nb,),
        in_specs=[blk, blk, blk, pl.BlockSpec((1, LANE), lambda t: (0, 0))],
        out_specs=[blk, pl.BlockSpec((SUBLANE, LANE), lambda t: (0, 0))],
        out_shape=[jax.ShapeDtypeStruct((s_len, LANE), BF16),
                   jax.ShapeDtypeStruct((SUBLANE, LANE), F32)],
        scratch_shapes=[pltpu.VMEM((1, LANE), F32)],
        compiler_params=_params(1))(df_a, df_b, f_raw, bias_row)


def _fox_fwd(proj, f_nat, f_t, heads):
    s_len = proj.shape[0]
    d = heads * HEAD_DIM
    t = _tile(s_len, 512)
    nq = s_len // t
    scale = HEAD_DIM ** -0.5

    pair = 2 if nq % 2 == 0 else 1

    def body(k_ref, q_ref, v_ref, fn_ref, ft_ref, ot_ref, lse_ref,
             acc_scr, m_scr, l_scr, fk_scr):
        j, p = pl.program_id(1), pl.program_id(2)
        h = pl.program_id(0)
        first = j // pair

        @pl.when((j == 0) & (p == 0))
        def _():
            m_scr[...] = jnp.full_like(m_scr, NEG)
            l_scr[...] = jnp.zeros_like(l_scr)
            acc_scr[...] = jnp.zeros_like(acc_scr)

        @pl.when(p == first)
        def _():
            fk_scr[...] = _pick_lane(fn_ref[...], h)

        def update(slot, diagonal):
            i = pair * p + slot
            rows = slice(slot * t, (slot + 1) * t)
            s_t = (_dot_nt(k_ref[...], q_ref[rows, :]) * scale
                   + (ft_ref[:, rows] - fk_scr[...]))
            if diagonal:
                s_t = jnp.where(_iota((t, t), 0) <= _iota((t, t), 1), s_t, NEG)
            m_prev = m_scr[i]
            m_new = jnp.maximum(m_prev, jnp.max(s_t, axis=0, keepdims=True))
            alpha = jnp.exp(m_prev - m_new)
            p_t = jnp.exp(s_t - m_new)
            l_scr[i] = alpha * l_scr[i] + jnp.sum(p_t, axis=0, keepdims=True)
            acc_scr[i] = alpha * acc_scr[i] + _dot_tn(v_ref[...], p_t.astype(BF16))
            m_scr[i] = m_new

        @pl.when(j < pair * p)
        def _():
            for slot in range(pair):
                update(slot, False)

        for diag_slot in range(pair):
            @pl.when(j == pair * p + diag_slot)
            def _(diag_slot=diag_slot):
                update(diag_slot, True)
                for slot in range(diag_slot + 1, pair):
                    update(slot, False)

        @pl.when((p == nq // pair - 1) & (j == nq - 1))
        def _():
            for blk in range(nq):
                cols = slice(blk * t, (blk + 1) * t)
                l_sum = l_scr[blk]
                ot_ref[:, cols] = acc_scr[blk] / l_sum
                lse_ref[:, cols] = m_scr[blk] + jnp.log(l_sum)

    qry = pl.BlockSpec((pair * t, HEAD_DIM), lambda h, j, p: (jnp.maximum(p, j // pair), h))
    return pl.pallas_call(
        body, name="fox_fwd", grid=(heads, nq, nq // pair),
        in_specs=[pl.BlockSpec((t, HEAD_DIM), lambda h, j, p: (j, heads + h)), qry,
                  pl.BlockSpec((t, HEAD_DIM), lambda h, j, p: (j, 2 * heads + h)),
                  pl.BlockSpec((t, LANE), lambda h, j, p: (j, 0)),
                  pl.BlockSpec((None, 1, pair * t),
                               lambda h, j, p: (h, 0, jnp.maximum(p, j // pair)))],
        out_specs=[pl.BlockSpec((HEAD_DIM, s_len), lambda h, j, i: (h, 0)),
                   pl.BlockSpec((None, 1, s_len), lambda h, j, i: (h, 0, 0))],
        out_shape=[jax.ShapeDtypeStruct((d, s_len), F32),
                   jax.ShapeDtypeStruct((heads, 1, s_len), F32)],
        scratch_shapes=[pltpu.VMEM((nq, HEAD_DIM, t), F32), pltpu.VMEM((nq, 1, t), F32),
                        pltpu.VMEM((nq, 1, t), F32), pltpu.VMEM((t, 1), F32)],
        compiler_params=_params(3))(proj, proj, proj, f_nat, f_t)


def _gate_fwd(o_t, proj, z_blk0, heads):
    d, s_len = o_t.shape
    t = _tile(s_len, 512)

    def body(ot_ref, z_ref, o_ref, u_ref):
        o_val = jnp.transpose(ot_ref[...])
        o_ref[...] = o_val.astype(BF16)
        z = z_ref[...].astype(F32)
        u_ref[...] = (o_val * (z * _sigmoid(z))).astype(BF16)

    out_blk = pl.BlockSpec((t, HEAD_DIM), lambda i, h: (i, h))
    return pl.pallas_call(
        body, name="gate_fwd", grid=(s_len // t, heads),
        in_specs=[pl.BlockSpec((HEAD_DIM, t), lambda i, h: (h, i)),
                  pl.BlockSpec((t, HEAD_DIM), lambda i, h: (i, z_blk0 + h))],
        out_specs=[out_blk, out_blk],
        out_shape=[jax.ShapeDtypeStruct((s_len, d), BF16)] * 2,
        compiler_params=_params(2))(o_t, proj)


def _fox_bwd(proj, d_o, f_nat, f_t, lse_t, delta_t, heads):
    s_len = proj.shape[0]
    d = heads * HEAD_DIM
    t = _tile(s_len, 512)
    nq = s_len // t
    pair = 2 if nq % 2 == 0 else 1
    scale = HEAD_DIM ** -0.5

    def body(k_ref, v_ref, q_ref, do_ref, fn_ref, ft_ref, lse_ref, delta_ref,
             dk_ref, dv_ref, dqt_ref, dfq_ref, dfk_ref,
             dk_acc, dv_acc, dq_acc, dfq_acc, dfk_acc, fk_scr):
        h, j, p = pl.program_id(0), pl.program_id(1), pl.program_id(2)
        head_start = (j == 0) & (p == 0)
        last = nq // pair - 1

        @pl.when(head_start)
        def _():
            dq_acc[...] = jnp.zeros_like(dq_acc)
            dfq_acc[...] = jnp.zeros_like(dfq_acc)

        @pl.when(head_start & (h == 0))
        def _():
            dfk_ref[...] = jnp.zeros_like(dfk_ref)

        @pl.when(p == j // pair)
        def _():
            dk_acc[...] = jnp.zeros_like(dk_acc)
            dv_acc[...] = jnp.zeros_like(dv_acc)
            dfk_acc[...] = jnp.zeros_like(dfk_acc)
            fk_scr[...] = _pick_lane(fn_ref[...], h)

        def update(slot, diagonal):
            i = pair * p + slot
            rows = slice(slot * t, (slot + 1) * t)
            q = q_ref[rows, :]
            d_out = do_ref[rows, :]
            s_t = (_dot_nt(k_ref[...], q) * scale + (ft_ref[:, rows] - fk_scr[...])
                   - lse_ref[:, rows])
            if diagonal:
                s_t = jnp.where(_iota((t, t), 0) <= _iota((t, t), 1), s_t, NEG)
            p_t = jnp.exp(s_t)
            dp_t = _dot_nt(v_ref[...], d_out)
            ds_t = p_t * (dp_t - delta_ref[:, rows])
            ds_b = ds_t.astype(BF16)
            dv_acc[...] += _dot(p_t.astype(BF16), d_out)
            dk_acc[...] += _dot(ds_b, q)
            dq_acc[i] += _dot_tn(k_ref[...], ds_b)
            dfq_acc[i] += jnp.sum(ds_t, axis=0, keepdims=True)
            dfk_acc[...] += jnp.sum(ds_t, axis=1, keepdims=True)

        @pl.when(j < pair * p)
        def _():
            for slot in range(pair):
                update(slot, False)

        for diag_slot in range(pair):
            @pl.when(j == pair * p + diag_slot)
            def _(diag_slot=diag_slot):
                update(diag_slot, True)
                for slot in range(diag_slot + 1, pair):
                    update(slot, False)

        @pl.when(p == last)
        def _():
            dk_ref[...] = (dk_acc[...] * scale).astype(BF16)
            dv_ref[...] = dv_acc[...].astype(BF16)
            rows = pl.ds(pl.multiple_of(j * t, t), t)
            dfk_ref[rows, :] += jnp.where(_iota((t, LANE), 1) == h, -dfk_acc[...], 0.0)

        @pl.when((p == last) & (j == nq - 1))
        def _():
            for blk in range(nq):
                cols = slice(blk * t, (blk + 1) * t)
                dqt_ref[:, cols] = (dq_acc[blk] * scale).astype(BF16)
                dfq_ref[:, cols] = dfq_acc[blk]

    key_col = lambda base: pl.BlockSpec((t, HEAD_DIM), lambda h, j, p: (j, base + h))
    qry = pl.BlockSpec((pair * t, HEAD_DIM), lambda h, j, p: (jnp.maximum(p, j // pair), h))
    qry_row = pl.BlockSpec((None, 1, pair * t),
                           lambda h, j, p: (h, 0, jnp.maximum(p, j // pair)))
    kv_out = pl.BlockSpec((t, HEAD_DIM), lambda h, j, i: (j, h))
    return pl.pallas_call(
        body, name="fox_bwd", grid=(heads, nq, nq // pair),
        in_specs=[key_col(heads), key_col(2 * heads),
                  qry, qry, pl.BlockSpec((t, LANE), lambda h, j, i: (j, 0)),
                  qry_row, qry_row, qry_row],
        out_specs=[kv_out, kv_out,
                   pl.BlockSpec((HEAD_DIM, s_len), lambda h, j, i: (h, 0)),
                   pl.BlockSpec((None, 1, s_len), lambda h, j, i: (h, 0, 0)),
                   pl.BlockSpec((s_len, LANE), lambda h, j, i: (0, 0))],
        out_shape=[jax.ShapeDtypeStruct((s_len, d), BF16), jax.ShapeDtypeStruct((s_len, d), BF16),
                   jax.ShapeDtypeStruct((d, s_len), BF16),
                   jax.ShapeDtypeStruct((heads, 1, s_len), F32),
                   jax.ShapeDtypeStruct((s_len, LANE), F32)],
        scratch_shapes=[pltpu.VMEM((t, HEAD_DIM), F32), pltpu.VMEM((t, HEAD_DIM), F32),
                        pltpu.VMEM((nq, HEAD_DIM, t), F32), pltpu.VMEM((nq, 1, t), F32),
                        pltpu.VMEM((t, 1), F32), pltpu.VMEM((t, 1), F32)],
        compiler_params=_params(3))(proj, proj, proj, d_o, f_nat, f_t, lse_t, delta_t)


def _swa_specs(heads, kv_heads):
    width = heads // kv_heads * HEAD_DIM
    wide = lambda base: pl.BlockSpec((SWA_BLOCK, width), lambda n, g: (n, base + g))
    blk = lambda fn: pl.BlockSpec((SWA_BLOCK, HEAD_DIM), fn)
    prev = lambda base: blk(lambda n, g: (jnp.maximum(n - 1, 0), base + g))
    cur = lambda base: blk(lambda n, g: (n, base + g))
    return wide, prev, cur


def _stack_heads(ref, group):
    return jnp.concatenate([ref[:, hh * HEAD_DIM:(hh + 1) * HEAD_DIM] for hh in range(group)], axis=0)


def _head_rows(stacked, hh):
    return stacked[hh * SWA_BLOCK:(hh + 1) * SWA_BLOCK]


def _swa_scores(q, kp, kc, n, scale):
    shape = (q.shape[0], SWA_BLOCK)
    r, c = _iota(shape, 0) & (SWA_BLOCK - 1), _iota(shape, 1)
    sp = jnp.where((c > r) & (n > 0), _dot_nt(q, kp) * scale, NEG)
    sc = jnp.where(c <= r, _dot_nt(q, kc) * scale, NEG)
    return sp, sc


def _per_head_column(values_row, first_head, group):
    head_of_row = _iota((group * SWA_BLOCK, 1), 0) // SWA_BLOCK
    col = jnp.zeros((group * SWA_BLOCK, 1), F32)
    for hh in range(group):
        col = jnp.where(head_of_row == hh, _pick_lane(values_row, first_head + hh), col)
    return col


def _swa_fwd(proj, kv, sinks_row, heads, kv_heads):
    s_len = proj.shape[0]
    d = heads * HEAD_DIM
    scale = HEAD_DIM ** -0.5
    group = heads // kv_heads
    wide, prev, cur = _swa_specs(heads, kv_heads)

    def body(q_ref, z_ref, kp_ref, kc_ref, vp_ref, vc_ref, sink_ref, o_ref, u_ref, lse_ref):
        n, g = pl.program_id(0), pl.program_id(1)
        sp, sc = _swa_scores(_stack_heads(q_ref, group), kp_ref[...], kc_ref[...], n, scale)
        sink = _per_head_column(sink_ref[...], g * group, group)
        m = jnp.maximum(jnp.maximum(jnp.max(sp, axis=1, keepdims=True),
                                    jnp.max(sc, axis=1, keepdims=True)), sink)
        pp, pc = jnp.exp(sp - m), jnp.exp(sc - m)
        den = (jnp.sum(pp, axis=1, keepdims=True) + jnp.sum(pc, axis=1, keepdims=True)
               + jnp.exp(sink - m))
        o_all = (_dot(pp.astype(BF16), vp_ref[...]) + _dot(pc.astype(BF16), vc_ref[...])) / den
        lse = m + jnp.log(den)
        lane = _iota((SWA_BLOCK, LANE), 1)
        lse_all = jnp.zeros((SWA_BLOCK, LANE), F32)
        for hh in range(group):
            cols = slice(hh * HEAD_DIM, (hh + 1) * HEAD_DIM)
            o_val = _head_rows(o_all, hh)
            o_ref[:, cols] = o_val.astype(BF16)
            z = z_ref[:, cols].astype(F32)
            u_ref[:, cols] = (o_val * (z * _sigmoid(z))).astype(BF16)
            lse_all = lse_all + jnp.where(lane == g * group + hh, _head_rows(lse, hh), 0.0)

        @pl.when(g == 0)
        def _():
            lse_ref[...] = lse_all

        @pl.when(g > 0)
        def _():
            lse_ref[...] += lse_all

    nat = pl.BlockSpec((SWA_BLOCK, LANE), lambda n, g: (n, 0))
    return pl.pallas_call(
        body, name="swa_fwd", grid=(s_len // SWA_BLOCK, kv_heads),
        in_specs=[wide(0), wide(kv_heads), prev(0), cur(0), prev(kv_heads), cur(kv_heads),
                  pl.BlockSpec((1, LANE), lambda n, g: (0, 0))],
        out_specs=[wide(0), wide(0), nat],
        out_shape=[jax.ShapeDtypeStruct((s_len, d), BF16), jax.ShapeDtypeStruct((s_len, d), BF16),
                   jax.ShapeDtypeStruct((s_len, LANE), F32)],
        compiler_params=_params(2))(proj, proj, kv, kv, kv, kv, sinks_row)


def _swa_bwd_q(proj, kv, d_o, lse, delta, sinks_row, cos, sin, heads, kv_heads):
    s_len = proj.shape[0]
    d = heads * HEAD_DIM
    scale = HEAD_DIM ** -0.5
    group = heads // kv_heads
    wide, prev, cur = _swa_specs(heads, kv_heads)

    def body(q_ref, kp_ref, kc_ref, vp_ref, vc_ref, do_ref, lse_ref, delta_ref, sink_ref,
             cos_ref, sin_ref, dq_ref, dsink_ref):
        n, g = pl.program_id(0), pl.program_id(1)

        @pl.when((n == 0) & (g == 0))
        def _():
            dsink_ref[...] = jnp.zeros_like(dsink_ref)

        kp, kc = kp_ref[...], kc_ref[...]
        first = g * group
        sp, sc = _swa_scores(_stack_heads(q_ref, group), kp, kc, n, scale)
        lse_blk, delta_blk = lse_ref[...], delta_ref[...]
        lse_col = jnp.concatenate([_pick_lane(lse_blk, first + hh) for hh in range(group)], axis=0)
        delta_col = jnp.concatenate([_pick_lane(delta_blk, first + hh) for hh in range(group)],
                                    axis=0)
        pp, pc = jnp.exp(sp - lse_col), jnp.exp(sc - lse_col)
        p_sink = jnp.exp(_per_head_column(sink_ref[...], first, group) - lse_col)
        d_out = _stack_heads(do_ref, group)
        dsp = pp * (_dot_nt(d_out, vp_ref[...]) - delta_col)
        dsc = pc * (_dot_nt(d_out, vc_ref[...]) - delta_col)
        dq = (_dot(dsp.astype(BF16), kp) + _dot(dsc.astype(BF16), kc)) * scale
        cos_v, sin_v = cos_ref[...], sin_ref[...]
        sink_part = -p_sink * delta_col
        lane = _iota((SUBLANE, LANE), 1)
        dsink_all = jnp.zeros((SUBLANE, LANE), F32)
        for hh in range(group):
            cols = slice(hh * HEAD_DIM, (hh + 1) * HEAD_DIM)
            dq_h = _head_rows(dq, hh)
            dq_ref[:, cols] = (dq_h * cos_v
                               - pltpu.roll(dq_h, HEAD_DIM // 2, 1) * sin_v).astype(BF16)
            d_sink = jnp.sum(_head_rows(sink_part, hh), axis=0, keepdims=True)
            dsink_all = dsink_all + jnp.where(lane == first + hh, d_sink, 0.0)
        dsink_ref[...] += dsink_all

    own = wide(0)
    nat = pl.BlockSpec((SWA_BLOCK, LANE), lambda n, g: (n, 0))
    return pl.pallas_call(
        body, name="swa_bwd_q", grid=(s_len // SWA_BLOCK, kv_heads),
        in_specs=[own, prev(0), cur(0), prev(kv_heads), cur(kv_heads), own, nat, nat,
                  pl.BlockSpec((1, LANE), lambda n, g: (0, 0)), nat, nat],
        out_specs=[own, pl.BlockSpec((SUBLANE, LANE), lambda n, g: (0, 0))],
        out_shape=[jax.ShapeDtypeStruct((s_len, d), BF16),
                   jax.ShapeDtypeStruct((SUBLANE, LANE), F32)],
        compiler_params=_params(2))(proj, kv, kv, kv, kv, d_o, lse, delta, sinks_row, cos, sin)


def _swa_bwd_kv(proj, kv, d_o, lse_t, delta_t, cos, sin, heads, kv_heads):
    s_len = proj.shape[0]
    nb = s_len // SWA_BLOCK
    group = heads // kv_heads
    scale = HEAD_DIM ** -0.5

    def body(k_ref, v_ref, qm_ref, qn_ref, dom_ref, don_ref, lsem_ref, lsen_ref,
             deltam_ref, deltan_ref, cos_ref, sin_ref, dk_ref, dv_ref):
        m = pl.program_id(1)
        k, v = k_ref[...], v_ref[...]
        shape = (SWA_BLOCK, group * SWA_BLOCK)
        key, qry = _iota(shape, 0), _iota(shape, 1) & (SWA_BLOCK - 1)
        own_valid = key <= qry
        next_valid = (key > qry) & (m + 1 < nb)
        dk = jnp.zeros((SWA_BLOCK, HEAD_DIM), F32)
        dv = jnp.zeros((SWA_BLOCK, HEAD_DIM), F32)
        for q_ref, do_ref, lse_ref, delta_ref, valid in (
                (qm_ref, dom_ref, lsem_ref, deltam_ref, own_valid),
                (qn_ref, don_ref, lsen_ref, deltan_ref, next_valid)):
            q, d_out = _stack_heads(q_ref, group), _stack_heads(do_ref, group)
            lse_row = jnp.concatenate([lse_ref[hh] for hh in range(group)], axis=1)
            delta_row = jnp.concatenate([delta_ref[hh] for hh in range(group)], axis=1)
            s_t = _dot_nt(k, q) * scale
            p_t = jnp.exp(jnp.where(valid, s_t - lse_row, NEG))
            ds_t = p_t * (_dot_nt(v, d_out) - delta_row)
            dv = dv + _dot(p_t.astype(BF16), d_out)
            dk = dk + _dot(ds_t.astype(BF16), q)
        dk = dk * scale
        dk_ref[...] = (dk * cos_ref[...]
                       - pltpu.roll(dk, HEAD_DIM // 2, 1) * sin_ref[...]).astype(BF16)
        dv_ref[...] = dv.astype(BF16)

    blk = lambda fn: pl.BlockSpec((SWA_BLOCK, HEAD_DIM), fn)
    nxt = lambda m: jnp.minimum(m + 1, nb - 1)
    wide = lambda fn: pl.BlockSpec((SWA_BLOCK, group * HEAD_DIM), fn)
    rows = lambda fn: pl.BlockSpec((group, 1, SWA_BLOCK), fn)
    q_m, q_n = wide(lambda g, m: (m, g)), wide(lambda g, m: (nxt(m), g))
    r_m, r_n = rows(lambda g, m: (g, 0, m)), rows(lambda g, m: (g, 0, nxt(m)))
    nat = pl.BlockSpec((SWA_BLOCK, LANE), lambda g, m: (m, 0))
    out_blk = blk(lambda g, m: (m, g))
    width = kv_heads * HEAD_DIM
    return pl.pallas_call(
        body, name="swa_bwd_kv", grid=(kv_heads, nb),
        in_specs=[blk(lambda g, m: (m, g)), blk(lambda g, m: (m, kv_heads + g)),
                  q_m, q_n, q_m, q_n, r_m, r_n, r_m, r_n, nat, nat],
        out_specs=[out_blk, out_blk],
        out_shape=[jax.ShapeDtypeStruct((s_len, width), BF16)] * 2,
        compiler_params=_params(2))(kv, kv, proj, proj, d_o, d_o, lse_t, lse_t,
                                    delta_t, delta_t, cos, sin)


def _ada_fwd(c_rows, ada_w, bias_loc):
    n_layers, d, cols = ada_w.shape
    rows = c_rows.shape[0]
    tk = _tile(d, 512)
    n_k = d // tk

    def body(c_ref, w_ref, b_ref, mod_ref, sc_ref, acc_ref):
        k = pl.program_id(1)

        @pl.when(k == 0)
        def _():
            acc_ref[...] = jnp.zeros_like(acc_ref)

        cv = c_ref[...]
        sc = cv * _sigmoid(cv)
        sc_ref[...] = sc
        acc_ref[...] += _dot(sc.astype(BF16), w_ref[...].astype(BF16))

        @pl.when(k == n_k - 1)
        def _():
            mod_ref[...] = acc_ref[...] + b_ref[...]

    return pl.pallas_call(
        body, name="ada_fwd", grid=(n_layers, n_k),
        in_specs=[pl.BlockSpec((rows, tk), lambda l, k: (0, k)),
                  pl.BlockSpec((None, tk, cols), lambda l, k: (l, k, 0)),
                  pl.BlockSpec((None, 1, cols), lambda l, k: (l, 0, 0))],
        out_specs=[pl.BlockSpec((None, rows, cols), lambda l, k: (l, 0, 0)),
                   pl.BlockSpec((None, rows, tk), lambda l, k: (l, 0, k))],
        out_shape=[jax.ShapeDtypeStruct((n_layers, rows, cols), F32),
                   jax.ShapeDtypeStruct((n_layers, rows, d), F32)],
        scratch_shapes=[pltpu.VMEM((rows, cols), F32)],
        compiler_params=_params(2))(c_rows, ada_w, bias_loc)


def _ada_update(sc_t, dmod, w, m, v):
    n_layers, d, cols = w.shape
    tr = _tile(d, 256)
    big = pl.BlockSpec((None, tr, cols), lambda l, i: (l, i, 0))

    def body(sc_ref, dm_ref, w_ref, m_ref, v_ref, g_out, d_out, m_out, v_out):
        g = _dot(sc_ref[...], dm_ref[...])
        delta, m_new, v_new = _adamw(w_ref[...], g, m_ref[...], v_ref[...])
        g_out[...] = g
        d_out[...] = delta
        m_out[...] = m_new
        v_out[...] = v_new

    return pl.pallas_call(
        body, name="ada_update", grid=(n_layers, d // tr),
        in_specs=[pl.BlockSpec((tr, LANE), lambda l, i: (i, 0)),
                  pl.BlockSpec((None, LANE, cols), lambda l, i: (l, 0, 0)), big, big, big],
        out_specs=[big] * 4, out_shape=[jax.ShapeDtypeStruct(w.shape, F32)] * 4,
        compiler_params=_params(2))(sc_t, dmod, w, m, v)


def _row_tile(rows, bytes_per_row):
    tr = SUBLANE * 2
    while tr * 2 <= rows and rows % (tr * 2) == 0 and tr * 2 * bytes_per_row <= 24 * 2 ** 20:
        tr *= 2
    return _tile(rows, tr)


def _slab_sum(name, arrays):
    rows = arrays[0].shape[1]
    per_row = sum(2 * a.shape[2] * (a.shape[0] * a.dtype.itemsize + 4) for a in arrays)
    tr = _row_tile(rows, per_row)
    n = len(arrays)

    def body(*refs):
        for s_ref, out_ref in zip(refs[:n], refs[n:]):
            total = s_ref[0].astype(F32)
            for slot in range(1, s_ref.shape[0]):
                total = total + s_ref[slot].astype(F32)
            out_ref[...] = total

    return pl.pallas_call(
        body, name=name, grid=(rows // tr,),
        in_specs=[pl.BlockSpec((a.shape[0], tr, a.shape[2]), lambda i: (0, i, 0)) for a in arrays],
        out_specs=[pl.BlockSpec((tr, a.shape[2]), lambda i: (i, 0)) for a in arrays],
        out_shape=[jax.ShapeDtypeStruct(a.shape[1:], F32) for a in arrays],
        compiler_params=_params(1))(*arrays)


def _decay(name, w, m, v):
    rows, cols = w.shape
    tr = _row_tile(rows, 2 * cols * 4 * 6)
    blk = pl.BlockSpec((tr, cols), lambda i: (i, 0))

    def body(w_ref, m_ref, v_ref, w_out, m_out, v_out):
        w_out[...] = ADAM_WD * w_ref[...]
        m_out[...] = ADAM_B1 * m_ref[...]
        v_out[...] = ADAM_B2 * v_ref[...]

    return pl.pallas_call(body, name=name, grid=(rows // tr,), in_specs=[blk] * 3,
                          out_specs=[blk] * 3, out_shape=[jax.ShapeDtypeStruct(w.shape, F32)] * 3,
                          compiler_params=_params(1))(w, m, v)


def _shard_update(name, slabs, w, m, v, decayed=False):
    rows, cols = w.shape
    n_slabs = slabs.shape[0]
    tr = _row_tile(rows, 2 * cols * (slabs.dtype.itemsize * n_slabs + 4 * 7))
    blk = pl.BlockSpec((tr, cols), lambda i: (i, 0))
    step = _adamw_decayed if decayed else _adamw

    def body(s_ref, w_ref, m_ref, v_ref, g_out, d_out, m_out, v_out):
        g = s_ref[0].astype(F32)
        for slot in range(1, n_slabs):
            g = g + s_ref[slot].astype(F32)
        delta, m_new, v_new = step(w_ref[...], g, m_ref[...], v_ref[...])
        g_out[...] = g
        d_out[...] = delta
        m_out[...] = m_new
        v_out[...] = v_new

    return pl.pallas_call(
        body, name=name, grid=(rows // tr,),
        in_specs=[pl.BlockSpec((n_slabs, tr, cols), lambda i: (0, i, 0)), blk, blk, blk],
        out_specs=[blk] * 4, out_shape=[jax.ShapeDtypeStruct((rows, cols), F32)] * 4,
        compiler_params=_params(1))(slabs, w, m, v)


def _small_update(gathered, w, m, v):
    shape = jax.ShapeDtypeStruct(w.shape, F32)

    def body(g_ref, w_ref, m_ref, v_ref, g_out, d_out, m_out, v_out):
        g = g_ref[0]
        for dev in range(1, N_DEV):
            g = g + g_ref[dev]
        delta, m_new, v_new = _adamw(w_ref[...], g, m_ref[...], v_ref[...])
        g_out[...] = g
        d_out[...] = delta
        m_out[...] = m_new
        v_out[...] = v_new

    return pl.pallas_call(body, name="small_update", out_shape=[shape] * 4,
                          compiler_params=pltpu.CompilerParams(vmem_limit_bytes=VMEM_LIMIT),
                          )(gathered, w, m, v)


def _rope_tables(s_len):
    half = HEAD_DIM // 2
    inv = ROPE_THETA ** (-jnp.arange(half, dtype=F32) / half)
    ang = jnp.arange(s_len, dtype=F32)[:, None] * inv[None, :]
    cos, sin = jnp.cos(ang), jnp.sin(ang)
    return jnp.concatenate([cos, cos], axis=1), jnp.concatenate([-sin, sin], axis=1)


def _pad_lanes(a):
    return jnp.pad(a, ((0, 0), (0, LANE - a.shape[1])))


def _rows_of(nat, heads):
    return jnp.transpose(nat[:, :heads])[:, None, :]


def _pack(parts):
    tile = SUBLANE * LANE
    flat = []
    for p in parts:
        p = p.reshape(-1)
        flat.append(jnp.pad(p, (0, (-p.shape[0]) % tile)))
    return jnp.concatenate(flat).reshape(-1, LANE)


def _unpack(packed, shapes):
    tile = SUBLANE * LANE
    flat = packed.reshape(-1)
    out, pos = [], 0
    for shape in shapes:
        size = 1
        for dim in shape:
            size *= dim
        out.append(flat[pos:pos + size].reshape(shape))
        pos += size + (-size) % tile
    return out


def kernel(x, c, norm_g, ada_w, ada_b, a_w_in, a_b_f, a_w_out, kv_norm_g, kv_w, b_w_in, b_sinks, b_w_out, final_norm_g, loss_target, m_norm_g, m_ada_w, m_ada_b, m_a_w_in, m_a_b_f, m_a_w_out, m_kv_norm_g, m_kv_w, m_b_w_in, m_b_sinks, m_b_w_out, m_final_norm_g, v_norm_g, v_ada_w, v_ada_b, v_a_w_in, v_a_b_f, v_a_w_out, v_kv_norm_g, v_kv_w, v_b_w_in, v_b_sinks, v_b_w_out, v_final_norm_g):
    s_len, d = x.shape[1], x.shape[2]
    heads = d // HEAD_DIM
    kv_heads = kv_w.shape[1] // (2 * HEAD_DIM)
    kv_width = kv_heads * HEAD_DIM
    ada_cols = ada_w.shape[2]
    assert heads <= LANE and heads % N_DEV == 0 and a_w_in.shape[2] * N_DEV == 4 * d + heads
    me = _slot(_mesh_pos())
    x0 = x[0]
    target = loss_target[0]
    vec = lambda a: a.reshape(1, d)

    sup, extra = 4 * d // N_DEV, heads // N_DEV
    padded = jnp.pad(a_w_in[0].astype(BF16), ((0, 0), (heads, LANE)))
    big_loc = lax.dynamic_slice_in_dim(padded, heads - extra * me, sup, axis=1)
    small_loc = lax.dynamic_slice_in_dim(padded, heads + sup - extra * me, LANE, axis=1)

    (c_all,) = _all_gather("gather_c", [c])
    c_rows = jnp.pad(c_all.reshape(N_DEV, d), ((0, 2 * SUBLANE - N_DEV), (0, 0)))
    bias_loc = lax.dynamic_slice_in_dim(ada_b, me * ada_cols, ada_cols, axis=1)[:, None, :]
    mod_part, sc_rows = _ada_fwd(c_rows, ada_w, bias_loc)
    (mod_recv,) = _all_to_all("exchange_mod", [jnp.transpose(mod_part[:, :N_DEV], (1, 0, 2))])
    mod = jnp.transpose(mod_recv, (1, 0, 2)).reshape(2, 3 * d)

    g_big, g_small = _all_gather("gather_a_w_in", _after([big_loc, small_loc], mod_recv),
                                 sequencer_id=5)
    shift0, scale0, gate0 = vec(mod[0, :d]), vec(mod[0, d:2 * d]), vec(mod[0, 2 * d:])
    shift1, scale1, gate1 = vec(mod[1, :d]), vec(mod[1, d:2 * d]), vec(mod[1, 2 * d:])
    g0, g1, g_kvn, g_fin = vec(norm_g[0]), vec(norm_g[1]), vec(kv_norm_g), vec(final_norm_g)
    h0 = _norm_fwd("norm0", x0, g0, scale0, shift0)
    a_in_decayed = _decay("decay_a_w_in", a_w_in[0], m_a_w_in[0], v_a_w_in[0])
    later = [a_w_out[0].astype(BF16), kv_w.astype(BF16), b_w_in[0].astype(BF16),
             b_w_out[0].astype(BF16)]
    g_a_out, g_kv, w_b_in, g_b_out = _all_gather(
        "gather_rest", _after(later, (g_big, h0, *a_in_decayed)), sequencer_id=1)
    w_a_out = g_a_out.reshape(d, d)
    w_kv = g_kv.reshape(d, 2 * kv_width)
    w_b_out = g_b_out.reshape(d, d)
    lane_id = jnp.arange(LANE)[None, :]
    patch = jnp.stack([jnp.where(lane_id < extra * s, g_small[s - 1], g_big[s, :, :LANE])
                       for s in range(1, N_DEV)])
    w_a_main = g_big.at[1:, :, :LANE].set(patch)
    w_a_f = g_small[N_DEV - 1]
    b_in_cols = b_w_in.shape[2]
    a_tn = _tile(sup, 1024)
    w_a_main_spec = lambda tk: pl.BlockSpec(
        (None, tk, a_tn), lambda i, j, k: (j // (sup // a_tn), k, j % (sup // a_tn)))

    cos, sin = _rope_tables(s_len)
    bias_f = _pad_lanes(a_b_f)
    sinks_row = _pad_lanes(b_sinks)

    proj0 = _mm_plain("proj0", h0, w_a_main, n_cols=4 * d, tn=a_tn, b_spec=w_a_main_spec)
    f_raw = _mm_plain("proj0_f", h0, w_a_f, out_dtype=F32, tn=LANE)
    f_nat = _fgate_fwd(f_raw, bias_f)
    f_t = _rows_of(f_nat, heads)
    o0_t, lse0_t = _fox_fwd(proj0, f_nat, f_t, heads)
    o0, u0 = _gate_fwd(o0_t, proj0, 3 * heads, heads)
    y0, x1 = _mm_residual("out0", u0, w_a_out, x0, gate0)

    h1, hk = _norm_fwd("norm1", x1, g1, scale1, shift1, gb=g_kvn)
    kv = _mm_rope("kv_proj", hk, w_kv, cos, sin, n_cols=2 * kv_width, rope_cols=kv_width,
                  tn=kv_width)
    proj1 = _mm_rope("proj1", h1, w_b_in, cos, sin, n_cols=2 * d, rope_cols=d, tn=b_in_cols,
                     b_spec=lambda tk: pl.BlockSpec((None, tk, b_in_cols),
                                                    lambda i, j, k: (j, k, 0)))
    o1, u1, lse1 = _swa_fwd(proj1, kv, sinks_row, heads, kv_heads)
    y1, x2 = _mm_residual("out1", u1, w_b_out, x1, gate1)

    loss_part, dx2, dy1, sums_f = _loss_bwd(x2, target, y1, g_fin, gate1)

    do1, dz1, delta1 = _mm_gate_bwd("out1_bwd", dy1, w_b_out, proj1, d, o1)
    gw_b_out = _mm_plain("out1_wgrad", u1, dy1, ta=True)
    do1 = _after(do1, gw_b_out)
    dq1, dsinks = _swa_bwd_q(proj1, kv, do1, lse1, delta1, sinks_row, cos, sin, heads, kv_heads)
    dk1, dv1 = _swa_bwd_kv(proj1, kv, do1, _rows_of(lse1, heads), _rows_of(delta1, heads),
                           cos, sin, heads, kv_heads)
    dproj1 = jnp.concatenate([dq1, dz1], axis=1)
    tk_b = _tile(2 * d, b_in_cols)
    dh1 = _mm_plain("proj1_bwd", dproj1, w_b_in, nt=True, n_cols=d, out_dtype=F32, tk=tk_b,
                    b_spec=lambda tk: pl.BlockSpec((None, _tile(d, 1024), tk),
                                                   lambda i, j, k: (k * tk // b_in_cols, j, 0)))
    gw_b_in = _mm_plain("proj1_wgrad", h1, dproj1, ta=True, tn=b_in_cols,
                        out_3d=(N_DEV, b_in_cols))
    dkv = jnp.concatenate([dk1, dv1], axis=1)
    dhk = _mm_plain("kv_bwd", dkv, w_kv, nt=True, out_dtype=F32)
    gw_kv = _mm_plain("kv_wgrad", hk, dkv, ta=True)
    dx1, dy0, sums1 = _norm_bwd("norm1_bwd", x1, dx2, dh1, g1, scale1,
                                dhb=_after(dhk, (gw_b_in, gw_kv)), gb=g_kvn, y=y0, gate=gate0)

    do0, dz0, delta0 = _mm_gate_bwd("out0_bwd", dy0, w_a_out, proj0, 3 * d, o0)
    gw_a_out = _mm_plain("out0_wgrad", u0, dy0, ta=True)
    r_b_out, r_b_in, r_kv, r_a_out = _all_to_all(
        "scatter_grads_early",
        [gw_b_out.reshape(N_DEV, d // N_DEV, d), gw_b_in,
         gw_kv.reshape(N_DEV, d // N_DEV, 2 * kv_width),
         gw_a_out.reshape(N_DEV, d // N_DEV, d)], sequencer_id=2)
    dk0, dv0, dq0_t, dfq_t, dfk_nat = _fox_bwd(proj0, _after(do0, gw_a_out), f_nat, f_t,
                                               lse0_t, _rows_of(delta0, heads), heads)
    dfq_nat = _pad_lanes(jnp.transpose(dfq_t[:, 0, :]))
    df, sums_bf = _fgate_bwd(dfq_nat, dfk_nat, f_raw, bias_f)
    dproj0 = jnp.concatenate([jnp.transpose(dq0_t), dk0, dv0, dz0], axis=1)
    gw_a_big = _mm_plain("proj0_wgrad", h0, dproj0, ta=True, tn=a_tn, out_3d=(N_DEV, sup))
    gw_a_f = _mm_plain("proj0_f_wgrad", h0, df, ta=True, tn=LANE)
    gw_a_small = jnp.concatenate([gw_a_big[1:, :, :LANE], gw_a_f[None]], axis=0)
    sib_big, sib_small = _pair_exchange("reduce_a_w_in_pair", [gw_a_big, gw_a_small],
                                        sequencer_id=3)
    r_b_out, r_b_in, r_kv, r_a_out = _after([r_b_out, r_b_in, r_kv, r_a_out], gw_a_big)
    up_b_out = _shard_update("update_b_w_out", r_b_out, b_w_out[0], m_b_w_out[0], v_b_w_out[0])
    up_b_in = _shard_update("update_b_w_in", r_b_in, b_w_in[0], m_b_w_in[0], v_b_w_in[0])
    up_kv = _shard_update("update_kv_w", r_kv, kv_w, m_kv_w, v_kv_w)
    up_a_out = _shard_update("update_a_w_out", r_a_out, a_w_out[0], m_a_w_out[0], v_a_w_out[0])
    sib_big, sib_small = _after((sib_big, sib_small),
                                (up_b_out[0], up_b_in[0], up_kv[0], up_a_out[0]))
    chip_big, chip_small = _pair_add("reduce_a_w_in_add", [gw_a_big, gw_a_small],
                                     [sib_big, sib_small])
    r_a_big, r_a_small = _chip_exchange("reduce_a_w_in_chips", [chip_big, chip_small],
                                        sequencer_id=4)
    df = _after(df, (chip_big, chip_small))
    dh0_f = _mm_plain("proj0_f_bwd", df, w_a_f, nt=True, out_dtype=F32)
    dh0 = _mm_plain("proj0_bwd", dproj0, w_a_main, nt=True, n_cols=d, out_dtype=F32, init=dh0_f,
                    tk=sup, b_spec=lambda tk: pl.BlockSpec((None, _tile(d, 1024), tk),
                                                           lambda i, j, k: (k, j, 0)))
    grad_x, sums0 = _norm_bwd("norm0_bwd", x0, dx1, dh0, g0, scale0)

    dmod = jnp.stack([jnp.concatenate([sums0[0], sums0[1], sums1[4]]),
                      jnp.concatenate([sums1[0], sums1[1], sums_f[1]])])
    small_shapes = [(2, 3 * d), (2, d), (1, heads), (d,), (1, heads), (d,), (1,)]
    small_grads = [dmod, jnp.stack([sums0[2], sums1[2]]), sums_bf[0:1, :heads], sums1[3],
                   dsinks[0:1, :heads], sums_f[0], loss_part[0, 0:1]]
    (small_all,) = _all_gather("gather_small", [_pack(small_grads)])
    zero = jnp.zeros((1,), F32)
    small = _small_update(
        small_all,
        _pack([ada_b, norm_g, a_b_f, kv_norm_g, b_sinks, final_norm_g, zero]),
        _pack([m_ada_b, m_norm_g, m_a_b_f, m_kv_norm_g, m_b_sinks, m_final_norm_g, zero]),
        _pack([v_ada_b, v_norm_g, v_a_b_f, v_kv_norm_g, v_b_sinks, v_final_norm_g, zero]))
    s_grad, s_delta, s_m, s_v = [_unpack(p, small_shapes) for p in small]
    loss = s_grad[6][0]

    r_a_big, r_a_small = _after((r_a_big, r_a_small), small[0])
    ga_big, ga_small = _slab_sum("sum_a_w_in", [r_a_big, r_a_small])
    ga_shard = lax.dynamic_slice_in_dim(jnp.concatenate([ga_big, ga_small], axis=1),
                                        extra * me, sup + extra, axis=1)
    up_a_in = _shard_update("update_a_w_in", ga_shard[None], *a_in_decayed, decayed=True)

    dmod_all = small_all.reshape(N_DEV, -1)[:, :2 * 3 * d].reshape(N_DEV, 2, 3 * d)
    dmod_loc = lax.dynamic_slice_in_dim(dmod_all, me * ada_cols, ada_cols, axis=2)
    dmod_loc = jnp.pad(jnp.transpose(dmod_loc, (1, 0, 2)), ((0, 0), (0, LANE - N_DEV), (0, 0)))
    sc_t = jnp.pad(jnp.transpose(sc_rows[0, :N_DEV]), ((0, 0), (0, LANE - N_DEV)))
    up_ada = _ada_update(sc_t.astype(BF16), _after(dmod_loc.astype(BF16), up_a_in[0]),
                         ada_w, m_ada_w, v_ada_w)

    lead = lambda a: a[None]
    per_kind = []
    for kind in range(4):
        sm = (s_grad, s_delta, s_m, s_v)[kind]
        per_kind.append([
            sm[1], up_ada[kind], sm[0], lead(up_a_in[kind]), sm[2], lead(up_a_out[kind]),
            sm[3], up_kv[kind], lead(up_b_in[kind]), sm[4], lead(up_b_out[kind]), sm[5]])
    return (loss, grad_x[None], *per_kind[0], *per_kind[1], *per_kind[2], *per_kind[3])
```

```python
import jax
import jax.numpy as jnp
from jax import lax
from jax.experimental import pallas as pl
from jax.experimental.pallas import tpu as pltpu
from jax.experimental.pallas import tpu_sc as plsc

F32 = jnp.float32
BF16 = jnp.bfloat16
LANE = 128
SUBLANE = 8
HEAD_DIM = 128
SWA_BLOCK = 128
FOX_QUERY_BLOCKS = 4
N_DEV = 8
N_PEER = N_DEV - 1
RMS_EPS = 1e-6
ROPE_THETA = 10000.0
NEG = -1e30
VMEM_LIMIT = 56 * 2 ** 20
MM_RESERVE = 10 * 2 ** 20
MESH = pl.DeviceIdType.MESH
HIGHEST = lax.Precision.HIGHEST

ADAM_LR = 0.001
ADAM_B1 = 0.9
ADAM_B2 = 0.999
ADAM_EPS = 1e-08
ADAM_WD = 0.01
ADAM_STEP = 10


def _tile(dim, pref):
    return pref if dim % pref == 0 else dim


def _params(n_axes):
    return pltpu.CompilerParams(dimension_semantics=("arbitrary",) * n_axes,
                                vmem_limit_bytes=VMEM_LIMIT)


def _dot(a, b):
    return jnp.dot(a, b, preferred_element_type=F32)


def _dot_nt(a, b):
    return lax.dot_general(a, b, (((1,), (1,)), ((), ())), preferred_element_type=F32)


def _dot_tn(a, b):
    return lax.dot_general(a, b, (((0,), (0,)), ((), ())), preferred_element_type=F32)


def _sigmoid(z):
    return 1.0 / (1.0 + jnp.exp(-z))


def _iota(shape, dim):
    return lax.broadcasted_iota(jnp.int32, shape, dim)


def _pick_lane(block, lane_index):
    lane = _iota(block.shape, 1)
    return jnp.sum(jnp.where(lane == lane_index, block, 0.0), axis=1, keepdims=True)


def _adamw_decayed(w_decay, g, m_decayed, v_decayed):
    m = m_decayed + (1.0 - ADAM_B1) * g
    v = v_decayed + (1.0 - ADAM_B2) * (g * g)
    m_hat = m / (1.0 - ADAM_B1 ** ADAM_STEP)
    v_hat = v / (1.0 - ADAM_B2 ** ADAM_STEP)
    delta = -ADAM_LR * (m_hat / (jnp.sqrt(v_hat) + ADAM_EPS) + w_decay)
    return delta, m, v


def _adamw(w, g, m, v):
    return _adamw_decayed(ADAM_WD * w, g, ADAM_B1 * m, ADAM_B2 * v)


def _mesh_pos():
    return lax.axis_index("x"), lax.axis_index("y"), lax.axis_index("c")


def _slot(pos):
    return 4 * pos[0] + 2 * pos[1] + pos[2]


def _handshake(peers):
    barrier = pltpu.get_barrier_semaphore()
    for peer in peers:
        pl.semaphore_signal(barrier, inc=1, device_id=peer, device_id_type=MESH)
    pl.semaphore_wait(barrier, len(peers))


def _launch(name, body, arrays, out_shape, sequencer_id):
    n = len(arrays)
    scratch = [pltpu.SemaphoreType.DMA((N_PEER * n,)), pltpu.SemaphoreType.DMA((N_PEER * n,)),
               pltpu.SemaphoreType.DMA((n,))]
    if sequencer_id is None:
        any_spec = pl.BlockSpec(memory_space=pl.ANY)
        return pl.pallas_call(body, name=name, out_shape=out_shape, in_specs=[any_spec] * n,
                              out_specs=[any_spec] * n, scratch_shapes=scratch)(*arrays)
    return pl.kernel(body, name=name, out_type=out_shape,
                     mesh=plsc.ScalarSubcoreMesh(axis_name="sequencer", num_cores=1),
                     scratch_types=scratch,
                     compiler_params=pltpu.CompilerParams(collective_id=sequencer_id))(*arrays)


def _all_gather(name, arrays, sequencer_id=None):
    n = len(arrays)

    def body(*refs):
        ins, outs = refs[:n], refs[n:2 * n]
        send_sems, recv_sems, local_sems = refs[2 * n:]
        x, y, c = _mesh_pos()
        me, sibling = (x, y, c), (x, y, 1 - c)
        chips = [(1 - x, y), (x, 1 - y), (1 - x, 1 - y)]
        if sequencer_id is not None:
            _handshake([sibling] + [(*chip, c) for chip in chips])

        def copy(a, k, block, to, src=None):
            dst = outs[a].at[_slot(block)]
            return pltpu.make_async_remote_copy(
                src_ref=dst if src is None else src, dst_ref=dst,
                send_sem=send_sems.at[N_PEER * a + k], recv_sem=recv_sems.at[N_PEER * a + k],
                device_id=to, device_id_type=MESH)

        local, first, passed = [], [], []
        for a in range(n):
            cp = pltpu.make_async_copy(ins[a], outs[a].at[_slot(me)], local_sems.at[a])
            cp.start()
            local.append(cp)
            sends = [copy(a, 0, me, sibling, src=ins[a])]
            sends += [copy(a, 1 + j, me, (*chip, c), src=ins[a]) for j, chip in enumerate(chips)]
            for cp in sends:
                cp.start()
            first += sends
        for a in range(n):
            for j, chip in enumerate(chips):
                copy(a, 1 + j, (*chip, c), me).wait_recv()
                cp = copy(a, 4 + j, (*chip, c), sibling)
                cp.start()
                passed.append(cp)
        for a in range(n):
            copy(a, 0, sibling, me).wait_recv()
            for j, chip in enumerate(chips):
                copy(a, 4 + j, (*chip, 1 - c), me).wait_recv()
        for cp in first + passed:
            cp.wait_send()
        for cp in local:
            cp.wait()

    out_shape = [jax.ShapeDtypeStruct((N_DEV,) + a.shape, a.dtype) for a in arrays]
    return _launch(name, body, arrays, out_shape, sequencer_id)


def _all_to_all(name, arrays, sequencer_id=None):
    n = len(arrays)

    def body(*refs):
        ins, outs = refs[:n], refs[n:2 * n]
        send_sems, recv_sems, local_sems = refs[2 * n:]
        x, y, c = _mesh_pos()
        me = _slot((x, y, c))
        if sequencer_id is not None:
            _handshake([(1 - x if k & 4 else x, 1 - y if k & 2 else y, 1 - c if k & 1 else c)
                        for k in range(1, N_DEV)])
        local, sends, recvs = [], [], []
        for a in range(n):
            cp = pltpu.make_async_copy(ins[a].at[me], outs[a].at[me], local_sems.at[a])
            cp.start()
            local.append(cp)
        for k in range(1, N_DEV):
            peer = (1 - x if k & 4 else x, 1 - y if k & 2 else y, 1 - c if k & 1 else c)
            ps = _slot(peer)
            for a in range(n):
                sem = N_PEER * a + k - 1
                cp = pltpu.make_async_remote_copy(
                    src_ref=ins[a].at[ps], dst_ref=outs[a].at[me],
                    send_sem=send_sems.at[sem], recv_sem=recv_sems.at[sem],
                    device_id=peer, device_id_type=MESH)
                cp.start()
                sends.append(cp)
                recvs.append(pltpu.make_async_remote_copy(
                    src_ref=ins[a].at[ps], dst_ref=outs[a].at[ps],
                    send_sem=send_sems.at[sem], recv_sem=recv_sems.at[sem],
                    device_id=peer, device_id_type=MESH))
        for cp in recvs:
            cp.wait_recv()
        for cp in sends:
            cp.wait_send()
        for cp in local:
            cp.wait()

    out_shape = [jax.ShapeDtypeStruct(a.shape, a.dtype) for a in arrays]
    return _launch(name, body, arrays, out_shape, sequencer_id)


def _pair_exchange(name, arrays, sequencer_id=None):
    n = len(arrays)
    chips = N_DEV // 2

    def body(*refs):
        ins, outs = refs[:n], refs[n:2 * n]
        send_sems, recv_sems = refs[2 * n], refs[2 * n + 1]
        x, y, c = _mesh_pos()
        sibling = (x, y, 1 - c)
        if sequencer_id is not None:
            _handshake([sibling])
        copies = [pltpu.make_async_remote_copy(
            src_ref=ins[a].at[2 * q + 1 - c], dst_ref=outs[a].at[q],
            send_sem=send_sems.at[chips * a + q], recv_sem=recv_sems.at[chips * a + q],
            device_id=sibling, device_id_type=MESH) for a in range(n) for q in range(chips)]
        for cp in copies:
            cp.start()
        for cp in copies:
            cp.wait()

    out_shape = [jax.ShapeDtypeStruct((chips,) + a.shape[1:], a.dtype) for a in arrays]
    return _launch(name, body, arrays, out_shape, sequencer_id)


def _chip_exchange(name, arrays, sequencer_id=None):
    n = len(arrays)
    chips = N_DEV // 2

    def body(*refs):
        ins, outs = refs[:n], refs[n:2 * n]
        send_sems, recv_sems, local_sems = refs[2 * n:]
        x, y, c = _mesh_pos()
        mine = 2 * x + y
        others = [(1 - x, y), (x, 1 - y), (1 - x, 1 - y)]
        if sequencer_id is not None:
            _handshake([(*chip, c) for chip in others])
        local = [pltpu.make_async_copy(ins[a].at[mine], outs[a].at[mine], local_sems.at[a])
                 for a in range(n)]
        for cp in local:
            cp.start()
        sends, recvs = [], []
        for j, chip in enumerate(others):
            theirs = 2 * chip[0] + chip[1]
            for a in range(n):
                both = dict(send_sem=send_sems.at[3 * a + j], recv_sem=recv_sems.at[3 * a + j],
                            device_id=(*chip, c), device_id_type=MESH)
                sends.append(pltpu.make_async_remote_copy(
                    src_ref=ins[a].at[theirs], dst_ref=outs[a].at[mine], **both))
                recvs.append(pltpu.make_async_remote_copy(
                    src_ref=ins[a].at[theirs], dst_ref=outs[a].at[theirs], **both))
        for cp in sends:
            cp.start()
        for cp in recvs:
            cp.wait_recv()
        for cp in sends:
            cp.wait_send()
        for cp in local:
            cp.wait()

    out_shape = [jax.ShapeDtypeStruct(a.shape, a.dtype) for a in arrays]
    return _launch(name, body, arrays, out_shape, sequencer_id)


def _pair_add(name, mine, theirs):
    chips, rows = theirs[0].shape[0], theirs[0].shape[1]
    per_row = sum(2 * 3 * a.shape[2] * a.dtype.itemsize for a in theirs)
    tr = _row_tile(rows, per_row)
    n = len(theirs)
    core = lax.axis_index("c").astype(jnp.int32).reshape(1)

    def body(core_ref, *refs):
        for a in range(n):
            refs[2 * n + a][...] = (refs[a][...].astype(F32)
                                    + refs[n + a][...].astype(F32)).astype(refs[2 * n + a].dtype)

    blk = lambda a, fn: pl.BlockSpec((None, tr, a.shape[2]), fn)
    grid_spec = pltpu.PrefetchScalarGridSpec(
        num_scalar_prefetch=1, grid=(chips, rows // tr),
        in_specs=[blk(a, lambda q, i, core_ref: (2 * q + core_ref[0], i, 0)) for a in mine]
        + [blk(a, lambda q, i, core_ref: (q, i, 0)) for a in theirs],
        out_specs=[blk(a, lambda q, i, core_ref: (q, i, 0)) for a in theirs])
    return pl.pallas_call(
        body, name=name, grid_spec=grid_spec,
        out_shape=[jax.ShapeDtypeStruct(a.shape, a.dtype) for a in theirs],
        compiler_params=_params(2))(core, *mine, *theirs)


def _after(value, token):
    return lax.optimization_barrier((value, token))[0]


def _k_tile(k_dim, tm, tn, fixed_bytes):
    budget = VMEM_LIMIT - MM_RESERVE - fixed_bytes
    tk = k_dim
    while tk % 2 == 0 and tk > 512 and (
            4 * (tm + tn) * tk + (4 * tm * tn if tk < k_dim else 0) > budget):
        tk //= 2
    return tk


def _matmul(name, a, b, *, nt, tm, tn, n_cols, out_shape, out_specs, epilogue,
            fixed_bytes, ta=False, tk=None, b_spec=None, extra=(), extra_specs=()):
    assert not (ta and nt)
    k_dim, m_rows = a.shape if ta else a.shape[::-1]
    tm, tn = _tile(m_rows, tm), _tile(n_cols, tn)
    tk = _k_tile(k_dim, tm, tn, fixed_bytes) if tk is None else _tile(k_dim, tk)
    grid = (m_rows // tm, n_cols // tn, k_dim // tk)
    n_k = grid[2]
    if ta:
        a_spec = pl.BlockSpec((tk, tm), lambda i, j, k: (k, i))
    else:
        a_spec = pl.BlockSpec((tm, tk), lambda i, j, k: (i, k))
    if b_spec is not None:
        b_blk = b_spec(tk)
    elif nt:
        b_blk = pl.BlockSpec((tn, tk), lambda i, j, k: (j, k))
    else:
        b_blk = pl.BlockSpec((tk, tn), lambda i, j, k: (k, j))
    n_extra, n_out = len(extra), len(out_shape)
    product = _dot_tn if ta else _dot_nt if nt else _dot

    def body(a_ref, b_ref, *rest):
        extra_refs = rest[:n_extra]
        out_refs = rest[n_extra:n_extra + n_out]
        if n_k == 1:
            epilogue(product(a_ref[...], b_ref[...]), extra_refs, out_refs)
            return
        acc_ref = rest[n_extra + n_out]
        k = pl.program_id(2)

        @pl.when(k == 0)
        def _():
            acc_ref[...] = jnp.zeros_like(acc_ref)

        acc_ref[...] += product(a_ref[...], b_ref[...])

        @pl.when(k == n_k - 1)
        def _():
            epilogue(acc_ref[...], extra_refs, out_refs)

    return pl.pallas_call(
        body, name=name, grid=grid,
        in_specs=[a_spec, b_blk, *extra_specs], out_specs=out_specs, out_shape=out_shape,
        scratch_shapes=[pltpu.VMEM((tm, tn), F32)] if n_k > 1 else [],
        compiler_params=_params(3),
    )(a, b, *extra)


def _mm_plain(name, a, b, *, nt=False, ta=False, n_cols=None, out_dtype=BF16, init=None,
              tm=1024, tn=1024, tk=None, b_spec=None, out_3d=None):
    m_rows = a.shape[1] if ta else a.shape[0]
    if n_cols is None:
        n_cols = b.shape[0] if nt else b.shape[1]
    tm, tn = _tile(m_rows, tm), _tile(n_cols, tn)
    fixed = 2 * tm * tn * (jnp.dtype(out_dtype).itemsize + (4 if init is not None else 0))
    if out_3d is None:
        shape = jax.ShapeDtypeStruct((m_rows, n_cols), out_dtype)
        spec = pl.BlockSpec((tm, tn), lambda i, j, k: (i, j))
    else:
        slabs, width = out_3d
        assert width % tn == 0 and slabs * width == n_cols
        per = width // tn
        shape = jax.ShapeDtypeStruct((slabs, m_rows, width), out_dtype)
        spec = pl.BlockSpec((None, tm, tn), lambda i, j, k: (j // per, i, j % per))
    extra, extra_specs = (), ()
    if init is not None:
        extra = (init,)
        extra_specs = (pl.BlockSpec((tm, tn), lambda i, j, k: (i, j)),)

    def epilogue(acc, extra_refs, out_refs):
        if init is not None:
            acc = acc + extra_refs[0][...]
        out_refs[0][...] = acc.astype(out_dtype)

    (out,) = _matmul(name, a, b, nt=nt, ta=ta, tm=tm, tn=tn, tk=tk, n_cols=n_cols,
                     out_shape=[shape], out_specs=[spec], epilogue=epilogue, fixed_bytes=fixed,
                     b_spec=b_spec, extra=extra, extra_specs=extra_specs)
    return out


def _mm_rope(name, a, b, cos, sin, *, n_cols, rope_cols, tn, b_spec=None):
    m_rows = a.shape[0]
    tm = _tile(m_rows, 1024)
    tn = _tile(n_cols, tn)
    assert rope_cols % tn == 0 and tn % HEAD_DIM == 0
    rope_blocks = rope_cols // tn
    table_spec = pl.BlockSpec((tm, LANE), lambda i, j, k: (i, 0))

    def epilogue(acc, extra_refs, out_refs):
        cos_ref, sin_ref = extra_refs
        j = pl.program_id(1)

        @pl.when(j < rope_blocks)
        def _():
            for head in range(tn // HEAD_DIM):
                cols = slice(head * HEAD_DIM, (head + 1) * HEAD_DIM)
                blk = acc[:, cols]
                rot = pltpu.roll(blk, HEAD_DIM // 2, 1)
                out_refs[0][:, cols] = (blk * cos_ref[...] + rot * sin_ref[...]).astype(BF16)

        @pl.when(j >= rope_blocks)
        def _():
            out_refs[0][...] = acc.astype(BF16)

    (out,) = _matmul(name, a, b, nt=False, tm=tm, tn=tn, n_cols=n_cols,
                     out_shape=[jax.ShapeDtypeStruct((m_rows, n_cols), BF16)],
                     out_specs=[pl.BlockSpec((tm, tn), lambda i, j, k: (i, j))],
                     epilogue=epilogue, fixed_bytes=4 * tm * tn + 16 * tm * LANE, b_spec=b_spec,
                     extra=(cos, sin), extra_specs=(table_spec, table_spec))
    return out


def _mm_residual(name, u, w, x_in, gate):
    m_rows, n_cols = x_in.shape
    tm, tn = _tile(m_rows, 512), _tile(n_cols, 1024)
    blk = pl.BlockSpec((tm, tn), lambda i, j, k: (i, j))

    def epilogue(acc, extra_refs, out_refs):
        x_ref, gate_ref = extra_refs
        out_refs[0][...] = acc
        out_refs[1][...] = x_ref[...] + gate_ref[...] * acc

    y, x_out = _matmul(
        name, u, w, nt=False, tm=tm, tn=tn, n_cols=n_cols,
        out_shape=[jax.ShapeDtypeStruct((m_rows, n_cols), F32)] * 2, out_specs=[blk, blk],
        epilogue=epilogue, fixed_bytes=3 * 8 * tm * tn, extra=(x_in, gate),
        extra_specs=(blk, pl.BlockSpec((1, tn), lambda i, j, k: (0, j))))
    return y, x_out


def _mm_gate_bwd(name, dy, w_out, z_src, z_col0, o):
    m_rows = dy.shape[0]
    n_cols = w_out.shape[0]
    tm, tn = _tile(m_rows, 1024), _tile(n_cols, 1024)
    assert z_col0 % tn == 0 and tn % HEAD_DIM == 0 and n_cols // HEAD_DIM <= LANE
    z_blk0 = z_col0 // tn
    blk = pl.BlockSpec((tm, tn), lambda i, j, k: (i, j))

    def epilogue(du, extra_refs, out_refs):
        z_ref, o_ref = extra_refs
        do_ref, dz_ref, delta_ref = out_refs
        j = pl.program_id(1)
        z = z_ref[...].astype(F32)
        o_val = o_ref[...].astype(F32)
        sig = _sigmoid(z)
        d_o = (du * (z * sig)).astype(BF16)
        do_ref[...] = d_o
        dz_ref[...] = (du * o_val * (sig * (1.0 + z * (1.0 - sig)))).astype(BF16)

        @pl.when(j == 0)
        def _():
            delta_ref[...] = jnp.zeros_like(delta_ref)

        prod = d_o.astype(F32) * o_val
        lane = _iota((tm, LANE), 1)
        delta = delta_ref[...]
        for head in range(tn // HEAD_DIM):
            rows = jnp.sum(prod[:, head * HEAD_DIM:(head + 1) * HEAD_DIM], axis=1, keepdims=True)
            delta = delta + jnp.where(lane == j * (tn // HEAD_DIM) + head, rows, 0.0)
        delta_ref[...] = delta

    d_o, dz, delta = _matmul(
        name, dy, w_out, nt=True, tm=tm, tn=tn, n_cols=n_cols,
        out_shape=[jax.ShapeDtypeStruct((m_rows, n_cols), BF16)] * 2
        + [jax.ShapeDtypeStruct((m_rows, LANE), F32)],
        out_specs=[blk, blk, pl.BlockSpec((tm, LANE), lambda i, j, k: (i, 0))],
        epilogue=epilogue, fixed_bytes=4 * 4 * tm * tn + 8 * tm * LANE, extra=(z_src, o),
        extra_specs=(pl.BlockSpec((tm, tn), lambda i, j, k: (i, z_blk0 + j)), blk))
    return d_o, dz, delta


def _norm_fwd(name, x, ga, sa, ta, gb=None):
    s_len, d = x.shape
    tr = _tile(s_len, 256)
    two = gb is not None
    row = pl.BlockSpec((tr, d), lambda i: (i, 0))
    vec = pl.BlockSpec((1, d), lambda i: (0, 0))

    def body(x_ref, ga_ref, sa_ref, ta_ref, *rest):
        xv = x_ref[...]
        y = xv * lax.rsqrt(jnp.mean(xv * xv, axis=-1, keepdims=True) + RMS_EPS)
        rest[-2 if two else -1][...] = ((y * ga_ref[...]) * (1.0 + sa_ref[...]) + ta_ref[...]).astype(BF16)
        if two:
            rest[-1][...] = (y * rest[0][...]).astype(BF16)

    ins = [x, ga, sa, ta] + ([gb] if two else [])
    outs = pl.pallas_call(
        body, name=name, grid=(s_len // tr,),
        in_specs=[row] + [vec] * (len(ins) - 1),
        out_specs=[row] * (2 if two else 1),
        out_shape=[jax.ShapeDtypeStruct((s_len, d), BF16)] * (2 if two else 1),
        compiler_params=_params(1))(*ins)
    return outs if two else outs[0]


def _loss_bwd(x2, target, y1, g_final, gate1):
    s_len, d = x2.shape
    tr = _tile(s_len, 128)
    row = pl.BlockSpec((tr, d), lambda i: (i, 0))
    vec = pl.BlockSpec((1, d), lambda i: (0, 0))

    def body(x_ref, t_ref, y_ref, g_ref, gate_ref, loss_ref, dx_ref, dy_ref, sums_ref):
        @pl.when(pl.program_id(0) == 0)
        def _():
            loss_ref[...] = jnp.zeros_like(loss_ref)
            sums_ref[...] = jnp.zeros_like(sums_ref)

        xv = x_ref[...]
        rstd = lax.rsqrt(jnp.mean(xv * xv, axis=-1, keepdims=True) + RMS_EPS)
        xhat = xv * rstd
        g = g_ref[...]
        err = xhat * g - t_ref[...]
        sq = jnp.sum(jnp.sum(err * err, axis=1, keepdims=True), axis=0, keepdims=True)
        loss_ref[...] += sq * (0.5 / d)
        dout = err * (1.0 / d)
        dxhat = dout * g
        dx = rstd * (dxhat - xhat * jnp.mean(dxhat * xhat, axis=-1, keepdims=True))
        dx_ref[...] = dx
        dy_ref[...] = (dx * gate_ref[...]).astype(BF16)
        sums_ref[0:1, :] += jnp.sum(dout * xhat, axis=0, keepdims=True)
        sums_ref[1:2, :] += jnp.sum(dx * y_ref[...], axis=0, keepdims=True)

    return pl.pallas_call(
        body, name="loss_bwd", grid=(s_len // tr,),
        in_specs=[row, row, row, vec, vec],
        out_specs=[pl.BlockSpec((SUBLANE, LANE), lambda i: (0, 0)), row, row,
                   pl.BlockSpec((SUBLANE, d), lambda i: (0, 0))],
        out_shape=[jax.ShapeDtypeStruct((SUBLANE, LANE), F32),
                   jax.ShapeDtypeStruct((s_len, d), F32),
                   jax.ShapeDtypeStruct((s_len, d), BF16),
                   jax.ShapeDtypeStruct((SUBLANE, d), F32)],
        compiler_params=_params(1))(x2, target, y1, g_final, gate1)


def _norm_bwd(name, x, dres, dha, ga, sa, dhb=None, gb=None, y=None, gate=None):
    s_len, d = x.shape
    tr = _tile(s_len, 128)
    has_b, has_y = dhb is not None, y is not None
    row = pl.BlockSpec((tr, d), lambda i: (i, 0))
    vec = pl.BlockSpec((1, d), lambda i: (0, 0))
    ins, specs = [x, dres, dha, ga, sa], [row, row, row, vec, vec]
    if has_b:
        ins += [dhb, gb]
        specs += [row, vec]
    if has_y:
        ins += [y, gate]
        specs += [row, vec]
    n_in = len(ins)

    def body(*refs):
        x_ref, dres_ref, dha_ref, ga_ref, sa_ref = refs[:5]
        pos = 5
        if has_b:
            dhb_ref, gb_ref = refs[pos:pos + 2]
            pos += 2
        if has_y:
            y_ref, gate_ref = refs[pos:pos + 2]
        outs = refs[n_in:]
        dx_ref, sums_ref = outs[0], outs[-1]

        @pl.when(pl.program_id(0) == 0)
        def _():
            sums_ref[...] = jnp.zeros_like(sums_ref)

        xv = x_ref[...]
        rstd = lax.rsqrt(jnp.mean(xv * xv, axis=-1, keepdims=True) + RMS_EPS)
        xhat = xv * rstd
        dha_v = dha_ref[...]
        ga_v, sa_v = ga_ref[...], sa_ref[...]
        dxhat = dha_v * (ga_v * (1.0 + sa_v))
        sums_ref[0:1, :] += jnp.sum(dha_v, axis=0, keepdims=True)
        sums_ref[1:2, :] += jnp.sum(dha_v * (xhat * ga_v), axis=0, keepdims=True)
        sums_ref[2:3, :] += jnp.sum(dha_v * ((1.0 + sa_v) * xhat), axis=0, keepdims=True)
        if has_b:
            dhb_v = dhb_ref[...]
            dxhat = dxhat + dhb_v * gb_ref[...]
            sums_ref[3:4, :] += jnp.sum(dhb_v * xhat, axis=0, keepdims=True)
        dx = dres_ref[...] + rstd * (dxhat - xhat * jnp.mean(dxhat * xhat, axis=-1, keepdims=True))
        dx_ref[...] = dx
        if has_y:
            outs[1][...] = (dx * gate_ref[...]).astype(BF16)
            sums_ref[4:5, :] += jnp.sum(dx * y_ref[...], axis=0, keepdims=True)

    out_shape = [jax.ShapeDtypeStruct((s_len, d), F32)]
    out_specs = [row]
    if has_y:
        out_shape.append(jax.ShapeDtypeStruct((s_len, d), BF16))
        out_specs.append(row)
    out_shape.append(jax.ShapeDtypeStruct((SUBLANE, d), F32))
    out_specs.append(pl.BlockSpec((SUBLANE, d), lambda i: (0, 0)))
    return pl.pallas_call(body, name=name, grid=(s_len // tr,), in_specs=specs,
                          out_specs=out_specs, out_shape=out_shape,
                          compiler_params=_params(1))(*ins)


def _fgate_fwd(f_raw, bias_row):
    s_len = f_raw.shape[0]
    tb = _tile(s_len, 512)
    blk = pl.BlockSpec((tb, LANE), lambda t: (t, 0))

    def body(f_ref, b_ref, out_ref, carry):
        @pl.when(pl.program_id(0) == 0)
        def _():
            carry[...] = jnp.zeros_like(carry)

        u = f_ref[...] + b_ref[...]
        logf = jnp.minimum(u, 0.0) - jnp.log1p(jnp.exp(-jnp.abs(u)))
        tri = (_iota((tb, tb), 1) <= _iota((tb, tb), 0)).astype(F32)
        run = jnp.dot(tri, logf, precision=HIGHEST, preferred_element_type=F32) + carry[...]
        out_ref[...] = run
        carry[...] = run[tb - 1:tb, :]

    return pl.pallas_call(
        body, name="fgate_fwd", grid=(s_len // tb,),
        in_specs=[blk, pl.BlockSpec((1, LANE), lambda t: (0, 0))], out_specs=blk,
        out_shape=jax.ShapeDtypeStruct((s_len, LANE), F32),
        scratch_shapes=[pltpu.VMEM((1, LANE), F32)],
        compiler_params=_params(1))(f_raw, bias_row)


def _fgate_bwd(df_a, df_b, f_raw, bias_row):
    s_len = f_raw.shape[0]
    tb = _tile(s_len, 512)
    nb = s_len // tb
    blk = pl.BlockSpec((tb, LANE), lambda t: (nb - 1 - t, 0))

    def body(a_ref, b2_ref, f_ref, b_ref, df_ref, sums_ref, carry):
        @pl.when(pl.program_id(0) == 0)
        def _():
            carry[...] = jnp.zeros_like(carry)
            sums_ref[...] = jnp.zeros_like(sums_ref)

        d_run = a_ref[...] + b2_ref[...]
        tri = (_iota((tb, tb), 1) >= _iota((tb, tb), 0)).astype(F32)
        dlogf = jnp.dot(tri, d_run, precision=HIGHEST, preferred_element_type=F32) + carry[...]
        carry[...] = dlogf[0:1, :]
        u = f_ref[...] + b_ref[...]
        df = dlogf * _sigmoid(-u)
        df_ref[...] = df.astype(BF16)
        sums_ref[...] += jnp.sum(df, axis=0, keepdims=True)

    return pl.pallas_call(
        body, name="fgate_bwd", grid=(nb,),
        in_specs=[blk, blk, blk, pl.BlockSpec((1, LANE), lambda t: (0, 0))],
        out_specs=[blk, pl.BlockSpec((SUBLANE, LANE), lambda t: (0, 0))],
        out_shape=[jax.ShapeDtypeStruct((s_len, LANE), BF16),
                   jax.ShapeDtypeStruct((SUBLANE, LANE), F32)],
        scratch_shapes=[pltpu.VMEM((1, LANE), F32)],
        compiler_params=_params(1))(df_a, df_b, f_raw, bias_row)


def _fox_fwd(proj, f_nat, f_t, heads):
    s_len = proj.shape[0]
    d = heads * HEAD_DIM
    t = _tile(s_len, 512)
    nq = s_len // t
    scale = HEAD_DIM ** -0.5

    pair = FOX_QUERY_BLOCKS if nq % FOX_QUERY_BLOCKS == 0 else 2 if nq % 2 == 0 else 1

    def body(k_ref, q_ref, v_ref, fn_ref, ft_ref, ot_ref, lse_ref,
             acc_scr, m_scr, l_scr, fk_scr):
        j, p = pl.program_id(1), pl.program_id(2)
        h = pl.program_id(0)
        first = j // pair

        @pl.when((j == 0) & (p == 0))
        def _():
            m_scr[...] = jnp.full_like(m_scr, NEG)
            l_scr[...] = jnp.zeros_like(l_scr)
            acc_scr[...] = jnp.zeros_like(acc_scr)

        @pl.when(p == first)
        def _():
            fk_scr[...] = _pick_lane(fn_ref[...], h)

        def update(slot, diagonal):
            i = pair * p + slot
            rows = slice(slot * t, (slot + 1) * t)
            s_t = (_dot_nt(k_ref[...], q_ref[rows, :]) * scale
                   + (ft_ref[:, rows] - fk_scr[...]))
            if diagonal:
                s_t = jnp.where(_iota((t, t), 0) <= _iota((t, t), 1), s_t, NEG)
            m_prev = m_scr[i]
            m_new = jnp.maximum(m_prev, jnp.max(s_t, axis=0, keepdims=True))
            alpha = jnp.exp(m_prev - m_new)
            p_t = jnp.exp(s_t - m_new)
            l_scr[i] = alpha * l_scr[i] + jnp.sum(p_t, axis=0, keepdims=True)
            acc_scr[i] = alpha * acc_scr[i] + _dot_tn(v_ref[...], p_t.astype(BF16))
            m_scr[i] = m_new

        @pl.when(j < pair * p)
        def _():
            for slot in range(pair):
                update(slot, False)

        for diag_slot in range(pair):
            @pl.when(j == pair * p + diag_slot)
            def _(diag_slot=diag_slot):
                update(diag_slot, True)
                for slot in range(diag_slot + 1, pair):
                    update(slot, False)

        @pl.when((p == nq // pair - 1) & (j == nq - 1))
        def _():
            for blk in range(nq):
                cols = slice(blk * t, (blk + 1) * t)
                l_sum = l_scr[blk]
                ot_ref[:, cols] = acc_scr[blk] / l_sum
                lse_ref[:, cols] = m_scr[blk] + jnp.log(l_sum)

    qry = pl.BlockSpec((pair * t, HEAD_DIM), lambda h, j, p: (jnp.maximum(p, j // pair), h))
    return pl.pallas_call(
        body, name="fox_fwd", grid=(heads, nq, nq // pair),
        in_specs=[pl.BlockSpec((t, HEAD_DIM), lambda h, j, p: (j, heads + h)), qry,
                  pl.BlockSpec((t, HEAD_DIM), lambda h, j, p: (j, 2 * heads + h)),
                  pl.BlockSpec((t, LANE), lambda h, j, p: (j, 0)),
                  pl.BlockSpec((None, 1, pair * t),
                               lambda h, j, p: (h, 0, jnp.maximum(p, j // pair)))],
        out_specs=[pl.BlockSpec((HEAD_DIM, s_len), lambda h, j, i: (h, 0)),
                   pl.BlockSpec((None, 1, s_len), lambda h, j, i: (h, 0, 0))],
        out_shape=[jax.ShapeDtypeStruct((d, s_len), F32),
                   jax.ShapeDtypeStruct((heads, 1, s_len), F32)],
        scratch_shapes=[pltpu.VMEM((nq, HEAD_DIM, t), F32), pltpu.VMEM((nq, 1, t), F32),
                        pltpu.VMEM((nq, 1, t), F32), pltpu.VMEM((t, 1), F32)],
        compiler_params=_params(3))(proj, proj, proj, f_nat, f_t)


def _gate_fwd(o_t, proj, z_blk0, heads):
    d, s_len = o_t.shape
    t = _tile(s_len, 512)

    def body(ot_ref, z_ref, o_ref, u_ref):
        o_val = jnp.transpose(ot_ref[...])
        o_ref[...] = o_val.astype(BF16)
        z = z_ref[...].astype(F32)
        u_ref[...] = (o_val * (z * _sigmoid(z))).astype(BF16)

    out_blk = pl.BlockSpec((t, HEAD_DIM), lambda i, h: (i, h))
    return pl.pallas_call(
        body, name="gate_fwd", grid=(s_len // t, heads),
        in_specs=[pl.BlockSpec((HEAD_DIM, t), lambda i, h: (h, i)),
                  pl.BlockSpec((t, HEAD_DIM), lambda i, h: (i, z_blk0 + h))],
        out_specs=[out_blk, out_blk],
        out_shape=[jax.ShapeDtypeStruct((s_len, d), BF16)] * 2,
        compiler_params=_params(2))(o_t, proj)


def _fox_bwd(proj, d_o, f_nat, f_t, lse_t, delta_t, heads):
    s_len = proj.shape[0]
    d = heads * HEAD_DIM
    t = _tile(s_len, 512)
    nq = s_len // t
    pair = FOX_QUERY_BLOCKS if nq % FOX_QUERY_BLOCKS == 0 else 2 if nq % 2 == 0 else 1
    scale = HEAD_DIM ** -0.5

    def body(k_ref, v_ref, q_ref, do_ref, fn_ref, ft_ref, lse_ref, delta_ref,
             dk_ref, dv_ref, dqt_ref, dfq_ref, dfk_ref,
             dk_acc, dv_acc, dq_acc, dfq_acc, dfk_acc, fk_scr):
        h, j, p = pl.program_id(0), pl.program_id(1), pl.program_id(2)
        head_start = (j == 0) & (p == 0)
        last = nq // pair - 1

        @pl.when(head_start)
        def _():
            dq_acc[...] = jnp.zeros_like(dq_acc)
            dfq_acc[...] = jnp.zeros_like(dfq_acc)

        @pl.when(head_start & (h == 0))
        def _():
            dfk_ref[...] = jnp.zeros_like(dfk_ref)

        @pl.when(p == j // pair)
        def _():
            dk_acc[...] = jnp.zeros_like(dk_acc)
            dv_acc[...] = jnp.zeros_like(dv_acc)
            dfk_acc[...] = jnp.zeros_like(dfk_acc)
            fk_scr[...] = _pick_lane(fn_ref[...], h)

        def update(slot, diagonal):
            i = pair * p + slot
            rows = slice(slot * t, (slot + 1) * t)
            q = q_ref[rows, :]
            d_out = do_ref[rows, :]
            s_t = (_dot_nt(k_ref[...], q) * scale + (ft_ref[:, rows] - fk_scr[...])
                   - lse_ref[:, rows])
            if diagonal:
                s_t = jnp.where(_iota((t, t), 0) <= _iota((t, t), 1), s_t, NEG)
            p_t = jnp.exp(s_t)
            dp_t = _dot_nt(v_ref[...], d_out)
            ds_t = p_t * (dp_t - delta_ref[:, rows])
            ds_b = ds_t.astype(BF16)
            dv_acc[...] += _dot(p_t.astype(BF16), d_out)
            dk_acc[...] += _dot(ds_b, q)
            dq_acc[i] += _dot_tn(k_ref[...], ds_b)
            dfq_acc[i] += jnp.sum(ds_t, axis=0, keepdims=True)
            dfk_acc[...] += jnp.sum(ds_t, axis=1, keepdims=True)

        @pl.when(j < pair * p)
        def _():
            for slot in range(pair):
                update(slot, False)

        for diag_slot in range(pair):
            @pl.when(j == pair * p + diag_slot)
            def _(diag_slot=diag_slot):
                update(diag_slot, True)
                for slot in range(diag_slot + 1, pair):
                    update(slot, False)

        @pl.when(p == last)
        def _():
            dk_ref[...] = (dk_acc[...] * scale).astype(BF16)
            dv_ref[...] = dv_acc[...].astype(BF16)
            rows = pl.ds(pl.multiple_of(j * t, t), t)
            dfk_ref[rows, :] += jnp.where(_iota((t, LANE), 1) == h, -dfk_acc[...], 0.0)

        @pl.when((p == last) & (j == nq - 1))
        def _():
            for blk in range(nq):
                cols = slice(blk * t, (blk + 1) * t)
                dqt_ref[:, cols] = (dq_acc[blk] * scale).astype(BF16)
                dfq_ref[:, cols] = dfq_acc[blk]

    key_col = lambda base: pl.BlockSpec((t, HEAD_DIM), lambda h, j, p: (j, base + h))
    qry = pl.BlockSpec((pair * t, HEAD_DIM), lambda h, j, p: (jnp.maximum(p, j // pair), h))
    qry_row = pl.BlockSpec((None, 1, pair * t),
                           lambda h, j, p: (h, 0, jnp.maximum(p, j // pair)))
    kv_out = pl.BlockSpec((t, HEAD_DIM), lambda h, j, i: (j, h))
    return pl.pallas_call(
        body, name="fox_bwd", grid=(heads, nq, nq // pair),
        in_specs=[key_col(heads), key_col(2 * heads),
                  qry, qry, pl.BlockSpec((t, LANE), lambda h, j, i: (j, 0)),
                  qry_row, qry_row, qry_row],
        out_specs=[kv_out, kv_out,
                   pl.BlockSpec((HEAD_DIM, s_len), lambda h, j, i: (h, 0)),
                   pl.BlockSpec((None, 1, s_len), lambda h, j, i: (h, 0, 0)),
                   pl.BlockSpec((s_len, LANE), lambda h, j, i: (0, 0))],
        out_shape=[jax.ShapeDtypeStruct((s_len, d), BF16), jax.ShapeDtypeStruct((s_len, d), BF16),
                   jax.ShapeDtypeStruct((d, s_len), BF16),
                   jax.ShapeDtypeStruct((heads, 1, s_len), F32),
                   jax.ShapeDtypeStruct((s_len, LANE), F32)],
        scratch_shapes=[pltpu.VMEM((t, HEAD_DIM), F32), pltpu.VMEM((t, HEAD_DIM), F32),
                        pltpu.VMEM((nq, HEAD_DIM, t), F32), pltpu.VMEM((nq, 1, t), F32),
                        pltpu.VMEM((t, 1), F32), pltpu.VMEM((t, 1), F32)],
        compiler_params=_params(3))(proj, proj, proj, d_o, f_nat, f_t, lse_t, delta_t)


def _swa_specs(heads, kv_heads):
    width = heads // kv_heads * HEAD_DIM
    wide = lambda base: pl.BlockSpec((SWA_BLOCK, width), lambda n, g: (n, base + g))
    blk = lambda fn: pl.BlockSpec((SWA_BLOCK, HEAD_DIM), fn)
    prev = lambda base: blk(lambda n, g: (jnp.maximum(n - 1, 0), base + g))
    cur = lambda base: blk(lambda n, g: (n, base + g))
    return wide, prev, cur


def _stack_heads(ref, group):
    return jnp.concatenate([ref[:, hh * HEAD_DIM:(hh + 1) * HEAD_DIM] for hh in range(group)], axis=0)


def _head_rows(stacked, hh):
    return stacked[hh * SWA_BLOCK:(hh + 1) * SWA_BLOCK]


def _swa_scores(q, kp, kc, n, scale):
    shape = (q.shape[0], SWA_BLOCK)
    r, c = _iota(shape, 0) & (SWA_BLOCK - 1), _iota(shape, 1)
    sp = jnp.where((c > r) & (n > 0), _dot_nt(q, kp) * scale, NEG)
    sc = jnp.where(c <= r, _dot_nt(q, kc) * scale, NEG)
    return sp, sc


def _per_head_column(values_row, first_head, group):
    head_of_row = _iota((group * SWA_BLOCK, 1), 0) // SWA_BLOCK
    col = jnp.zeros((group * SWA_BLOCK, 1), F32)
    for hh in range(group):
        col = jnp.where(head_of_row == hh, _pick_lane(values_row, first_head + hh), col)
    return col


def _swa_fwd(proj, kv, sinks_row, heads, kv_heads):
    s_len = proj.shape[0]
    d = heads * HEAD_DIM
    scale = HEAD_DIM ** -0.5
    group = heads // kv_heads
    wide, prev, cur = _swa_specs(heads, kv_heads)

    def body(q_ref, z_ref, kp_ref, kc_ref, vp_ref, vc_ref, sink_ref, o_ref, u_ref, lse_ref):
        n, g = pl.program_id(0), pl.program_id(1)
        sp, sc = _swa_scores(_stack_heads(q_ref, group), kp_ref[...], kc_ref[...], n, scale)
        sink = _per_head_column(sink_ref[...], g * group, group)
        m = jnp.maximum(jnp.maximum(jnp.max(sp, axis=1, keepdims=True),
                                    jnp.max(sc, axis=1, keepdims=True)), sink)
        pp, pc = jnp.exp(sp - m), jnp.exp(sc - m)
        den = (jnp.sum(pp, axis=1, keepdims=True) + jnp.sum(pc, axis=1, keepdims=True)
               + jnp.exp(sink - m))
        o_all = (_dot(pp.astype(BF16), vp_ref[...]) + _dot(pc.astype(BF16), vc_ref[...])) / den
        lse = m + jnp.log(den)
        lane = _iota((SWA_BLOCK, LANE), 1)
        lse_all = jnp.zeros((SWA_BLOCK, LANE), F32)
        for hh in range(group):
            cols = slice(hh * HEAD_DIM, (hh + 1) * HEAD_DIM)
            o_val = _head_rows(o_all, hh)
            o_ref[:, cols] = o_val.astype(BF16)
            z = z_ref[:, cols].astype(F32)
            u_ref[:, cols] = (o_val * (z * _sigmoid(z))).astype(BF16)
            lse_all = lse_all + jnp.where(lane == g * group + hh, _head_rows(lse, hh), 0.0)

        @pl.when(g == 0)
        def _():
            lse_ref[...] = lse_all

        @pl.when(g > 0)
        def _():
            lse_ref[...] += lse_all

    nat = pl.BlockSpec((SWA_BLOCK, LANE), lambda n, g: (n, 0))
    return pl.pallas_call(
        body, name="swa_fwd", grid=(s_len // SWA_BLOCK, kv_heads),
        in_specs=[wide(0), wide(kv_heads), prev(0), cur(0), prev(kv_heads), cur(kv_heads),
                  pl.BlockSpec((1, LANE), lambda n, g: (0, 0))],
        out_specs=[wide(0), wide(0), nat],
        out_shape=[jax.ShapeDtypeStruct((s_len, d), BF16), jax.ShapeDtypeStruct((s_len, d), BF16),
                   jax.ShapeDtypeStruct((s_len, LANE), F32)],
        compiler_params=_params(2))(proj, proj, kv, kv, kv, kv, sinks_row)


def _swa_bwd_q(proj, kv, d_o, lse, delta, sinks_row, cos, sin, heads, kv_heads):
    s_len = proj.shape[0]
    d = heads * HEAD_DIM
    scale = HEAD_DIM ** -0.5
    group = heads // kv_heads
    wide, prev, cur = _swa_specs(heads, kv_heads)

    def body(q_ref, kp_ref, kc_ref, vp_ref, vc_ref, do_ref, lse_ref, delta_ref, sink_ref,
             cos_ref, sin_ref, dq_ref, dsink_ref):
        n, g = pl.program_id(0), pl.program_id(1)

        @pl.when((n == 0) & (g == 0))
        def _():
            dsink_ref[...] = jnp.zeros_like(dsink_ref)

        kp, kc = kp_ref[...], kc_ref[...]
        first = g * group
        sp, sc = _swa_scores(_stack_heads(q_ref, group), kp, kc, n, scale)
        lse_blk, delta_blk = lse_ref[...], delta_ref[...]
        lse_col = jnp.concatenate([_pick_lane(lse_blk, first + hh) for hh in range(group)], axis=0)
        delta_col = jnp.concatenate([_pick_lane(delta_blk, first + hh) for hh in range(group)],
                                    axis=0)
        pp, pc = jnp.exp(sp - lse_col), jnp.exp(sc - lse_col)
        p_sink = jnp.exp(_per_head_column(sink_ref[...], first, group) - lse_col)
        d_out = _stack_heads(do_ref, group)
        dsp = pp * (_dot_nt(d_out, vp_ref[...]) - delta_col)
        dsc = pc * (_dot_nt(d_out, vc_ref[...]) - delta_col)
        dq = (_dot(dsp.astype(BF16), kp) + _dot(dsc.astype(BF16), kc)) * scale
        cos_v, sin_v = cos_ref[...], sin_ref[...]
        sink_part = -p_sink * delta_col
        lane = _iota((SUBLANE, LANE), 1)
        dsink_all = jnp.zeros((SUBLANE, LANE), F32)
        for hh in range(group):
            cols = slice(hh * HEAD_DIM, (hh + 1) * HEAD_DIM)
            dq_h = _head_rows(dq, hh)
            dq_ref[:, cols] = (dq_h * cos_v
                               - pltpu.roll(dq_h, HEAD_DIM // 2, 1) * sin_v).astype(BF16)
            d_sink = jnp.sum(_head_rows(sink_part, hh), axis=0, keepdims=True)
            dsink_all = dsink_all + jnp.where(lane == first + hh, d_sink, 0.0)
        dsink_ref[...] += dsink_all

    own = wide(0)
    nat = pl.BlockSpec((SWA_BLOCK, LANE), lambda n, g: (n, 0))
    return pl.pallas_call(
        body, name="swa_bwd_q", grid=(s_len // SWA_BLOCK, kv_heads),
        in_specs=[own, prev(0), cur(0), prev(kv_heads), cur(kv_heads), own, nat, nat,
                  pl.BlockSpec((1, LANE), lambda n, g: (0, 0)), nat, nat],
        out_specs=[own, pl.BlockSpec((SUBLANE, LANE), lambda n, g: (0, 0))],
        out_shape=[jax.ShapeDtypeStruct((s_len, d), BF16),
                   jax.ShapeDtypeStruct((SUBLANE, LANE), F32)],
        compiler_params=_params(2))(proj, kv, kv, kv, kv, d_o, lse, delta, sinks_row, cos, sin)


def _swa_bwd_kv(proj, kv, d_o, lse_t, delta_t, cos, sin, heads, kv_heads):
    s_len = proj.shape[0]
    nb = s_len // SWA_BLOCK
    group = heads // kv_heads
    scale = HEAD_DIM ** -0.5

    def body(k_ref, v_ref, qm_ref, qn_ref, dom_ref, don_ref, lsem_ref, lsen_ref,
             deltam_ref, deltan_ref, cos_ref, sin_ref, dk_ref, dv_ref):
        m = pl.program_id(1)
        k, v = k_ref[...], v_ref[...]
        shape = (SWA_BLOCK, group * SWA_BLOCK)
        key, qry = _iota(shape, 0), _iota(shape, 1) & (SWA_BLOCK - 1)
        own_valid = key <= qry
        next_valid = (key > qry) & (m + 1 < nb)
        dk = jnp.zeros((SWA_BLOCK, HEAD_DIM), F32)
        dv = jnp.zeros((SWA_BLOCK, HEAD_DIM), F32)
        for q_ref, do_ref, lse_ref, delta_ref, valid in (
                (qm_ref, dom_ref, lsem_ref, deltam_ref, own_valid),
                (qn_ref, don_ref, lsen_ref, deltan_ref, next_valid)):
            q, d_out = _stack_heads(q_ref, group), _stack_heads(do_ref, group)
            lse_row = jnp.concatenate([lse_ref[hh] for hh in range(group)], axis=1)
            delta_row = jnp.concatenate([delta_ref[hh] for hh in range(group)], axis=1)
            s_t = _dot_nt(k, q) * scale
            p_t = jnp.exp(jnp.where(valid, s_t - lse_row, NEG))
            ds_t = p_t * (_dot_nt(v, d_out) - delta_row)
            dv = dv + _dot(p_t.astype(BF16), d_out)
            dk = dk + _dot(ds_t.astype(BF16), q)
        dk = dk * scale
        dk_ref[...] = (dk * cos_ref[...]
                       - pltpu.roll(dk, HEAD_DIM // 2, 1) * sin_ref[...]).astype(BF16)
        dv_ref[...] = dv.astype(BF16)

    blk = lambda fn: pl.BlockSpec((SWA_BLOCK, HEAD_DIM), fn)
    nxt = lambda m: jnp.minimum(m + 1, nb - 1)
    wide = lambda fn: pl.BlockSpec((SWA_BLOCK, group * HEAD_DIM), fn)
    rows = lambda fn: pl.BlockSpec((group, 1, SWA_BLOCK), fn)
    q_m, q_n = wide(lambda g, m: (m, g)), wide(lambda g, m: (nxt(m), g))
    r_m, r_n = rows(lambda g, m: (g, 0, m)), rows(lambda g, m: (g, 0, nxt(m)))
    nat = pl.BlockSpec((SWA_BLOCK, LANE), lambda g, m: (m, 0))
    out_blk = blk(lambda g, m: (m, g))
    width = kv_heads * HEAD_DIM
    return pl.pallas_call(
        body, name="swa_bwd_kv", grid=(kv_heads, nb),
        in_specs=[blk(lambda g, m: (m, g)), blk(lambda g, m: (m, kv_heads + g)),
                  q_m, q_n, q_m, q_n, r_m, r_n, r_m, r_n, nat, nat],
        out_specs=[out_blk, out_blk],
        out_shape=[jax.ShapeDtypeStruct((s_len, width), BF16)] * 2,
        compiler_params=_params(2))(kv, kv, proj, proj, d_o, d_o, lse_t, lse_t,
                                    delta_t, delta_t, cos, sin)


def _ada_fwd(c_rows, ada_w, bias_loc):
    n_layers, d, cols = ada_w.shape
    rows = c_rows.shape[0]
    tk = _tile(d, 512)
    n_k = d // tk

    def body(c_ref, w_ref, b_ref, mod_ref, sc_ref, acc_ref):
        k = pl.program_id(1)

        @pl.when(k == 0)
        def _():
            acc_ref[...] = jnp.zeros_like(acc_ref)

        cv = c_ref[...]
        sc = cv * _sigmoid(cv)
        sc_ref[...] = sc
        acc_ref[...] += _dot(sc.astype(BF16), w_ref[...].astype(BF16))

        @pl.when(k == n_k - 1)
        def _():
            mod_ref[...] = acc_ref[...] + b_ref[...]

    return pl.pallas_call(
        body, name="ada_fwd", grid=(n_layers, n_k),
        in_specs=[pl.BlockSpec((rows, tk), lambda l, k: (0, k)),
                  pl.BlockSpec((None, tk, cols), lambda l, k: (l, k, 0)),
                  pl.BlockSpec((None, 1, cols), lambda l, k: (l, 0, 0))],
        out_specs=[pl.BlockSpec((None, rows, cols), lambda l, k: (l, 0, 0)),
                   pl.BlockSpec((None, rows, tk), lambda l, k: (l, 0, k))],
        out_shape=[jax.ShapeDtypeStruct((n_layers, rows, cols), F32),
                   jax.ShapeDtypeStruct((n_layers, rows, d), F32)],
        scratch_shapes=[pltpu.VMEM((rows, cols), F32)],
        compiler_params=_params(2))(c_rows, ada_w, bias_loc)


def _ada_update(sc_t, dmod, w, m, v):
    n_layers, d, cols = w.shape
    tr = _tile(d, 256)
    big = pl.BlockSpec((None, tr, cols), lambda l, i: (l, i, 0))

    def body(sc_ref, dm_ref, w_ref, m_ref, v_ref, g_out, d_out, m_out, v_out):
        g = _dot(sc_ref[...], dm_ref[...])
        delta, m_new, v_new = _adamw(w_ref[...], g, m_ref[...], v_ref[...])
        g_out[...] = g
        d_out[...] = delta
        m_out[...] = m_new
        v_out[...] = v_new

    return pl.pallas_call(
        body, name="ada_update", grid=(n_layers, d // tr),
        in_specs=[pl.BlockSpec((tr, LANE), lambda l, i: (i, 0)),
                  pl.BlockSpec((None, LANE, cols), lambda l, i: (l, 0, 0)), big, big, big],
        out_specs=[big] * 4, out_shape=[jax.ShapeDtypeStruct(w.shape, F32)] * 4,
        compiler_params=_params(2))(sc_t, dmod, w, m, v)


def _row_tile(rows, bytes_per_row):
    tr = SUBLANE * 2
    while tr * 2 <= rows and rows % (tr * 2) == 0 and tr * 2 * bytes_per_row <= 24 * 2 ** 20:
        tr *= 2
    return _tile(rows, tr)


def _slab_sum(name, arrays):
    rows = arrays[0].shape[1]
    per_row = sum(2 * a.shape[2] * (a.shape[0] * a.dtype.itemsize + 4) for a in arrays)
    tr = _row_tile(rows, per_row)
    n = len(arrays)

    def body(*refs):
        for s_ref, out_ref in zip(refs[:n], refs[n:]):
            total = s_ref[0].astype(F32)
            for slot in range(1, s_ref.shape[0]):
                total = total + s_ref[slot].astype(F32)
            out_ref[...] = total

    return pl.pallas_call(
        body, name=name, grid=(rows // tr,),
        in_specs=[pl.BlockSpec((a.shape[0], tr, a.shape[2]), lambda i: (0, i, 0)) for a in arrays],
        out_specs=[pl.BlockSpec((tr, a.shape[2]), lambda i: (i, 0)) for a in arrays],
        out_shape=[jax.ShapeDtypeStruct(a.shape[1:], F32) for a in arrays],
        compiler_params=_params(1))(*arrays)


def _decay(name, w, m, v):
    rows, cols = w.shape
    tr = _row_tile(rows, 2 * cols * 4 * 6)
    blk = pl.BlockSpec((tr, cols), lambda i: (i, 0))

    def body(w_ref, m_ref, v_ref, w_out, m_out, v_out):
        w_out[...] = ADAM_WD * w_ref[...]
        m_out[...] = ADAM_B1 * m_ref[...]
        v_out[...] = ADAM_B2 * v_ref[...]

    return pl.pallas_call(body, name=name, grid=(rows // tr,), in_specs=[blk] * 3,
                          out_specs=[blk] * 3, out_shape=[jax.ShapeDtypeStruct(w.shape, F32)] * 3,
                          compiler_params=_params(1))(w, m, v)


def _shard_update(name, slabs, w, m, v, decayed=False):
    rows, cols = w.shape
    n_slabs = slabs.shape[0]
    tr = _row_tile(rows, 2 * cols * (slabs.dtype.itemsize * n_slabs + 4 * 7))
    blk = pl.BlockSpec((tr, cols), lambda i: (i, 0))
    step = _adamw_decayed if decayed else _adamw

    def body(s_ref, w_ref, m_ref, v_ref, g_out, d_out, m_out, v_out):
        g = s_ref[0].astype(F32)
        for slot in range(1, n_slabs):
            g = g + s_ref[slot].astype(F32)
        delta, m_new, v_new = step(w_ref[...], g, m_ref[...], v_ref[...])
        g_out[...] = g
        d_out[...] = delta
        m_out[...] = m_new
        v_out[...] = v_new

    return pl.pallas_call(
        body, name=name, grid=(rows // tr,),
        in_specs=[pl.BlockSpec((n_slabs, tr, cols), lambda i: (0, i, 0)), blk, blk, blk],
        out_specs=[blk] * 4, out_shape=[jax.ShapeDtypeStruct((rows, cols), F32)] * 4,
        compiler_params=_params(1))(slabs, w, m, v)


def _small_update(gathered, w, m, v):
    shape = jax.ShapeDtypeStruct(w.shape, F32)

    def body(g_ref, w_ref, m_ref, v_ref, g_out, d_out, m_out, v_out):
        g = g_ref[0]
        for dev in range(1, N_DEV):
            g = g + g_ref[dev]
        delta, m_new, v_new = _adamw(w_ref[...], g, m_ref[...], v_ref[...])
        g_out[...] = g
        d_out[...] = delta
        m_out[...] = m_new
        v_out[...] = v_new

    return pl.pallas_call(body, name="small_update", out_shape=[shape] * 4,
                          compiler_params=pltpu.CompilerParams(vmem_limit_bytes=VMEM_LIMIT),
                          )(gathered, w, m, v)


def _rope_tables(s_len):
    half = HEAD_DIM // 2
    inv = ROPE_THETA ** (-jnp.arange(half, dtype=F32) / half)
    ang = jnp.arange(s_len, dtype=F32)[:, None] * inv[None, :]
    cos, sin = jnp.cos(ang), jnp.sin(ang)
    return jnp.concatenate([cos, cos], axis=1), jnp.concatenate([-sin, sin], axis=1)


def _pad_lanes(a):
    return jnp.pad(a, ((0, 0), (0, LANE - a.shape[1])))


def _rows_of(nat, heads):
    return jnp.transpose(nat[:, :heads])[:, None, :]


def _pack(parts):
    tile = SUBLANE * LANE
    flat = []
    for p in parts:
        p = p.reshape(-1)
        flat.append(jnp.pad(p, (0, (-p.shape[0]) % tile)))
    return jnp.concatenate(flat).reshape(-1, LANE)


def _unpack(packed, shapes):
    tile = SUBLANE * LANE
    flat = packed.reshape(-1)
    out, pos = [], 0
    for shape in shapes:
        size = 1
        for dim in shape:
            size *= dim
        out.append(flat[pos:pos + size].reshape(shape))
        pos += size + (-size) % tile
    return out


def kernel(x, c, norm_g, ada_w, ada_b, a_w_in, a_b_f, a_w_out, kv_norm_g, kv_w, b_w_in, b_sinks, b_w_out, final_norm_g, loss_target, m_norm_g, m_ada_w, m_ada_b, m_a_w_in, m_a_b_f, m_a_w_out, m_kv_norm_g, m_kv_w, m_b_w_in, m_b_sinks, m_b_w_out, m_final_norm_g, v_norm_g, v_ada_w, v_ada_b, v_a_w_in, v_a_b_f, v_a_w_out, v_kv_norm_g, v_kv_w, v_b_w_in, v_b_sinks, v_b_w_out, v_final_norm_g):
    s_len, d = x.shape[1], x.shape[2]
    heads = d // HEAD_DIM
    kv_heads = kv_w.shape[1] // (2 * HEAD_DIM)
    kv_width = kv_heads * HEAD_DIM
    ada_cols = ada_w.shape[2]
    assert heads <= LANE and heads % N_DEV == 0 and a_w_in.shape[2] * N_DEV == 4 * d + heads
    me = _slot(_mesh_pos())
    x0 = x[0]
    target = loss_target[0]
    vec = lambda a: a.reshape(1, d)

    sup, extra = 4 * d // N_DEV, heads // N_DEV
    padded = jnp.pad(a_w_in[0].astype(BF16), ((0, 0), (heads, LANE)))
    big_loc = lax.dynamic_slice_in_dim(padded, heads - extra * me, sup, axis=1)
    small_loc = lax.dynamic_slice_in_dim(padded, heads + sup - extra * me, LANE, axis=1)

    (c_all,) = _all_gather("gather_c", [c])
    c_rows = jnp.pad(c_all.reshape(N_DEV, d), ((0, 2 * SUBLANE - N_DEV), (0, 0)))
    bias_loc = lax.dynamic_slice_in_dim(ada_b, me * ada_cols, ada_cols, axis=1)[:, None, :]
    mod_part, sc_rows = _ada_fwd(c_rows, ada_w, bias_loc)
    (mod_recv,) = _all_to_all("exchange_mod", [jnp.transpose(mod_part[:, :N_DEV], (1, 0, 2))])
    mod = jnp.transpose(mod_recv, (1, 0, 2)).reshape(2, 3 * d)

    g_big, g_small = _all_gather("gather_a_w_in", _after([big_loc, small_loc], mod_recv),
                                 sequencer_id=5)
    shift0, scale0, gate0 = vec(mod[0, :d]), vec(mod[0, d:2 * d]), vec(mod[0, 2 * d:])
    shift1, scale1, gate1 = vec(mod[1, :d]), vec(mod[1, d:2 * d]), vec(mod[1, 2 * d:])
    g0, g1, g_kvn, g_fin = vec(norm_g[0]), vec(norm_g[1]), vec(kv_norm_g), vec(final_norm_g)
    h0 = _norm_fwd("norm0", x0, g0, scale0, shift0)
    a_in_decayed = _decay("decay_a_w_in", a_w_in[0], m_a_w_in[0], v_a_w_in[0])
    later = [a_w_out[0].astype(BF16), kv_w.astype(BF16), b_w_in[0].astype(BF16),
             b_w_out[0].astype(BF16)]
    g_a_out, g_kv, w_b_in, g_b_out = _all_gather(
        "gather_rest", _after(later, (g_big, h0, *a_in_decayed)), sequencer_id=1)
    w_a_out = g_a_out.reshape(d, d)
    w_kv = g_kv.reshape(d, 2 * kv_width)
    w_b_out = g_b_out.reshape(d, d)
    lane_id = jnp.arange(LANE)[None, :]
    patch = jnp.stack([jnp.where(lane_id < extra * s, g_small[s - 1], g_big[s, :, :LANE])
                       for s in range(1, N_DEV)])
    w_a_main = g_big.at[1:, :, :LANE].set(patch)
    w_a_f = g_small[N_DEV - 1]
    b_in_cols = b_w_in.shape[2]
    a_tn = _tile(sup, 1024)
    w_a_main_spec = lambda tk: pl.BlockSpec(
        (None, tk, a_tn), lambda i, j, k: (j // (sup // a_tn), k, j % (sup // a_tn)))

    cos, sin = _rope_tables(s_len)
    bias_f = _pad_lanes(a_b_f)
    sinks_row = _pad_lanes(b_sinks)

    proj0 = _mm_plain("proj0", h0, w_a_main, n_cols=4 * d, tn=a_tn, b_spec=w_a_main_spec)
    f_raw = _mm_plain("proj0_f", h0, w_a_f, out_dtype=F32, tn=LANE)
    f_nat = _fgate_fwd(f_raw, bias_f)
    f_t = _rows_of(f_nat, heads)
    o0_t, lse0_t = _fox_fwd(proj0, f_nat, f_t, heads)
    o0, u0 = _gate_fwd(o0_t, proj0, 3 * heads, heads)
    y0, x1 = _mm_residual("out0", u0, w_a_out, x0, gate0)

    h1, hk = _norm_fwd("norm1", x1, g1, scale1, shift1, gb=g_kvn)
    kv = _mm_rope("kv_proj", hk, w_kv, cos, sin, n_cols=2 * kv_width, rope_cols=kv_width,
                  tn=kv_width)
    proj1 = _mm_rope("proj1", h1, w_b_in, cos, sin, n_cols=2 * d, rope_cols=d, tn=b_in_cols,
                     b_spec=lambda tk: pl.BlockSpec((None, tk, b_in_cols),
                                                    lambda i, j, k: (j, k, 0)))
    o1, u1, lse1 = _swa_fwd(proj1, kv, sinks_row, heads, kv_heads)
    y1, x2 = _mm_residual("out1", u1, w_b_out, x1, gate1)

    loss_part, dx2, dy1, sums_f = _loss_bwd(x2, target, y1, g_fin, gate1)

    do1, dz1, delta1 = _mm_gate_bwd("out1_bwd", dy1, w_b_out, proj1, d, o1)
    gw_b_out = _mm_plain("out1_wgrad", u1, dy1, ta=True)
    do1 = _after(do1, gw_b_out)
    dq1, dsinks = _swa_bwd_q(proj1, kv, do1, lse1, delta1, sinks_row, cos, sin, heads, kv_heads)
    dk1, dv1 = _swa_bwd_kv(proj1, kv, do1, _rows_of(lse1, heads), _rows_of(delta1, heads),
                           cos, sin, heads, kv_heads)
    dproj1 = jnp.concatenate([dq1, dz1], axis=1)
    tk_b = _tile(2 * d, b_in_cols)
    dh1 = _mm_plain("proj1_bwd", dproj1, w_b_in, nt=True, n_cols=d, out_dtype=F32, tk=tk_b,
                    b_spec=lambda tk: pl.BlockSpec((None, _tile(d, 1024), tk),
                                                   lambda i, j, k: (k * tk // b_in_cols, j, 0)))
    gw_b_in = _mm_plain("proj1_wgrad", h1, dproj1, ta=True, tn=b_in_cols,
                        out_3d=(N_DEV, b_in_cols))
    dkv = jnp.concatenate([dk1, dv1], axis=1)
    dhk = _mm_plain("kv_bwd", dkv, w_kv, nt=True, out_dtype=F32)
    gw_kv = _mm_plain("kv_wgrad", hk, dkv, ta=True)
    dx1, dy0, sums1 = _norm_bwd("norm1_bwd", x1, dx2, dh1, g1, scale1,
                                dhb=_after(dhk, (gw_b_in, gw_kv)), gb=g_kvn, y=y0, gate=gate0)

    do0, dz0, delta0 = _mm_gate_bwd("out0_bwd", dy0, w_a_out, proj0, 3 * d, o0)
    gw_a_out = _mm_plain("out0_wgrad", u0, dy0, ta=True)
    r_b_out, r_b_in, r_kv, r_a_out = _all_to_all(
        "scatter_grads_early",
        [gw_b_out.reshape(N_DEV, d // N_DEV, d), gw_b_in,
         gw_kv.reshape(N_DEV, d // N_DEV, 2 * kv_width),
         gw_a_out.reshape(N_DEV, d // N_DEV, d)], sequencer_id=2)
    dk0, dv0, dq0_t, dfq_t, dfk_nat = _fox_bwd(proj0, _after(do0, gw_a_out), f_nat, f_t,
                                               lse0_t, _rows_of(delta0, heads), heads)
    dfq_nat = _pad_lanes(jnp.transpose(dfq_t[:, 0, :]))
    df, sums_bf = _fgate_bwd(dfq_nat, dfk_nat, f_raw, bias_f)
    dproj0 = jnp.concatenate([jnp.transpose(dq0_t), dk0, dv0, dz0], axis=1)
    gw_a_big = _mm_plain("proj0_wgrad", h0, dproj0, ta=True, tn=a_tn, out_3d=(N_DEV, sup))
    gw_a_f = _mm_plain("proj0_f_wgrad", h0, df, ta=True, tn=LANE)
    gw_a_small = jnp.concatenate([gw_a_big[1:, :, :LANE], gw_a_f[None]], axis=0)
    sib_big, sib_small = _pair_exchange("reduce_a_w_in_pair", [gw_a_big, gw_a_small],
                                        sequencer_id=3)
    r_b_out, r_b_in, r_kv, r_a_out = _after([r_b_out, r_b_in, r_kv, r_a_out], gw_a_big)
    up_b_out = _shard_update("update_b_w_out", r_b_out, b_w_out[0], m_b_w_out[0], v_b_w_out[0])
    up_b_in = _shard_update("update_b_w_in", r_b_in, b_w_in[0], m_b_w_in[0], v_b_w_in[0])
    up_kv = _shard_update("update_kv_w", r_kv, kv_w, m_kv_w, v_kv_w)
    up_a_out = _shard_update("update_a_w_out", r_a_out, a_w_out[0], m_a_w_out[0], v_a_w_out[0])
    sib_big, sib_small = _after((sib_big, sib_small),
                                (up_b_out[0], up_b_in[0], up_kv[0], up_a_out[0]))
    chip_big, chip_small = _pair_add("reduce_a_w_in_add", [gw_a_big, gw_a_small],
                                     [sib_big, sib_small])
    r_a_big, r_a_small = _chip_exchange("reduce_a_w_in_chips", [chip_big, chip_small],
                                        sequencer_id=4)
    df = _after(df, (chip_big, chip_small))
    dh0_f = _mm_plain("proj0_f_bwd", df, w_a_f, nt=True, out_dtype=F32)
    dh0 = _mm_plain("proj0_bwd", dproj0, w_a_main, nt=True, n_cols=d, out_dtype=F32, init=dh0_f,
                    tk=sup, b_spec=lambda tk: pl.BlockSpec((None, _tile(d, 1024), tk),
                                                           lambda i, j, k: (k, j, 0)))
    grad_x, sums0 = _norm_bwd("norm0_bwd", x0, dx1, dh0, g0, scale0)

    dmod = jnp.stack([jnp.concatenate([sums0[0], sums0[1], sums1[4]]),
                      jnp.concatenate([sums1[0], sums1[1], sums_f[1]])])
    small_shapes = [(2, 3 * d), (2, d), (1, heads), (d,), (1, heads), (d,), (1,)]
    small_grads = [dmod, jnp.stack([sums0[2], sums1[2]]), sums_bf[0:1, :heads], sums1[3],
                   dsinks[0:1, :heads], sums_f[0], loss_part[0, 0:1]]
    (small_all,) = _all_gather("gather_small", [_pack(small_grads)])
    zero = jnp.zeros((1,), F32)
    small = _small_update(
        small_all,
        _pack([ada_b, norm_g, a_b_f, kv_norm_g, b_sinks, final_norm_g, zero]),
        _pack([m_ada_b, m_norm_g, m_a_b_f, m_kv_norm_g, m_b_sinks, m_final_norm_g, zero]),
        _pack([v_ada_b, v_norm_g, v_a_b_f, v_kv_norm_g, v_b_sinks, v_final_norm_g, zero]))
    s_grad, s_delta, s_m, s_v = [_unpack(p, small_shapes) for p in small]
    loss = s_grad[6][0]

    r_a_big, r_a_small = _after((r_a_big, r_a_small), small[0])
    ga_big, ga_small = _slab_sum("sum_a_w_in", [r_a_big, r_a_small])
    ga_shard = lax.dynamic_slice_in_dim(jnp.concatenate([ga_big, ga_small], axis=1),
                                        extra * me, sup + extra, axis=1)
    up_a_in = _shard_update("update_a_w_in", ga_shard[None], *a_in_decayed, decayed=True)

    dmod_all = small_all.reshape(N_DEV, -1)[:, :2 * 3 * d].reshape(N_DEV, 2, 3 * d)
    dmod_loc = lax.dynamic_slice_in_dim(dmod_all, me * ada_cols, ada_cols, axis=2)
    dmod_loc = jnp.pad(jnp.transpose(dmod_loc, (1, 0, 2)), ((0, 0), (0, LANE - N_DEV), (0, 0)))
    sc_t = jnp.pad(jnp.transpose(sc_rows[0, :N_DEV]), ((0, 0), (0, LANE - N_DEV)))
    up_ada = _ada_update(sc_t.astype(BF16), _after(dmod_loc.astype(BF16), up_a_in[0]),
                         ada_w, m_ada_w, v_ada_w)

    lead = lambda a: a[None]
    per_kind = []
    for kind in range(4):
        sm = (s_grad, s_delta, s_m, s_v)[kind]
        per_kind.append([
            sm[1], up_ada[kind], sm[0], lead(up_a_in[kind]), sm[2], lead(up_a_out[kind]),
            sm[3], up_kv[kind], lead(up_b_in[kind]), sm[4], lead(up_b_out[kind]), sm[5]])
    return (loss, grad_x[None], *per_kind[0], *per_kind[1], *per_kind[2], *per_kind[3])
```

```python
import jax
import jax.numpy as jnp
from jax import lax
from jax.experimental import pallas as pl
from jax.experimental.pallas import tpu as pltpu
from jax.experimental.pallas import tpu_sc as plsc

F32 = jnp.float32
BF16 = jnp.bfloat16
LANE = 128
SUBLANE = 8
HEAD_DIM = 128
SWA_BLOCK = 128
FOX_QUERY_BLOCKS = 8
N_DEV = 8
N_PEER = N_DEV - 1
RMS_EPS = 1e-6
ROPE_THETA = 10000.0
NEG = -1e30
VMEM_LIMIT = 56 * 2 ** 20
MM_RESERVE = 10 * 2 ** 20
MESH = pl.DeviceIdType.MESH
HIGHEST = lax.Precision.HIGHEST

ADAM_LR = 0.001
ADAM_B1 = 0.9
ADAM_B2 = 0.999
ADAM_EPS = 1e-08
ADAM_WD = 0.01
ADAM_STEP = 10


def _tile(dim, pref):
    return pref if dim % pref == 0 else dim


def _params(n_axes):
    return pltpu.CompilerParams(dimension_semantics=("arbitrary",) * n_axes,
                                vmem_limit_bytes=VMEM_LIMIT)


def _dot(a, b):
    return jnp.dot(a, b, preferred_element_type=F32)


def _dot_nt(a, b):
    return lax.dot_general(a, b, (((1,), (1,)), ((), ())), preferred_element_type=F32)


def _dot_tn(a, b):
    return lax.dot_general(a, b, (((0,), (0,)), ((), ())), preferred_element_type=F32)


def _sigmoid(z):
    return 1.0 / (1.0 + jnp.exp(-z))


def _iota(shape, dim):
    return lax.broadcasted_iota(jnp.int32, shape, dim)


def _pick_lane(block, lane_index):
    lane = _iota(block.shape, 1)
    return jnp.sum(jnp.where(lane == lane_index, block, 0.0), axis=1, keepdims=True)


def _adamw_decayed(w_decay, g, m_decayed, v_decayed):
    m = m_decayed + (1.0 - ADAM_B1) * g
    v = v_decayed + (1.0 - ADAM_B2) * (g * g)
    m_hat = m / (1.0 - ADAM_B1 ** ADAM_STEP)
    v_hat = v / (1.0 - ADAM_B2 ** ADAM_STEP)
    delta = -ADAM_LR * (m_hat / (jnp.sqrt(v_hat) + ADAM_EPS) + w_decay)
    return delta, m, v


def _adamw(w, g, m, v):
    return _adamw_decayed(ADAM_WD * w, g, ADAM_B1 * m, ADAM_B2 * v)


def _mesh_pos():
    return lax.axis_index("x"), lax.axis_index("y"), lax.axis_index("c")


def _slot(pos):
    return 4 * pos[0] + 2 * pos[1] + pos[2]


def _handshake(peers):
    barrier = pltpu.get_barrier_semaphore()
    for peer in peers:
        pl.semaphore_signal(barrier, inc=1, device_id=peer, device_id_type=MESH)
    pl.semaphore_wait(barrier, len(peers))


def _launch(name, body, arrays, out_shape, sequencer_id):
    n = len(arrays)
    scratch = [pltpu.SemaphoreType.DMA((N_PEER * n,)), pltpu.SemaphoreType.DMA((N_PEER * n,)),
               pltpu.SemaphoreType.DMA((n,))]
    if sequencer_id is None:
        any_spec = pl.BlockSpec(memory_space=pl.ANY)
        return pl.pallas_call(body, name=name, out_shape=out_shape, in_specs=[any_spec] * n,
                              out_specs=[any_spec] * n, scratch_shapes=scratch)(*arrays)
    return pl.kernel(body, name=name, out_type=out_shape,
                     mesh=plsc.ScalarSubcoreMesh(axis_name="sequencer", num_cores=1),
                     scratch_types=scratch,
                     compiler_params=pltpu.CompilerParams(collective_id=sequencer_id))(*arrays)


def _all_gather(name, arrays, sequencer_id=None):
    n = len(arrays)

    def body(*refs):
        ins, outs = refs[:n], refs[n:2 * n]
        send_sems, recv_sems, local_sems = refs[2 * n:]
        x, y, c = _mesh_pos()
        me, sibling = (x, y, c), (x, y, 1 - c)
        chips = [(1 - x, y), (x, 1 - y), (1 - x, 1 - y)]
        if sequencer_id is not None:
            _handshake([sibling] + [(*chip, c) for chip in chips])

        def copy(a, k, block, to, src=None):
            dst = outs[a].at[_slot(block)]
            return pltpu.make_async_remote_copy(
                src_ref=dst if src is None else src, dst_ref=dst,
                send_sem=send_sems.at[N_PEER * a + k], recv_sem=recv_sems.at[N_PEER * a + k],
                device_id=to, device_id_type=MESH)

        local, first, passed = [], [], []
        for a in range(n):
            cp = pltpu.make_async_copy(ins[a], outs[a].at[_slot(me)], local_sems.at[a])
            cp.start()
            local.append(cp)
            sends = [copy(a, 0, me, sibling, src=ins[a])]
            sends += [copy(a, 1 + j, me, (*chip, c), src=ins[a]) for j, chip in enumerate(chips)]
            for cp in sends:
                cp.start()
            first += sends
        for a in range(n):
            for j, chip in enumerate(chips):
                copy(a, 1 + j, (*chip, c), me).wait_recv()
                cp = copy(a, 4 + j, (*chip, c), sibling)
                cp.start()
                passed.append(cp)
        for a in range(n):
            copy(a, 0, sibling, me).wait_recv()
            for j, chip in enumerate(chips):
                copy(a, 4 + j, (*chip, 1 - c), me).wait_recv()
        for cp in first + passed:
            cp.wait_send()
        for cp in local:
            cp.wait()

    out_shape = [jax.ShapeDtypeStruct((N_DEV,) + a.shape, a.dtype) for a in arrays]
    return _launch(name, body, arrays, out_shape, sequencer_id)


def _all_to_all(name, arrays, sequencer_id=None):
    n = len(arrays)

    def body(*refs):
        ins, outs = refs[:n], refs[n:2 * n]
        send_sems, recv_sems, local_sems = refs[2 * n:]
        x, y, c = _mesh_pos()
        me = _slot((x, y, c))
        if sequencer_id is not None:
            _handshake([(1 - x if k & 4 else x, 1 - y if k & 2 else y, 1 - c if k & 1 else c)
                        for k in range(1, N_DEV)])
        local, sends, recvs = [], [], []
        for a in range(n):
            cp = pltpu.make_async_copy(ins[a].at[me], outs[a].at[me], local_sems.at[a])
            cp.start()
            local.append(cp)
        for k in range(1, N_DEV):
            peer = (1 - x if k & 4 else x, 1 - y if k & 2 else y, 1 - c if k & 1 else c)
            ps = _slot(peer)
            for a in range(n):
                sem = N_PEER * a + k - 1
                cp = pltpu.make_async_remote_copy(
                    src_ref=ins[a].at[ps], dst_ref=outs[a].at[me],
                    send_sem=send_sems.at[sem], recv_sem=recv_sems.at[sem],
                    device_id=peer, device_id_type=MESH)
                cp.start()
                sends.append(cp)
                recvs.append(pltpu.make_async_remote_copy(
                    src_ref=ins[a].at[ps], dst_ref=outs[a].at[ps],
                    send_sem=send_sems.at[sem], recv_sem=recv_sems.at[sem],
                    device_id=peer, device_id_type=MESH))
        for cp in recvs:
            cp.wait_recv()
        for cp in sends:
            cp.wait_send()
        for cp in local:
            cp.wait()

    out_shape = [jax.ShapeDtypeStruct(a.shape, a.dtype) for a in arrays]
    return _launch(name, body, arrays, out_shape, sequencer_id)


def _pair_exchange(name, arrays, sequencer_id=None):
    n = len(arrays)
    chips = N_DEV // 2

    def body(*refs):
        ins, outs = refs[:n], refs[n:2 * n]
        send_sems, recv_sems = refs[2 * n], refs[2 * n + 1]
        x, y, c = _mesh_pos()
        sibling = (x, y, 1 - c)
        if sequencer_id is not None:
            _handshake([sibling])
        copies = [pltpu.make_async_remote_copy(
            src_ref=ins[a].at[2 * q + 1 - c], dst_ref=outs[a].at[q],
            send_sem=send_sems.at[chips * a + q], recv_sem=recv_sems.at[chips * a + q],
            device_id=sibling, device_id_type=MESH) for a in range(n) for q in range(chips)]
        for cp in copies:
            cp.start()
        for cp in copies:
            cp.wait()

    out_shape = [jax.ShapeDtypeStruct((chips,) + a.shape[1:], a.dtype) for a in arrays]
    return _launch(name, body, arrays, out_shape, sequencer_id)


def _chip_exchange(name, arrays, sequencer_id=None):
    n = len(arrays)
    chips = N_DEV // 2

    def body(*refs):
        ins, outs = refs[:n], refs[n:2 * n]
        send_sems, recv_sems, local_sems = refs[2 * n:]
        x, y, c = _mesh_pos()
        mine = 2 * x + y
        others = [(1 - x, y), (x, 1 - y), (1 - x, 1 - y)]
        if sequencer_id is not None:
            _handshake([(*chip, c) for chip in others])
        local = [pltpu.make_async_copy(ins[a].at[mine], outs[a].at[mine], local_sems.at[a])
                 for a in range(n)]
        for cp in local:
            cp.start()
        sends, recvs = [], []
        for j, chip in enumerate(others):
            theirs = 2 * chip[0] + chip[1]
            for a in range(n):
                both = dict(send_sem=send_sems.at[3 * a + j], recv_sem=recv_sems.at[3 * a + j],
                            device_id=(*chip, c), device_id_type=MESH)
                sends.append(pltpu.make_async_remote_copy(
                    src_ref=ins[a].at[theirs], dst_ref=outs[a].at[mine], **both))
                recvs.append(pltpu.make_async_remote_copy(
                    src_ref=ins[a].at[theirs], dst_ref=outs[a].at[theirs], **both))
        for cp in sends:
            cp.start()
        for cp in recvs:
            cp.wait_recv()
        for cp in sends:
            cp.wait_send()
        for cp in local:
            cp.wait()

    out_shape = [jax.ShapeDtypeStruct(a.shape, a.dtype) for a in arrays]
    return _launch(name, body, arrays, out_shape, sequencer_id)


def _pair_add(name, mine, theirs):
    chips, rows = theirs[0].shape[0], theirs[0].shape[1]
    per_row = sum(2 * 3 * a.shape[2] * a.dtype.itemsize for a in theirs)
    tr = _row_tile(rows, per_row)
    n = len(theirs)
    core = lax.axis_index("c").astype(jnp.int32).reshape(1)

    def body(core_ref, *refs):
        for a in range(n):
            refs[2 * n + a][...] = (refs[a][...].astype(F32)
                                    + refs[n + a][...].astype(F32)).astype(refs[2 * n + a].dtype)

    blk = lambda a, fn: pl.BlockSpec((None, tr, a.shape[2]), fn)
    grid_spec = pltpu.PrefetchScalarGridSpec(
        num_scalar_prefetch=1, grid=(chips, rows // tr),
        in_specs=[blk(a, lambda q, i, core_ref: (2 * q + core_ref[0], i, 0)) for a in mine]
        + [blk(a, lambda q, i, core_ref: (q, i, 0)) for a in theirs],
        out_specs=[blk(a, lambda q, i, core_ref: (q, i, 0)) for a in theirs])
    return pl.pallas_call(
        body, name=name, grid_spec=grid_spec,
        out_shape=[jax.ShapeDtypeStruct(a.shape, a.dtype) for a in theirs],
        compiler_params=_params(2))(core, *mine, *theirs)


def _after(value, token):
    return lax.optimization_barrier((value, token))[0]


def _k_tile(k_dim, tm, tn, fixed_bytes):
    budget = VMEM_LIMIT - MM_RESERVE - fixed_bytes
    tk = k_dim
    while tk % 2 == 0 and tk > 512 and (
            4 * (tm + tn) * tk + (4 * tm * tn if tk < k_dim else 0) > budget):
        tk //= 2
    return tk


def _matmul(name, a, b, *, nt, tm, tn, n_cols, out_shape, out_specs, epilogue,
            fixed_bytes, ta=False, tk=None, b_spec=None, extra=(), extra_specs=()):
    assert not (ta and nt)
    k_dim, m_rows = a.shape if ta else a.shape[::-1]
    tm, tn = _tile(m_rows, tm), _tile(n_cols, tn)
    tk = _k_tile(k_dim, tm, tn, fixed_bytes) if tk is None else _tile(k_dim, tk)
    grid = (m_rows // tm, n_cols // tn, k_dim // tk)
    n_k = grid[2]
    if ta:
        a_spec = pl.BlockSpec((tk, tm), lambda i, j, k: (k, i))
    else:
        a_spec = pl.BlockSpec((tm, tk), lambda i, j, k: (i, k))
    if b_spec is not None:
        b_blk = b_spec(tk)
    elif nt:
        b_blk = pl.BlockSpec((tn, tk), lambda i, j, k: (j, k))
    else:
        b_blk = pl.BlockSpec((tk, tn), lambda i, j, k: (k, j))
    n_extra, n_out = len(extra), len(out_shape)
    product = _dot_tn if ta else _dot_nt if nt else _dot

    def body(a_ref, b_ref, *rest):
        extra_refs = rest[:n_extra]
        out_refs = rest[n_extra:n_extra + n_out]
        if n_k == 1:
            epilogue(product(a_ref[...], b_ref[...]), extra_refs, out_refs)
            return
        acc_ref = rest[n_extra + n_out]
        k = pl.program_id(2)

        @pl.when(k == 0)
        def _():
            acc_ref[...] = jnp.zeros_like(acc_ref)

        acc_ref[...] += product(a_ref[...], b_ref[...])

        @pl.when(k == n_k - 1)
        def _():
            epilogue(acc_ref[...], extra_refs, out_refs)

    return pl.pallas_call(
        body, name=name, grid=grid,
        in_specs=[a_spec, b_blk, *extra_specs], out_specs=out_specs, out_shape=out_shape,
        scratch_shapes=[pltpu.VMEM((tm, tn), F32)] if n_k > 1 else [],
        compiler_params=_params(3),
    )(a, b, *extra)


def _mm_plain(name, a, b, *, nt=False, ta=False, n_cols=None, out_dtype=BF16, init=None,
              tm=1024, tn=1024, tk=None, b_spec=None, out_3d=None):
    m_rows = a.shape[1] if ta else a.shape[0]
    if n_cols is None:
        n_cols = b.shape[0] if nt else b.shape[1]
    tm, tn = _tile(m_rows, tm), _tile(n_cols, tn)
    fixed = 2 * tm * tn * (jnp.dtype(out_dtype).itemsize + (4 if init is not None else 0))
    if out_3d is None:
        shape = jax.ShapeDtypeStruct((m_rows, n_cols), out_dtype)
        spec = pl.BlockSpec((tm, tn), lambda i, j, k: (i, j))
    else:
        slabs, width = out_3d
        assert width % tn == 0 and slabs * width == n_cols
        per = width // tn
        shape = jax.ShapeDtypeStruct((slabs, m_rows, width), out_dtype)
        spec = pl.BlockSpec((None, tm, tn), lambda i, j, k: (j // per, i, j % per))
    extra, extra_specs = (), ()
    if init is not None:
        extra = (init,)
        extra_specs = (pl.BlockSpec((tm, tn), lambda i, j, k: (i, j)),)

    def epilogue(acc, extra_refs, out_refs):
        if init is not None:
            acc = acc + extra_refs[0][...]
        out_refs[0][...] = acc.astype(out_dtype)

    (out,) = _matmul(name, a, b, nt=nt, ta=ta, tm=tm, tn=tn, tk=tk, n_cols=n_cols,
                     out_shape=[shape], out_specs=[spec], epilogue=epilogue, fixed_bytes=fixed,
                     b_spec=b_spec, extra=extra, extra_specs=extra_specs)
    return out


def _mm_rope(name, a, b, cos, sin, *, n_cols, rope_cols, tn, b_spec=None):
    m_rows = a.shape[0]
    tm = _tile(m_rows, 1024)
    tn = _tile(n_cols, tn)
    assert rope_cols % tn == 0 and tn % HEAD_DIM == 0
    rope_blocks = rope_cols // tn
    table_spec = pl.BlockSpec((tm, LANE), lambda i, j, k: (i, 0))

    def epilogue(acc, extra_refs, out_refs):
        cos_ref, sin_ref = extra_refs
        j = pl.program_id(1)

        @pl.when(j < rope_blocks)
        def _():
            for head in range(tn // HEAD_DIM):
                cols = slice(head * HEAD_DIM, (head + 1) * HEAD_DIM)
                blk = acc[:, cols]
                rot = pltpu.roll(blk, HEAD_DIM // 2, 1)
                out_refs[0][:, cols] = (blk * cos_ref[...] + rot * sin_ref[...]).astype(BF16)

        @pl.when(j >= rope_blocks)
        def _():
            out_refs[0][...] = acc.astype(BF16)

    (out,) = _matmul(name, a, b, nt=False, tm=tm, tn=tn, n_cols=n_cols,
                     out_shape=[jax.ShapeDtypeStruct((m_rows, n_cols), BF16)],
                     out_specs=[pl.BlockSpec((tm, tn), lambda i, j, k: (i, j))],
                     epilogue=epilogue, fixed_bytes=4 * tm * tn + 16 * tm * LANE, b_spec=b_spec,
                     extra=(cos, sin), extra_specs=(table_spec, table_spec))
    return out


def _mm_residual(name, u, w, x_in, gate):
    m_rows, n_cols = x_in.shape
    tm, tn = _tile(m_rows, 512), _tile(n_cols, 1024)
    blk = pl.BlockSpec((tm, tn), lambda i, j, k: (i, j))

    def epilogue(acc, extra_refs, out_refs):
        x_ref, gate_ref = extra_refs
        out_refs[0][...] = acc
        out_refs[1][...] = x_ref[...] + gate_ref[...] * acc

    y, x_out = _matmul(
        name, u, w, nt=False, tm=tm, tn=tn, n_cols=n_cols,
        out_shape=[jax.ShapeDtypeStruct((m_rows, n_cols), F32)] * 2, out_specs=[blk, blk],
        epilogue=epilogue, fixed_bytes=3 * 8 * tm * tn, extra=(x_in, gate),
        extra_specs=(blk, pl.BlockSpec((1, tn), lambda i, j, k: (0, j))))
    return y, x_out


def _mm_gate_bwd(name, dy, w_out, z_src, z_col0, o):
    m_rows = dy.shape[0]
    n_cols = w_out.shape[0]
    tm, tn = _tile(m_rows, 1024), _tile(n_cols, 1024)
    assert z_col0 % tn == 0 and tn % HEAD_DIM == 0 and n_cols // HEAD_DIM <= LANE
    z_blk0 = z_col0 // tn
    blk = pl.BlockSpec((tm, tn), lambda i, j, k: (i, j))

    def epilogue(du, extra_refs, out_refs):
        z_ref, o_ref = extra_refs
        do_ref, dz_ref, delta_ref = out_refs
        j = pl.program_id(1)
        z = z_ref[...].astype(F32)
        o_val = o_ref[...].astype(F32)
        sig = _sigmoid(z)
        d_o = (du * (z * sig)).astype(BF16)
        do_ref[...] = d_o
        dz_ref[...] = (du * o_val * (sig * (1.0 + z * (1.0 - sig)))).astype(BF16)

        @pl.when(j == 0)
        def _():
            delta_ref[...] = jnp.zeros_like(delta_ref)

        prod = d_o.astype(F32) * o_val
        lane = _iota((tm, LANE), 1)
        delta = delta_ref[...]
        for head in range(tn // HEAD_DIM):
            rows = jnp.sum(prod[:, head * HEAD_DIM:(head + 1) * HEAD_DIM], axis=1, keepdims=True)
            delta = delta + jnp.where(lane == j * (tn // HEAD_DIM) + head, rows, 0.0)
        delta_ref[...] = delta

    d_o, dz, delta = _matmul(
        name, dy, w_out, nt=True, tm=tm, tn=tn, n_cols=n_cols,
        out_shape=[jax.ShapeDtypeStruct((m_rows, n_cols), BF16)] * 2
        + [jax.ShapeDtypeStruct((m_rows, LANE), F32)],
        out_specs=[blk, blk, pl.BlockSpec((tm, LANE), lambda i, j, k: (i, 0))],
        epilogue=epilogue, fixed_bytes=4 * 4 * tm * tn + 8 * tm * LANE, extra=(z_src, o),
        extra_specs=(pl.BlockSpec((tm, tn), lambda i, j, k: (i, z_blk0 + j)), blk))
    return d_o, dz, delta


def _norm_fwd(name, x, ga, sa, ta, gb=None):
    s_len, d = x.shape
    tr = _tile(s_len, 256)
    two = gb is not None
    row = pl.BlockSpec((tr, d), lambda i: (i, 0))
    vec = pl.BlockSpec((1, d), lambda i: (0, 0))

    def body(x_ref, ga_ref, sa_ref, ta_ref, *rest):
        xv = x_ref[...]
        y = xv * lax.rsqrt(jnp.mean(xv * xv, axis=-1, keepdims=True) + RMS_EPS)
        rest[-2 if two else -1][...] = ((y * ga_ref[...]) * (1.0 + sa_ref[...]) + ta_ref[...]).astype(BF16)
        if two:
            rest[-1][...] = (y * rest[0][...]).astype(BF16)

    ins = [x, ga, sa, ta] + ([gb] if two else [])
    outs = pl.pallas_call(
        body, name=name, grid=(s_len // tr,),
        in_specs=[row] + [vec] * (len(ins) - 1),
        out_specs=[row] * (2 if two else 1),
        out_shape=[jax.ShapeDtypeStruct((s_len, d), BF16)] * (2 if two else 1),
        compiler_params=_params(1))(*ins)
    return outs if two else outs[0]


def _loss_bwd(x2, target, y1, g_final, gate1):
    s_len, d = x2.shape
    tr = _tile(s_len, 128)
    row = pl.BlockSpec((tr, d), lambda i: (i, 0))
    vec = pl.BlockSpec((1, d), lambda i: (0, 0))

    def body(x_ref, t_ref, y_ref, g_ref, gate_ref, loss_ref, dx_ref, dy_ref, sums_ref):
        @pl.when(pl.program_id(0) == 0)
        def _():
            loss_ref[...] = jnp.zeros_like(loss_ref)
            sums_ref[...] = jnp.zeros_like(sums_ref)

        xv = x_ref[...]
        rstd = lax.rsqrt(jnp.mean(xv * xv, axis=-1, keepdims=True) + RMS_EPS)
        xhat = xv * rstd
        g = g_ref[...]
        err = xhat * g - t_ref[...]
        sq = jnp.sum(jnp.sum(err * err, axis=1, keepdims=True), axis=0, keepdims=True)
        loss_ref[...] += sq * (0.5 / d)
        dout = err * (1.0 / d)
        dxhat = dout * g
        dx = rstd * (dxhat - xhat * jnp.mean(dxhat * xhat, axis=-1, keepdims=True))
        dx_ref[...] = dx
        dy_ref[...] = (dx * gate_ref[...]).astype(BF16)
        sums_ref[0:1, :] += jnp.sum(dout * xhat, axis=0, keepdims=True)
        sums_ref[1:2, :] += jnp.sum(dx * y_ref[...], axis=0, keepdims=True)

    return pl.pallas_call(
        body, name="loss_bwd", grid=(s_len // tr,),
        in_specs=[row, row, row, vec, vec],
        out_specs=[pl.BlockSpec((SUBLANE, LANE), lambda i: (0, 0)), row, row,
                   pl.BlockSpec((SUBLANE, d), lambda i: (0, 0))],
        out_shape=[jax.ShapeDtypeStruct((SUBLANE, LANE), F32),
                   jax.ShapeDtypeStruct((s_len, d), F32),
                   jax.ShapeDtypeStruct((s_len, d), BF16),
                   jax.ShapeDtypeStruct((SUBLANE, d), F32)],
        compiler_params=_params(1))(x2, target, y1, g_final, gate1)


def _norm_bwd(name, x, dres, dha, ga, sa, dhb=None, gb=None, y=None, gate=None):
    s_len, d = x.shape
    tr = _tile(s_len, 128)
    has_b, has_y = dhb is not None, y is not None
    row = pl.BlockSpec((tr, d), lambda i: (i, 0))
    vec = pl.BlockSpec((1, d), lambda i: (0, 0))
    ins, specs = [x, dres, dha, ga, sa], [row, row, row, vec, vec]
    if has_b:
        ins += [dhb, gb]
        specs += [row, vec]
    if has_y:
        ins += [y, gate]
        specs += [row, vec]
    n_in = len(ins)

    def body(*refs):
        x_ref, dres_ref, dha_ref, ga_ref, sa_ref = refs[:5]
        pos = 5
        if has_b:
            dhb_ref, gb_ref = refs[pos:pos + 2]
            pos += 2
        if has_y:
            y_ref, gate_ref = refs[pos:pos + 2]
        outs = refs[n_in:]
        dx_ref, sums_ref = outs[0], outs[-1]

        @pl.when(pl.program_id(0) == 0)
        def _():
            sums_ref[...] = jnp.zeros_like(sums_ref)

        xv = x_ref[...]
        rstd = lax.rsqrt(jnp.mean(xv * xv, axis=-1, keepdims=True) + RMS_EPS)
        xhat = xv * rstd
        dha_v = dha_ref[...]
        ga_v, sa_v = ga_ref[...], sa_ref[...]
        dxhat = dha_v * (ga_v * (1.0 + sa_v))
        sums_ref[0:1, :] += jnp.sum(dha_v, axis=0, keepdims=True)
        sums_ref[1:2, :] += jnp.sum(dha_v * (xhat * ga_v), axis=0, keepdims=True)
        sums_ref[2:3, :] += jnp.sum(dha_v * ((1.0 + sa_v) * xhat), axis=0, keepdims=True)
        if has_b:
            dhb_v = dhb_ref[...]
            dxhat = dxhat + dhb_v * gb_ref[...]
            sums_ref[3:4, :] += jnp.sum(dhb_v * xhat, axis=0, keepdims=True)
        dx = dres_ref[...] + rstd * (dxhat - xhat * jnp.mean(dxhat * xhat, axis=-1, keepdims=True))
        dx_ref[...] = dx
        if has_y:
            outs[1][...] = (dx * gate_ref[...]).astype(BF16)
            sums_ref[4:5, :] += jnp.sum(dx * y_ref[...], axis=0, keepdims=True)

    out_shape = [jax.ShapeDtypeStruct((s_len, d), F32)]
    out_specs = [row]
    if has_y:
        out_shape.append(jax.ShapeDtypeStruct((s_len, d), BF16))
        out_specs.append(row)
    out_shape.append(jax.ShapeDtypeStruct((SUBLANE, d), F32))
    out_specs.append(pl.BlockSpec((SUBLANE, d), lambda i: (0, 0)))
    return pl.pallas_call(body, name=name, grid=(s_len // tr,), in_specs=specs,
                          out_specs=out_specs, out_shape=out_shape,
                          compiler_params=_params(1))(*ins)


def _fgate_fwd(f_raw, bias_row):
    s_len = f_raw.shape[0]
    tb = _tile(s_len, 512)
    blk = pl.BlockSpec((tb, LANE), lambda t: (t, 0))

    def body(f_ref, b_ref, out_ref, carry):
        @pl.when(pl.program_id(0) == 0)
        def _():
            carry[...] = jnp.zeros_like(carry)

        u = f_ref[...] + b_ref[...]
        logf = jnp.minimum(u, 0.0) - jnp.log1p(jnp.exp(-jnp.abs(u)))
        tri = (_iota((tb, tb), 1) <= _iota((tb, tb), 0)).astype(F32)
        run = jnp.dot(tri, logf, precision=HIGHEST, preferred_element_type=F32) + carry[...]
        out_ref[...] = run
        carry[...] = run[tb - 1:tb, :]

    return pl.pallas_call(
        body, name="fgate_fwd", grid=(s_len // tb,),
        in_specs=[blk, pl.BlockSpec((1, LANE), lambda t: (0, 0))], out_specs=blk,
        out_shape=jax.ShapeDtypeStruct((s_len, LANE), F32),
        scratch_shapes=[pltpu.VMEM((1, LANE), F32)],
        compiler_params=_params(1))(f_raw, bias_row)


def _fgate_bwd(df_a, df_b, f_raw, bias_row):
    s_len = f_raw.shape[0]
    tb = _tile(s_len, 512)
    nb = s_len // tb
    blk = pl.BlockSpec((tb, LANE), lambda t: (nb - 1 - t, 0))

    def body(a_ref, b2_ref, f_ref, b_ref, df_ref, sums_ref, carry):
        @pl.when(pl.program_id(0) == 0)
        def _():
            carry[...] = jnp.zeros_like(carry)
            sums_ref[...] = jnp.zeros_like(sums_ref)

        d_run = a_ref[...] + b2_ref[...]
        tri = (_iota((tb, tb), 1) >= _iota((tb, tb), 0)).astype(F32)
        dlogf = jnp.dot(tri, d_run, precision=HIGHEST, preferred_element_type=F32) + carry[...]
        carry[...] = dlogf[0:1, :]
        u = f_ref[...] + b_ref[...]
        df = dlogf * _sigmoid(-u)
        df_ref[...] = df.astype(BF16)
        sums_ref[...] += jnp.sum(df, axis=0, keepdims=True)

    return pl.pallas_call(
        body, name="fgate_bwd", grid=(nb,),
        in_specs=[blk, blk, blk, pl.BlockSpec((1, LANE), lambda t: (0, 0))],
        out_specs=[blk, pl.BlockSpec((SUBLANE, LANE), lambda t: (0, 0))],
        out_shape=[jax.ShapeDtypeStruct((s_len, LANE), BF16),
                   jax.ShapeDtypeStruct((SUBLANE, LANE), F32)],
        scratch_shapes=[pltpu.VMEM((1, LANE), F32)],
        compiler_params=_params(1))(df_a, df_b, f_raw, bias_row)


def _fox_fwd(proj, f_nat, f_t, heads):
    s_len = proj.shape[0]
    d = heads * HEAD_DIM
    t = _tile(s_len, 512)
    nq = s_len // t
    scale = HEAD_DIM ** -0.5

    pair = FOX_QUERY_BLOCKS if nq % FOX_QUERY_BLOCKS == 0 else 2 if nq % 2 == 0 else 1

    def body(k_ref, q_ref, v_ref, fn_ref, ft_ref, ot_ref, lse_ref,
             acc_scr, m_scr, l_scr, fk_scr):
        j, p = pl.program_id(1), pl.program_id(2)
        h = pl.program_id(0)
        first = j // pair

        @pl.when((j == 0) & (p == 0))
        def _():
            m_scr[...] = jnp.full_like(m_scr, NEG)
            l_scr[...] = jnp.zeros_like(l_scr)
            acc_scr[...] = jnp.zeros_like(acc_scr)

        @pl.when(p == first)
        def _():
            fk_scr[...] = _pick_lane(fn_ref[...], h)

        def update(slot, diagonal):
            i = pair * p + slot
            rows = slice(slot * t, (slot + 1) * t)
            s_t = (_dot_nt(k_ref[...], q_ref[rows, :]) * scale
                   + (ft_ref[:, rows] - fk_scr[...]))
            if diagonal:
                s_t = jnp.where(_iota((t, t), 0) <= _iota((t, t), 1), s_t, NEG)
            m_prev = m_scr[i]
            m_new = jnp.maximum(m_prev, jnp.max(s_t, axis=0, keepdims=True))
            alpha = jnp.exp(m_prev - m_new)
            p_t = jnp.exp(s_t - m_new)
            l_scr[i] = alpha * l_scr[i] + jnp.sum(p_t, axis=0, keepdims=True)
            acc_scr[i] = alpha * acc_scr[i] + _dot_tn(v_ref[...], p_t.astype(BF16))
            m_scr[i] = m_new

        @pl.when(j < pair * p)
        def _():
            for slot in range(pair):
                update(slot, False)

        for diag_slot in range(pair):
            @pl.when(j == pair * p + diag_slot)
            def _(diag_slot=diag_slot):
                update(diag_slot, True)
                for slot in range(diag_slot + 1, pair):
                    update(slot, False)

        @pl.when((p == nq // pair - 1) & (j == nq - 1))
        def _():
            for blk in range(nq):
                cols = slice(blk * t, (blk + 1) * t)
                l_sum = l_scr[blk]
                ot_ref[:, cols] = acc_scr[blk] / l_sum
                lse_ref[:, cols] = m_scr[blk] + jnp.log(l_sum)

    qry = pl.BlockSpec((pair * t, HEAD_DIM), lambda h, j, p: (jnp.maximum(p, j // pair), h))
    return pl.pallas_call(
        body, name="fox_fwd", grid=(heads, nq, nq // pair),
        in_specs=[pl.BlockSpec((t, HEAD_DIM), lambda h, j, p: (j, heads + h)), qry,
                  pl.BlockSpec((t, HEAD_DIM), lambda h, j, p: (j, 2 * heads + h)),
                  pl.BlockSpec((t, LANE), lambda h, j, p: (j, 0)),
                  pl.BlockSpec((None, 1, pair * t),
                               lambda h, j, p: (h, 0, jnp.maximum(p, j // pair)))],
        out_specs=[pl.BlockSpec((HEAD_DIM, s_len), lambda h, j, i: (h, 0)),
                   pl.BlockSpec((None, 1, s_len), lambda h, j, i: (h, 0, 0))],
        out_shape=[jax.ShapeDtypeStruct((d, s_len), F32),
                   jax.ShapeDtypeStruct((heads, 1, s_len), F32)],
        scratch_shapes=[pltpu.VMEM((nq, HEAD_DIM, t), F32), pltpu.VMEM((nq, 1, t), F32),
                        pltpu.VMEM((nq, 1, t), F32), pltpu.VMEM((t, 1), F32)],
        compiler_params=_params(3))(proj, proj, proj, f_nat, f_t)


def _gate_fwd(o_t, proj, z_blk0, heads):
    d, s_len = o_t.shape
    t = _tile(s_len, 512)

    def body(ot_ref, z_ref, o_ref, u_ref):
        o_val = jnp.transpose(ot_ref[...])
        o_ref[...] = o_val.astype(BF16)
        z = z_ref[...].astype(F32)
        u_ref[...] = (o_val * (z * _sigmoid(z))).astype(BF16)

    out_blk = pl.BlockSpec((t, HEAD_DIM), lambda i, h: (i, h))
    return pl.pallas_call(
        body, name="gate_fwd", grid=(s_len // t, heads),
        in_specs=[pl.BlockSpec((HEAD_DIM, t), lambda i, h: (h, i)),
                  pl.BlockSpec((t, HEAD_DIM), lambda i, h: (i, z_blk0 + h))],
        out_specs=[out_blk, out_blk],
        out_shape=[jax.ShapeDtypeStruct((s_len, d), BF16)] * 2,
        compiler_params=_params(2))(o_t, proj)


def _fox_bwd(proj, d_o, f_nat, f_t, lse_t, delta_t, heads):
    s_len = proj.shape[0]
    d = heads * HEAD_DIM
    t = _tile(s_len, 512)
    nq = s_len // t
    pair = FOX_QUERY_BLOCKS if nq % FOX_QUERY_BLOCKS == 0 else 2 if nq % 2 == 0 else 1
    scale = HEAD_DIM ** -0.5

    def body(k_ref, v_ref, q_ref, do_ref, fn_ref, ft_ref, lse_ref, delta_ref,
             dk_ref, dv_ref, dqt_ref, dfq_ref, dfk_ref,
             dk_acc, dv_acc, dq_acc, dfq_acc, dfk_acc, fk_scr):
        h, j, p = pl.program_id(0), pl.program_id(1), pl.program_id(2)
        head_start = (j == 0) & (p == 0)
        last = nq // pair - 1

        @pl.when(head_start)
        def _():
            dq_acc[...] = jnp.zeros_like(dq_acc)
            dfq_acc[...] = jnp.zeros_like(dfq_acc)

        @pl.when(head_start & (h == 0))
        def _():
            dfk_ref[...] = jnp.zeros_like(dfk_ref)

        @pl.when(p == j // pair)
        def _():
            dk_acc[...] = jnp.zeros_like(dk_acc)
            dv_acc[...] = jnp.zeros_like(dv_acc)
            dfk_acc[...] = jnp.zeros_like(dfk_acc)
            fk_scr[...] = _pick_lane(fn_ref[...], h)

        def update(slot, diagonal):
            i = pair * p + slot
            rows = slice(slot * t, (slot + 1) * t)
            q = q_ref[rows, :]
            d_out = do_ref[rows, :]
            s_t = (_dot_nt(k_ref[...], q) * scale + (ft_ref[:, rows] - fk_scr[...])
                   - lse_ref[:, rows])
            if diagonal:
                s_t = jnp.where(_iota((t, t), 0) <= _iota((t, t), 1), s_t, NEG)
            p_t = jnp.exp(s_t)
            dp_t = _dot_nt(v_ref[...], d_out)
            ds_t = p_t * (dp_t - delta_ref[:, rows])
            ds_b = ds_t.astype(BF16)
            dv_acc[...] += _dot(p_t.astype(BF16), d_out)
            dk_acc[...] += _dot(ds_b, q)
            dq_acc[i] += _dot_tn(k_ref[...], ds_b)
            dfq_acc[i] += jnp.sum(ds_t, axis=0, keepdims=True)
            dfk_acc[...] += jnp.sum(ds_t, axis=1, keepdims=True)

        @pl.when(j < pair * p)
        def _():
            for slot in range(pair):
                update(slot, False)

        for diag_slot in range(pair):
            @pl.when(j == pair * p + diag_slot)
            def _(diag_slot=diag_slot):
                update(diag_slot, True)
                for slot in range(diag_slot + 1, pair):
                    update(slot, False)

        @pl.when(p == last)
        def _():
            dk_ref[...] = (dk_acc[...] * scale).astype(BF16)
            dv_ref[...] = dv_acc[...].astype(BF16)
            rows = pl.ds(pl.multiple_of(j * t, t), t)
            dfk_ref[rows, :] += jnp.where(_iota((t, LANE), 1) == h, -dfk_acc[...], 0.0)

        @pl.when((p == last) & (j == nq - 1))
        def _():
            for blk in range(nq):
                cols = slice(blk * t, (blk + 1) * t)
                dqt_ref[:, cols] = (dq_acc[blk] * scale).astype(BF16)
                dfq_ref[:, cols] = dfq_acc[blk]

    key_col = lambda base: pl.BlockSpec((t, HEAD_DIM), lambda h, j, p: (j, base + h))
    qry = pl.BlockSpec((pair * t, HEAD_DIM), lambda h, j, p: (jnp.maximum(p, j // pair), h))
    qry_row = pl.BlockSpec((None, 1, pair * t),
                           lambda h, j, p: (h, 0, jnp.maximum(p, j // pair)))
    kv_out = pl.BlockSpec((t, HEAD_DIM), lambda h, j, i: (j, h))
    return pl.pallas_call(
        body, name="fox_bwd", grid=(heads, nq, nq // pair),
        in_specs=[key_col(heads), key_col(2 * heads),
                  qry, qry, pl.BlockSpec((t, LANE), lambda h, j, i: (j, 0)),
                  qry_row, qry_row, qry_row],
        out_specs=[kv_out, kv_out,
                   pl.BlockSpec((HEAD_DIM, s_len), lambda h, j, i: (h, 0)),
                   pl.BlockSpec((None, 1, s_len), lambda h, j, i: (h, 0, 0)),
                   pl.BlockSpec((s_len, LANE), lambda h, j, i: (0, 0))],
        out_shape=[jax.ShapeDtypeStruct((s_len, d), BF16), jax.ShapeDtypeStruct((s_len, d), BF16),
                   jax.ShapeDtypeStruct((d, s_len), BF16),
                   jax.ShapeDtypeStruct((heads, 1, s_len), F32),
                   jax.ShapeDtypeStruct((s_len, LANE), F32)],
        scratch_shapes=[pltpu.VMEM((t, HEAD_DIM), F32), pltpu.VMEM((t, HEAD_DIM), F32),
                        pltpu.VMEM((nq, HEAD_DIM, t), F32), pltpu.VMEM((nq, 1, t), F32),
                        pltpu.VMEM((t, 1), F32), pltpu.VMEM((t, 1), F32)],
        compiler_params=_params(3))(proj, proj, proj, d_o, f_nat, f_t, lse_t, delta_t)


def _swa_specs(heads, kv_heads):
    width = heads // kv_heads * HEAD_DIM
    wide = lambda base: pl.BlockSpec((SWA_BLOCK, width), lambda n, g: (n, base + g))
    blk = lambda fn: pl.BlockSpec((SWA_BLOCK, HEAD_DIM), fn)
    prev = lambda base: blk(lambda n, g: (jnp.maximum(n - 1, 0), base + g))
    cur = lambda base: blk(lambda n, g: (n, base + g))
    return wide, prev, cur


def _stack_heads(ref, group):
    return jnp.concatenate([ref[:, hh * HEAD_DIM:(hh + 1) * HEAD_DIM] for hh in range(group)], axis=0)


def _head_rows(stacked, hh):
    return stacked[hh * SWA_BLOCK:(hh + 1) * SWA_BLOCK]


def _swa_scores(q, kp, kc, n, scale):
    shape = (q.shape[0], SWA_BLOCK)
    r, c = _iota(shape, 0) & (SWA_BLOCK - 1), _iota(shape, 1)
    sp = jnp.where((c > r) & (n > 0), _dot_nt(q, kp) * scale, NEG)
    sc = jnp.where(c <= r, _dot_nt(q, kc) * scale, NEG)
    return sp, sc


def _per_head_column(values_row, first_head, group):
    head_of_row = _iota((group * SWA_BLOCK, 1), 0) // SWA_BLOCK
    col = jnp.zeros((group * SWA_BLOCK, 1), F32)
    for hh in range(group):
        col = jnp.where(head_of_row == hh, _pick_lane(values_row, first_head + hh), col)
    return col


def _swa_fwd(proj, kv, sinks_row, heads, kv_heads):
    s_len = proj.shape[0]
    d = heads * HEAD_DIM
    scale = HEAD_DIM ** -0.5
    group = heads // kv_heads
    wide, prev, cur = _swa_specs(heads, kv_heads)

    def body(q_ref, z_ref, kp_ref, kc_ref, vp_ref, vc_ref, sink_ref, o_ref, u_ref, lse_ref):
        n, g = pl.program_id(0), pl.program_id(1)
        sp, sc = _swa_scores(_stack_heads(q_ref, group), kp_ref[...], kc_ref[...], n, scale)
        sink = _per_head_column(sink_ref[...], g * group, group)
        m = jnp.maximum(jnp.maximum(jnp.max(sp, axis=1, keepdims=True),
                                    jnp.max(sc, axis=1, keepdims=True)), sink)
        pp, pc = jnp.exp(sp - m), jnp.exp(sc - m)
        den = (jnp.sum(pp, axis=1, keepdims=True) + jnp.sum(pc, axis=1, keepdims=True)
               + jnp.exp(sink - m))
        o_all = (_dot(pp.astype(BF16), vp_ref[...]) + _dot(pc.astype(BF16), vc_ref[...])) / den
        lse = m + jnp.log(den)
        lane = _iota((SWA_BLOCK, LANE), 1)
        lse_all = jnp.zeros((SWA_BLOCK, LANE), F32)
        for hh in range(group):
            cols = slice(hh * HEAD_DIM, (hh + 1) * HEAD_DIM)
            o_val = _head_rows(o_all, hh)
            o_ref[:, cols] = o_val.astype(BF16)
            z = z_ref[:, cols].astype(F32)
            u_ref[:, cols] = (o_val * (z * _sigmoid(z))).astype(BF16)
            lse_all = lse_all + jnp.where(lane == g * group + hh, _head_rows(lse, hh), 0.0)

        @pl.when(g == 0)
        def _():
            lse_ref[...] = lse_all

        @pl.when(g > 0)
        def _():
            lse_ref[...] += lse_all

    nat = pl.BlockSpec((SWA_BLOCK, LANE), lambda n, g: (n, 0))
    return pl.pallas_call(
        body, name="swa_fwd", grid=(s_len // SWA_BLOCK, kv_heads),
        in_specs=[wide(0), wide(kv_heads), prev(0), cur(0), prev(kv_heads), cur(kv_heads),
                  pl.BlockSpec((1, LANE), lambda n, g: (0, 0))],
        out_specs=[wide(0), wide(0), nat],
        out_shape=[jax.ShapeDtypeStruct((s_len, d), BF16), jax.ShapeDtypeStruct((s_len, d), BF16),
                   jax.ShapeDtypeStruct((s_len, LANE), F32)],
        compiler_params=_params(2))(proj, proj, kv, kv, kv, kv, sinks_row)


def _swa_bwd_q(proj, kv, d_o, lse, delta, sinks_row, cos, sin, heads, kv_heads):
    s_len = proj.shape[0]
    d = heads * HEAD_DIM
    scale = HEAD_DIM ** -0.5
    group = heads // kv_heads
    wide, prev, cur = _swa_specs(heads, kv_heads)

    def body(q_ref, kp_ref, kc_ref, vp_ref, vc_ref, do_ref, lse_ref, delta_ref, sink_ref,
             cos_ref, sin_ref, dq_ref, dsink_ref):
        n, g = pl.program_id(0), pl.program_id(1)

        @pl.when((n == 0) & (g == 0))
        def _():
            dsink_ref[...] = jnp.zeros_like(dsink_ref)

        kp, kc = kp_ref[...], kc_ref[...]
        first = g * group
        sp, sc = _swa_scores(_stack_heads(q_ref, group), kp, kc, n, scale)
        lse_blk, delta_blk = lse_ref[...], delta_ref[...]
        lse_col = jnp.concatenate([_pick_lane(lse_blk, first + hh) for hh in range(group)], axis=0)
        delta_col = jnp.concatenate([_pick_lane(delta_blk, first + hh) for hh in range(group)],
                                    axis=0)
        pp, pc = jnp.exp(sp - lse_col), jnp.exp(sc - lse_col)
        p_sink = jnp.exp(_per_head_column(sink_ref[...], first, group) - lse_col)
        d_out = _stack_heads(do_ref, group)
        dsp = pp * (_dot_nt(d_out, vp_ref[...]) - delta_col)
        dsc = pc * (_dot_nt(d_out, vc_ref[...]) - delta_col)
        dq = (_dot(dsp.astype(BF16), kp) + _dot(dsc.astype(BF16), kc)) * scale
        cos_v, sin_v = cos_ref[...], sin_ref[...]
        sink_part = -p_sink * delta_col
        lane = _iota((SUBLANE, LANE), 1)
        dsink_all = jnp.zeros((SUBLANE, LANE), F32)
        for hh in range(group):
            cols = slice(hh * HEAD_DIM, (hh + 1) * HEAD_DIM)
            dq_h = _head_rows(dq, hh)
            dq_ref[:, cols] = (dq_h * cos_v
                               - pltpu.roll(dq_h, HEAD_DIM // 2, 1) * sin_v).astype(BF16)
            d_sink = jnp.sum(_head_rows(sink_part, hh), axis=0, keepdims=True)
            dsink_all = dsink_all + jnp.where(lane == first + hh, d_sink, 0.0)
        dsink_ref[...] += dsink_all

    own = wide(0)
    nat = pl.BlockSpec((SWA_BLOCK, LANE), lambda n, g: (n, 0))
    return pl.pallas_call(
        body, name="swa_bwd_q", grid=(s_len // SWA_BLOCK, kv_heads),
        in_specs=[own, prev(0), cur(0), prev(kv_heads), cur(kv_heads), own, nat, nat,
                  pl.BlockSpec((1, LANE), lambda n, g: (0, 0)), nat, nat],
        out_specs=[own, pl.BlockSpec((SUBLANE, LANE), lambda n, g: (0, 0))],
        out_shape=[jax.ShapeDtypeStruct((s_len, d), BF16),
                   jax.ShapeDtypeStruct((SUBLANE, LANE), F32)],
        compiler_params=_params(2))(proj, kv, kv, kv, kv, d_o, lse, delta, sinks_row, cos, sin)


def _swa_bwd_kv(proj, kv, d_o, lse_t, delta_t, cos, sin, heads, kv_heads):
    s_len = proj.shape[0]
    nb = s_len // SWA_BLOCK
    group = heads // kv_heads
    scale = HEAD_DIM ** -0.5

    def body(k_ref, v_ref, qm_ref, qn_ref, dom_ref, don_ref, lsem_ref, lsen_ref,
             deltam_ref, deltan_ref, cos_ref, sin_ref, dk_ref, dv_ref):
        m = pl.program_id(1)
        k, v = k_ref[...], v_ref[...]
        shape = (SWA_BLOCK, group * SWA_BLOCK)
        key, qry = _iota(shape, 0), _iota(shape, 1) & (SWA_BLOCK - 1)
        own_valid = key <= qry
        next_valid = (key > qry) & (m + 1 < nb)
        dk = jnp.zeros((SWA_BLOCK, HEAD_DIM), F32)
        dv = jnp.zeros((SWA_BLOCK, HEAD_DIM), F32)
        for q_ref, do_ref, lse_ref, delta_ref, valid in (
                (qm_ref, dom_ref, lsem_ref, deltam_ref, own_valid),
                (qn_ref, don_ref, lsen_ref, deltan_ref, next_valid)):
            q, d_out = _stack_heads(q_ref, group), _stack_heads(do_ref, group)
            lse_row = jnp.concatenate([lse_ref[hh] for hh in range(group)], axis=1)
            delta_row = jnp.concatenate([delta_ref[hh] for hh in range(group)], axis=1)
            s_t = _dot_nt(k, q) * scale
            p_t = jnp.exp(jnp.where(valid, s_t - lse_row, NEG))
            ds_t = p_t * (_dot_nt(v, d_out) - delta_row)
            dv = dv + _dot(p_t.astype(BF16), d_out)
            dk = dk + _dot(ds_t.astype(BF16), q)
        dk = dk * scale
        dk_ref[...] = (dk * cos_ref[...]
                       - pltpu.roll(dk, HEAD_DIM // 2, 1) * sin_ref[...]).astype(BF16)
        dv_ref[...] = dv.astype(BF16)

    blk = lambda fn: pl.BlockSpec((SWA_BLOCK, HEAD_DIM), fn)
    nxt = lambda m: jnp.minimum(m + 1, nb - 1)
    wide = lambda fn: pl.BlockSpec((SWA_BLOCK, group * HEAD_DIM), fn)
    rows = lambda fn: pl.BlockSpec((group, 1, SWA_BLOCK), fn)
    q_m, q_n = wide(lambda g, m: (m, g)), wide(lambda g, m: (nxt(m), g))
    r_m, r_n = rows(lambda g, m: (g, 0, m)), rows(lambda g, m: (g, 0, nxt(m)))
    nat = pl.BlockSpec((SWA_BLOCK, LANE), lambda g, m: (m, 0))
    out_blk = blk(lambda g, m: (m, g))
    width = kv_heads * HEAD_DIM
    return pl.pallas_call(
        body, name="swa_bwd_kv", grid=(kv_heads, nb),
        in_specs=[blk(lambda g, m: (m, g)), blk(lambda g, m: (m, kv_heads + g)),
                  q_m, q_n, q_m, q_n, r_m, r_n, r_m, r_n, nat, nat],
        out_specs=[out_blk, out_blk],
        out_shape=[jax.ShapeDtypeStruct((s_len, width), BF16)] * 2,
        compiler_params=_params(2))(kv, kv, proj, proj, d_o, d_o, lse_t, lse_t,
                                    delta_t, delta_t, cos, sin)


def _ada_fwd(c_rows, ada_w, bias_loc):
    n_layers, d, cols = ada_w.shape
    rows = c_rows.shape[0]
    tk = _tile(d, 512)
    n_k = d // tk

    def body(c_ref, w_ref, b_ref, mod_ref, sc_ref, acc_ref):
        k = pl.program_id(1)

        @pl.when(k == 0)
        def _():
            acc_ref[...] = jnp.zeros_like(acc_ref)

        cv = c_ref[...]
        sc = cv * _sigmoid(cv)
        sc_ref[...] = sc
        acc_ref[...] += _dot(sc.astype(BF16), w_ref[...].astype(BF16))

        @pl.when(k == n_k - 1)
        def _():
            mod_ref[...] = acc_ref[...] + b_ref[...]

    return pl.pallas_call(
        body, name="ada_fwd", grid=(n_layers, n_k),
        in_specs=[pl.BlockSpec((rows, tk), lambda l, k: (0, k)),
                  pl.BlockSpec((None, tk, cols), lambda l, k: (l, k, 0)),
                  pl.BlockSpec((None, 1, cols), lambda l, k: (l, 0, 0))],
        out_specs=[pl.BlockSpec((None, rows, cols), lambda l, k: (l, 0, 0)),
                   pl.BlockSpec((None, rows, tk), lambda l, k: (l, 0, k))],
        out_shape=[jax.ShapeDtypeStruct((n_layers, rows, cols), F32),
                   jax.ShapeDtypeStruct((n_layers, rows, d), F32)],
        scratch_shapes=[pltpu.VMEM((rows, cols), F32)],
        compiler_params=_params(2))(c_rows, ada_w, bias_loc)


def _ada_update(sc_t, dmod, w, m, v):
    n_layers, d, cols = w.shape
    tr = _tile(d, 256)
    big = pl.BlockSpec((None, tr, cols), lambda l, i: (l, i, 0))

    def body(sc_ref, dm_ref, w_ref, m_ref, v_ref, g_out, d_out, m_out, v_out):
        g = _dot(sc_ref[...], dm_ref[...])
        delta, m_new, v_new = _adamw(w_ref[...], g, m_ref[...], v_ref[...])
        g_out[...] = g
        d_out[...] = delta
        m_out[...] = m_new
        v_out[...] = v_new

    return pl.pallas_call(
        body, name="ada_update", grid=(n_layers, d // tr),
        in_specs=[pl.BlockSpec((tr, LANE), lambda l, i: (i, 0)),
                  pl.BlockSpec((None, LANE, cols), lambda l, i: (l, 0, 0)), big, big, big],
        out_specs=[big] * 4, out_shape=[jax.ShapeDtypeStruct(w.shape, F32)] * 4,
        compiler_params=_params(2))(sc_t, dmod, w, m, v)


def _row_tile(rows, bytes_per_row):
    tr = SUBLANE * 2
    while tr * 2 <= rows and rows % (tr * 2) == 0 and tr * 2 * bytes_per_row <= 24 * 2 ** 20:
        tr *= 2
    return _tile(rows, tr)


def _slab_sum(name, arrays):
    rows = arrays[0].shape[1]
    per_row = sum(2 * a.shape[2] * (a.shape[0] * a.dtype.itemsize + 4) for a in arrays)
    tr = _row_tile(rows, per_row)
    n = len(arrays)

    def body(*refs):
        for s_ref, out_ref in zip(refs[:n], refs[n:]):
            total = s_ref[0].astype(F32)
            for slot in range(1, s_ref.shape[0]):
                total = total + s_ref[slot].astype(F32)
            out_ref[...] = total

    return pl.pallas_call(
        body, name=name, grid=(rows // tr,),
        in_specs=[pl.BlockSpec((a.shape[0], tr, a.shape[2]), lambda i: (0, i, 0)) for a in arrays],
        out_specs=[pl.BlockSpec((tr, a.shape[2]), lambda i: (i, 0)) for a in arrays],
        out_shape=[jax.ShapeDtypeStruct(a.shape[1:], F32) for a in arrays],
        compiler_params=_params(1))(*arrays)


def _decay(name, w, m, v):
    rows, cols = w.shape
    tr = _row_tile(rows, 2 * cols * 4 * 6)
    blk = pl.BlockSpec((tr, cols), lambda i: (i, 0))

    def body(w_ref, m_ref, v_ref, w_out, m_out, v_out):
        w_out[...] = ADAM_WD * w_ref[...]
        m_out[...] = ADAM_B1 * m_ref[...]
        v_out[...] = ADAM_B2 * v_ref[...]

    return pl.pallas_call(body, name=name, grid=(rows // tr,), in_specs=[blk] * 3,
                          out_specs=[blk] * 3, out_shape=[jax.ShapeDtypeStruct(w.shape, F32)] * 3,
                          compiler_params=_params(1))(w, m, v)


def _shard_update(name, slabs, w, m, v, decayed=False):
    rows, cols = w.shape
    n_slabs = slabs.shape[0]
    tr = _row_tile(rows, 2 * cols * (slabs.dtype.itemsize * n_slabs + 4 * 7))
    blk = pl.BlockSpec((tr, cols), lambda i: (i, 0))
    step = _adamw_decayed if decayed else _adamw

    def body(s_ref, w_ref, m_ref, v_ref, g_out, d_out, m_out, v_out):
        g = s_ref[0].astype(F32)
        for slot in range(1, n_slabs):
            g = g + s_ref[slot].astype(F32)
        delta, m_new, v_new = step(w_ref[...], g, m_ref[...], v_ref[...])
        g_out[...] = g
        d_out[...] = delta
        m_out[...] = m_new
        v_out[...] = v_new

    return pl.pallas_call(
        body, name=name, grid=(rows // tr,),
        in_specs=[pl.BlockSpec((n_slabs, tr, cols), lambda i: (0, i, 0)), blk, blk, blk],
        out_specs=[blk] * 4, out_shape=[jax.ShapeDtypeStruct((rows, cols), F32)] * 4,
        compiler_params=_params(1))(slabs, w, m, v)


def _small_update(gathered, w, m, v):
    shape = jax.ShapeDtypeStruct(w.shape, F32)

    def body(g_ref, w_ref, m_ref, v_ref, g_out, d_out, m_out, v_out):
        g = g_ref[0]
        for dev in range(1, N_DEV):
            g = g + g_ref[dev]
        delta, m_new, v_new = _adamw(w_ref[...], g, m_ref[...], v_ref[...])
        g_out[...] = g
        d_out[...] = delta
        m_out[...] = m_new
        v_out[...] = v_new

    return pl.pallas_call(body, name="small_update", out_shape=[shape] * 4,
                          compiler_params=pltpu.CompilerParams(vmem_limit_bytes=VMEM_LIMIT),
                          )(gathered, w, m, v)


def _rope_tables(s_len):
    half = HEAD_DIM // 2
    inv = ROPE_THETA ** (-jnp.arange(half, dtype=F32) / half)
    ang = jnp.arange(s_len, dtype=F32)[:, None] * inv[None, :]
    cos, sin = jnp.cos(ang), jnp.sin(ang)
    return jnp.concatenate([cos, cos], axis=1), jnp.concatenate([-sin, sin], axis=1)


def _pad_lanes(a):
    return jnp.pad(a, ((0, 0), (0, LANE - a.shape[1])))


def _rows_of(nat, heads):
    return jnp.transpose(nat[:, :heads])[:, None, :]


def _pack(parts):
    tile = SUBLANE * LANE
    flat = []
    for p in parts:
        p = p.reshape(-1)
        flat.append(jnp.pad(p, (0, (-p.shape[0]) % tile)))
    return jnp.concatenate(flat).reshape(-1, LANE)


def _unpack(packed, shapes):
    tile = SUBLANE * LANE
    flat = packed.reshape(-1)
    out, pos = [], 0
    for shape in shapes:
        size = 1
        for dim in shape:
            size *= dim
        out.append(flat[pos:pos + size].reshape(shape))
        pos += size + (-size) % tile
    return out


def kernel(x, c, norm_g, ada_w, ada_b, a_w_in, a_b_f, a_w_out, kv_norm_g, kv_w, b_w_in, b_sinks, b_w_out, final_norm_g, loss_target, m_norm_g, m_ada_w, m_ada_b, m_a_w_in, m_a_b_f, m_a_w_out, m_kv_norm_g, m_kv_w, m_b_w_in, m_b_sinks, m_b_w_out, m_final_norm_g, v_norm_g, v_ada_w, v_ada_b, v_a_w_in, v_a_b_f, v_a_w_out, v_kv_norm_g, v_kv_w, v_b_w_in, v_b_sinks, v_b_w_out, v_final_norm_g):
    s_len, d = x.shape[1], x.shape[2]
    heads = d // HEAD_DIM
    kv_heads = kv_w.shape[1] // (2 * HEAD_DIM)
    kv_width = kv_heads * HEAD_DIM
    ada_cols = ada_w.shape[2]
    assert heads <= LANE and heads % N_DEV == 0 and a_w_in.shape[2] * N_DEV == 4 * d + heads
    me = _slot(_mesh_pos())
    x0 = x[0]
    target = loss_target[0]
    vec = lambda a: a.reshape(1, d)

    sup, extra = 4 * d // N_DEV, heads // N_DEV
    padded = jnp.pad(a_w_in[0].astype(BF16), ((0, 0), (heads, LANE)))
    big_loc = lax.dynamic_slice_in_dim(padded, heads - extra * me, sup, axis=1)
    small_loc = lax.dynamic_slice_in_dim(padded, heads + sup - extra * me, LANE, axis=1)

    (c_all,) = _all_gather("gather_c", [c])
    c_rows = jnp.pad(c_all.reshape(N_DEV, d), ((0, 2 * SUBLANE - N_DEV), (0, 0)))
    bias_loc = lax.dynamic_slice_in_dim(ada_b, me * ada_cols, ada_cols, axis=1)[:, None, :]
    mod_part, sc_rows = _ada_fwd(c_rows, ada_w, bias_loc)
    (mod_recv,) = _all_to_all("exchange_mod", [jnp.transpose(mod_part[:, :N_DEV], (1, 0, 2))])
    mod = jnp.transpose(mod_recv, (1, 0, 2)).reshape(2, 3 * d)

    g_big, g_small = _all_gather("gather_a_w_in", _after([big_loc, small_loc], mod_recv),
                                 sequencer_id=5)
    shift0, scale0, gate0 = vec(mod[0, :d]), vec(mod[0, d:2 * d]), vec(mod[0, 2 * d:])
    shift1, scale1, gate1 = vec(mod[1, :d]), vec(mod[1, d:2 * d]), vec(mod[1, 2 * d:])
    g0, g1, g_kvn, g_fin = vec(norm_g[0]), vec(norm_g[1]), vec(kv_norm_g), vec(final_norm_g)
    h0 = _norm_fwd("norm0", x0, g0, scale0, shift0)
    a_in_decayed = _decay("decay_a_w_in", a_w_in[0], m_a_w_in[0], v_a_w_in[0])
    later = [a_w_out[0].astype(BF16), kv_w.astype(BF16), b_w_in[0].astype(BF16),
             b_w_out[0].astype(BF16)]
    g_a_out, g_kv, w_b_in, g_b_out = _all_gather(
        "gather_rest", _after(later, (g_big, h0, *a_in_decayed)), sequencer_id=1)
    w_a_out = g_a_out.reshape(d, d)
    w_kv = g_kv.reshape(d, 2 * kv_width)
    w_b_out = g_b_out.reshape(d, d)
    lane_id = jnp.arange(LANE)[None, :]
    patch = jnp.stack([jnp.where(lane_id < extra * s, g_small[s - 1], g_big[s, :, :LANE])
                       for s in range(1, N_DEV)])
    w_a_main = g_big.at[1:, :, :LANE].set(patch)
    w_a_f = g_small[N_DEV - 1]
    b_in_cols = b_w_in.shape[2]
    a_tn = _tile(sup, 1024)
    w_a_main_spec = lambda tk: pl.BlockSpec(
        (None, tk, a_tn), lambda i, j, k: (j // (sup // a_tn), k, j % (sup // a_tn)))

    cos, sin = _rope_tables(s_len)
    bias_f = _pad_lanes(a_b_f)
    sinks_row = _pad_lanes(b_sinks)

    proj0 = _mm_plain("proj0", h0, w_a_main, n_cols=4 * d, tn=a_tn, b_spec=w_a_main_spec)
    f_raw = _mm_plain("proj0_f", h0, w_a_f, out_dtype=F32, tn=LANE)
    f_nat = _fgate_fwd(f_raw, bias_f)
    f_t = _rows_of(f_nat, heads)
    o0_t, lse0_t = _fox_fwd(proj0, f_nat, f_t, heads)
    o0, u0 = _gate_fwd(o0_t, proj0, 3 * heads, heads)
    y0, x1 = _mm_residual("out0", u0, w_a_out, x0, gate0)

    h1, hk = _norm_fwd("norm1", x1, g1, scale1, shift1, gb=g_kvn)
    kv = _mm_rope("kv_proj", hk, w_kv, cos, sin, n_cols=2 * kv_width, rope_cols=kv_width,
                  tn=kv_width)
    proj1 = _mm_rope("proj1", h1, w_b_in, cos, sin, n_cols=2 * d, rope_cols=d, tn=b_in_cols,
                     b_spec=lambda tk: pl.BlockSpec((None, tk, b_in_cols),
                                                    lambda i, j, k: (j, k, 0)))
    o1, u1, lse1 = _swa_fwd(proj1, kv, sinks_row, heads, kv_heads)
    y1, x2 = _mm_residual("out1", u1, w_b_out, x1, gate1)

    loss_part, dx2, dy1, sums_f = _loss_bwd(x2, target, y1, g_fin, gate1)

    do1, dz1, delta1 = _mm_gate_bwd("out1_bwd", dy1, w_b_out, proj1, d, o1)
    gw_b_out = _mm_plain("out1_wgrad", u1, dy1, ta=True)
    do1 = _after(do1, gw_b_out)
    dq1, dsinks = _swa_bwd_q(proj1, kv, do1, lse1, delta1, sinks_row, cos, sin, heads, kv_heads)
    dk1, dv1 = _swa_bwd_kv(proj1, kv, do1, _rows_of(lse1, heads), _rows_of(delta1, heads),
                           cos, sin, heads, kv_heads)
    dproj1 = jnp.concatenate([dq1, dz1], axis=1)
    tk_b = _tile(2 * d, b_in_cols)
    dh1 = _mm_plain("proj1_bwd", dproj1, w_b_in, nt=True, n_cols=d, out_dtype=F32, tk=tk_b,
                    b_spec=lambda tk: pl.BlockSpec((None, _tile(d, 1024), tk),
                                                   lambda i, j, k: (k * tk // b_in_cols, j, 0)))
    gw_b_in = _mm_plain("proj1_wgrad", h1, dproj1, ta=True, tn=b_in_cols,
                        out_3d=(N_DEV, b_in_cols))
    dkv = jnp.concatenate([dk1, dv1], axis=1)
    dhk = _mm_plain("kv_bwd", dkv, w_kv, nt=True, out_dtype=F32)
    gw_kv = _mm_plain("kv_wgrad", hk, dkv, ta=True)
    dx1, dy0, sums1 = _norm_bwd("norm1_bwd", x1, dx2, dh1, g1, scale1,
                                dhb=_after(dhk, (gw_b_in, gw_kv)), gb=g_kvn, y=y0, gate=gate0)

    do0, dz0, delta0 = _mm_gate_bwd("out0_bwd", dy0, w_a_out, proj0, 3 * d, o0)
    gw_a_out = _mm_plain("out0_wgrad", u0, dy0, ta=True)
    r_b_out, r_b_in, r_kv, r_a_out = _all_to_all(
        "scatter_grads_early",
        [gw_b_out.reshape(N_DEV, d // N_DEV, d), gw_b_in,
         gw_kv.reshape(N_DEV, d // N_DEV, 2 * kv_width),
         gw_a_out.reshape(N_DEV, d // N_DEV, d)], sequencer_id=2)
    dk0, dv0, dq0_t, dfq_t, dfk_nat = _fox_bwd(proj0, _after(do0, gw_a_out), f_nat, f_t,
                                               lse0_t, _rows_of(delta0, heads), heads)
    dfq_nat = _pad_lanes(jnp.transpose(dfq_t[:, 0, :]))
    df, sums_bf = _fgate_bwd(dfq_nat, dfk_nat, f_raw, bias_f)
    dproj0 = jnp.concatenate([jnp.transpose(dq0_t), dk0, dv0, dz0], axis=1)
    gw_a_big = _mm_plain("proj0_wgrad", h0, dproj0, ta=True, tn=a_tn, out_3d=(N_DEV, sup))
    gw_a_f = _mm_plain("proj0_f_wgrad", h0, df, ta=True, tn=LANE)
    gw_a_small = jnp.concatenate([gw_a_big[1:, :, :LANE], gw_a_f[None]], axis=0)
    sib_big, sib_small = _pair_exchange("reduce_a_w_in_pair", [gw_a_big, gw_a_small],
                                        sequencer_id=3)
    r_b_out, r_b_in, r_kv, r_a_out = _after([r_b_out, r_b_in, r_kv, r_a_out], gw_a_big)
    up_b_out = _shard_update("update_b_w_out", r_b_out, b_w_out[0], m_b_w_out[0], v_b_w_out[0])
    up_b_in = _shard_update("update_b_w_in", r_b_in, b_w_in[0], m_b_w_in[0], v_b_w_in[0])
    up_kv = _shard_update("update_kv_w", r_kv, kv_w, m_kv_w, v_kv_w)
    up_a_out = _shard_update("update_a_w_out", r_a_out, a_w_out[0], m_a_w_out[0], v_a_w_out[0])
    sib_big, sib_small = _after((sib_big, sib_small),
                                (up_b_out[0], up_b_in[0], up_kv[0], up_a_out[0]))
    chip_big, chip_small = _pair_add("reduce_a_w_in_add", [gw_a_big, gw_a_small],
                                     [sib_big, sib_small])
    r_a_big, r_a_small = _chip_exchange("reduce_a_w_in_chips", [chip_big, chip_small],
                                        sequencer_id=4)
    df = _after(df, (chip_big, chip_small))
    dh0_f = _mm_plain("proj0_f_bwd", df, w_a_f, nt=True, out_dtype=F32)
    dh0 = _mm_plain("proj0_bwd", dproj0, w_a_main, nt=True, n_cols=d, out_dtype=F32, init=dh0_f,
                    tk=sup, b_spec=lambda tk: pl.BlockSpec((None, _tile(d, 1024), tk),
                                                           lambda i, j, k: (k, j, 0)))
    grad_x, sums0 = _norm_bwd("norm0_bwd", x0, dx1, dh0, g0, scale0)

    dmod = jnp.stack([jnp.concatenate([sums0[0], sums0[1], sums1[4]]),
                      jnp.concatenate([sums1[0], sums1[1], sums_f[1]])])
    small_shapes = [(2, 3 * d), (2, d), (1, heads), (d,), (1, heads), (d,), (1,)]
    small_grads = [dmod, jnp.stack([sums0[2], sums1[2]]), sums_bf[0:1, :heads], sums1[3],
                   dsinks[0:1, :heads], sums_f[0], loss_part[0, 0:1]]
    (small_all,) = _all_gather("gather_small", [_pack(small_grads)])
    zero = jnp.zeros((1,), F32)
    small = _small_update(
        small_all,
        _pack([ada_b, norm_g, a_b_f, kv_norm_g, b_sinks, final_norm_g, zero]),
        _pack([m_ada_b, m_norm_g, m_a_b_f, m_kv_norm_g, m_b_sinks, m_final_norm_g, zero]),
        _pack([v_ada_b, v_norm_g, v_a_b_f, v_kv_norm_g, v_b_sinks, v_final_norm_g, zero]))
    s_grad, s_delta, s_m, s_v = [_unpack(p, small_shapes) for p in small]
    loss = s_grad[6][0]

    r_a_big, r_a_small = _after((r_a_big, r_a_small), small[0])
    ga_big, ga_small = _slab_sum("sum_a_w_in", [r_a_big, r_a_small])
    ga_shard = lax.dynamic_slice_in_dim(jnp.concatenate([ga_big, ga_small], axis=1),
                                        extra * me, sup + extra, axis=1)
    up_a_in = _shard_update("update_a_w_in", ga_shard[None], *a_in_decayed, decayed=True)

    dmod_all = small_all.reshape(N_DEV, -1)[:, :2 * 3 * d].reshape(N_DEV, 2, 3 * d)
    dmod_loc = lax.dynamic_slice_in_dim(dmod_all, me * ada_cols, ada_cols, axis=2)
    dmod_loc = jnp.pad(jnp.transpose(dmod_loc, (1, 0, 2)), ((0, 0), (0, LANE - N_DEV), (0, 0)))
    sc_t = jnp.pad(jnp.transpose(sc_rows[0, :N_DEV]), ((0, 0), (0, LANE - N_DEV)))
    up_ada = _ada_update(sc_t.astype(BF16), _after(dmod_loc.astype(BF16), up_a_in[0]),
                         ada_w, m_ada_w, v_ada_w)

    lead = lambda a: a[None]
    per_kind = []
    for kind in range(4):
        sm = (s_grad, s_delta, s_m, s_v)[kind]
        per_kind.append([
            sm[1], up_ada[kind], sm[0], lead(up_a_in[kind]), sm[2], lead(up_a_out[kind]),
            sm[3], up_kv[kind], lead(up_b_in[kind]), sm[4], lead(up_b_out[kind]), sm[5]])
    return (loss, grad_x[None], *per_kind[0], *per_kind[1], *per_kind[2], *per_kind[3])
```
